```python
import math
import jax, jax.numpy as jnp
from jax import lax
import numpy as np

D_MODEL = 1024
BATCH = 4
SEQ = 8192
DEPTH = 2
DEC_BATCH = 2
DEC_SEQ = 16384
PAST_LEN = 128

GRID_W = 64
HEAD_DIM = 64
D_A = D_MODEL // 2
HYENA_ORDER = 2
HYENA_BANDS = 16
HYENA_EMB = 2 * HYENA_BANDS + 1
HYENA_FO = 64
HYENA_DECAY_MIN = math.log(100.0) / 1.5
HYENA_DECAY_MAX = math.log(100.0) / 0.3
H_B = D_MODEL // 4 // HEAD_DIM
D_B = H_B * HEAD_DIM
NA_KR = 8
NA_KC = 16
H_C = D_MODEL // 4 // HEAD_DIM
D_C = H_C * HEAD_DIM
DIL_PATTERNS = ((128, 1), (512, 4), (2048, 16))
ROPE_THETA = 10000.0
N_BRANCH = 3
IN_SPLITS = (3 * D_A, D_A, 3 * D_B, D_B, 3 * D_C, D_C, N_BRANCH * D_MODEL)
N_IN = sum(IN_SPLITS)
DEEPNORM_ALPHA = (2 * DEPTH) ** 0.25
DEEPNORM_BETA = (8 * DEPTH) ** -0.25
LN_EPS = 1e-5
NEG_INF = -1e30
F32 = jnp.float32

kernel_name = "hybrid_hyena_natten_dilated_encoder"


def _layernorm(x):
    xf = x.astype(F32)
    mu = jnp.mean(xf, -1, keepdims=True)
    var = jnp.mean(jnp.square(xf - mu), -1, keepdims=True)
    return (xf - mu) * lax.rsqrt(var + LN_EPS)


def _rope(x):
    L, dh = x.shape[1], x.shape[-1]
    half = dh // 2
    inv = ROPE_THETA ** (-jnp.arange(half, dtype=F32) / half)
    ang = jnp.arange(L, dtype=F32)[:, None] * inv[None, :]
    cos = jnp.cos(ang)[None, :, None, :]
    sin = jnp.sin(ang)[None, :, None, :]
    xf = x.astype(F32)
    x1, x2 = xf[..., :half], xf[..., half:]
    return jnp.concatenate([x1 * cos - x2 * sin, x2 * cos + x1 * sin], -1).astype(x.dtype)


def _hyena_spectrum(L, w1, b1, freq, w2, b2, w3, b3, decay):
    t = jnp.arange(L, dtype=F32) / L
    bands = jnp.arange(1, HYENA_BANDS + 1, dtype=F32)
    ang = 2.0 * math.pi * t[:, None] * bands[None, :]
    z = jnp.concatenate([t[:, None], jnp.cos(ang), jnp.sin(ang)], -1)
    freq = freq.astype(F32)
    h = jnp.sin(freq[0] * (z @ w1.astype(F32) + b1.astype(F32)))
    h = jnp.sin(freq[1] * (h @ w2.astype(F32) + b2.astype(F32)))
    h = (h @ w3.astype(F32) + b3.astype(F32)).reshape(L, 2, HYENA_ORDER, D_A)
    h = h * jnp.exp(-t[:, None, None, None] * jnp.abs(decay.astype(F32)))
    fwd, bwd = h[:, 0], h[:, 1]
    k = jnp.concatenate([fwd, jnp.zeros((1, HYENA_ORDER, D_A), F32), bwd[:0:-1]], 0)
    k = k / (jnp.sum(jnp.abs(k), 0, keepdims=True) + 1e-6)
    return jnp.fft.rfft(k, axis=0)


def _hyena(u, conv_w, conv_b, kf, skip):
    L = u.shape[1]
    up = jnp.pad(u, ((0, 0), (1, 1), (0, 0)))
    uc = up[:, :-2] * conv_w[0] + up[:, 1:-1] * conv_w[1] + up[:, 2:] * conv_w[2] + conv_b
    v, x1, x2 = jnp.split(uc.astype(F32), 3, axis=-1)
    z = v
    skip = skip.astype(F32)
    for o, xg in enumerate((x1, x2)):
        zf = jnp.fft.rfft(z, n=2 * L, axis=1)
        z = jnp.fft.irfft(zf * kf[:, o], n=2 * L, axis=1)[:, :L] + skip[o] * z
        z = xg * z
    return z.astype(u.dtype)


def _neighbourhood_attention(q, k, v, rpb):
    B, L, H, dh = q.shape
    rows = L // GRID_W
    kr = min(NA_KR, rows)
    r = jnp.arange(rows)
    c = jnp.arange(GRID_W)
    row_idx = jnp.clip(r - kr // 2, 0, rows - kr)[:, None] + jnp.arange(kr)[None, :]
    col_start = jnp.clip(c - NA_KC // 2, 0, GRID_W - NA_KC)
    col_ok = (c[None, :] >= col_start[:, None]) & (c[None, :] < col_start[:, None] + NA_KC)
    dr = row_idx - r[:, None] + NA_KR - 1
    dc = jnp.clip(c[None, :] - c[:, None], -(NA_KC - 1), NA_KC - 1) + NA_KC - 1
    bias = rpb[:, dr[:, None, :, None], dc[None, :, None, :]].astype(F32)
    qg = q.reshape(B, rows, GRID_W, H, dh)
    kg = k.reshape(B, rows, GRID_W, H, dh)[:, row_idx]
    vg = v.reshape(B, rows, GRID_W, H, dh)[:, row_idx]
    s = jnp.einsum('brqhd,brikhd->bhrqik', qg, kg, preferred_element_type=F32) * (dh ** -0.5)
    s = jnp.where(col_ok[:, None, :], s + bias[None], NEG_INF)
    p = jax.nn.softmax(s.reshape(B, H, rows, GRID_W, kr * GRID_W), -1).reshape(s.shape)
    o = jnp.einsum('bhrqik,brikhd->brqhd', p.astype(v.dtype), vg, preferred_element_type=F32)
    return o.reshape(B, L, H * dh).astype(q.dtype)


def _dilated_pattern(q, k, v, window, dilation):
    B, L, H, dh = q.shape
    blk = window // (2 * dilation)
    Ld = L // dilation
    nb = -(-Ld // blk)
    Lp = nb * blk

    def sub(x):
        x = x.reshape(B, Ld, dilation, H, dh)
        return jnp.pad(x, ((0, 0), (0, Lp - Ld), (0, 0), (0, 0), (0, 0)))

    def win(x):
        xb = jnp.pad(sub(x), ((0, 0), (blk, blk), (0, 0), (0, 0), (0, 0)))
        xb = xb.reshape(B, nb + 2, blk, dilation, H, dh)
        return jnp.concatenate([xb[:, :-2], xb[:, 1:-1], xb[:, 2:]], axis=2)

    qs = sub(q).reshape(B, nb, blk, dilation, H, dh)
    ks, vs = win(k), win(v)
    qi = jnp.arange(blk)
    ki = jnp.arange(3 * blk)
    kpos = (jnp.arange(nb)[:, None] - 1) * blk + ki[None, :]
    off = ki[None, :] - blk - qi[:, None]
    valid = (jnp.abs(off)[None] <= blk) & (kpos[:, None, :] >= 0) & (kpos[:, None, :] < Ld)
    s = jnp.einsum('bnqjhd,bnkjhd->bnjhqk', qs, ks, preferred_element_type=F32) * (dh ** -0.5)
    s = jnp.where(valid[None, :, None, None], s, NEG_INF)
    m = jnp.max(s, -1, keepdims=True)
    e = jnp.exp(s - m)
    l = jnp.sum(e, -1, keepdims=True)
    o = jnp.einsum('bnjhqk,bnkjhd->bnqjhd', (e / l).astype(v.dtype), vs, preferred_element_type=F32)
    lse = jnp.transpose((m + jnp.log(l))[..., 0], (0, 1, 4, 2, 3))
    o = o.reshape(B, Lp, dilation, H, dh)[:, :Ld].reshape(B, L, H, dh)
    lse = lse.reshape(B, Lp, dilation, H)[:, :Ld].reshape(B, L, H)
    return o, lse


def _dilated_mixture(q, k, v):
    B, L, H, dh = q.shape
    res = [_dilated_pattern(q, k, v, w, d) for (w, d) in DIL_PATTERNS]
    outs = jnp.stack([r[0] for r in res], 0)
    wts = jax.nn.softmax(jnp.stack([r[1] for r in res], 0), axis=0)
    o = jnp.sum(wts[..., None] * outs, 0)
    return o.reshape(B, L, H * dh).astype(q.dtype)


def _layer(x, c, l, w_ada, b_ada, w_in, b_in, hy_conv_w, hy_conv_b, hy_w1, hy_b1, hy_freq,
           hy_w2, hy_b2, hy_w3, hy_b3, hy_decay, hy_skip, na_rpb, w_branch_a, w_branch_b,
           w_branch_c, w_out, ln_g, ln_b):
    B, L, _ = x.shape
    ada = jax.nn.silu(c) @ w_ada[l] + b_ada[l]
    shift, scale, gate = jnp.split(ada, 3, axis=-1)
    h = (_layernorm(x) * (1.0 + scale[:, None]) + shift[:, None]).astype(x.dtype)
    proj = h @ w_in[l] + b_in[l]
    points = np.cumsum(IN_SPLITS)[:-1].tolist()
    a_in, a_z, b_qkv, b_z, c_qkv, c_z, g_all = jnp.split(proj, points, axis=-1)
    kf = _hyena_spectrum(L, hy_w1[l], hy_b1[l], hy_freq[l], hy_w2[l], hy_b2[l], hy_w3[l], hy_b3[l], hy_decay[l])
    y_a = _hyena(a_in, hy_conv_w[l], hy_conv_b[l], kf, hy_skip[l]) * jax.nn.silu(a_z)
    qb, kb, vb = [t.reshape(B, L, H_B, HEAD_DIM) for t in jnp.split(b_qkv, 3, axis=-1)]
    y_b = _neighbourhood_attention(qb, kb, vb, na_rpb[l]) * jax.nn.silu(b_z)
    qc, kc, vc = [t.reshape(B, L, H_C, HEAD_DIM) for t in jnp.split(c_qkv, 3, axis=-1)]
    y_c = _dilated_mixture(_rope(qc), _rope(kc), vc) * jax.nn.silu(c_z)
    g_a, g_b, g_c = jnp.split(jax.nn.sigmoid(g_all), 3, axis=-1)
    merged = g_a * (y_a @ w_branch_a[l]) + g_b * (y_b @ w_branch_b[l]) + g_c * (y_c @ w_branch_c[l])
    sub = (merged @ w_out[l]) * gate[:, None]
    res = DEEPNORM_ALPHA * x + sub
    return (_layernorm(res) * ln_g[l] + ln_b[l]).astype(x.dtype)


def setup_inputs(seed: int = 0) -> dict:
    key = jax.random.key(seed)
    ks = jax.random.split(key, 32)
    D = D_MODEL

    def nrm(k, shape, scale):
        return jax.random.normal(k, shape, F32) * scale

    return {
        "x_prompt": nrm(ks[0], (BATCH, SEQ, D), 1.0),
        "x_sample": nrm(ks[1], (DEC_BATCH, DEC_SEQ, D), 1.0),
        "c_prompt": nrm(ks[2], (BATCH, D), 1.0),
        "c_sample": nrm(ks[3], (DEC_BATCH, D), 1.0),
        "w_ada": nrm(ks[4], (DEPTH, D, 3 * D), D ** -0.5),
        "b_ada": nrm(ks[5], (DEPTH, 3 * D), 0.02),
        "w_in": nrm(ks[6], (DEPTH, D, N_IN), D ** -0.5),
        "b_in": nrm(ks[7], (DEPTH, N_IN), 0.02),
        "hy_conv_w": nrm(ks[8], (DEPTH, 3, 3 * D_A), 3 ** -0.5),
        "hy_conv_b": nrm(ks[9], (DEPTH, 3 * D_A), 0.02),
        "hy_w1": nrm(ks[10], (DEPTH, HYENA_EMB, HYENA_FO), HYENA_EMB ** -0.5),
        "hy_b1": nrm(ks[11], (DEPTH, HYENA_FO), 0.02),
        "hy_freq": 1.0 + nrm(ks[12], (DEPTH, 2, HYENA_FO), 0.1),
        "hy_w2": nrm(ks[13], (DEPTH, HYENA_FO, HYENA_FO), HYENA_FO ** -0.5),
        "hy_b2": nrm(ks[14], (DEPTH, HYENA_FO), 0.02),
        "hy_w3": nrm(ks[15], (DEPTH, HYENA_FO, 2 * HYENA_ORDER * D_A), HYENA_FO ** -0.5),
        "hy_b3": nrm(ks[16], (DEPTH, 2 * HYENA_ORDER * D_A), 0.02),
        "hy_decay": jax.random.uniform(ks[17], (DEPTH, D_A), F32, HYENA_DECAY_MIN, HYENA_DECAY_MAX),
        "hy_skip": nrm(ks[18], (DEPTH, HYENA_ORDER, D_A), 0.5),
        "na_rpb": nrm(ks[19], (DEPTH, H_B, 2 * NA_KR - 1, 2 * NA_KC - 1), 0.1),
        "w_branch_a": nrm(ks[20], (DEPTH, D_A, D), DEEPNORM_BETA * D_A ** -0.5),
        "w_branch_b": nrm(ks[21], (DEPTH, D_B, D), DEEPNORM_BETA * D_B ** -0.5),
        "w_branch_c": nrm(ks[22], (DEPTH, D_C, D), DEEPNORM_BETA * D_C ** -0.5),
        "w_out": nrm(ks[23], (DEPTH, D, D), DEEPNORM_BETA * D ** -0.5),
        "ln_g": 1.0 + nrm(ks[24], (DEPTH, D), 0.02),
        "ln_b": nrm(ks[25], (DEPTH, D), 0.02),
    }


def reference(x_prompt, x_sample, c_prompt, c_sample, w_ada, b_ada, w_in, b_in, hy_conv_w, hy_conv_b,
              hy_w1, hy_b1, hy_freq, hy_w2, hy_b2, hy_w3, hy_b3, hy_decay, hy_skip, na_rpb,
              w_branch_a, w_branch_b, w_branch_c, w_out, ln_g, ln_b):
    y_prompt, y_sample = x_prompt, x_sample
    for l in range(DEPTH):
        y_prompt = _layer(y_prompt, c_prompt, l, w_ada, b_ada, w_in, b_in, hy_conv_w, hy_conv_b,
                          hy_w1, hy_b1, hy_freq, hy_w2, hy_b2, hy_w3, hy_b3, hy_decay, hy_skip, na_rpb,
                          w_branch_a, w_branch_b, w_branch_c, w_out, ln_g, ln_b)
        y_sample = _layer(y_sample, c_sample, l, w_ada, b_ada, w_in, b_in, hy_conv_w, hy_conv_b,
                          hy_w1, hy_b1, hy_freq, hy_w2, hy_b2, hy_w3, hy_b3, hy_decay, hy_skip, na_rpb,
                          w_branch_a, w_branch_b, w_branch_c, w_out, ln_g, ln_b)
    return (y_prompt, y_sample)
```

```python
import functools
import math

import jax
import jax.numpy as jnp
from jax import lax
from jax.experimental import pallas as pl
from jax.experimental.pallas import tpu as pltpu

F32 = jnp.float32
BF16 = jnp.bfloat16

GRID_W = 64
HEAD_DIM = 64
HYENA_BANDS = 16
HYENA_EMB = 2 * HYENA_BANDS + 1
NA_KR = 8
NA_KC = 16
DIL_PATTERNS = ((128, 1), (512, 4), (2048, 16))
DIL_BLK = 64
ROPE_THETA = 10000.0
LN_EPS = 1e-5
NEG_INF = -1e30

V7X_LANES = 128
V7X_VMEM_LIMIT_BYTES = 56 * 1024 * 1024

DFT_N2 = V7X_LANES
FEAT_PAD = V7X_LANES


def _params(sem):
    return pltpu.CompilerParams(dimension_semantics=sem, vmem_limit_bytes=V7X_VMEM_LIMIT_BYTES)


def _sigmoid(x):
    return 1.0 / (1.0 + jnp.exp(-x))


def _silu(x):
    return x * _sigmoid(x)


def _ada_body(c_ref, w_ref, b_ref, o_ref):
    s = _silu(c_ref[...])
    o_ref[...] = jnp.dot(s, w_ref[...], preferred_element_type=F32,
                         precision=lax.Precision.HIGHEST) + b_ref[...]


def _ada(c_all, w_ada, b_ada):
    depth, d, n = w_ada.shape
    rows = c_all.shape[0]
    tn = 1024
    return pl.pallas_call(
        _ada_body,
        grid=(depth, n // tn),
        in_specs=[
            pl.BlockSpec((rows, d), lambda l, j: (0, 0)),
            pl.BlockSpec((None, d, tn), lambda l, j: (l, 0, j)),
            pl.BlockSpec((None, 1, tn), lambda l, j: (l, 0, j)),
        ],
        out_specs=pl.BlockSpec((None, rows, tn), lambda l, j: (l, 0, j)),
        out_shape=jax.ShapeDtypeStruct((depth, rows, n), F32),
        compiler_params=_params(("parallel", "parallel")),
        name="ada",
    )(c_all, w_ada, b_ada.reshape(depth, 1, n))


def _layernorm(x):
    mu = jnp.mean(x, -1, keepdims=True)
    xc = x - mu
    var = jnp.mean(xc * xc, -1, keepdims=True)
    return xc * lax.rsqrt(var + LN_EPS)


def _ln_mod_body(x_ref, sc_ref, sh_ref, o_ref):
    h = _layernorm(x_ref[...]) * (1.0 + sc_ref[...]) + sh_ref[...]
    o_ref[...] = h.astype(BF16)


def _ln_mod(x, scale, shift):
    b, l, d = x.shape
    t = 512
    return pl.pallas_call(
        _ln_mod_body,
        grid=(b, l // t),
        in_specs=[
            pl.BlockSpec((None, t, d), lambda i, j: (i, j, 0)),
            pl.BlockSpec((None, 1, d), lambda i, j: (i, 0, 0)),
            pl.BlockSpec((None, 1, d), lambda i, j: (i, 0, 0)),
        ],
        out_specs=pl.BlockSpec((None, t, d), lambda i, j: (i, j, 0)),
        out_shape=jax.ShapeDtypeStruct((b, l, d), BF16),
        compiler_params=_params(("parallel", "parallel")),
        name="ln_mod",
    )(x, scale.reshape(b, 1, d), shift.reshape(b, 1, d))


def _proj_body(h_ref, w_ref, b_ref, o_ref):
    acc = jnp.dot(h_ref[...], w_ref[...], preferred_element_type=F32) + b_ref[...]
    o_ref[...] = acc.astype(o_ref.dtype)


def _proj(h, w, bias):
    b, l, d = h.shape
    n = w.shape[1]
    t, tn = 512, 1024
    return pl.pallas_call(
        _proj_body,
        grid=(n // tn, b, l // t),
        in_specs=[
            pl.BlockSpec((None, t, d), lambda j, i, k: (i, k, 0)),
            pl.BlockSpec((d, tn), lambda j, i, k: (0, j)),
            pl.BlockSpec((1, tn), lambda j, i, k: (0, j)),
        ],
        out_specs=pl.BlockSpec((None, t, tn), lambda j, i, k: (i, k, j)),
        out_shape=jax.ShapeDtypeStruct((b, l, n), BF16),
        compiler_params=_params(("parallel", "parallel", "parallel")),
        name="proj",
    )(h, w, bias.reshape(1, n))


def _rope_lanes(x, cos, sin_signed):
    outs = []
    lane = lax.broadcasted_iota(jnp.int32, (x.shape[0], V7X_LANES), 1)
    first_half = (lane % HEAD_DIM) < (HEAD_DIM // 2)
    for c0 in range(0, x.shape[1], V7X_LANES):
        xc = x[:, c0:c0 + V7X_LANES]
        partner = jnp.where(first_half,
                            pltpu.roll(xc, V7X_LANES - HEAD_DIM // 2, 1),
                            pltpu.roll(xc, HEAD_DIM // 2, 1))
        outs.append(xc * cos[:, c0:c0 + V7X_LANES] + partner * sin_signed[:, c0:c0 + V7X_LANES])
    return jnp.concatenate(outs, 1)


def _proj_rope_body(h_ref, w_ref, b_ref, cos_ref, sin_ref, q_ref, k_ref, v_ref, z_ref):
    dc = q_ref.shape[-1]
    acc = jnp.dot(h_ref[...], w_ref[...], preferred_element_type=F32) + b_ref[...]
    cos, sin = cos_ref[...], sin_ref[...]
    q = _rope_lanes(acc[:, :dc], cos, sin) * (HEAD_DIM ** -0.5)
    k = _rope_lanes(acc[:, dc:2 * dc], cos, sin)
    q_ref[...] = q.astype(BF16)
    k_ref[...] = k.astype(BF16)
    v_ref[...] = acc[:, 2 * dc:3 * dc].astype(BF16)
    z_ref[...] = acc[:, 3 * dc:].astype(BF16)


def _proj_rope(h, w, bias, cos_t, sin_t):
    b, l, d = h.shape
    n = w.shape[1]
    dc = n // 4
    t = 512
    tok = pl.BlockSpec((None, t, dc), lambda i, k: (i, k, 0))
    shp = jax.ShapeDtypeStruct((b, l, dc), BF16)
    return pl.pallas_call(
        _proj_rope_body,
        grid=(b, l // t),
        in_specs=[
            pl.BlockSpec((None, t, d), lambda i, k: (i, k, 0)),
            pl.BlockSpec((d, n), lambda i, k: (0, 0)),
            pl.BlockSpec((1, n), lambda i, k: (0, 0)),
            pl.BlockSpec((t, dc), lambda i, k: (k, 0)),
            pl.BlockSpec((t, dc), lambda i, k: (k, 0)),
        ],
        out_specs=[tok, tok, tok, tok],
        out_shape=[shp, shp, shp, shp],
        compiler_params=_params(("parallel", "parallel")),
        name="proj_rope",
    )(h, w, bias.reshape(1, n), cos_t, sin_t)


def _hy_pre_body(main_ref, prev_ref, next_ref, cw_ref, cb_ref, v_ref, x1_ref, x2_ref, pad_ref, *, da):
    i = pl.program_id(1)
    last = pl.num_programs(1) - 1
    t = main_ref.shape[0]
    halo = prev_ref.shape[0]
    outs = (v_ref, x1_ref, x2_ref)
    az = main_ref[:, 3 * da:].astype(F32)
    gate = _silu(az)
    for part in range(3):
        cols = slice(part * da, (part + 1) * da)
        prev_row = jnp.where(i > 0, prev_ref[:, cols].astype(F32)[halo - 1:halo], 0.0)
        next_row = jnp.where(i < last, next_ref[:, cols].astype(F32)[0:1], 0.0)
        pad_ref[7:8, :] = prev_row
        pad_ref[8:8 + t, :] = main_ref[:, cols].astype(F32)
        pad_ref[8 + t:9 + t, :] = next_row
        uc = (pad_ref[7:7 + t, :] * cw_ref[0:1, cols] + pad_ref[8:8 + t, :] * cw_ref[1:2, cols]
              + pad_ref[9:9 + t, :] * cw_ref[2:3, cols] + cb_ref[:, cols])
        if part == 2:
            uc = uc * gate
        outs[part][...] = uc.astype(BF16)


def _hy_pre(proj_a, conv_w, conv_b):
    b, l, n = proj_a.shape
    da = n // 4
    t, halo = 512, 16
    nh = t // halo
    tok = pl.BlockSpec((None, t, da), lambda i, j: (i, j, 0))
    shp = jax.ShapeDtypeStruct((b, l, da), BF16)
    return pl.pallas_call(
        functools.partial(_hy_pre_body, da=da),
        grid=(b, l // t),
        in_specs=[
            pl.BlockSpec((None, t, n), lambda i, j: (i, j, 0)),
            pl.BlockSpec((None, halo, n), lambda i, j: (i, jnp.maximum(j * nh - 1, 0), 0)),
            pl.BlockSpec((None, halo, n), lambda i, j: (i, jnp.minimum((j + 1) * nh, l // halo - 1), 0)),
            pl.BlockSpec((3, 3 * da), lambda i, j: (0, 0)),
            pl.BlockSpec((1, 3 * da), lambda i, j: (0, 0)),
        ],
        out_specs=[tok, tok, tok],
        out_shape=[shp, shp, shp],
        scratch_shapes=[pltpu.VMEM((t + 16, da), F32)],
        compiler_params=_params(("parallel", "parallel")),
        name="hy_pre",
    )(proj_a, proj_a, proj_a, conv_w, conv_b.reshape(1, 3 * da))


def _filter_body(feat_ref, w1_ref, b1_ref, f0_ref, w2_ref, b2_ref, f1_ref, w3_ref, b3_ref, dec_ref,
                 hf_ref, sum_ref):
    i = pl.program_id(0)
    hp = lax.Precision.HIGHEST
    feat = feat_ref[...]
    t = feat[:, 0:1]
    h = jnp.sin(f0_ref[...] * (jnp.dot(feat, w1_ref[...], preferred_element_type=F32, precision=hp)
                               + b1_ref[...]))
    h = jnp.sin(f1_ref[...] * (jnp.dot(h, w2_ref[...], preferred_element_type=F32, precision=hp)
                               + b2_ref[...]))
    h = jnp.dot(h, w3_ref[...], preferred_element_type=F32, precision=hp) + b3_ref[...]
    h = h * jnp.exp(-t * jnp.abs(dec_ref[...]))
    rows = lax.broadcasted_iota(jnp.int32, h.shape, 0) + i * h.shape[0]
    cols = lax.broadcasted_iota(jnp.int32, h.shape, 1)
    h = jnp.where((rows == 0) & (cols >= h.shape[1] // 2), 0.0, h)
    hf_ref[...] = h.astype(BF16)

    @pl.when(i == 0)
    def _():
        sum_ref[...] = jnp.zeros_like(sum_ref)

    sum_ref[...] += jnp.sum(jnp.abs(h), 0, keepdims=True)


def _filter_taps(l, w1, b1, freq, w2, b2, w3, b3, decay):
    fo = w1.shape[1]
    n = w3.shape[1]
    da = decay.shape[0]
    t = jnp.arange(l, dtype=F32) / l
    bands = jnp.arange(1, HYENA_BANDS + 1, dtype=F32)
    ang = 2.0 * math.pi * t[:, None] * bands[None, :]
    feat = jnp.concatenate([t[:, None], jnp.cos(ang), jnp.sin(ang)], -1)
    feat = jnp.pad(feat, ((0, 0), (0, FEAT_PAD - HYENA_EMB)))
    w1p = jnp.pad(w1, ((0, FEAT_PAD - HYENA_EMB), (0, 0)))
    dec = jnp.tile(decay, n // da).reshape(1, n)
    tt = 512
    const = lambda shape: pl.BlockSpec(shape, lambda i: (0,) * len(shape))
    return pl.pallas_call(
        _filter_body,
        grid=(l // tt,),
        in_specs=[
            pl.BlockSpec((tt, FEAT_PAD), lambda i: (i, 0)),
            const((FEAT_PAD, fo)), const((1, fo)), const((1, fo)),
            const((fo, fo)), const((1, fo)), const((1, fo)),
            const((fo, n)), const((1, n)), const((1, n)),
        ],
        out_specs=[pl.BlockSpec((tt, n), lambda i: (i, 0)), const((1, n))],
        out_shape=[jax.ShapeDtypeStruct((l, n), BF16), jax.ShapeDtypeStruct((1, n), F32)],
        compiler_params=_params(("arbitrary",)),
        name="filter_taps",
    )(feat, w1p, b1.reshape(1, fo), freq[0].reshape(1, fo), w2, b2.reshape(1, fo),
      freq[1].reshape(1, fo), w3, b3.reshape(1, n), dec)


def _dft_tables(l):
    n = 2 * l
    n1 = n // DFT_N2
    kk = jnp.arange(n1, dtype=jnp.int32)
    nn = jnp.arange(n1 // 2, dtype=jnp.int32)
    th = (2.0 * math.pi / n1) * ((kk[:, None] * nn[None, :]) % n1).astype(F32)
    f1 = jnp.concatenate([jnp.cos(th), -jnp.sin(th)], 0).astype(BF16)
    g = (jnp.concatenate([jnp.cos(th).T, -jnp.sin(th).T], 1) / n).astype(BF16)
    k2 = jnp.arange(DFT_N2, dtype=jnp.int32)
    n2 = jnp.arange(DFT_N2, dtype=jnp.int32)
    ph = (n2[None, None, :] * (k2[None, :, None] * n1 + kk[:, None, None])) % n
    ang = (2.0 * math.pi / n) * ph.astype(F32)
    c, s = jnp.cos(ang), jnp.sin(ang)
    mf = jnp.concatenate([jnp.concatenate([c, s], 2), jnp.concatenate([-s, c], 2)], 1).astype(BF16)
    mi = jnp.swapaxes(mf, 1, 2)
    return f1, g, mf, mi


def _lmm_body(f_ref, z_ref, o_ref):
    o_ref[...] = jnp.dot(f_ref[...], z_ref[...], preferred_element_type=F32).astype(o_ref.dtype)


def _lmm_gate_body(f_ref, b_ref, x_ref, z_ref, skip_ref, o_ref):
    y = jnp.dot(f_ref[...], b_ref[...], preferred_element_type=F32)
    z = z_ref[...].astype(F32)
    o_ref[...] = (x_ref[...].astype(F32) * (y + skip_ref[...] * z)).astype(o_ref.dtype)


def _lmm(f, z, gate_inputs=None):
    b, kd, nc = z.shape
    r = f.shape[0]
    tn = 2048
    col = lambda rows: pl.BlockSpec((None, rows, tn), lambda i, j: (i, 0, j))
    in_specs = [pl.BlockSpec((r, kd), lambda i, j: (0, 0)), col(kd)]
    args = [f, z]
    body = _lmm_body
    if gate_inputs is not None:
        in_specs += [col(r), col(r), pl.BlockSpec((1, tn), lambda i, j: (0, j))]
        args += list(gate_inputs)
        body = _lmm_gate_body
    return pl.pallas_call(
        body,
        grid=(b, nc // tn),
        in_specs=in_specs,
        out_specs=col(r),
        out_shape=jax.ShapeDtypeStruct((b, r, nc), BF16),
        compiler_params=_params(("parallel", "parallel")),
        name="dft_outer",
    )(*args)


def _slab_conv_body(mf_ref, mi_ref, k_ref, a_ref, o_ref):
    n2, c = a_ref.shape[1], a_ref.shape[2]
    x = jnp.dot(mf_ref[...], a_ref[...].reshape(2 * n2, c), preferred_element_type=F32)
    xr, xi = x[:n2], x[n2:]
    kr, ki = k_ref[0].astype(F32), k_ref[1].astype(F32)
    y = jnp.concatenate([xr * kr - xi * ki, xr * ki + xi * kr], 0).astype(BF16)
    out = jnp.dot(mi_ref[...], y, preferred_element_type=F32)
    o_ref[...] = out.astype(BF16).reshape(2, n2, c)


def _slab_conv(mf, mi, kspec, a, order):
    b, _, n1, n2, c = a.shape
    m = 2 * n2
    return pl.pallas_call(
        _slab_conv_body,
        grid=(n1, b),
        in_specs=[
            pl.BlockSpec((None, m, m), lambda k, i: (k, 0, 0)),
            pl.BlockSpec((None, m, m), lambda k, i: (k, 0, 0)),
            pl.BlockSpec((None, 2, n2, c), lambda k, i: (k, 0, 0, order)),
            pl.BlockSpec((None, 2, None, n2, c), lambda k, i: (i, 0, k, 0, 0)),
        ],
        out_specs=pl.BlockSpec((None, 2, None, n2, c), lambda k, i: (i, 0, k, 0, 0)),
        out_shape=jax.ShapeDtypeStruct(a.shape, BF16),
        compiler_params=_params(("parallel", "parallel")),
        name="slab_conv",
    )(mf, mi, kspec, a)


def _slab_spec_body(mf_ref, sum_ref, a_ref, k_ref):
    n2, c = a_ref.shape[1], a_ref.shape[2]
    half = c // 2
    x = jnp.dot(mf_ref[...], a_ref[...].reshape(2 * n2, c), preferred_element_type=F32)
    inv = 1.0 / (sum_ref[:, :half] + sum_ref[:, half:] + 1e-6)
    k_ref[0] = ((x[:n2, :half] + x[:n2, half:]) * inv).astype(BF16)
    k_ref[1] = ((x[n2:, :half] - x[n2:, half:]) * inv).astype(BF16)


def _slab_spec(mf, sums, a):
    _, _, n1, n2, c = a.shape
    m = 2 * n2
    return pl.pallas_call(
        _slab_spec_body,
        grid=(n1,),
        in_specs=[
            pl.BlockSpec((None, m, m), lambda k: (k, 0, 0)),
            pl.BlockSpec((1, c), lambda k: (0, 0)),
            pl.BlockSpec((None, 2, None, n2, c), lambda k: (0, 0, k, 0, 0)),
        ],
        out_specs=pl.BlockSpec((None, 2, n2, c // 2), lambda k: (k, 0, 0, 0)),
        out_shape=jax.ShapeDtypeStruct((n1, 2, n2, c // 2), BF16),
        compiler_params=_params(("parallel",)),
        name="slab_spec",
    )(mf, sums, a)


def _hyena_spectrum(l, tables, w1, b1, freq, w2, b2, w3, b3, decay):
    f1, _, mf, _ = tables
    n1 = f1.shape[0] // 2
    taps, sums = _filter_taps(l, w1, b1, freq, w2, b2, w3, b3, decay)
    c = taps.shape[1]
    a = _lmm(f1, taps.reshape(1, n1 // 2, DFT_N2 * c))
    return _slab_spec(mf, sums, a.reshape(1, 2, n1, DFT_N2, c))


def _hyena(v, x1, x2g, kspec, skip, tables):
    f1, g, mf, mi = tables
    b, l, c = v.shape
    n1 = f1.shape[0] // 2
    rows = n1 // 2
    flat = lambda u: u.reshape(b, rows, DFT_N2 * c)
    z = flat(v)
    for order, xg in enumerate((flat(x1), flat(x2g))):
        a = _lmm(f1, z)
        bb = _slab_conv(mf, mi, kspec, a.reshape(b, 2, n1, DFT_N2, c), order)
        skip_tile = jnp.tile(skip[order].astype(F32), DFT_N2).reshape(1, DFT_N2 * c)
        z = _lmm(g, bb.reshape(b, 2 * n1, DFT_N2 * c), (xg, z, skip_tile))
    return z.reshape(b, l, c)


def _head_masks(rows, heads):
    lane = lax.broadcasted_iota(jnp.int32, (rows, heads * HEAD_DIM), 1)
    return [(lane >= h * HEAD_DIM) & (lane < (h + 1) * HEAD_DIM) for h in range(heads)]


def _stack_heads(q, masks):
    zero = jnp.zeros_like(q)
    return jnp.concatenate([jnp.where(m, q, zero) for m in masks], 0)


def _unstack_heads(res, masks, rows):
    out = jnp.where(masks[0], res[:rows], 0.0)
    for h in range(1, len(masks)):
        out = out + jnp.where(masks[h], res[h * rows:(h + 1) * rows], 0.0)
    return out


def _na_bias_table(rpb):
    heads = rpb.shape[0]
    c = jnp.arange(GRID_W)
    col_start = jnp.clip(c - NA_KC // 2, 0, GRID_W - NA_KC)
    col_ok = (c[None, :] >= col_start[:, None]) & (c[None, :] < col_start[:, None] + NA_KC)
    dc = jnp.clip(c[None, :] - c[:, None], -(NA_KC - 1), NA_KC - 1) + NA_KC - 1
    idx = jnp.arange(NA_KR)
    dr = jnp.arange(NA_KR)[None, :] - idx[:, None] + NA_KR - 1
    bias = rpb[:, dr[:, :, None, None], dc[None, None]].astype(F32)
    bias = jnp.transpose(bias, (1, 0, 3, 2, 4))
    bias = jnp.where(col_ok[None, None, :, None, :], bias, NEG_INF)
    return bias.reshape(NA_KR, heads * GRID_W, NA_KR * GRID_W)


def _na_body(q_ref, k_ref, v_ref, z_ref, bias_ref, o_ref, *, rows, rb, heads):
    blk = pl.program_id(1)
    masks = _head_masks(GRID_W, heads)
    span = NA_KR * GRID_W

    def one_row(rr, carry):
        r = blk * rb + rr
        start = jnp.clip(r - NA_KR // 2, 0, rows - NA_KR)
        idx = r - start
        koff = pl.multiple_of(start * GRID_W, GRID_W)
        qoff = pl.multiple_of(rr * GRID_W, GRID_W)
        q = q_ref[pl.ds(qoff, GRID_W), :] * jnp.asarray(HEAD_DIM ** -0.5, BF16)
        kk = k_ref[pl.ds(koff, span), :]
        vv = v_ref[pl.ds(koff, span), :]
        s = lax.dot_general(_stack_heads(q, masks), kk, (((1,), (1,)), ((), ())),
                            preferred_element_type=F32)
        s = s + bias_ref[idx]
        m = jnp.max(s, -1, keepdims=True)
        e = jnp.exp(s - m)
        p = (e / jnp.sum(e, -1, keepdims=True)).astype(BF16)
        o = _unstack_heads(jnp.dot(p, vv, preferred_element_type=F32), masks, GRID_W)
        z = z_ref[pl.ds(qoff, GRID_W), :].astype(F32)
        o_ref[pl.ds(qoff, GRID_W), :] = (o * _silu(z)).astype(BF16)
        return carry

    lax.fori_loop(0, rb, one_row, 0)


def _na(proj_b, bias):
    b, l, n = proj_b.shape
    db = n // 4
    heads = db // HEAD_DIM
    rows = l // GRID_W
    rb = 8
    t = rb * GRID_W
    return pl.pallas_call(
        functools.partial(_na_body, rows=rows, rb=rb, heads=heads),
        grid=(b, rows // rb),
        in_specs=[
            pl.BlockSpec((None, t, db), lambda i, j: (i, j, 0)),
            pl.BlockSpec((None, l, db), lambda i, j: (i, 0, 1)),
            pl.BlockSpec((None, l, db), lambda i, j: (i, 0, 2)),
            pl.BlockSpec((None, t, db), lambda i, j: (i, j, 3)),
            pl.BlockSpec(bias.shape, lambda i, j: (0, 0, 0)),
        ],
        out_specs=pl.BlockSpec((None, t, db), lambda i, j: (i, j, 0)),
        out_shape=jax.ShapeDtypeStruct((b, l, db), BF16),
        compiler_params=_params(("parallel", "parallel")),
        name="na",
    )(proj_b, proj_b, proj_b, proj_b, bias)


def _dil_body(q_ref, kp_ref, kc_ref, kn_ref, vp_ref, vc_ref, vn_ref, o_ref, lse_ref, *, heads):
    n = pl.program_id(2)
    nb = pl.num_programs(2)
    blk = DIL_BLK
    masks = _head_masks(blk, heads)
    kcat = jnp.concatenate([kp_ref[...], kc_ref[...], kn_ref[...]], 0)
    vcat = jnp.concatenate([vp_ref[...], vc_ref[...], vn_ref[...]], 0)
    s = lax.dot_general(_stack_heads(q_ref[...], masks), kcat, (((1,), (1,)), ((), ())),
                        preferred_element_type=F32)
    qi = lax.broadcasted_iota(jnp.int32, s.shape, 0) % blk
    ki = lax.broadcasted_iota(jnp.int32, s.shape, 1)
    off = ki - blk - qi
    valid = (jnp.abs(off) <= blk) & ((ki >= blk) | (n > 0)) & ((ki < 2 * blk) | (n < nb - 1))
    s = jnp.where(valid, s, NEG_INF)
    m = jnp.max(s, -1, keepdims=True)
    e = jnp.exp(s - m)
    l = jnp.sum(e, -1, keepdims=True)
    p = (e / l).astype(BF16)
    o = _unstack_heads(jnp.dot(p, vcat, preferred_element_type=F32), masks, blk)
    lse = jnp.broadcast_to(m + jnp.log(l), (heads * blk, heads * HEAD_DIM))
    o_ref[...] = o.astype(BF16)
    lse_ref[...] = _unstack_heads(lse, masks, blk)


def _dilated_pattern(q, k, v, dilation):
    b, l, dc = q.shape
    heads = dc // HEAD_DIM
    ld = l // dilation
    nb = ld // DIL_BLK
    view = lambda u: u.reshape(b, ld, dilation * dc)
    blk = lambda f: pl.BlockSpec((None, DIL_BLK, dc), f)
    cur = lambda i, j, n: (i, n, j)
    prev = lambda i, j, n: (i, jnp.maximum(n - 1, 0), j)
    nxt = lambda i, j, n: (i, jnp.minimum(n + 1, nb - 1), j)
    o, lse = pl.pallas_call(
        functools.partial(_dil_body, heads=heads),
        grid=(b, dilation, nb),
        in_specs=[blk(cur), blk(prev), blk(cur), blk(nxt), blk(prev), blk(cur), blk(nxt)],
        out_specs=[blk(cur), blk(cur)],
        out_shape=[jax.ShapeDtypeStruct((b, ld, dilation * dc), BF16),
                   jax.ShapeDtypeStruct((b, ld, dilation * dc), F32)],
        compiler_params=_params(("parallel", "parallel", "parallel")),
        name=f"dil_d{dilation}",
    )(view(q), view(k), view(k), view(k), view(v), view(v), view(v))
    return o.reshape(b, l, dc), lse.reshape(b, l, dc)


def _tail_body(x_ref, ya_ref, yb_ref, o1_ref, o2_ref, o3_ref, l1_ref, l2_ref, l3_ref, cz_ref, g_ref,
               gate_ref, wa_ref, wb_ref, wc_ref, wo_ref, lng_ref, lnb_ref, out_ref, *, alpha):
    d = x_ref.shape[-1]
    la, lb, lc = l1_ref[...], l2_ref[...], l3_ref[...]
    m = jnp.maximum(jnp.maximum(la, lb), lc)
    ea, eb, ec = jnp.exp(la - m), jnp.exp(lb - m), jnp.exp(lc - m)
    den = ea + eb + ec
    o = ((ea / den) * o1_ref[...].astype(F32) + (eb / den) * o2_ref[...].astype(F32)
         + (ec / den) * o3_ref[...].astype(F32))
    yc = (o * _silu(cz_ref[...].astype(F32))).astype(BF16)
    pa = jnp.dot(ya_ref[...], wa_ref[...], preferred_element_type=F32)
    pb = jnp.dot(yb_ref[...], wb_ref[...], preferred_element_type=F32)
    pc = jnp.dot(yc, wc_ref[...], preferred_element_type=F32)
    g = _sigmoid(g_ref[...].astype(F32))
    merged = g[:, :d] * pa + g[:, d:2 * d] * pb + g[:, 2 * d:] * pc
    sub = jnp.dot(merged.astype(BF16), wo_ref[...], preferred_element_type=F32) * gate_ref[...]
    res = alpha * x_ref[...] + sub
    out_ref[...] = _layernorm(res) * lng_ref[...] + lnb_ref[...]


def _tail(x, ya, yb, dil, cz, g_all, gate, wa, wb, wc, wo, ln_g, ln_b, alpha):
    b, l, d = x.shape
    t = 256
    tok = lambda w: pl.BlockSpec((None, t, w), lambda i, j: (i, j, 0))
    const = lambda a: pl.BlockSpec(a.shape, lambda i, j: (0,) * a.ndim)
    (o1, l1), (o2, l2), (o3, l3) = dil
    dc = cz.shape[-1]
    ln_g, ln_b = ln_g.reshape(1, d), ln_b.reshape(1, d)
    return pl.pallas_call(
        functools.partial(_tail_body, alpha=alpha),
        grid=(b, l // t),
        in_specs=[tok(d), tok(ya.shape[-1]), tok(yb.shape[-1]), tok(dc), tok(dc), tok(dc),
                  tok(dc), tok(dc), tok(dc), tok(dc), tok(3 * d),
                  pl.BlockSpec((None, 1, d), lambda i, j: (i, 0, 0)),
                  const(wa), const(wb), const(wc), const(wo), const(ln_g), const(ln_b)],
        out_specs=tok(d),
        out_shape=jax.ShapeDtypeStruct((b, l, d), F32),
        compiler_params=_params(("parallel", "parallel")),
        name="tail",
    )(x, ya, yb, o1, o2, o3, l1, l2, l3, cz, g_all, gate.reshape(b, 1, d), wa, wb, wc, wo, ln_g, ln_b)


def _rope_tables(l, heads):
    half = HEAD_DIM // 2
    inv = ROPE_THETA ** (-jnp.arange(half, dtype=F32) / half)
    ang = jnp.arange(l, dtype=F32)[:, None] * inv[None, :]
    cos, sin = jnp.cos(ang), jnp.sin(ang)
    return (jnp.tile(jnp.concatenate([cos, cos], -1), (1, heads)),
            jnp.tile(jnp.concatenate([-sin, sin], -1), (1, heads)))


def _layer(x, ada, lw, consts, alpha):
    d = x.shape[-1]
    shift, scale, gate = ada[:, :d], ada[:, d:2 * d], ada[:, 2 * d:]
    h = _ln_mod(x, scale, shift)
    w_in, b_in = lw["w_in"], lw["b_in"]
    proj_a = _proj(h, w_in[:, :2 * d], b_in[:2 * d])
    proj_b = _proj(h, w_in[:, 2 * d:3 * d], b_in[2 * d:3 * d])
    cq, ck, cv, cz = _proj_rope(h, w_in[:, 3 * d:4 * d], b_in[3 * d:4 * d], *consts["rope"])
    g_all = _proj(h, w_in[:, 4 * d:], b_in[4 * d:])

    v, x1, x2g = _hy_pre(proj_a, lw["conv_w"], lw["conv_b"])
    ya = _hyena(v, x1, x2g, consts["kspec"], lw["skip"], consts["dft"])
    yb = _na(proj_b, lw["na_bias"])
    dil = [_dilated_pattern(cq, ck, cv, dil_d) for _, dil_d in DIL_PATTERNS]
    return _tail(x, ya, yb, dil, cz, g_all, gate, lw["wa"], lw["wb"], lw["wc"], lw["wo"],
                 lw["ln_g"], lw["ln_b"], alpha)


def kernel(x_prompt, x_sample, c_prompt, c_sample, w_ada, b_ada, w_in, b_in, hy_conv_w, hy_conv_b, hy_w1, hy_b1, hy_freq, hy_w2, hy_b2, hy_w3, hy_b3, hy_decay, hy_skip, na_rpb, w_branch_a, w_branch_b, w_branch_c, w_out, ln_g, ln_b):
    depth, d, _ = w_in.shape
    heads_c = (d // 4) // HEAD_DIM
    alpha = (2 * depth) ** 0.25
    groups = [(x_prompt, c_prompt), (x_sample, c_sample)]

    nb_p = c_prompt.shape[0]
    c_all = jnp.concatenate([c_prompt, c_sample], 0)
    pad_rows = -c_all.shape[0] % 8
    ada_all = _ada(jnp.pad(c_all, ((0, pad_rows), (0, 0))), w_ada, b_ada)
    adas = [ada_all[:, :nb_p], ada_all[:, nb_p:nb_p + c_sample.shape[0]]]

    shared = {}
    for x, _ in groups:
        l = x.shape[1]
        if l not in shared:
            shared[l] = {"dft": _dft_tables(l), "rope": _rope_tables(l, heads_c)}

    ys = [x for x, _ in groups]
    for layer in range(depth):
        lw = {
            "w_in": w_in[layer].astype(BF16), "b_in": b_in[layer],
            "conv_w": hy_conv_w[layer], "conv_b": hy_conv_b[layer], "skip": hy_skip[layer],
            "na_bias": _na_bias_table(na_rpb[layer]),
            "wa": w_branch_a[layer].astype(BF16), "wb": w_branch_b[layer].astype(BF16),
            "wc": w_branch_c[layer].astype(BF16), "wo": w_out[layer].astype(BF16),
            "ln_g": ln_g[layer], "ln_b": ln_b[layer],
        }
        kspecs = {}
        for gi in range(len(groups)):
            l = ys[gi].shape[1]
            if l not in kspecs:
                kspecs[l] = _hyena_spectrum(l, shared[l]["dft"], hy_w1[layer], hy_b1[layer], hy_freq[layer],
                                            hy_w2[layer], hy_b2[layer], hy_w3[layer], hy_b3[layer],
                                            hy_decay[layer])
            consts = dict(shared[l], kspec=kspecs[l])
            ys[gi] = _layer(ys[gi], adas[gi][layer], lw, consts, alpha)
    return tuple(ys)
```

```python
import functools
import math

import jax
import jax.numpy as jnp
from jax import lax
from jax.experimental import pallas as pl
from jax.experimental.pallas import tpu as pltpu

F32 = jnp.float32
BF16 = jnp.bfloat16

GRID_W = 64
HEAD_DIM = 64
HYENA_BANDS = 16
HYENA_EMB = 2 * HYENA_BANDS + 1
NA_KR = 8
NA_KC = 16
DIL_PATTERNS = ((128, 1), (512, 4), (2048, 16))
DIL_BLK = 64
ROPE_THETA = 10000.0
LN_EPS = 1e-5
NEG_INF = -1e30

V7X_LANES = 128
V7X_SUBLANES = 8
V7X_VMEM_LIMIT_BYTES = 56 * 1024 * 1024

DFT_N2 = V7X_LANES
PAIR_GROUP = V7X_SUBLANES
DFT_STEP_ROWS = 64
FEAT_PAD = V7X_LANES


def _params(sem):
    return pltpu.CompilerParams(dimension_semantics=sem, vmem_limit_bytes=V7X_VMEM_LIMIT_BYTES)


def _sigmoid(x):
    return 1.0 / (1.0 + jnp.exp(-x))


def _silu(x):
    return x * _sigmoid(x)


def _ada_body(c_ref, w_ref, b_ref, o_ref):
    s = _silu(c_ref[...])
    o_ref[...] = jnp.dot(s, w_ref[...], preferred_element_type=F32,
                         precision=lax.Precision.HIGHEST) + b_ref[...]


def _ada(c_all, w_ada, b_ada):
    depth, d, n = w_ada.shape
    rows = c_all.shape[0]
    tn = 1024
    return pl.pallas_call(
        _ada_body,
        grid=(depth, n // tn),
        in_specs=[
            pl.BlockSpec((rows, d), lambda l, j: (0, 0)),
            pl.BlockSpec((None, d, tn), lambda l, j: (l, 0, j)),
            pl.BlockSpec((None, 1, tn), lambda l, j: (l, 0, j)),
        ],
        out_specs=pl.BlockSpec((None, rows, tn), lambda l, j: (l, 0, j)),
        out_shape=jax.ShapeDtypeStruct((depth, rows, n), F32),
        compiler_params=_params(("parallel", "parallel")),
        name="ada",
    )(c_all, w_ada, b_ada.reshape(depth, 1, n))


def _layernorm(x):
    mu = jnp.mean(x, -1, keepdims=True)
    xc = x - mu
    var = jnp.mean(xc * xc, -1, keepdims=True)
    return xc * lax.rsqrt(var + LN_EPS)


def _ln_mod_body(x_ref, sc_ref, sh_ref, o_ref):
    h = _layernorm(x_ref[...]) * (1.0 + sc_ref[...]) + sh_ref[...]
    o_ref[...] = h.astype(BF16)


def _ln_mod(x, scale, shift):
    b, l, d = x.shape
    t = 512
    return pl.pallas_call(
        _ln_mod_body,
        grid=(b, l // t),
        in_specs=[
            pl.BlockSpec((None, t, d), lambda i, j: (i, j, 0)),
            pl.BlockSpec((None, 1, d), lambda i, j: (i, 0, 0)),
            pl.BlockSpec((None, 1, d), lambda i, j: (i, 0, 0)),
        ],
        out_specs=pl.BlockSpec((None, t, d), lambda i, j: (i, j, 0)),
        out_shape=jax.ShapeDtypeStruct((b, l, d), BF16),
        compiler_params=_params(("parallel", "parallel")),
        name="ln_mod",
    )(x, scale.reshape(b, 1, d), shift.reshape(b, 1, d))


def _proj_body(h_ref, w_ref, b_ref, o_ref):
    acc = jnp.dot(h_ref[...], w_ref[...], preferred_element_type=F32) + b_ref[...]
    o_ref[...] = acc.astype(o_ref.dtype)


def _proj(h, w, bias):
    b, l, d = h.shape
    n = w.shape[1]
    t, tn = 512, 1024
    return pl.pallas_call(
        _proj_body,
        grid=(n // tn, b, l // t),
        in_specs=[
            pl.BlockSpec((None, t, d), lambda j, i, k: (i, k, 0)),
            pl.BlockSpec((d, tn), lambda j, i, k: (0, j)),
            pl.BlockSpec((1, tn), lambda j, i, k: (0, j)),
        ],
        out_specs=pl.BlockSpec((None, t, tn), lambda j, i, k: (i, k, j)),
        out_shape=jax.ShapeDtypeStruct((b, l, n), BF16),
        compiler_params=_params(("parallel", "parallel", "parallel")),
        name="proj",
    )(h, w, bias.reshape(1, n))


def _rope_lanes(x, cos, sin_signed):
    outs = []
    lane = lax.broadcasted_iota(jnp.int32, (x.shape[0], V7X_LANES), 1)
    first_half = (lane % HEAD_DIM) < (HEAD_DIM // 2)
    for c0 in range(0, x.shape[1], V7X_LANES):
        xc = x[:, c0:c0 + V7X_LANES]
        partner = jnp.where(first_half,
                            pltpu.roll(xc, V7X_LANES - HEAD_DIM // 2, 1),
                            pltpu.roll(xc, HEAD_DIM // 2, 1))
        outs.append(xc * cos[:, c0:c0 + V7X_LANES] + partner * sin_signed[:, c0:c0 + V7X_LANES])
    return jnp.concatenate(outs, 1)


def _proj_rope_body(h_ref, w_ref, b_ref, cos_ref, sin_ref, q_ref, k_ref, v_ref, z_ref):
    dc = q_ref.shape[-1]
    acc = jnp.dot(h_ref[...], w_ref[...], preferred_element_type=F32) + b_ref[...]
    cos, sin = cos_ref[...], sin_ref[...]
    q = _rope_lanes(acc[:, :dc], cos, sin) * (HEAD_DIM ** -0.5)
    k = _rope_lanes(acc[:, dc:2 * dc], cos, sin)
    q_ref[...] = q.astype(BF16)
    k_ref[...] = k.astype(BF16)
    v_ref[...] = acc[:, 2 * dc:3 * dc].astype(BF16)
    z_ref[...] = acc[:, 3 * dc:].astype(BF16)


def _proj_rope(h, w, bias, cos_t, sin_t):
    b, l, d = h.shape
    n = w.shape[1]
    dc = n // 4
    t = 512
    tok = pl.BlockSpec((None, t, dc), lambda i, k: (i, k, 0))
    shp = jax.ShapeDtypeStruct((b, l, dc), BF16)
    return pl.pallas_call(
        _proj_rope_body,
        grid=(b, l // t),
        in_specs=[
            pl.BlockSpec((None, t, d), lambda i, k: (i, k, 0)),
            pl.BlockSpec((d, n), lambda i, k: (0, 0)),
            pl.BlockSpec((1, n), lambda i, k: (0, 0)),
            pl.BlockSpec((t, dc), lambda i, k: (k, 0)),
            pl.BlockSpec((t, dc), lambda i, k: (k, 0)),
        ],
        out_specs=[tok, tok, tok, tok],
        out_shape=[shp, shp, shp, shp],
        compiler_params=_params(("parallel", "parallel")),
        name="proj_rope",
    )(h, w, bias.reshape(1, n), cos_t, sin_t)


def _hy_pre_body(main_ref, prev_ref, next_ref, cw_ref, cb_ref, v_ref, x1_ref, x2_ref, pad_ref, *, da):
    i = pl.program_id(1)
    last = pl.num_programs(1) - 1
    t = main_ref.shape[0]
    halo = prev_ref.shape[0]
    outs = (v_ref, x1_ref, x2_ref)
    az = main_ref[:, 3 * da:].astype(F32)
    gate = _silu(az)
    for part in range(3):
        cols = slice(part * da, (part + 1) * da)
        prev_row = jnp.where(i > 0, prev_ref[:, cols].astype(F32)[halo - 1:halo], 0.0)
        next_row = jnp.where(i < last, next_ref[:, cols].astype(F32)[0:1], 0.0)
        pad_ref[7:8, :] = prev_row
        pad_ref[8:8 + t, :] = main_ref[:, cols].astype(F32)
        pad_ref[8 + t:9 + t, :] = next_row
        uc = (pad_ref[7:7 + t, :] * cw_ref[0:1, cols] + pad_ref[8:8 + t, :] * cw_ref[1:2, cols]
              + pad_ref[9:9 + t, :] * cw_ref[2:3, cols] + cb_ref[:, cols])
        if part == 2:
            uc = uc * gate
        outs[part][...] = uc.astype(BF16)


def _hy_pre(proj_a, conv_w, conv_b):
    b, l, n = proj_a.shape
    da = n // 4
    t, halo = 512, 16
    nh = t // halo
    tok = pl.BlockSpec((None, t, da), lambda i, j: (i, j, 0))
    shp = jax.ShapeDtypeStruct((b, l, da), BF16)
    return pl.pallas_call(
        functools.partial(_hy_pre_body, da=da),
        grid=(b, l // t),
        in_specs=[
            pl.BlockSpec((None, t, n), lambda i, j: (i, j, 0)),
            pl.BlockSpec((None, halo, n), lambda i, j: (i, jnp.maximum(j * nh - 1, 0), 0)),
            pl.BlockSpec((None, halo, n), lambda i, j: (i, jnp.minimum((j + 1) * nh, l // halo - 1), 0)),
            pl.BlockSpec((3, 3 * da), lambda i, j: (0, 0)),
            pl.BlockSpec((1, 3 * da), lambda i, j: (0, 0)),
        ],
        out_specs=[tok, tok, tok],
        out_shape=[shp, shp, shp],
        scratch_shapes=[pltpu.VMEM((t + 16, da), F32)],
        compiler_params=_params(("parallel", "parallel")),
        name="hy_pre",
    )(proj_a, proj_a, proj_a, conv_w, conv_b.reshape(1, 3 * da))


def _filter_body(feat_ref, w1_ref, b1_ref, f0_ref, w2_ref, b2_ref, f1_ref, w3_ref, b3_ref, dec_ref,
                 hf_ref, sum_ref):
    i = pl.program_id(0)
    hp = lax.Precision.HIGHEST
    feat = feat_ref[...]
    t = feat[:, 0:1]
    h = jnp.sin(f0_ref[...] * (jnp.dot(feat, w1_ref[...], preferred_element_type=F32, precision=hp)
                               + b1_ref[...]))
    h = jnp.sin(f1_ref[...] * (jnp.dot(h, w2_ref[...], preferred_element_type=F32, precision=hp)
                               + b2_ref[...]))
    h = jnp.dot(h, w3_ref[...], preferred_element_type=F32, precision=hp) + b3_ref[...]
    h = h * jnp.exp(-t * jnp.abs(dec_ref[...]))
    rows = lax.broadcasted_iota(jnp.int32, h.shape, 0) + i * h.shape[0]
    cols = lax.broadcasted_iota(jnp.int32, h.shape, 1)
    h = jnp.where((rows == 0) & (cols >= h.shape[1] // 2), 0.0, h)
    hf_ref[...] = h.astype(BF16)

    @pl.when(i == 0)
    def _():
        sum_ref[...] = jnp.zeros_like(sum_ref)

    sum_ref[...] += jnp.sum(jnp.abs(h), 0, keepdims=True)


def _filter_taps(l, w1, b1, freq, w2, b2, w3, b3, decay):
    fo = w1.shape[1]
    n = w3.shape[1]
    da = decay.shape[0]
    t = jnp.arange(l, dtype=F32) / l
    bands = jnp.arange(1, HYENA_BANDS + 1, dtype=F32)
    ang = 2.0 * math.pi * t[:, None] * bands[None, :]
    feat = jnp.concatenate([t[:, None], jnp.cos(ang), jnp.sin(ang)], -1)
    feat = jnp.pad(feat, ((0, 0), (0, FEAT_PAD - HYENA_EMB)))
    w1p = jnp.pad(w1, ((0, FEAT_PAD - HYENA_EMB), (0, 0)))
    dec = jnp.tile(decay, n // da).reshape(1, n)
    tt = 512
    const = lambda shape: pl.BlockSpec(shape, lambda i: (0,) * len(shape))
    return pl.pallas_call(
        _filter_body,
        grid=(l // tt,),
        in_specs=[
            pl.BlockSpec((tt, FEAT_PAD), lambda i: (i, 0)),
            const((FEAT_PAD, fo)), const((1, fo)), const((1, fo)),
            const((fo, fo)), const((1, fo)), const((1, fo)),
            const((fo, n)), const((1, n)), const((1, n)),
        ],
        out_specs=[pl.BlockSpec((tt, n), lambda i: (i, 0)), const((1, n))],
        out_shape=[jax.ShapeDtypeStruct((l, n), BF16), jax.ShapeDtypeStruct((1, n), F32)],
        compiler_params=_params(("arbitrary",)),
        name="filter_taps",
    )(feat, w1p, b1.reshape(1, fo), freq[0].reshape(1, fo), w2, b2.reshape(1, fo),
      freq[1].reshape(1, fo), w3, b3.reshape(1, n), dec)


def _dft_tables(l):
    n = 2 * l
    n1 = n // DFT_N2
    kk = jnp.arange(n1, dtype=jnp.int32)
    nn = jnp.arange(n1 // 2, dtype=jnp.int32)
    th = (2.0 * math.pi / n1) * ((kk[:, None] * nn[None, :]) % n1).astype(F32)
    eye2 = jnp.eye(2, dtype=F32)
    f1 = jnp.kron(jnp.concatenate([jnp.cos(th), -jnp.sin(th)], 0), eye2).astype(BF16)
    g = jnp.kron(jnp.concatenate([jnp.cos(th).T, -jnp.sin(th).T], 1) / n, eye2).astype(BF16)
    k2 = jnp.arange(DFT_N2, dtype=jnp.int32)
    n2 = jnp.arange(DFT_N2, dtype=jnp.int32)
    ph = (n2[None, None, :] * (k2[None, :, None] * n1 + kk[:, None, None])) % n
    ang = (2.0 * math.pi / n) * ph.astype(F32)
    c, s = jnp.cos(ang), jnp.sin(ang)
    mf = jnp.concatenate([jnp.concatenate([c, s], 2), jnp.concatenate([-s, c], 2)], 1).astype(BF16)
    mi = jnp.swapaxes(mf, 1, 2)
    return f1, g, mf, mi


def _load_pair_group(ref, lead, g):
    start = pl.multiple_of(g * PAIR_GROUP, PAIR_GROUP)
    words = ref.bitcast(jnp.uint32)[(*lead, slice(None), pl.ds(start, PAIR_GROUP), slice(None))]
    words = jnp.swapaxes(words, 0, 1)
    return [pltpu.bitcast(words[i], BF16) for i in range(PAIR_GROUP)]


def _store_pair_group(ref, lead, g, vals):
    start = pl.multiple_of(g * PAIR_GROUP, PAIR_GROUP)
    words = jnp.stack([pltpu.bitcast(v, jnp.uint32) for v in vals], 0)
    ref.bitcast(jnp.uint32)[(*lead, slice(None), pl.ds(start, PAIR_GROUP), slice(None))] = (
        jnp.swapaxes(words, 0, 1))


def _dft_in_body(f_ref, z_ref, a_ref):
    n1 = a_ref.shape[2]
    cb = a_ref.shape[-1]

    def group(g, carry):
        zcat = jnp.concatenate(_load_pair_group(z_ref, (0,), g), 1)
        r = jnp.dot(f_ref[...], zcat, preferred_element_type=F32).astype(BF16)
        cols = [r[:, i * cb:(i + 1) * cb] for i in range(PAIR_GROUP)]
        _store_pair_group(a_ref, (0, 0), g, [c[:2 * n1] for c in cols])
        _store_pair_group(a_ref, (0, 1), g, [c[2 * n1:] for c in cols])
        return carry

    lax.fori_loop(0, z_ref.shape[2] // (2 * PAIR_GROUP), group, 0)


def _dft_in(f1p, z):
    b, half, n2, c = z.shape
    n1 = 2 * half
    cb, rs = V7X_LANES, DFT_STEP_ROWS
    return pl.pallas_call(
        _dft_in_body,
        grid=(b, c // cb, n2 // rs),
        in_specs=[
            pl.BlockSpec(f1p.shape, lambda i, j, s: (0, 0)),
            pl.BlockSpec((1, half, rs, cb), lambda i, j, s: (i, 0, s, j)),
        ],
        out_specs=pl.BlockSpec((1, 2, n1, rs, cb), lambda i, j, s: (i, 0, 0, s, j)),
        out_shape=jax.ShapeDtypeStruct((b, 2, n1, n2, c), BF16),
        compiler_params=_params(("parallel", "parallel", "parallel")),
        name="dft_in",
    )(f1p, z)


def _dft_out_body(g_ref, b_ref, x_ref, z_ref, skip_ref, o_ref):
    cb = o_ref.shape[-1]

    def group(g, carry):
        re, im = _load_pair_group(b_ref, (0, 0), g), _load_pair_group(b_ref, (0, 1), g)
        bcat = jnp.concatenate([jnp.concatenate([r, i], 0) for r, i in zip(re, im)], 1)
        y = jnp.dot(g_ref[...], bcat, preferred_element_type=F32)
        xs, zs = _load_pair_group(x_ref, (0,), g), _load_pair_group(z_ref, (0,), g)
        outs = [(xs[i].astype(F32) * (y[:, i * cb:(i + 1) * cb] + skip_ref[...] * zs[i].astype(F32))
                 ).astype(BF16) for i in range(PAIR_GROUP)]
        _store_pair_group(o_ref, (0,), g, outs)
        return carry

    lax.fori_loop(0, x_ref.shape[2] // (2 * PAIR_GROUP), group, 0)


def _dft_out(gp, bb, x, z, skip):
    b, half, n2, c = z.shape
    n1 = 2 * half
    cb, rs = V7X_LANES, DFT_STEP_ROWS
    slab = pl.BlockSpec((1, half, rs, cb), lambda i, j, s: (i, 0, s, j))
    return pl.pallas_call(
        _dft_out_body,
        grid=(b, c // cb, n2 // rs),
        in_specs=[
            pl.BlockSpec(gp.shape, lambda i, j, s: (0, 0)),
            pl.BlockSpec((1, 2, n1, rs, cb), lambda i, j, s: (i, 0, 0, s, j)),
            slab, slab,
            pl.BlockSpec((1, cb), lambda i, j, s: (0, j)),
        ],
        out_specs=slab,
        out_shape=jax.ShapeDtypeStruct(z.shape, BF16),
        compiler_params=_params(("parallel", "parallel", "parallel")),
        name="dft_out",
    )(gp, bb, x, z, skip.astype(F32).reshape(1, c))


def _slab_conv_body(mf_ref, mi_ref, k_ref, a_ref, o_ref):
    n2, c = a_ref.shape[1], a_ref.shape[2]
    x = jnp.dot(mf_ref[...], a_ref[...].reshape(2 * n2, c), preferred_element_type=F32)
    xr, xi = x[:n2], x[n2:]
    kr, ki = k_ref[0].astype(F32), k_ref[1].astype(F32)
    y = jnp.concatenate([xr * kr - xi * ki, xr * ki + xi * kr], 0).astype(BF16)
    out = jnp.dot(mi_ref[...], y, preferred_element_type=F32)
    o_ref[...] = out.astype(BF16).reshape(2, n2, c)


def _slab_conv(mf, mi, kspec, a, order):
    b, _, n1, n2, c = a.shape
    m = 2 * n2
    return pl.pallas_call(
        _slab_conv_body,
        grid=(n1, b),
        in_specs=[
            pl.BlockSpec((None, m, m), lambda k, i: (k, 0, 0)),
            pl.BlockSpec((None, m, m), lambda k, i: (k, 0, 0)),
            pl.BlockSpec((None, 2, n2, c), lambda k, i: (k, 0, 0, order)),
            pl.BlockSpec((None, 2, None, n2, c), lambda k, i: (i, 0, k, 0, 0)),
        ],
        out_specs=pl.BlockSpec((None, 2, None, n2, c), lambda k, i: (i, 0, k, 0, 0)),
        out_shape=jax.ShapeDtypeStruct(a.shape, BF16),
        compiler_params=_params(("parallel", "parallel")),
        name="slab_conv",
    )(mf, mi, kspec, a)


def _slab_spec_body(mf_ref, sum_ref, a_ref, k_ref):
    n2, c = a_ref.shape[1], a_ref.shape[2]
    half = c // 2
    x = jnp.dot(mf_ref[...], a_ref[...].reshape(2 * n2, c), preferred_element_type=F32)
    inv = 1.0 / (sum_ref[:, :half] + sum_ref[:, half:] + 1e-6)
    k_ref[0] = ((x[:n2, :half] + x[:n2, half:]) * inv).astype(BF16)
    k_ref[1] = ((x[n2:, :half] - x[n2:, half:]) * inv).astype(BF16)


def _slab_spec(mf, sums, a):
    _, _, n1, n2, c = a.shape
    m = 2 * n2
    return pl.pallas_call(
        _slab_spec_body,
        grid=(n1,),
        in_specs=[
            pl.BlockSpec((None, m, m), lambda k: (k, 0, 0)),
            pl.BlockSpec((1, c), lambda k: (0, 0)),
            pl.BlockSpec((None, 2, None, n2, c), lambda k: (0, 0, k, 0, 0)),
        ],
        out_specs=pl.BlockSpec((None, 2, n2, c // 2), lambda k: (k, 0, 0, 0)),
        out_shape=jax.ShapeDtypeStruct((n1, 2, n2, c // 2), BF16),
        compiler_params=_params(("parallel",)),
        name="slab_spec",
    )(mf, sums, a)


def _hyena_spectrum(l, tables, w1, b1, freq, w2, b2, w3, b3, decay):
    f1p, _, mf, _ = tables
    taps, sums = _filter_taps(l, w1, b1, freq, w2, b2, w3, b3, decay)
    a = _dft_in(f1p, taps.reshape(1, l // DFT_N2, DFT_N2, taps.shape[1]))
    return _slab_spec(mf, sums, a)


def _hyena(v, x1, x2g, kspec, skip, tables):
    f1p, gp, mf, mi = tables
    b, l, c = v.shape
    slabs = lambda u: u.reshape(b, l // DFT_N2, DFT_N2, c)
    z = slabs(v)
    for order, xg in enumerate((slabs(x1), slabs(x2g))):
        bb = _slab_conv(mf, mi, kspec, _dft_in(f1p, z), order)
        z = _dft_out(gp, bb, xg, z, skip[order])
    return z.reshape(b, l, c)


def _head_masks(rows, heads):
    lane = lax.broadcasted_iota(jnp.int32, (rows, heads * HEAD_DIM), 1)
    return [(lane >= h * HEAD_DIM) & (lane < (h + 1) * HEAD_DIM) for h in range(heads)]


def _stack_heads(q, masks):
    zero = jnp.zeros_like(q)
    return jnp.concatenate([jnp.where(m, q, zero) for m in masks], 0)


def _unstack_heads(res, masks, rows):
    out = jnp.where(masks[0], res[:rows], 0.0)
    for h in range(1, len(masks)):
        out = out + jnp.where(masks[h], res[h * rows:(h + 1) * rows], 0.0)
    return out


def _na_bias_table(rpb):
    heads = rpb.shape[0]
    c = jnp.arange(GRID_W)
    col_start = jnp.clip(c - NA_KC // 2, 0, GRID_W - NA_KC)
    col_ok = (c[None, :] >= col_start[:, None]) & (c[None, :] < col_start[:, None] + NA_KC)
    dc = jnp.clip(c[None, :] - c[:, None], -(NA_KC - 1), NA_KC - 1) + NA_KC - 1
    onehot = (dc[:, :, None] == jnp.arange(2 * NA_KC - 1)[None, None, :]).astype(F32)
    tcol = jnp.einsum("hrj,qkj->hrqk", rpb.astype(F32), onehot, precision=lax.Precision.HIGHEST)
    per_idx = [tcol[:, NA_KR - 1 - idx:2 * NA_KR - 1 - idx] for idx in range(NA_KR)]
    bias = jnp.transpose(jnp.stack(per_idx, 0), (0, 1, 3, 2, 4))
    bias = jnp.where(col_ok[None, None, :, None, :], bias, NEG_INF)
    return bias.reshape(NA_KR, heads * GRID_W, NA_KR * GRID_W)


def _na_body(q_ref, k_ref, v_ref, z_ref, bias_ref, o_ref, *, rows, rb, heads):
    blk = pl.program_id(1)
    masks = _head_masks(GRID_W, heads)
    span = NA_KR * GRID_W

    def one_row(rr, carry):
        r = blk * rb + rr
        start = jnp.clip(r - NA_KR // 2, 0, rows - NA_KR)
        idx = r - start
        koff = pl.multiple_of(start * GRID_W, GRID_W)
        qoff = pl.multiple_of(rr * GRID_W, GRID_W)
        q = q_ref[pl.ds(qoff, GRID_W), :] * jnp.asarray(HEAD_DIM ** -0.5, BF16)
        kk = k_ref[pl.ds(koff, span), :]
        vv = v_ref[pl.ds(koff, span), :]
        s = lax.dot_general(_stack_heads(q, masks), kk, (((1,), (1,)), ((), ())),
                            preferred_element_type=F32)
        s = s + bias_ref[idx]
        m = jnp.max(s, -1, keepdims=True)
        e = jnp.exp(s - m)
        p = (e / jnp.sum(e, -1, keepdims=True)).astype(BF16)
        o = _unstack_heads(jnp.dot(p, vv, preferred_element_type=F32), masks, GRID_W)
        z = z_ref[pl.ds(qoff, GRID_W), :].astype(F32)
        o_ref[pl.ds(qoff, GRID_W), :] = (o * _silu(z)).astype(BF16)
        return carry

    lax.fori_loop(0, rb, one_row, 0)


def _na(proj_b, bias):
    b, l, n = proj_b.shape
    db = n // 4
    heads = db // HEAD_DIM
    rows = l // GRID_W
    rb = 8
    t = rb * GRID_W
    return pl.pallas_call(
        functools.partial(_na_body, rows=rows, rb=rb, heads=heads),
        grid=(b, rows // rb),
        in_specs=[
            pl.BlockSpec((None, t, db), lambda i, j: (i, j, 0)),
            pl.BlockSpec((None, l, db), lambda i, j: (i, 0, 1)),
            pl.BlockSpec((None, l, db), lambda i, j: (i, 0, 2)),
            pl.BlockSpec((None, t, db), lambda i, j: (i, j, 3)),
            pl.BlockSpec(bias.shape, lambda i, j: (0, 0, 0)),
        ],
        out_specs=pl.BlockSpec((None, t, db), lambda i, j: (i, j, 0)),
        out_shape=jax.ShapeDtypeStruct((b, l, db), BF16),
        compiler_params=_params(("parallel", "parallel")),
        name="na",
    )(proj_b, proj_b, proj_b, proj_b, bias)


def _dil_body(q_ref, kp_ref, kc_ref, kn_ref, vp_ref, vc_ref, vn_ref, o_ref, lse_ref, *, heads):
    n = pl.program_id(2)
    nb = pl.num_programs(2)
    blk = DIL_BLK
    masks = _head_masks(blk, heads)
    kcat = jnp.concatenate([kp_ref[...], kc_ref[...], kn_ref[...]], 0)
    vcat = jnp.concatenate([vp_ref[...], vc_ref[...], vn_ref[...]], 0)
    s = lax.dot_general(_stack_heads(q_ref[...], masks), kcat, (((1,), (1,)), ((), ())),
                        preferred_element_type=F32)
    qi = lax.broadcasted_iota(jnp.int32, s.shape, 0) % blk
    ki = lax.broadcasted_iota(jnp.int32, s.shape, 1)
    off = ki - blk - qi
    valid = (jnp.abs(off) <= blk) & ((ki >= blk) | (n > 0)) & ((ki < 2 * blk) | (n < nb - 1))
    s = jnp.where(valid, s, NEG_INF)
    m = jnp.max(s, -1, keepdims=True)
    e = jnp.exp(s - m)
    l = jnp.sum(e, -1, keepdims=True)
    p = (e / l).astype(BF16)
    o = _unstack_heads(jnp.dot(p, vcat, preferred_element_type=F32), masks, blk)
    lse = jnp.broadcast_to(m + jnp.log(l), (heads * blk, heads * HEAD_DIM))
    o_ref[...] = o.astype(BF16)
    lse_ref[...] = _unstack_heads(lse, masks, blk)


def _dilated_pattern(q, k, v, dilation):
    b, l, dc = q.shape
    heads = dc // HEAD_DIM
    ld = l // dilation
    nb = ld // DIL_BLK
    view = lambda u: u.reshape(b, ld, dilation * dc)
    blk = lambda f: pl.BlockSpec((None, DIL_BLK, dc), f)
    cur = lambda i, j, n: (i, n, j)
    prev = lambda i, j, n: (i, jnp.maximum(n - 1, 0), j)
    nxt = lambda i, j, n: (i, jnp.minimum(n + 1, nb - 1), j)
    o, lse = pl.pallas_call(
        functools.partial(_dil_body, heads=heads),
        grid=(b, dilation, nb),
        in_specs=[blk(cur), blk(prev), blk(cur), blk(nxt), blk(prev), blk(cur), blk(nxt)],
        out_specs=[blk(cur), blk(cur)],
        out_shape=[jax.ShapeDtypeStruct((b, ld, dilation * dc), BF16),
                   jax.ShapeDtypeStruct((b, ld, dilation * dc), F32)],
        compiler_params=_params(("parallel", "parallel", "parallel")),
        name=f"dil_d{dilation}",
    )(view(q), view(k), view(k), view(k), view(v), view(v), view(v))
    return o.reshape(b, l, dc), lse.reshape(b, l, dc)


def _tail_body(x_ref, ya_ref, yb_ref, o1_ref, o2_ref, o3_ref, l1_ref, l2_ref, l3_ref, cz_ref, g_ref,
               gate_ref, wa_ref, wb_ref, wc_ref, wo_ref, lng_ref, lnb_ref, out_ref, *, alpha):
    d = x_ref.shape[-1]
    la, lb, lc = l1_ref[...], l2_ref[...], l3_ref[...]
    m = jnp.maximum(jnp.maximum(la, lb), lc)
    ea, eb, ec = jnp.exp(la - m), jnp.exp(lb - m), jnp.exp(lc - m)
    den = ea + eb + ec
    o = ((ea / den) * o1_ref[...].astype(F32) + (eb / den) * o2_ref[...].astype(F32)
         + (ec / den) * o3_ref[...].astype(F32))
    yc = (o * _silu(cz_ref[...].astype(F32))).astype(BF16)
    pa = jnp.dot(ya_ref[...], wa_ref[...], preferred_element_type=F32)
    pb = jnp.dot(yb_ref[...], wb_ref[...], preferred_element_type=F32)
    pc = jnp.dot(yc, wc_ref[...], preferred_element_type=F32)
    g = _sigmoid(g_ref[...].astype(F32))
    merged = g[:, :d] * pa + g[:, d:2 * d] * pb + g[:, 2 * d:] * pc
    sub = jnp.dot(merged.astype(BF16), wo_ref[...], preferred_element_type=F32) * gate_ref[...]
    res = alpha * x_ref[...] + sub
    out_ref[...] = _layernorm(res) * lng_ref[...] + lnb_ref[...]


def _tail(x, ya, yb, dil, cz, g_all, gate, wa, wb, wc, wo, ln_g, ln_b, alpha):
    b, l, d = x.shape
    t = 256
    tok = lambda w: pl.BlockSpec((None, t, w), lambda i, j: (i, j, 0))
    const = lambda a: pl.BlockSpec(a.shape, lambda i, j: (0,) * a.ndim)
    (o1, l1), (o2, l2), (o3, l3) = dil
    dc = cz.shape[-1]
    ln_g, ln_b = ln_g.reshape(1, d), ln_b.reshape(1, d)
    return pl.pallas_call(
        functools.partial(_tail_body, alpha=alpha),
        grid=(b, l // t),
        in_specs=[tok(d), tok(ya.shape[-1]), tok(yb.shape[-1]), tok(dc), tok(dc), tok(dc),
                  tok(dc), tok(dc), tok(dc), tok(dc), tok(3 * d),
                  pl.BlockSpec((None, 1, d), lambda i, j: (i, 0, 0)),
                  const(wa), const(wb), const(wc), const(wo), const(ln_g), const(ln_b)],
        out_specs=tok(d),
        out_shape=jax.ShapeDtypeStruct((b, l, d), F32),
        compiler_params=_params(("parallel", "parallel")),
        name="tail",
    )(x, ya, yb, o1, o2, o3, l1, l2, l3, cz, g_all, gate.reshape(b, 1, d), wa, wb, wc, wo, ln_g, ln_b)


def _rope_tables(l, heads):
    half = HEAD_DIM // 2
    inv = ROPE_THETA ** (-jnp.arange(half, dtype=F32) / half)
    ang = jnp.arange(l, dtype=F32)[:, None] * inv[None, :]
    cos, sin = jnp.cos(ang), jnp.sin(ang)
    return (jnp.tile(jnp.concatenate([cos, cos], -1), (1, heads)),
            jnp.tile(jnp.concatenate([-sin, sin], -1), (1, heads)))


def _layer(x, ada, lw, consts, alpha):
    d = x.shape[-1]
    shift, scale, gate = ada[:, :d], ada[:, d:2 * d], ada[:, 2 * d:]
    h = _ln_mod(x, scale, shift)
    w_in, b_in = lw["w_in"], lw["b_in"]
    proj_a = _proj(h, w_in[:, :2 * d], b_in[:2 * d])
    proj_b = _proj(h, w_in[:, 2 * d:3 * d], b_in[2 * d:3 * d])
    cq, ck, cv, cz = _proj_rope(h, w_in[:, 3 * d:4 * d], b_in[3 * d:4 * d], *consts["rope"])
    g_all = _proj(h, w_in[:, 4 * d:], b_in[4 * d:])

    v, x1, x2g = _hy_pre(proj_a, lw["conv_w"], lw["conv_b"])
    ya = _hyena(v, x1, x2g, consts["kspec"], lw["skip"], consts["dft"])
    yb = _na(proj_b, lw["na_bias"])
    dil = [_dilated_pattern(cq, ck, cv, dil_d) for _, dil_d in DIL_PATTERNS]
    return _tail(x, ya, yb, dil, cz, g_all, gate, lw["wa"], lw["wb"], lw["wc"], lw["wo"],
                 lw["ln_g"], lw["ln_b"], alpha)


def kernel(x_prompt, x_sample, c_prompt, c_sample, w_ada, b_ada, w_in, b_in, hy_conv_w, hy_conv_b, hy_w1, hy_b1, hy_freq, hy_w2, hy_b2, hy_w3, hy_b3, hy_decay, hy_skip, na_rpb, w_branch_a, w_branch_b, w_branch_c, w_out, ln_g, ln_b):
    depth, d, _ = w_in.shape
    heads_c = (d // 4) // HEAD_DIM
    alpha = (2 * depth) ** 0.25
    groups = [(x_prompt, c_prompt), (x_sample, c_sample)]

    nb_p = c_prompt.shape[0]
    c_all = jnp.concatenate([c_prompt, c_sample], 0)
    pad_rows = -c_all.shape[0] % 8
    ada_all = _ada(jnp.pad(c_all, ((0, pad_rows), (0, 0))), w_ada, b_ada)
    adas = [ada_all[:, :nb_p], ada_all[:, nb_p:nb_p + c_sample.shape[0]]]

    shared = {}
    for x, _ in groups:
        l = x.shape[1]
        if l not in shared:
            shared[l] = {"dft": _dft_tables(l), "rope": _rope_tables(l, heads_c)}

    ys = [x for x, _ in groups]
    for layer in range(depth):
        lw = {
            "w_in": w_in[layer].astype(BF16), "b_in": b_in[layer],
            "conv_w": hy_conv_w[layer], "conv_b": hy_conv_b[layer], "skip": hy_skip[layer],
            "na_bias": _na_bias_table(na_rpb[layer]),
            "wa": w_branch_a[layer].astype(BF16), "wb": w_branch_b[layer].astype(BF16),
            "wc": w_branch_c[layer].astype(BF16), "wo": w_out[layer].astype(BF16),
            "ln_g": ln_g[layer], "ln_b": ln_b[layer],
        }
        kspecs = {}
        for gi in range(len(groups)):
            l = ys[gi].shape[1]
            if l not in kspecs:
                kspecs[l] = _hyena_spectrum(l, shared[l]["dft"], hy_w1[layer], hy_b1[layer], hy_freq[layer],
                                            hy_w2[layer], hy_b2[layer], hy_w3[layer], hy_b3[layer],
                                            hy_decay[layer])
            consts = dict(shared[l], kspec=kspecs[l])
            ys[gi] = _layer(ys[gi], adas[gi][layer], lw, consts, alpha)
    return tuple(ys)
```

```python
import functools
import math

import jax
import jax.numpy as jnp
from jax import lax
from jax.experimental import pallas as pl
from jax.experimental.pallas import tpu as pltpu

F32 = jnp.float32
BF16 = jnp.bfloat16

GRID_W = 64
HEAD_DIM = 64
HYENA_BANDS = 16
HYENA_EMB = 2 * HYENA_BANDS + 1
NA_KR = 8
NA_KC = 16
DIL_PATTERNS = ((128, 1), (512, 4), (2048, 16))
DIL_BLK = 64
DIL_BLOCKS_PER_STEP = 8
ROPE_THETA = 10000.0
LN_EPS = 1e-5
NEG_INF = -1e30

V7X_LANES = 128
V7X_SUBLANES = 8
V7X_VMEM_LIMIT_BYTES = 56 * 1024 * 1024

DFT_N2 = V7X_LANES
PAIR_GROUP = V7X_SUBLANES
DFT_STEP_ROWS = 64
SLABS_PER_STEP = 4
FEAT_PAD = V7X_LANES


def _params(sem):
    return pltpu.CompilerParams(dimension_semantics=sem, vmem_limit_bytes=V7X_VMEM_LIMIT_BYTES)


def _sigmoid(x):
    return 1.0 / (1.0 + jnp.exp(-x))


def _silu(x):
    return x * _sigmoid(x)


def _ada_body(c_ref, w_ref, b_ref, o_ref):
    s = _silu(c_ref[...])
    o_ref[...] = jnp.dot(s, w_ref[...], preferred_element_type=F32,
                         precision=lax.Precision.HIGHEST) + b_ref[...]


def _ada(c_all, w_ada, b_ada):
    depth, d, n = w_ada.shape
    rows = c_all.shape[0]
    tn = 1024
    return pl.pallas_call(
        _ada_body,
        grid=(depth, n // tn),
        in_specs=[
            pl.BlockSpec((rows, d), lambda l, j: (0, 0)),
            pl.BlockSpec((None, d, tn), lambda l, j: (l, 0, j)),
            pl.BlockSpec((None, 1, tn), lambda l, j: (l, 0, j)),
        ],
        out_specs=pl.BlockSpec((None, rows, tn), lambda l, j: (l, 0, j)),
        out_shape=jax.ShapeDtypeStruct((depth, rows, n), F32),
        compiler_params=_params(("parallel", "parallel")),
        name="ada",
    )(c_all, w_ada, b_ada.reshape(depth, 1, n))


def _layernorm(x):
    mu = jnp.mean(x, -1, keepdims=True)
    xc = x - mu
    var = jnp.mean(xc * xc, -1, keepdims=True)
    return xc * lax.rsqrt(var + LN_EPS)


def _ln_mod_body(x_ref, sc_ref, sh_ref, o_ref):
    h = _layernorm(x_ref[...]) * (1.0 + sc_ref[...]) + sh_ref[...]
    o_ref[...] = h.astype(BF16)


def _ln_mod(x, scale, shift):
    b, l, d = x.shape
    t = 512
    return pl.pallas_call(
        _ln_mod_body,
        grid=(b, l // t),
        in_specs=[
            pl.BlockSpec((None, t, d), lambda i, j: (i, j, 0)),
            pl.BlockSpec((None, 1, d), lambda i, j: (i, 0, 0)),
            pl.BlockSpec((None, 1, d), lambda i, j: (i, 0, 0)),
        ],
        out_specs=pl.BlockSpec((None, t, d), lambda i, j: (i, j, 0)),
        out_shape=jax.ShapeDtypeStruct((b, l, d), BF16),
        compiler_params=_params(("parallel", "parallel")),
        name="ln_mod",
    )(x, scale.reshape(b, 1, d), shift.reshape(b, 1, d))


def _proj_body(h_ref, w_ref, b_ref, o_ref):
    acc = jnp.dot(h_ref[...], w_ref[...], preferred_element_type=F32) + b_ref[...]
    o_ref[...] = acc.astype(o_ref.dtype)


def _proj(h, w, bias):
    b, l, d = h.shape
    n = w.shape[1]
    t, tn = 512, 1024
    return pl.pallas_call(
        _proj_body,
        grid=(n // tn, b, l // t),
        in_specs=[
            pl.BlockSpec((None, t, d), lambda j, i, k: (i, k, 0)),
            pl.BlockSpec((d, tn), lambda j, i, k: (0, j)),
            pl.BlockSpec((1, tn), lambda j, i, k: (0, j)),
        ],
        out_specs=pl.BlockSpec((None, t, tn), lambda j, i, k: (i, k, j)),
        out_shape=jax.ShapeDtypeStruct((b, l, n), BF16),
        compiler_params=_params(("parallel", "parallel", "parallel")),
        name="proj",
    )(h, w, bias.reshape(1, n))


def _rope_lanes(x, cos, sin_signed):
    outs = []
    lane = lax.broadcasted_iota(jnp.int32, (x.shape[0], V7X_LANES), 1)
    first_half = (lane % HEAD_DIM) < (HEAD_DIM // 2)
    for c0 in range(0, x.shape[1], V7X_LANES):
        xc = x[:, c0:c0 + V7X_LANES]
        partner = jnp.where(first_half,
                            pltpu.roll(xc, V7X_LANES - HEAD_DIM // 2, 1),
                            pltpu.roll(xc, HEAD_DIM // 2, 1))
        outs.append(xc * cos[:, c0:c0 + V7X_LANES] + partner * sin_signed[:, c0:c0 + V7X_LANES])
    return jnp.concatenate(outs, 1)


def _proj_rope_body(h_ref, w_ref, b_ref, cos_ref, sin_ref, q_ref, k_ref, v_ref, z_ref):
    dc = q_ref.shape[-1]
    acc = jnp.dot(h_ref[...], w_ref[...], preferred_element_type=F32) + b_ref[...]
    cos, sin = cos_ref[...], sin_ref[...]
    q = _rope_lanes(acc[:, :dc], cos, sin) * (HEAD_DIM ** -0.5)
    k = _rope_lanes(acc[:, dc:2 * dc], cos, sin)
    q_ref[...] = q.astype(BF16)
    k_ref[...] = k.astype(BF16)
    v_ref[...] = acc[:, 2 * dc:3 * dc].astype(BF16)
    z_ref[...] = acc[:, 3 * dc:].astype(BF16)


def _proj_rope(h, w, bias, cos_t, sin_t):
    b, l, d = h.shape
    n = w.shape[1]
    dc = n // 4
    t = 512
    tok = pl.BlockSpec((None, t, dc), lambda i, k: (i, k, 0))
    shp = jax.ShapeDtypeStruct((b, l, dc), BF16)
    return pl.pallas_call(
        _proj_rope_body,
        grid=(b, l // t),
        in_specs=[
            pl.BlockSpec((None, t, d), lambda i, k: (i, k, 0)),
            pl.BlockSpec((d, n), lambda i, k: (0, 0)),
            pl.BlockSpec((1, n), lambda i, k: (0, 0)),
            pl.BlockSpec((t, dc), lambda i, k: (k, 0)),
            pl.BlockSpec((t, dc), lambda i, k: (k, 0)),
        ],
        out_specs=[tok, tok, tok, tok],
        out_shape=[shp, shp, shp, shp],
        compiler_params=_params(("parallel", "parallel")),
        name="proj_rope",
    )(h, w, bias.reshape(1, n), cos_t, sin_t)


def _hy_pre_body(main_ref, prev_ref, next_ref, cw_ref, cb_ref, v_ref, x1_ref, x2_ref, pad_ref, *, da):
    i = pl.program_id(1)
    last = pl.num_programs(1) - 1
    t = main_ref.shape[0]
    halo = prev_ref.shape[0]
    outs = (v_ref, x1_ref, x2_ref)
    az = main_ref[:, 3 * da:].astype(F32)
    gate = _silu(az)
    for part in range(3):
        cols = slice(part * da, (part + 1) * da)
        prev_row = jnp.where(i > 0, prev_ref[:, cols].astype(F32)[halo - 1:halo], 0.0)
        next_row = jnp.where(i < last, next_ref[:, cols].astype(F32)[0:1], 0.0)
        pad_ref[7:8, :] = prev_row
        pad_ref[8:8 + t, :] = main_ref[:, cols].astype(F32)
        pad_ref[8 + t:9 + t, :] = next_row
        uc = (pad_ref[7:7 + t, :] * cw_ref[0:1, cols] + pad_ref[8:8 + t, :] * cw_ref[1:2, cols]
              + pad_ref[9:9 + t, :] * cw_ref[2:3, cols] + cb_ref[:, cols])
        if part == 2:
            uc = uc * gate
        outs[part][...] = uc.astype(BF16)


def _hy_pre(proj_a, conv_w, conv_b):
    b, l, n = proj_a.shape
    da = n // 4
    t, halo = 512, 16
    nh = t // halo
    tok = pl.BlockSpec((None, t, da), lambda i, j: (i, j, 0))
    shp = jax.ShapeDtypeStruct((b, l, da), BF16)
    return pl.pallas_call(
        functools.partial(_hy_pre_body, da=da),
        grid=(b, l // t),
        in_specs=[
            pl.BlockSpec((None, t, n), lambda i, j: (i, j, 0)),
            pl.BlockSpec((None, halo, n), lambda i, j: (i, jnp.maximum(j * nh - 1, 0), 0)),
            pl.BlockSpec((None, halo, n), lambda i, j: (i, jnp.minimum((j + 1) * nh, l // halo - 1), 0)),
            pl.BlockSpec((3, 3 * da), lambda i, j: (0, 0)),
            pl.BlockSpec((1, 3 * da), lambda i, j: (0, 0)),
        ],
        out_specs=[tok, tok, tok],
        out_shape=[shp, shp, shp],
        scratch_shapes=[pltpu.VMEM((t + 16, da), F32)],
        compiler_params=_params(("parallel", "parallel")),
        name="hy_pre",
    )(proj_a, proj_a, proj_a, conv_w, conv_b.reshape(1, 3 * da))


def _filter_body(feat_ref, w1_ref, b1_ref, f0_ref, w2_ref, b2_ref, f1_ref, w3_ref, b3_ref, dec_ref,
                 hf_ref, sum_ref):
    i = pl.program_id(0)
    hp = lax.Precision.HIGHEST
    feat = feat_ref[...]
    t = feat[:, 0:1]
    h = jnp.sin(f0_ref[...] * (jnp.dot(feat, w1_ref[...], preferred_element_type=F32, precision=hp)
                               + b1_ref[...]))
    h = jnp.sin(f1_ref[...] * (jnp.dot(h, w2_ref[...], preferred_element_type=F32, precision=hp)
                               + b2_ref[...]))
    h = jnp.dot(h, w3_ref[...], preferred_element_type=F32, precision=hp) + b3_ref[...]
    h = h * jnp.exp(-t * jnp.abs(dec_ref[...]))
    rows = lax.broadcasted_iota(jnp.int32, h.shape, 0) + i * h.shape[0]
    cols = lax.broadcasted_iota(jnp.int32, h.shape, 1)
    h = jnp.where((rows == 0) & (cols >= h.shape[1] // 2), 0.0, h)
    hf_ref[...] = h.astype(BF16)

    @pl.when(i == 0)
    def _():
        sum_ref[...] = jnp.zeros_like(sum_ref)

    sum_ref[...] += jnp.sum(jnp.abs(h), 0, keepdims=True)


def _filter_taps(l, w1, b1, freq, w2, b2, w3, b3, decay):
    fo = w1.shape[1]
    n = w3.shape[1]
    da = decay.shape[0]
    t = jnp.arange(l, dtype=F32) / l
    bands = jnp.arange(1, HYENA_BANDS + 1, dtype=F32)
    ang = 2.0 * math.pi * t[:, None] * bands[None, :]
    feat = jnp.concatenate([t[:, None], jnp.cos(ang), jnp.sin(ang)], -1)
    feat = jnp.pad(feat, ((0, 0), (0, FEAT_PAD - HYENA_EMB)))
    w1p = jnp.pad(w1, ((0, FEAT_PAD - HYENA_EMB), (0, 0)))
    dec = jnp.tile(decay, n // da).reshape(1, n)
    tt = 512
    const = lambda shape: pl.BlockSpec(shape, lambda i: (0,) * len(shape))
    return pl.pallas_call(
        _filter_body,
        grid=(l // tt,),
        in_specs=[
            pl.BlockSpec((tt, FEAT_PAD), lambda i: (i, 0)),
            const((FEAT_PAD, fo)), const((1, fo)), const((1, fo)),
            const((fo, fo)), const((1, fo)), const((1, fo)),
            const((fo, n)), const((1, n)), const((1, n)),
        ],
        out_specs=[pl.BlockSpec((tt, n), lambda i: (i, 0)), const((1, n))],
        out_shape=[jax.ShapeDtypeStruct((l, n), BF16), jax.ShapeDtypeStruct((1, n), F32)],
        compiler_params=_params(("arbitrary",)),
        name="filter_taps",
    )(feat, w1p, b1.reshape(1, fo), freq[0].reshape(1, fo), w2, b2.reshape(1, fo),
      freq[1].reshape(1, fo), w3, b3.reshape(1, n), dec)


def _dft_tables(l):
    n = 2 * l
    n1 = n // DFT_N2
    kk = jnp.arange(n1, dtype=jnp.int32)
    nn = jnp.arange(n1 // 2, dtype=jnp.int32)
    th = (2.0 * math.pi / n1) * ((kk[:, None] * nn[None, :]) % n1).astype(F32)
    eye2 = jnp.eye(2, dtype=F32)
    f1 = jnp.kron(jnp.concatenate([jnp.cos(th), -jnp.sin(th)], 0), eye2).astype(BF16)
    g = jnp.kron(jnp.concatenate([jnp.cos(th).T, -jnp.sin(th).T], 1) / n, eye2).astype(BF16)
    k2 = jnp.arange(DFT_N2, dtype=jnp.int32)
    n2 = jnp.arange(DFT_N2, dtype=jnp.int32)
    ph = (n2[None, None, :] * (k2[None, :, None] * n1 + kk[:, None, None])) % n
    ang = (2.0 * math.pi / n) * ph.astype(F32)
    c, s = jnp.cos(ang), jnp.sin(ang)
    mf = jnp.concatenate([jnp.concatenate([c, s], 2), jnp.concatenate([-s, c], 2)], 1).astype(BF16)
    mi = jnp.swapaxes(mf, 1, 2)
    return f1, g, mf, mi


def _load_pair_group(ref, lead, g):
    start = pl.multiple_of(g * PAIR_GROUP, PAIR_GROUP)
    words = ref.bitcast(jnp.uint32)[(*lead, slice(None), pl.ds(start, PAIR_GROUP), slice(None))]
    words = jnp.swapaxes(words, 0, 1)
    return [pltpu.bitcast(words[i], BF16) for i in range(PAIR_GROUP)]


def _store_pair_group(ref, lead, g, vals):
    start = pl.multiple_of(g * PAIR_GROUP, PAIR_GROUP)
    words = jnp.stack([pltpu.bitcast(v, jnp.uint32) for v in vals], 0)
    ref.bitcast(jnp.uint32)[(*lead, slice(None), pl.ds(start, PAIR_GROUP), slice(None))] = (
        jnp.swapaxes(words, 0, 1))


def _dft_in_body(f_ref, z_ref, a_ref):
    n1 = a_ref.shape[2]
    cb = a_ref.shape[-1]

    def group(g, carry):
        zcat = jnp.concatenate(_load_pair_group(z_ref, (0,), g), 1)
        r = jnp.dot(f_ref[...], zcat, preferred_element_type=F32).astype(BF16)
        cols = [r[:, i * cb:(i + 1) * cb] for i in range(PAIR_GROUP)]
        _store_pair_group(a_ref, (0, 0), g, [c[:2 * n1] for c in cols])
        _store_pair_group(a_ref, (0, 1), g, [c[2 * n1:] for c in cols])
        return carry

    lax.fori_loop(0, z_ref.shape[2] // (2 * PAIR_GROUP), group, 0)


def _dft_in(f1p, z):
    b, half, n2, c = z.shape
    n1 = 2 * half
    cb, rs = V7X_LANES, DFT_STEP_ROWS
    return pl.pallas_call(
        _dft_in_body,
        grid=(b, c // cb, n2 // rs),
        in_specs=[
            pl.BlockSpec(f1p.shape, lambda i, j, s: (0, 0)),
            pl.BlockSpec((1, half, rs, cb), lambda i, j, s: (i, 0, s, j)),
        ],
        out_specs=pl.BlockSpec((1, 2, n1, rs, cb), lambda i, j, s: (i, 0, 0, s, j)),
        out_shape=jax.ShapeDtypeStruct((b, 2, n1, n2, c), BF16),
        compiler_params=_params(("parallel", "parallel", "parallel")),
        name="dft_in",
    )(f1p, z)


def _dft_out_body(g_ref, b_ref, x_ref, z_ref, skip_ref, o_ref):
    cb = o_ref.shape[-1]

    def group(g, carry):
        re, im = _load_pair_group(b_ref, (0, 0), g), _load_pair_group(b_ref, (0, 1), g)
        bcat = jnp.concatenate([jnp.concatenate([r, i], 0) for r, i in zip(re, im)], 1)
        y = jnp.dot(g_ref[...], bcat, preferred_element_type=F32)
        xs, zs = _load_pair_group(x_ref, (0,), g), _load_pair_group(z_ref, (0,), g)
        outs = [(xs[i].astype(F32) * (y[:, i * cb:(i + 1) * cb] + skip_ref[...] * zs[i].astype(F32))
                 ).astype(BF16) for i in range(PAIR_GROUP)]
        _store_pair_group(o_ref, (0,), g, outs)
        return carry

    lax.fori_loop(0, x_ref.shape[2] // (2 * PAIR_GROUP), group, 0)


def _dft_out(gp, bb, x, z, skip):
    b, half, n2, c = z.shape
    n1 = 2 * half
    cb, rs = V7X_LANES, DFT_STEP_ROWS
    slab = pl.BlockSpec((1, half, rs, cb), lambda i, j, s: (i, 0, s, j))
    return pl.pallas_call(
        _dft_out_body,
        grid=(b, c // cb, n2 // rs),
        in_specs=[
            pl.BlockSpec(gp.shape, lambda i, j, s: (0, 0)),
            pl.BlockSpec((1, 2, n1, rs, cb), lambda i, j, s: (i, 0, 0, s, j)),
            slab, slab,
            pl.BlockSpec((1, cb), lambda i, j, s: (0, j)),
        ],
        out_specs=slab,
        out_shape=jax.ShapeDtypeStruct(z.shape, BF16),
        compiler_params=_params(("parallel", "parallel", "parallel")),
        name="dft_out",
    )(gp, bb, x, z, skip.astype(F32).reshape(1, c))


def _slab_conv_body(mf_ref, mi_ref, k_ref, a_ref, o_ref):
    nb, _, ks, n2, c = a_ref.shape
    for kk in range(ks):
        kr, ki = k_ref[kk, 0].astype(F32), k_ref[kk, 1].astype(F32)
        for b in range(nb):
            a = a_ref[b, :, kk].reshape(2 * n2, c)
            x = jnp.dot(mf_ref[kk], a, preferred_element_type=F32)
            xr, xi = x[:n2], x[n2:]
            y = jnp.concatenate([xr * kr - xi * ki, xr * ki + xi * kr], 0).astype(BF16)
            out = jnp.dot(mi_ref[kk], y, preferred_element_type=F32)
            o_ref[b, :, kk] = out.astype(BF16).reshape(2, n2, c)


def _slab_conv(mf, mi, kspec, a, order):
    b, _, n1, n2, c = a.shape
    m = 2 * n2
    ks = max(1, SLABS_PER_STEP // b)
    return pl.pallas_call(
        _slab_conv_body,
        grid=(n1 // ks,),
        in_specs=[
            pl.BlockSpec((ks, m, m), lambda k: (k, 0, 0)),
            pl.BlockSpec((ks, m, m), lambda k: (k, 0, 0)),
            pl.BlockSpec((ks, 2, n2, c), lambda k: (k, 0, 0, order)),
            pl.BlockSpec((b, 2, ks, n2, c), lambda k: (0, 0, k, 0, 0)),
        ],
        out_specs=pl.BlockSpec((b, 2, ks, n2, c), lambda k: (0, 0, k, 0, 0)),
        out_shape=jax.ShapeDtypeStruct(a.shape, BF16),
        compiler_params=_params(("parallel",)),
        name="slab_conv",
    )(mf, mi, kspec, a)


def _slab_spec_body(mf_ref, sum_ref, a_ref, k_ref):
    n2, c = a_ref.shape[1], a_ref.shape[2]
    half = c // 2
    x = jnp.dot(mf_ref[...], a_ref[...].reshape(2 * n2, c), preferred_element_type=F32)
    inv = 1.0 / (sum_ref[:, :half] + sum_ref[:, half:] + 1e-6)
    k_ref[0] = ((x[:n2, :half] + x[:n2, half:]) * inv).astype(BF16)
    k_ref[1] = ((x[n2:, :half] - x[n2:, half:]) * inv).astype(BF16)


def _slab_spec(mf, sums, a):
    _, _, n1, n2, c = a.shape
    m = 2 * n2
    return pl.pallas_call(
        _slab_spec_body,
        grid=(n1,),
        in_specs=[
            pl.BlockSpec((None, m, m), lambda k: (k, 0, 0)),
            pl.BlockSpec((1, c), lambda k: (0, 0)),
            pl.BlockSpec((None, 2, None, n2, c), lambda k: (0, 0, k, 0, 0)),
        ],
        out_specs=pl.BlockSpec((None, 2, n2, c // 2), lambda k: (k, 0, 0, 0)),
        out_shape=jax.ShapeDtypeStruct((n1, 2, n2, c // 2), BF16),
        compiler_params=_params(("parallel",)),
        name="slab_spec",
    )(mf, sums, a)


def _hyena_spectrum(l, tables, w1, b1, freq, w2, b2, w3, b3, decay):
    f1p, _, mf, _ = tables
    taps, sums = _filter_taps(l, w1, b1, freq, w2, b2, w3, b3, decay)
    a = _dft_in(f1p, taps.reshape(1, l // DFT_N2, DFT_N2, taps.shape[1]))
    return _slab_spec(mf, sums, a)


def _hyena(v, x1, x2g, kspec, skip, tables):
    f1p, gp, mf, mi = tables
    b, l, c = v.shape
    slabs = lambda u: u.reshape(b, l // DFT_N2, DFT_N2, c)
    z = slabs(v)
    for order, xg in enumerate((slabs(x1), slabs(x2g))):
        bb = _slab_conv(mf, mi, kspec, _dft_in(f1p, z), order)
        z = _dft_out(gp, bb, xg, z, skip[order])
    return z.reshape(b, l, c)


def _head_masks(rows, heads):
    lane = lax.broadcasted_iota(jnp.int32, (rows, heads * HEAD_DIM), 1)
    return [(lane >= h * HEAD_DIM) & (lane < (h + 1) * HEAD_DIM) for h in range(heads)]


def _stack_heads(q, masks):
    zero = jnp.zeros_like(q)
    return jnp.concatenate([jnp.where(m, q, zero) for m in masks], 0)


def _unstack_heads(res, masks, rows):
    out = jnp.where(masks[0], res[:rows], 0.0)
    for h in range(1, len(masks)):
        out = out + jnp.where(masks[h], res[h * rows:(h + 1) * rows], 0.0)
    return out


def _na_bias_table(rpb):
    heads = rpb.shape[0]
    c = jnp.arange(GRID_W)
    col_start = jnp.clip(c - NA_KC // 2, 0, GRID_W - NA_KC)
    col_ok = (c[None, :] >= col_start[:, None]) & (c[None, :] < col_start[:, None] + NA_KC)
    dc = jnp.clip(c[None, :] - c[:, None], -(NA_KC - 1), NA_KC - 1) + NA_KC - 1
    onehot = (dc[:, :, None] == jnp.arange(2 * NA_KC - 1)[None, None, :]).astype(F32)
    tcol = jnp.einsum("hrj,qkj->hrqk", rpb.astype(F32), onehot, precision=lax.Precision.HIGHEST)
    per_idx = [tcol[:, NA_KR - 1 - idx:2 * NA_KR - 1 - idx] for idx in range(NA_KR)]
    bias = jnp.transpose(jnp.stack(per_idx, 0), (0, 1, 3, 2, 4))
    bias = jnp.where(col_ok[None, None, :, None, :], bias, NEG_INF)
    return bias.reshape(NA_KR, heads * GRID_W, NA_KR * GRID_W)


def _na_body(q_ref, k_ref, v_ref, z_ref, bias_ref, o_ref, *, rows, rb, heads):
    blk = pl.program_id(1)
    masks = _head_masks(GRID_W, heads)
    span = NA_KR * GRID_W

    def one_row(rr, carry):
        r = blk * rb + rr
        start = jnp.clip(r - NA_KR // 2, 0, rows - NA_KR)
        idx = r - start
        koff = pl.multiple_of(start * GRID_W, GRID_W)
        qoff = pl.multiple_of(rr * GRID_W, GRID_W)
        q = q_ref[pl.ds(qoff, GRID_W), :] * jnp.asarray(HEAD_DIM ** -0.5, BF16)
        kk = k_ref[pl.ds(koff, span), :]
        vv = v_ref[pl.ds(koff, span), :]
        s = lax.dot_general(_stack_heads(q, masks), kk, (((1,), (1,)), ((), ())),
                            preferred_element_type=F32)
        s = s + bias_ref[idx]
        m = jnp.max(s, -1, keepdims=True)
        e = jnp.exp(s - m)
        p = (e / jnp.sum(e, -1, keepdims=True)).astype(BF16)
        o = _unstack_heads(jnp.dot(p, vv, preferred_element_type=F32), masks, GRID_W)
        z = z_ref[pl.ds(qoff, GRID_W), :].astype(F32)
        o_ref[pl.ds(qoff, GRID_W), :] = (o * _silu(z)).astype(BF16)
        return carry

    lax.fori_loop(0, rb, one_row, 0)


def _na(proj_b, bias):
    b, l, n = proj_b.shape
    db = n // 4
    heads = db // HEAD_DIM
    rows = l // GRID_W
    rb = 8
    t = rb * GRID_W
    return pl.pallas_call(
        functools.partial(_na_body, rows=rows, rb=rb, heads=heads),
        grid=(b, rows // rb),
        in_specs=[
            pl.BlockSpec((None, t, db), lambda i, j: (i, j, 0)),
            pl.BlockSpec((None, l, db), lambda i, j: (i, 0, 1)),
            pl.BlockSpec((None, l, db), lambda i, j: (i, 0, 2)),
            pl.BlockSpec((None, t, db), lambda i, j: (i, j, 3)),
            pl.BlockSpec(bias.shape, lambda i, j: (0, 0, 0)),
        ],
        out_specs=pl.BlockSpec((None, t, db), lambda i, j: (i, j, 0)),
        out_shape=jax.ShapeDtypeStruct((b, l, db), BF16),
        compiler_params=_params(("parallel", "parallel")),
        name="na",
    )(proj_b, proj_b, proj_b, proj_b, bias)


def _dil_body(q_ref, kp_ref, kc_ref, kn_ref, vp_ref, vc_ref, vn_ref, o_ref, lse_ref, kbuf, vbuf, *,
              heads, qb, nb):
    step = pl.program_id(2)
    blk = DIL_BLK
    main = qb * blk
    masks = _head_masks(blk, heads)
    kbuf[0:blk] = kp_ref[...]
    kbuf[blk:blk + main] = kc_ref[...]
    kbuf[blk + main:] = kn_ref[...]
    vbuf[0:blk] = vp_ref[...]
    vbuf[blk:blk + main] = vc_ref[...]
    vbuf[blk + main:] = vn_ref[...]
    qi = lax.broadcasted_iota(jnp.int32, (heads * blk, 3 * blk), 0) % blk
    ki = lax.broadcasted_iota(jnp.int32, (heads * blk, 3 * blk), 1)
    band = jnp.abs(ki - blk - qi) <= blk

    def one_block(i, carry):
        n = step * qb + i
        off = pl.multiple_of(i * blk, blk)
        s = lax.dot_general(_stack_heads(q_ref[pl.ds(off, blk), :], masks), kbuf[pl.ds(off, 3 * blk), :],
                            (((1,), (1,)), ((), ())), preferred_element_type=F32)
        valid = band & ((ki >= blk) | (n > 0)) & ((ki < 2 * blk) | (n < nb - 1))
        s = jnp.where(valid, s, NEG_INF)
        m = jnp.max(s, -1, keepdims=True)
        e = jnp.exp(s - m)
        l = jnp.sum(e, -1, keepdims=True)
        p = (e / l).astype(BF16)
        o = _unstack_heads(jnp.dot(p, vbuf[pl.ds(off, 3 * blk), :], preferred_element_type=F32), masks, blk)
        lse = jnp.broadcast_to(m + jnp.log(l), (heads * blk, heads * HEAD_DIM))
        o_ref[pl.ds(off, blk), :] = o.astype(BF16)
        lse_ref[pl.ds(off, blk), :] = _unstack_heads(lse, masks, blk)
        return carry

    lax.fori_loop(0, qb, one_block, 0)


def _dilated_pattern(q, k, v, dilation):
    b, l, dc = q.shape
    heads = dc // HEAD_DIM
    ld = l // dilation
    nb = ld // DIL_BLK
    qb = math.gcd(DIL_BLOCKS_PER_STEP, nb)
    view = lambda u: u.reshape(b, ld, dilation * dc)
    main = pl.BlockSpec((None, qb * DIL_BLK, dc), lambda i, j, n: (i, n, j))
    prev = pl.BlockSpec((None, DIL_BLK, dc), lambda i, j, n: (i, jnp.maximum(n * qb - 1, 0), j))
    nxt = pl.BlockSpec((None, DIL_BLK, dc), lambda i, j, n: (i, jnp.minimum((n + 1) * qb, nb - 1), j))
    halo_rows = (qb + 2) * DIL_BLK
    o, lse = pl.pallas_call(
        functools.partial(_dil_body, heads=heads, qb=qb, nb=nb),
        grid=(b, dilation, nb // qb),
        in_specs=[main, prev, main, nxt, prev, main, nxt],
        out_specs=[main, main],
        out_shape=[jax.ShapeDtypeStruct((b, ld, dilation * dc), BF16),
                   jax.ShapeDtypeStruct((b, ld, dilation * dc), F32)],
        scratch_shapes=[pltpu.VMEM((halo_rows, dc), BF16), pltpu.VMEM((halo_rows, dc), BF16)],
        compiler_params=_params(("parallel", "parallel", "parallel")),
        name=f"dil_d{dilation}",
    )(view(q), view(k), view(k), view(k), view(v), view(v), view(v))
    return o.reshape(b, l, dc), lse.reshape(b, l, dc)


def _tail_body(x_ref, ya_ref, yb_ref, o1_ref, o2_ref, o3_ref, l1_ref, l2_ref, l3_ref, cz_ref, g_ref,
               gate_ref, wa_ref, wb_ref, wc_ref, wo_ref, lng_ref, lnb_ref, out_ref, *, alpha):
    d = x_ref.shape[-1]
    la, lb, lc = l1_ref[...], l2_ref[...], l3_ref[...]
    m = jnp.maximum(jnp.maximum(la, lb), lc)
    ea, eb, ec = jnp.exp(la - m), jnp.exp(lb - m), jnp.exp(lc - m)
    den = ea + eb + ec
    o = ((ea / den) * o1_ref[...].astype(F32) + (eb / den) * o2_ref[...].astype(F32)
         + (ec / den) * o3_ref[...].astype(F32))
    yc = (o * _silu(cz_ref[...].astype(F32))).astype(BF16)
    pa = jnp.dot(ya_ref[...], wa_ref[...], preferred_element_type=F32)
    pb = jnp.dot(yb_ref[...], wb_ref[...], preferred_element_type=F32)
    pc = jnp.dot(yc, wc_ref[...], preferred_element_type=F32)
    g = _sigmoid(g_ref[...].astype(F32))
    merged = g[:, :d] * pa + g[:, d:2 * d] * pb + g[:, 2 * d:] * pc
    sub = jnp.dot(merged.astype(BF16), wo_ref[...], preferred_element_type=F32) * gate_ref[...]
    res = alpha * x_ref[...] + sub
    out_ref[...] = _layernorm(res) * lng_ref[...] + lnb_ref[...]


def _tail(x, ya, yb, dil, cz, g_all, gate, wa, wb, wc, wo, ln_g, ln_b, alpha):
    b, l, d = x.shape
    t = 256
    tok = lambda w: pl.BlockSpec((None, t, w), lambda i, j: (i, j, 0))
    const = lambda a: pl.BlockSpec(a.shape, lambda i, j: (0,) * a.ndim)
    (o1, l1), (o2, l2), (o3, l3) = dil
    dc = cz.shape[-1]
    ln_g, ln_b = ln_g.reshape(1, d), ln_b.reshape(1, d)
    return pl.pallas_call(
        functools.partial(_tail_body, alpha=alpha),
        grid=(b, l // t),
        in_specs=[tok(d), tok(ya.shape[-1]), tok(yb.shape[-1]), tok(dc), tok(dc), tok(dc),
                  tok(dc), tok(dc), tok(dc), tok(dc), tok(3 * d),
                  pl.BlockSpec((None, 1, d), lambda i, j: (i, 0, 0)),
                  const(wa), const(wb), const(wc), const(wo), const(ln_g), const(ln_b)],
        out_specs=tok(d),
        out_shape=jax.ShapeDtypeStruct((b, l, d), F32),
        compiler_params=_params(("parallel", "parallel")),
        name="tail",
    )(x, ya, yb, o1, o2, o3, l1, l2, l3, cz, g_all, gate.reshape(b, 1, d), wa, wb, wc, wo, ln_g, ln_b)


def _rope_tables(l, heads):
    half = HEAD_DIM // 2
    inv = ROPE_THETA ** (-jnp.arange(half, dtype=F32) / half)
    ang = jnp.arange(l, dtype=F32)[:, None] * inv[None, :]
    cos, sin = jnp.cos(ang), jnp.sin(ang)
    return (jnp.tile(jnp.concatenate([cos, cos], -1), (1, heads)),
            jnp.tile(jnp.concatenate([-sin, sin], -1), (1, heads)))


def _layer(x, ada, lw, consts, alpha):
    d = x.shape[-1]
    shift, scale, gate = ada[:, :d], ada[:, d:2 * d], ada[:, 2 * d:]
    h = _ln_mod(x, scale, shift)
    w_in, b_in = lw["w_in"], lw["b_in"]
    proj_a = _proj(h, w_in[:, :2 * d], b_in[:2 * d])
    proj_b = _proj(h, w_in[:, 2 * d:3 * d], b_in[2 * d:3 * d])
    cq, ck, cv, cz = _proj_rope(h, w_in[:, 3 * d:4 * d], b_in[3 * d:4 * d], *consts["rope"])
    g_all = _proj(h, w_in[:, 4 * d:], b_in[4 * d:])

    v, x1, x2g = _hy_pre(proj_a, lw["conv_w"], lw["conv_b"])
    ya = _hyena(v, x1, x2g, consts["kspec"], lw["skip"], consts["dft"])
    yb = _na(proj_b, lw["na_bias"])
    dil = [_dilated_pattern(cq, ck, cv, dil_d) for _, dil_d in DIL_PATTERNS]
    return _tail(x, ya, yb, dil, cz, g_all, gate, lw["wa"], lw["wb"], lw["wc"], lw["wo"],
                 lw["ln_g"], lw["ln_b"], alpha)


def kernel(x_prompt, x_sample, c_prompt, c_sample, w_ada, b_ada, w_in, b_in, hy_conv_w, hy_conv_b, hy_w1, hy_b1, hy_freq, hy_w2, hy_b2, hy_w3, hy_b3, hy_decay, hy_skip, na_rpb, w_branch_a, w_branch_b, w_branch_c, w_out, ln_g, ln_b):
    depth, d, _ = w_in.shape
    heads_c = (d // 4) // HEAD_DIM
    alpha = (2 * depth) ** 0.25
    groups = [(x_prompt, c_prompt), (x_sample, c_sample)]

    nb_p = c_prompt.shape[0]
    c_all = jnp.concatenate([c_prompt, c_sample], 0)
    pad_rows = -c_all.shape[0] % 8
    ada_all = _ada(jnp.pad(c_all, ((0, pad_rows), (0, 0))), w_ada, b_ada)
    adas = [ada_all[:, :nb_p], ada_all[:, nb_p:nb_p + c_sample.shape[0]]]

    shared = {}
    for x, _ in groups:
        l = x.shape[1]
        if l not in shared:
            shared[l] = {"dft": _dft_tables(l), "rope": _rope_tables(l, heads_c)}

    ys = [x for x, _ in groups]
    for layer in range(depth):
        lw = {
            "w_in": w_in[layer].astype(BF16), "b_in": b_in[layer],
            "conv_w": hy_conv_w[layer], "conv_b": hy_conv_b[layer], "skip": hy_skip[layer],
            "na_bias": _na_bias_table(na_rpb[layer]),
            "wa": w_branch_a[layer].astype(BF16), "wb": w_branch_b[layer].astype(BF16),
            "wc": w_branch_c[layer].astype(BF16), "wo": w_out[layer].astype(BF16),
            "ln_g": ln_g[layer], "ln_b": ln_b[layer],
        }
        kspecs = {}
        for gi in range(len(groups)):
            l = ys[gi].shape[1]
            if l not in kspecs:
                kspecs[l] = _hyena_spectrum(l, shared[l]["dft"], hy_w1[layer], hy_b1[layer], hy_freq[layer],
                                            hy_w2[layer], hy_b2[layer], hy_w3[layer], hy_b3[layer],
                                            hy_decay[layer])
            consts = dict(shared[l], kspec=kspecs[l])
            ys[gi] = _layer(ys[gi], adas[gi][layer], lw, consts, alpha)
    return tuple(ys)
```

```python
import functools
import math

import jax
import jax.numpy as jnp
from jax import lax
from jax.experimental import pallas as pl
from jax.experimental.pallas import tpu as pltpu

F32 = jnp.float32
BF16 = jnp.bfloat16

GRID_W = 64
HEAD_DIM = 64
HYENA_BANDS = 16
HYENA_EMB = 2 * HYENA_BANDS + 1
NA_KR = 8
NA_KC = 16
DIL_PATTERNS = ((128, 1), (512, 4), (2048, 16))
DIL_BLK = 64
DIL_BLOCKS_PER_STEP = 8
NA_ROW_UNROLL = 4
ROPE_THETA = 10000.0
LN_EPS = 1e-5
NEG_INF = -1e30

V7X_LANES = 128
V7X_SUBLANES = 8
V7X_VMEM_LIMIT_BYTES = 56 * 1024 * 1024

DFT_N2 = V7X_LANES
PAIR_GROUP = V7X_SUBLANES
DFT_STEP_ROWS = 64
SLABS_PER_STEP = 4
FEAT_PAD = V7X_LANES


def _params(sem):
    return pltpu.CompilerParams(dimension_semantics=sem, vmem_limit_bytes=V7X_VMEM_LIMIT_BYTES)


def _sigmoid(x):
    return 1.0 / (1.0 + jnp.exp(-x))


def _silu(x):
    return x * _sigmoid(x)


def _ada_body(c_ref, w_ref, b_ref, o_ref):
    s = _silu(c_ref[...])
    o_ref[...] = jnp.dot(s, w_ref[...], preferred_element_type=F32,
                         precision=lax.Precision.HIGHEST) + b_ref[...]


def _ada(c_all, w_ada, b_ada):
    depth, d, n = w_ada.shape
    rows = c_all.shape[0]
    tn = 1024
    return pl.pallas_call(
        _ada_body,
        grid=(depth, n // tn),
        in_specs=[
            pl.BlockSpec((rows, d), lambda l, j: (0, 0)),
            pl.BlockSpec((None, d, tn), lambda l, j: (l, 0, j)),
            pl.BlockSpec((None, 1, tn), lambda l, j: (l, 0, j)),
        ],
        out_specs=pl.BlockSpec((None, rows, tn), lambda l, j: (l, 0, j)),
        out_shape=jax.ShapeDtypeStruct((depth, rows, n), F32),
        compiler_params=_params(("parallel", "parallel")),
        name="ada",
    )(c_all, w_ada, b_ada.reshape(depth, 1, n))


def _layernorm(x):
    mu = jnp.mean(x, -1, keepdims=True)
    xc = x - mu
    var = jnp.mean(xc * xc, -1, keepdims=True)
    return xc * lax.rsqrt(var + LN_EPS)


def _ln_mod_body(x_ref, sc_ref, sh_ref, o_ref):
    h = _layernorm(x_ref[...]) * (1.0 + sc_ref[...]) + sh_ref[...]
    o_ref[...] = h.astype(BF16)


def _ln_mod(x, scale, shift):
    b, l, d = x.shape
    t = 512
    return pl.pallas_call(
        _ln_mod_body,
        grid=(b, l // t),
        in_specs=[
            pl.BlockSpec((None, t, d), lambda i, j: (i, j, 0)),
            pl.BlockSpec((None, 1, d), lambda i, j: (i, 0, 0)),
            pl.BlockSpec((None, 1, d), lambda i, j: (i, 0, 0)),
        ],
        out_specs=pl.BlockSpec((None, t, d), lambda i, j: (i, j, 0)),
        out_shape=jax.ShapeDtypeStruct((b, l, d), BF16),
        compiler_params=_params(("parallel", "parallel")),
        name="ln_mod",
    )(x, scale.reshape(b, 1, d), shift.reshape(b, 1, d))


def _proj_body(h_ref, w_ref, b_ref, o_ref):
    acc = jnp.dot(h_ref[...], w_ref[...], preferred_element_type=F32) + b_ref[...]
    o_ref[...] = acc.astype(o_ref.dtype)


def _proj(h, w, bias):
    b, l, d = h.shape
    n = w.shape[1]
    t, tn = 1024, 1024
    return pl.pallas_call(
        _proj_body,
        grid=(n // tn, b, l // t),
        in_specs=[
            pl.BlockSpec((None, t, d), lambda j, i, k: (i, k, 0)),
            pl.BlockSpec((d, tn), lambda j, i, k: (0, j)),
            pl.BlockSpec((1, tn), lambda j, i, k: (0, j)),
        ],
        out_specs=pl.BlockSpec((None, t, tn), lambda j, i, k: (i, k, j)),
        out_shape=jax.ShapeDtypeStruct((b, l, n), BF16),
        compiler_params=_params(("parallel", "parallel", "parallel")),
        name="proj",
    )(h, w, bias.reshape(1, n))


def _rope_lanes(x, cos, sin_signed):
    outs = []
    lane = lax.broadcasted_iota(jnp.int32, (x.shape[0], V7X_LANES), 1)
    first_half = (lane % HEAD_DIM) < (HEAD_DIM // 2)
    for c0 in range(0, x.shape[1], V7X_LANES):
        xc = x[:, c0:c0 + V7X_LANES]
        partner = jnp.where(first_half,
                            pltpu.roll(xc, V7X_LANES - HEAD_DIM // 2, 1),
                            pltpu.roll(xc, HEAD_DIM // 2, 1))
        outs.append(xc * cos[:, c0:c0 + V7X_LANES] + partner * sin_signed[:, c0:c0 + V7X_LANES])
    return jnp.concatenate(outs, 1)


def _proj_rope_body(h_ref, w_ref, b_ref, cos_ref, sin_ref, q_ref, k_ref, v_ref, z_ref):
    dc = q_ref.shape[-1]
    acc = jnp.dot(h_ref[...], w_ref[...], preferred_element_type=F32) + b_ref[...]
    cos, sin = cos_ref[...], sin_ref[...]
    q = _rope_lanes(acc[:, :dc], cos, sin) * (HEAD_DIM ** -0.5)
    k = _rope_lanes(acc[:, dc:2 * dc], cos, sin)
    q_ref[...] = q.astype(BF16)
    k_ref[...] = k.astype(BF16)
    v_ref[...] = acc[:, 2 * dc:3 * dc].astype(BF16)
    z_ref[...] = acc[:, 3 * dc:].astype(BF16)


def _proj_rope(h, w, bias, cos_t, sin_t):
    b, l, d = h.shape
    n = w.shape[1]
    dc = n // 4
    t = 512
    tok = pl.BlockSpec((None, t, dc), lambda i, k: (i, k, 0))
    shp = jax.ShapeDtypeStruct((b, l, dc), BF16)
    return pl.pallas_call(
        _proj_rope_body,
        grid=(b, l // t),
        in_specs=[
            pl.BlockSpec((None, t, d), lambda i, k: (i, k, 0)),
            pl.BlockSpec((d, n), lambda i, k: (0, 0)),
            pl.BlockSpec((1, n), lambda i, k: (0, 0)),
            pl.BlockSpec((t, dc), lambda i, k: (k, 0)),
            pl.BlockSpec((t, dc), lambda i, k: (k, 0)),
        ],
        out_specs=[tok, tok, tok, tok],
        out_shape=[shp, shp, shp, shp],
        compiler_params=_params(("parallel", "parallel")),
        name="proj_rope",
    )(h, w, bias.reshape(1, n), cos_t, sin_t)


def _hy_pre_body(main_ref, prev_ref, next_ref, cw_ref, cb_ref, v_ref, x1_ref, x2_ref, pad_ref, *, da):
    i = pl.program_id(1)
    last = pl.num_programs(1) - 1
    t = main_ref.shape[0]
    halo = prev_ref.shape[0]
    outs = (v_ref, x1_ref, x2_ref)
    az = main_ref[:, 3 * da:].astype(F32)
    gate = _silu(az)
    for part in range(3):
        cols = slice(part * da, (part + 1) * da)
        prev_row = jnp.where(i > 0, prev_ref[:, cols].astype(F32)[halo - 1:halo], 0.0)
        next_row = jnp.where(i < last, next_ref[:, cols].astype(F32)[0:1], 0.0)
        pad_ref[7:8, :] = prev_row
        pad_ref[8:8 + t, :] = main_ref[:, cols].astype(F32)
        pad_ref[8 + t:9 + t, :] = next_row
        uc = (pad_ref[7:7 + t, :] * cw_ref[0:1, cols] + pad_ref[8:8 + t, :] * cw_ref[1:2, cols]
              + pad_ref[9:9 + t, :] * cw_ref[2:3, cols] + cb_ref[:, cols])
        if part == 2:
            uc = uc * gate
        outs[part][...] = uc.astype(BF16)


def _hy_pre(proj_a, conv_w, conv_b):
    b, l, n = proj_a.shape
    da = n // 4
    t, halo = 512, 16
    nh = t // halo
    tok = pl.BlockSpec((None, t, da), lambda i, j: (i, j, 0))
    shp = jax.ShapeDtypeStruct((b, l, da), BF16)
    return pl.pallas_call(
        functools.partial(_hy_pre_body, da=da),
        grid=(b, l // t),
        in_specs=[
            pl.BlockSpec((None, t, n), lambda i, j: (i, j, 0)),
            pl.BlockSpec((None, halo, n), lambda i, j: (i, jnp.maximum(j * nh - 1, 0), 0)),
            pl.BlockSpec((None, halo, n), lambda i, j: (i, jnp.minimum((j + 1) * nh, l // halo - 1), 0)),
            pl.BlockSpec((3, 3 * da), lambda i, j: (0, 0)),
            pl.BlockSpec((1, 3 * da), lambda i, j: (0, 0)),
        ],
        out_specs=[tok, tok, tok],
        out_shape=[shp, shp, shp],
        scratch_shapes=[pltpu.VMEM((t + 16, da), F32)],
        compiler_params=_params(("parallel", "parallel")),
        name="hy_pre",
    )(proj_a, proj_a, proj_a, conv_w, conv_b.reshape(1, 3 * da))


def _filter_body(feat_ref, w1_ref, b1_ref, f0_ref, w2_ref, b2_ref, f1_ref, w3_ref, b3_ref, dec_ref,
                 hf_ref, sum_ref):
    i = pl.program_id(0)
    hp = lax.Precision.HIGHEST
    feat = feat_ref[...]
    t = feat[:, 0:1]
    h = jnp.sin(f0_ref[...] * (jnp.dot(feat, w1_ref[...], preferred_element_type=F32, precision=hp)
                               + b1_ref[...]))
    h = jnp.sin(f1_ref[...] * (jnp.dot(h, w2_ref[...], preferred_element_type=F32, precision=hp)
                               + b2_ref[...]))
    h = jnp.dot(h.astype(BF16), w3_ref[...], preferred_element_type=F32) + b3_ref[...]
    h = h * jnp.exp(-t * jnp.abs(dec_ref[...]))
    rows = lax.broadcasted_iota(jnp.int32, h.shape, 0) + i * h.shape[0]
    cols = lax.broadcasted_iota(jnp.int32, h.shape, 1)
    h = jnp.where((rows == 0) & (cols >= h.shape[1] // 2), 0.0, h)
    hf_ref[...] = h.astype(BF16)

    @pl.when(i == 0)
    def _():
        sum_ref[...] = jnp.zeros_like(sum_ref)

    sum_ref[...] += jnp.sum(jnp.abs(h), 0, keepdims=True)


def _filter_taps(l, w1, b1, freq, w2, b2, w3, b3, decay):
    fo = w1.shape[1]
    n = w3.shape[1]
    da = decay.shape[0]
    t = jnp.arange(l, dtype=F32) / l
    bands = jnp.arange(1, HYENA_BANDS + 1, dtype=F32)
    ang = 2.0 * math.pi * t[:, None] * bands[None, :]
    feat = jnp.concatenate([t[:, None], jnp.cos(ang), jnp.sin(ang)], -1)
    feat = jnp.pad(feat, ((0, 0), (0, FEAT_PAD - HYENA_EMB)))
    w1p = jnp.pad(w1, ((0, FEAT_PAD - HYENA_EMB), (0, 0)))
    dec = jnp.tile(decay, n // da).reshape(1, n)
    tt = 512
    const = lambda shape: pl.BlockSpec(shape, lambda i: (0,) * len(shape))
    return pl.pallas_call(
        _filter_body,
        grid=(l // tt,),
        in_specs=[
            pl.BlockSpec((tt, FEAT_PAD), lambda i: (i, 0)),
            const((FEAT_PAD, fo)), const((1, fo)), const((1, fo)),
            const((fo, fo)), const((1, fo)), const((1, fo)),
            const((fo, n)), const((1, n)), const((1, n)),
        ],
        out_specs=[pl.BlockSpec((tt, n), lambda i: (i, 0)), const((1, n))],
        out_shape=[jax.ShapeDtypeStruct((l, n), BF16), jax.ShapeDtypeStruct((1, n), F32)],
        compiler_params=_params(("arbitrary",)),
        name="filter_taps",
    )(feat, w1p, b1.reshape(1, fo), freq[0].reshape(1, fo), w2, b2.reshape(1, fo),
      freq[1].reshape(1, fo), w3.astype(BF16), b3.reshape(1, n), dec)


def _dft_tables(l):
    n = 2 * l
    n1 = n // DFT_N2
    kk = jnp.arange(n1, dtype=jnp.int32)
    nn = jnp.arange(n1 // 2, dtype=jnp.int32)
    th = (2.0 * math.pi / n1) * ((kk[:, None] * nn[None, :]) % n1).astype(F32)
    eye2 = jnp.eye(2, dtype=F32)
    f1 = jnp.kron(jnp.concatenate([jnp.cos(th), -jnp.sin(th)], 0), eye2).astype(BF16)
    g = jnp.kron(jnp.concatenate([jnp.cos(th).T, -jnp.sin(th).T], 1) / n, eye2).astype(BF16)
    k2 = jnp.arange(DFT_N2, dtype=jnp.int32)
    n2 = jnp.arange(DFT_N2, dtype=jnp.int32)
    ph = (n2[None, None, :] * (k2[None, :, None] * n1 + kk[:, None, None])) % n
    ang = (2.0 * math.pi / n) * ph.astype(F32)
    c, s = jnp.cos(ang), jnp.sin(ang)
    mf = jnp.concatenate([jnp.concatenate([c, s], 2), jnp.concatenate([-s, c], 2)], 1).astype(BF16)
    mi = jnp.swapaxes(mf, 1, 2)
    return f1, g, mf, mi


def _load_pair_group(ref, lead, g):
    start = pl.multiple_of(g * PAIR_GROUP, PAIR_GROUP)
    words = ref.bitcast(jnp.uint32)[(*lead, slice(None), pl.ds(start, PAIR_GROUP), slice(None))]
    words = jnp.swapaxes(words, 0, 1)
    return [pltpu.bitcast(words[i], BF16) for i in range(PAIR_GROUP)]


def _store_pair_group(ref, lead, g, vals):
    start = pl.multiple_of(g * PAIR_GROUP, PAIR_GROUP)
    words = jnp.stack([pltpu.bitcast(v, jnp.uint32) for v in vals], 0)
    ref.bitcast(jnp.uint32)[(*lead, slice(None), pl.ds(start, PAIR_GROUP), slice(None))] = (
        jnp.swapaxes(words, 0, 1))


def _dft_in_body(f_ref, z_ref, a_ref):
    n1 = a_ref.shape[2]
    cb = a_ref.shape[-1]

    def group(g, carry):
        zcat = jnp.concatenate(_load_pair_group(z_ref, (0,), g), 1)
        r = jnp.dot(f_ref[...], zcat, preferred_element_type=F32).astype(BF16)
        cols = [r[:, i * cb:(i + 1) * cb] for i in range(PAIR_GROUP)]
        _store_pair_group(a_ref, (0, 0), g, [c[:2 * n1] for c in cols])
        _store_pair_group(a_ref, (0, 1), g, [c[2 * n1:] for c in cols])
        return carry

    lax.fori_loop(0, z_ref.shape[2] // (2 * PAIR_GROUP), group, 0)


def _dft_in(f1p, z):
    b, half, n2, c = z.shape
    n1 = 2 * half
    cb, rs = V7X_LANES, DFT_STEP_ROWS
    return pl.pallas_call(
        _dft_in_body,
        grid=(b, c // cb, n2 // rs),
        in_specs=[
            pl.BlockSpec(f1p.shape, lambda i, j, s: (0, 0)),
            pl.BlockSpec((1, half, rs, cb), lambda i, j, s: (i, 0, s, j)),
        ],
        out_specs=pl.BlockSpec((1, 2, n1, rs, cb), lambda i, j, s: (i, 0, 0, s, j)),
        out_shape=jax.ShapeDtypeStruct((b, 2, n1, n2, c), BF16),
        compiler_params=_params(("parallel", "parallel", "parallel")),
        name="dft_in",
    )(f1p, z)


def _dft_out_body(g_ref, b_ref, x_ref, z_ref, skip_ref, o_ref):
    cb = o_ref.shape[-1]

    def group(g, carry):
        re, im = _load_pair_group(b_ref, (0, 0), g), _load_pair_group(b_ref, (0, 1), g)
        bcat = jnp.concatenate([jnp.concatenate([r, i], 0) for r, i in zip(re, im)], 1)
        y = jnp.dot(g_ref[...], bcat, preferred_element_type=F32)
        xs, zs = _load_pair_group(x_ref, (0,), g), _load_pair_group(z_ref, (0,), g)
        outs = [(xs[i].astype(F32) * (y[:, i * cb:(i + 1) * cb] + skip_ref[...] * zs[i].astype(F32))
                 ).astype(BF16) for i in range(PAIR_GROUP)]
        _store_pair_group(o_ref, (0,), g, outs)
        return carry

    lax.fori_loop(0, x_ref.shape[2] // (2 * PAIR_GROUP), group, 0)


def _dft_out(gp, bb, x, z, skip):
    b, half, n2, c = z.shape
    n1 = 2 * half
    cb, rs = V7X_LANES, DFT_STEP_ROWS
    slab = pl.BlockSpec((1, half, rs, cb), lambda i, j, s: (i, 0, s, j))
    return pl.pallas_call(
        _dft_out_body,
        grid=(b, c // cb, n2 // rs),
        in_specs=[
            pl.BlockSpec(gp.shape, lambda i, j, s: (0, 0)),
            pl.BlockSpec((1, 2, n1, rs, cb), lambda i, j, s: (i, 0, 0, s, j)),
            slab, slab,
            pl.BlockSpec((1, cb), lambda i, j, s: (0, j)),
        ],
        out_specs=slab,
        out_shape=jax.ShapeDtypeStruct(z.shape, BF16),
        compiler_params=_params(("parallel", "parallel", "parallel")),
        name="dft_out",
    )(gp, bb, x, z, skip.astype(F32).reshape(1, c))


def _slab_conv_body(mf_ref, mi_ref, k_ref, a_ref, o_ref):
    nb, _, ks, n2, c = a_ref.shape
    for kk in range(ks):
        kr, ki = k_ref[kk, 0].astype(F32), k_ref[kk, 1].astype(F32)
        for b in range(nb):
            a = a_ref[b, :, kk].reshape(2 * n2, c)
            x = jnp.dot(mf_ref[kk], a, preferred_element_type=F32)
            xr, xi = x[:n2], x[n2:]
            y = jnp.concatenate([xr * kr - xi * ki, xr * ki + xi * kr], 0).astype(BF16)
            out = jnp.dot(mi_ref[kk], y, preferred_element_type=F32)
            o_ref[b, :, kk] = out.astype(BF16).reshape(2, n2, c)


def _slab_conv(mf, mi, kspec, a, order):
    b, _, n1, n2, c = a.shape
    m = 2 * n2
    ks = max(1, SLABS_PER_STEP // b)
    return pl.pallas_call(
        _slab_conv_body,
        grid=(n1 // ks,),
        in_specs=[
            pl.BlockSpec((ks, m, m), lambda k: (k, 0, 0)),
            pl.BlockSpec((ks, m, m), lambda k: (k, 0, 0)),
            pl.BlockSpec((ks, 2, n2, c), lambda k: (k, 0, 0, order)),
            pl.BlockSpec((b, 2, ks, n2, c), lambda k: (0, 0, k, 0, 0)),
        ],
        out_specs=pl.BlockSpec((b, 2, ks, n2, c), lambda k: (0, 0, k, 0, 0)),
        out_shape=jax.ShapeDtypeStruct(a.shape, BF16),
        compiler_params=_params(("parallel",)),
        name="slab_conv",
    )(mf, mi, kspec, a)


def _slab_spec_body(mf_ref, sum_ref, a_ref, k_ref):
    n2, c = a_ref.shape[1], a_ref.shape[2]
    half = c // 2
    x = jnp.dot(mf_ref[...], a_ref[...].reshape(2 * n2, c), preferred_element_type=F32)
    inv = 1.0 / (sum_ref[:, :half] + sum_ref[:, half:] + 1e-6)
    k_ref[0] = ((x[:n2, :half] + x[:n2, half:]) * inv).astype(BF16)
    k_ref[1] = ((x[n2:, :half] - x[n2:, half:]) * inv).astype(BF16)


def _slab_spec(mf, sums, a):
    _, _, n1, n2, c = a.shape
    m = 2 * n2
    return pl.pallas_call(
        _slab_spec_body,
        grid=(n1,),
        in_specs=[
            pl.BlockSpec((None, m, m), lambda k: (k, 0, 0)),
            pl.BlockSpec((1, c), lambda k: (0, 0)),
            pl.BlockSpec((None, 2, None, n2, c), lambda k: (0, 0, k, 0, 0)),
        ],
        out_specs=pl.BlockSpec((None, 2, n2, c // 2), lambda k: (k, 0, 0, 0)),
        out_shape=jax.ShapeDtypeStruct((n1, 2, n2, c // 2), BF16),
        compiler_params=_params(("parallel",)),
        name="slab_spec",
    )(mf, sums, a)


def _hyena_spectrum(l, tables, w1, b1, freq, w2, b2, w3, b3, decay):
    f1p, _, mf, _ = tables
    taps, sums = _filter_taps(l, w1, b1, freq, w2, b2, w3, b3, decay)
    a = _dft_in(f1p, taps.reshape(1, l // DFT_N2, DFT_N2, taps.shape[1]))
    return _slab_spec(mf, sums, a)


def _hyena(v, x1, x2g, kspec, skip, tables):
    f1p, gp, mf, mi = tables
    b, l, c = v.shape
    slabs = lambda u: u.reshape(b, l // DFT_N2, DFT_N2, c)
    z = slabs(v)
    for order, xg in enumerate((slabs(x1), slabs(x2g))):
        bb = _slab_conv(mf, mi, kspec, _dft_in(f1p, z), order)
        z = _dft_out(gp, bb, xg, z, skip[order])
    return z.reshape(b, l, c)


def _head_masks(rows, heads):
    lane = lax.broadcasted_iota(jnp.int32, (rows, heads * HEAD_DIM), 1)
    return [(lane >= h * HEAD_DIM) & (lane < (h + 1) * HEAD_DIM) for h in range(heads)]


def _stack_heads(q, masks):
    zero = jnp.zeros_like(q)
    return jnp.concatenate([jnp.where(m, q, zero) for m in masks], 0)


def _unstack_heads(res, masks, rows):
    out = jnp.where(masks[0], res[:rows], 0.0)
    for h in range(1, len(masks)):
        out = out + jnp.where(masks[h], res[h * rows:(h + 1) * rows], 0.0)
    return out


def _na_bias_table(rpb):
    heads = rpb.shape[0]
    c = jnp.arange(GRID_W)
    col_start = jnp.clip(c - NA_KC // 2, 0, GRID_W - NA_KC)
    col_ok = (c[None, :] >= col_start[:, None]) & (c[None, :] < col_start[:, None] + NA_KC)
    dc = jnp.clip(c[None, :] - c[:, None], -(NA_KC - 1), NA_KC - 1) + NA_KC - 1
    onehot = (dc[:, :, None] == jnp.arange(2 * NA_KC - 1)[None, None, :]).astype(F32)
    tcol = jnp.einsum("hrj,qkj->hrqk", rpb.astype(F32), onehot, precision=lax.Precision.HIGHEST)
    per_idx = [tcol[:, NA_KR - 1 - idx:2 * NA_KR - 1 - idx] for idx in range(NA_KR)]
    bias = jnp.transpose(jnp.stack(per_idx, 0), (0, 1, 3, 2, 4))
    bias = jnp.where(col_ok[None, None, :, None, :], bias, NEG_INF)
    return bias.reshape(NA_KR, heads * GRID_W, NA_KR * GRID_W)


def _na_body(q_ref, k_ref, v_ref, z_ref, bias_ref, o_ref, *, rows, rb, heads):
    blk = pl.program_id(1)
    masks = _head_masks(GRID_W, heads)
    span = NA_KR * GRID_W

    def one_row(rr, carry):
        r = blk * rb + rr
        start = jnp.clip(r - NA_KR // 2, 0, rows - NA_KR)
        idx = r - start
        koff = pl.multiple_of(start * GRID_W, GRID_W)
        qoff = pl.multiple_of(rr * GRID_W, GRID_W)
        q = q_ref[pl.ds(qoff, GRID_W), :] * jnp.asarray(HEAD_DIM ** -0.5, BF16)
        kk = k_ref[pl.ds(koff, span), :]
        vv = v_ref[pl.ds(koff, span), :]
        s = lax.dot_general(_stack_heads(q, masks), kk, (((1,), (1,)), ((), ())),
                            preferred_element_type=F32)
        s = s + bias_ref[idx]
        m = jnp.max(s, -1, keepdims=True)
        e = jnp.exp(s - m)
        p = (e / jnp.sum(e, -1, keepdims=True)).astype(BF16)
        o = _unstack_heads(jnp.dot(p, vv, preferred_element_type=F32), masks, GRID_W)
        z = z_ref[pl.ds(qoff, GRID_W), :].astype(F32)
        o_ref[pl.ds(qoff, GRID_W), :] = (o * _silu(z)).astype(BF16)
        return carry

    lax.fori_loop(0, rb, one_row, 0, unroll=NA_ROW_UNROLL)


def _na(proj_b, bias):
    b, l, n = proj_b.shape
    db = n // 4
    heads = db // HEAD_DIM
    rows = l // GRID_W
    rb = 8
    t = rb * GRID_W
    return pl.pallas_call(
        functools.partial(_na_body, rows=rows, rb=rb, heads=heads),
        grid=(b, rows // rb),
        in_specs=[
            pl.BlockSpec((None, t, db), lambda i, j: (i, j, 0)),
            pl.BlockSpec((None, l, db), lambda i, j: (i, 0, 1)),
            pl.BlockSpec((None, l, db), lambda i, j: (i, 0, 2)),
            pl.BlockSpec((None, t, db), lambda i, j: (i, j, 3)),
            pl.BlockSpec(bias.shape, lambda i, j: (0, 0, 0)),
        ],
        out_specs=pl.BlockSpec((None, t, db), lambda i, j: (i, j, 0)),
        out_shape=jax.ShapeDtypeStruct((b, l, db), BF16),
        compiler_params=_params(("parallel", "parallel")),
        name="na",
    )(proj_b, proj_b, proj_b, proj_b, bias)


def _dil_body(q_ref, kp_ref, kc_ref, kn_ref, vp_ref, vc_ref, vn_ref, o_ref, lse_ref, kbuf, vbuf, *,
              heads, qb, nb):
    step = pl.program_id(2)
    blk = DIL_BLK
    main = qb * blk
    masks = _head_masks(blk, heads)
    kbuf[0:blk] = kp_ref[...]
    kbuf[blk:blk + main] = kc_ref[...]
    kbuf[blk + main:] = kn_ref[...]
    vbuf[0:blk] = vp_ref[...]
    vbuf[blk:blk + main] = vc_ref[...]
    vbuf[blk + main:] = vn_ref[...]
    qi = lax.broadcasted_iota(jnp.int32, (heads * blk, 3 * blk), 0) % blk
    ki = lax.broadcasted_iota(jnp.int32, (heads * blk, 3 * blk), 1)
    band = jnp.abs(ki - blk - qi) <= blk

    def one_block(i, carry):
        n = step * qb + i
        off = pl.multiple_of(i * blk, blk)
        s = lax.dot_general(_stack_heads(q_ref[pl.ds(off, blk), :], masks), kbuf[pl.ds(off, 3 * blk), :],
                            (((1,), (1,)), ((), ())), preferred_element_type=F32)
        valid = band & ((ki >= blk) | (n > 0)) & ((ki < 2 * blk) | (n < nb - 1))
        s = jnp.where(valid, s, NEG_INF)
        m = jnp.max(s, -1, keepdims=True)
        e = jnp.exp(s - m)
        l = jnp.sum(e, -1, keepdims=True)
        p = (e / l).astype(BF16)
        o = _unstack_heads(jnp.dot(p, vbuf[pl.ds(off, 3 * blk), :], preferred_element_type=F32), masks, blk)
        lse = jnp.broadcast_to(m + jnp.log(l), (heads * blk, heads * HEAD_DIM))
        o_ref[pl.ds(off, blk), :] = o.astype(BF16)
        lse_ref[pl.ds(off, blk), :] = _unstack_heads(lse, masks, blk)
        return carry

    lax.fori_loop(0, qb, one_block, 0, unroll=True)


def _dilated_pattern(q, k, v, dilation):
    b, l, dc = q.shape
    heads = dc // HEAD_DIM
    ld = l // dilation
    nb = ld // DIL_BLK
    qb = math.gcd(DIL_BLOCKS_PER_STEP, nb)
    view = lambda u: u.reshape(b, ld, dilation * dc)
    main = pl.BlockSpec((None, qb * DIL_BLK, dc), lambda i, j, n: (i, n, j))
    prev = pl.BlockSpec((None, DIL_BLK, dc), lambda i, j, n: (i, jnp.maximum(n * qb - 1, 0), j))
    nxt = pl.BlockSpec((None, DIL_BLK, dc), lambda i, j, n: (i, jnp.minimum((n + 1) * qb, nb - 1), j))
    halo_rows = (qb + 2) * DIL_BLK
    o, lse = pl.pallas_call(
        functools.partial(_dil_body, heads=heads, qb=qb, nb=nb),
        grid=(b, dilation, nb // qb),
        in_specs=[main, prev, main, nxt, prev, main, nxt],
        out_specs=[main, main],
        out_shape=[jax.ShapeDtypeStruct((b, ld, dilation * dc), BF16),
                   jax.ShapeDtypeStruct((b, ld, dilation * dc), F32)],
        scratch_shapes=[pltpu.VMEM((halo_rows, dc), BF16), pltpu.VMEM((halo_rows, dc), BF16)],
        compiler_params=_params(("parallel", "parallel", "parallel")),
        name=f"dil_d{dilation}",
    )(view(q), view(k), view(k), view(k), view(v), view(v), view(v))
    return o.reshape(b, l, dc), lse.reshape(b, l, dc)


def _tail_body(x_ref, ya_ref, yb_ref, o1_ref, o2_ref, o3_ref, l1_ref, l2_ref, l3_ref, cz_ref, g_ref,
               gate_ref, wa_ref, wb_ref, wc_ref, wo_ref, lng_ref, lnb_ref, out_ref, *, alpha):
    d = x_ref.shape[-1]
    la, lb, lc = l1_ref[...], l2_ref[...], l3_ref[...]
    m = jnp.maximum(jnp.maximum(la, lb), lc)
    ea, eb, ec = jnp.exp(la - m), jnp.exp(lb - m), jnp.exp(lc - m)
    den = ea + eb + ec
    o = ((ea / den) * o1_ref[...].astype(F32) + (eb / den) * o2_ref[...].astype(F32)
         + (ec / den) * o3_ref[...].astype(F32))
    yc = (o * _silu(cz_ref[...].astype(F32))).astype(BF16)
    pa = jnp.dot(ya_ref[...], wa_ref[...], preferred_element_type=F32)
    pb = jnp.dot(yb_ref[...], wb_ref[...], preferred_element_type=F32)
    pc = jnp.dot(yc, wc_ref[...], preferred_element_type=F32)
    g = _sigmoid(g_ref[...].astype(F32))
    merged = g[:, :d] * pa + g[:, d:2 * d] * pb + g[:, 2 * d:] * pc
    sub = jnp.dot(merged.astype(BF16), wo_ref[...], preferred_element_type=F32) * gate_ref[...]
    res = alpha * x_ref[...] + sub
    out_ref[...] = _layernorm(res) * lng_ref[...] + lnb_ref[...]


def _tail(x, ya, yb, dil, cz, g_all, gate, wa, wb, wc, wo, ln_g, ln_b, alpha):
    b, l, d = x.shape
    t = 512
    tok = lambda w: pl.BlockSpec((None, t, w), lambda i, j: (i, j, 0))
    const = lambda a: pl.BlockSpec(a.shape, lambda i, j: (0,) * a.ndim)
    (o1, l1), (o2, l2), (o3, l3) = dil
    dc = cz.shape[-1]
    ln_g, ln_b = ln_g.reshape(1, d), ln_b.reshape(1, d)
    return pl.pallas_call(
        functools.partial(_tail_body, alpha=alpha),
        grid=(b, l // t),
        in_specs=[tok(d), tok(ya.shape[-1]), tok(yb.shape[-1]), tok(dc), tok(dc), tok(dc),
                  tok(dc), tok(dc), tok(dc), tok(dc), tok(3 * d),
                  pl.BlockSpec((None, 1, d), lambda i, j: (i, 0, 0)),
                  const(wa), const(wb), const(wc), const(wo), const(ln_g), const(ln_b)],
        out_specs=tok(d),
        out_shape=jax.ShapeDtypeStruct((b, l, d), F32),
        compiler_params=_params(("parallel", "parallel")),
        name="tail",
    )(x, ya, yb, o1, o2, o3, l1, l2, l3, cz, g_all, gate.reshape(b, 1, d), wa, wb, wc, wo, ln_g, ln_b)


def _rope_tables(l, heads):
    half = HEAD_DIM // 2
    inv = ROPE_THETA ** (-jnp.arange(half, dtype=F32) / half)
    ang = jnp.arange(l, dtype=F32)[:, None] * inv[None, :]
    cos, sin = jnp.cos(ang), jnp.sin(ang)
    return (jnp.tile(jnp.concatenate([cos, cos], -1), (1, heads)),
            jnp.tile(jnp.concatenate([-sin, sin], -1), (1, heads)))


def _layer(x, ada, lw, consts, alpha):
    d = x.shape[-1]
    shift, scale, gate = ada[:, :d], ada[:, d:2 * d], ada[:, 2 * d:]
    h = _ln_mod(x, scale, shift)
    w_in, b_in = lw["w_in"], lw["b_in"]
    proj_a = _proj(h, w_in[:, :2 * d], b_in[:2 * d])
    proj_b = _proj(h, w_in[:, 2 * d:3 * d], b_in[2 * d:3 * d])
    cq, ck, cv, cz = _proj_rope(h, w_in[:, 3 * d:4 * d], b_in[3 * d:4 * d], *consts["rope"])
    g_all = _proj(h, w_in[:, 4 * d:], b_in[4 * d:])

    v, x1, x2g = _hy_pre(proj_a, lw["conv_w"], lw["conv_b"])
    ya = _hyena(v, x1, x2g, consts["kspec"], lw["skip"], consts["dft"])
    yb = _na(proj_b, lw["na_bias"])
    dil = [_dilated_pattern(cq, ck, cv, dil_d) for _, dil_d in DIL_PATTERNS]
    return _tail(x, ya, yb, dil, cz, g_all, gate, lw["wa"], lw["wb"], lw["wc"], lw["wo"],
                 lw["ln_g"], lw["ln_b"], alpha)


def kernel(x_prompt, x_sample, c_prompt, c_sample, w_ada, b_ada, w_in, b_in, hy_conv_w, hy_conv_b, hy_w1, hy_b1, hy_freq, hy_w2, hy_b2, hy_w3, hy_b3, hy_decay, hy_skip, na_rpb, w_branch_a, w_branch_b, w_branch_c, w_out, ln_g, ln_b):
    depth, d, _ = w_in.shape
    heads_c = (d // 4) // HEAD_DIM
    alpha = (2 * depth) ** 0.25
    groups = [(x_prompt, c_prompt), (x_sample, c_sample)]

    nb_p = c_prompt.shape[0]
    c_all = jnp.concatenate([c_prompt, c_sample], 0)
    pad_rows = -c_all.shape[0] % 8
    ada_all = _ada(jnp.pad(c_all, ((0, pad_rows), (0, 0))), w_ada, b_ada)
    adas = [ada_all[:, :nb_p], ada_all[:, nb_p:nb_p + c_sample.shape[0]]]

    shared = {}
    for x, _ in groups:
        l = x.shape[1]
        if l not in shared:
            shared[l] = {"dft": _dft_tables(l), "rope": _rope_tables(l, heads_c)}

    ys = [x for x, _ in groups]
    for layer in range(depth):
        lw = {
            "w_in": w_in[layer].astype(BF16), "b_in": b_in[layer],
            "conv_w": hy_conv_w[layer], "conv_b": hy_conv_b[layer], "skip": hy_skip[layer],
            "na_bias": _na_bias_table(na_rpb[layer]),
            "wa": w_branch_a[layer].astype(BF16), "wb": w_branch_b[layer].astype(BF16),
            "wc": w_branch_c[layer].astype(BF16), "wo": w_out[layer].astype(BF16),
            "ln_g": ln_g[layer], "ln_b": ln_b[layer],
        }
        kspecs = {}
        for gi in range(len(groups)):
            l = ys[gi].shape[1]
            if l not in kspecs:
                kspecs[l] = _hyena_spectrum(l, shared[l]["dft"], hy_w1[layer], hy_b1[layer], hy_freq[layer],
                                            hy_w2[layer], hy_b2[layer], hy_w3[layer], hy_b3[layer],
                                            hy_decay[layer])
            consts = dict(shared[l], kspec=kspecs[l])
            ys[gi] = _layer(ys[gi], adas[gi][layer], lw, consts, alpha)
    return tuple(ys)
```

```python
import functools
import math

import jax
import jax.numpy as jnp
from jax import lax
from jax.experimental import pallas as pl
from jax.experimental.pallas import tpu as pltpu

F32 = jnp.float32
BF16 = jnp.bfloat16

GRID_W = 64
HEAD_DIM = 64
HYENA_BANDS = 16
HYENA_EMB = 2 * HYENA_BANDS + 1
NA_KR = 8
NA_KC = 16
DIL_PATTERNS = ((128, 1), (512, 4), (2048, 16))
DIL_BLK = 64
DIL_BLOCKS_PER_STEP = 8
NA_ROW_UNROLL = 4
ROPE_THETA = 10000.0
LN_EPS = 1e-5
NEG_INF = -1e30

V7X_LANES = 128
V7X_SUBLANES = 8
V7X_VMEM_LIMIT_BYTES = 56 * 1024 * 1024

DFT_N2 = V7X_LANES
PAIR_GROUP = V7X_SUBLANES
DFT_STEP_ROWS = 64
SLABS_PER_STEP = 4
FEAT_PAD = V7X_LANES


def _params(sem):
    return pltpu.CompilerParams(dimension_semantics=sem, vmem_limit_bytes=V7X_VMEM_LIMIT_BYTES)


def _sigmoid(x):
    return 1.0 / (1.0 + jnp.exp(-x))


def _silu(x):
    return x * _sigmoid(x)


def _ada_body(c_ref, w_ref, b_ref, o_ref):
    s = _silu(c_ref[...])
    o_ref[...] = jnp.dot(s, w_ref[...], preferred_element_type=F32,
                         precision=lax.Precision.HIGHEST) + b_ref[...]


def _ada(c_all, w_ada, b_ada):
    depth, d, n = w_ada.shape
    rows = c_all.shape[0]
    tn = 1024
    return pl.pallas_call(
        _ada_body,
        grid=(depth, n // tn),
        in_specs=[
            pl.BlockSpec((rows, d), lambda l, j: (0, 0)),
            pl.BlockSpec((None, d, tn), lambda l, j: (l, 0, j)),
            pl.BlockSpec((None, 1, tn), lambda l, j: (l, 0, j)),
        ],
        out_specs=pl.BlockSpec((None, rows, tn), lambda l, j: (l, 0, j)),
        out_shape=jax.ShapeDtypeStruct((depth, rows, n), F32),
        compiler_params=_params(("parallel", "parallel")),
        name="ada",
    )(c_all, w_ada, b_ada.reshape(depth, 1, n))


def _layernorm(x):
    mu = jnp.mean(x, -1, keepdims=True)
    xc = x - mu
    var = jnp.mean(xc * xc, -1, keepdims=True)
    return xc * lax.rsqrt(var + LN_EPS)


def _ln_mod_body(x_ref, sc_ref, sh_ref, o_ref):
    h = _layernorm(x_ref[...]) * (1.0 + sc_ref[...]) + sh_ref[...]
    o_ref[...] = h.astype(BF16)


def _ln_mod(x, scale, shift):
    b, l, d = x.shape
    t = 512
    return pl.pallas_call(
        _ln_mod_body,
        grid=(b, l // t),
        in_specs=[
            pl.BlockSpec((None, t, d), lambda i, j: (i, j, 0)),
            pl.BlockSpec((None, 1, d), lambda i, j: (i, 0, 0)),
            pl.BlockSpec((None, 1, d), lambda i, j: (i, 0, 0)),
        ],
        out_specs=pl.BlockSpec((None, t, d), lambda i, j: (i, j, 0)),
        out_shape=jax.ShapeDtypeStruct((b, l, d), BF16),
        compiler_params=_params(("parallel", "parallel")),
        name="ln_mod",
    )(x, scale.reshape(b, 1, d), shift.reshape(b, 1, d))


def _proj_body(h_ref, w_ref, b_ref, o_ref):
    acc = jnp.dot(h_ref[...], w_ref[...], preferred_element_type=F32) + b_ref[...]
    o_ref[...] = acc.astype(o_ref.dtype)


def _proj(h, w, bias):
    b, l, d = h.shape
    n = w.shape[1]
    t, tn = 1024, 1024
    return pl.pallas_call(
        _proj_body,
        grid=(n // tn, b, l // t),
        in_specs=[
            pl.BlockSpec((None, t, d), lambda j, i, k: (i, k, 0)),
            pl.BlockSpec((d, tn), lambda j, i, k: (0, j)),
            pl.BlockSpec((1, tn), lambda j, i, k: (0, j)),
        ],
        out_specs=pl.BlockSpec((None, t, tn), lambda j, i, k: (i, k, j)),
        out_shape=jax.ShapeDtypeStruct((b, l, n), BF16),
        compiler_params=_params(("parallel", "parallel", "parallel")),
        name="proj",
    )(h, w, bias.reshape(1, n))


def _rope_lanes(x, cos, sin_signed):
    outs = []
    lane = lax.broadcasted_iota(jnp.int32, (x.shape[0], V7X_LANES), 1)
    first_half = (lane % HEAD_DIM) < (HEAD_DIM // 2)
    for c0 in range(0, x.shape[1], V7X_LANES):
        xc = x[:, c0:c0 + V7X_LANES]
        partner = jnp.where(first_half,
                            pltpu.roll(xc, V7X_LANES - HEAD_DIM // 2, 1),
                            pltpu.roll(xc, HEAD_DIM // 2, 1))
        outs.append(xc * cos[:, c0:c0 + V7X_LANES] + partner * sin_signed[:, c0:c0 + V7X_LANES])
    return jnp.concatenate(outs, 1)


def _proj_rope_body(h_ref, w_ref, b_ref, cos_ref, sin_ref, q_ref, k_ref, v_ref, z_ref):
    dc = q_ref.shape[-1]
    acc = jnp.dot(h_ref[...], w_ref[...], preferred_element_type=F32) + b_ref[...]
    cos, sin = cos_ref[...], sin_ref[...]
    q = _rope_lanes(acc[:, :dc], cos, sin) * (HEAD_DIM ** -0.5)
    k = _rope_lanes(acc[:, dc:2 * dc], cos, sin)
    q_ref[...] = q.astype(BF16)
    k_ref[...] = k.astype(BF16)
    v_ref[...] = acc[:, 2 * dc:3 * dc].astype(BF16)
    z_ref[...] = acc[:, 3 * dc:].astype(BF16)


def _proj_rope(h, w, bias, cos_t, sin_t):
    b, l, d = h.shape
    n = w.shape[1]
    dc = n // 4
    t = 512
    tok = pl.BlockSpec((None, t, dc), lambda i, k: (i, k, 0))
    shp = jax.ShapeDtypeStruct((b, l, dc), BF16)
    return pl.pallas_call(
        _proj_rope_body,
        grid=(b, l // t),
        in_specs=[
            pl.BlockSpec((None, t, d), lambda i, k: (i, k, 0)),
            pl.BlockSpec((d, n), lambda i, k: (0, 0)),
            pl.BlockSpec((1, n), lambda i, k: (0, 0)),
            pl.BlockSpec((t, dc), lambda i, k: (k, 0)),
            pl.BlockSpec((t, dc), lambda i, k: (k, 0)),
        ],
        out_specs=[tok, tok, tok, tok],
        out_shape=[shp, shp, shp, shp],
        compiler_params=_params(("parallel", "parallel")),
        name="proj_rope",
    )(h, w, bias.reshape(1, n), cos_t, sin_t)


def _hy_pre_body(main_ref, prev_ref, next_ref, cw_ref, cb_ref, v_ref, x1_ref, x2_ref, pad_ref, *, da):
    i = pl.program_id(1)
    last = pl.num_programs(1) - 1
    t = main_ref.shape[0]
    halo = prev_ref.shape[0]
    outs = (v_ref, x1_ref, x2_ref)
    az = main_ref[:, 3 * da:].astype(F32)
    gate = _silu(az)
    for part in range(3):
        cols = slice(part * da, (part + 1) * da)
        prev_row = jnp.where(i > 0, prev_ref[:, cols].astype(F32)[halo - 1:halo], 0.0)
        next_row = jnp.where(i < last, next_ref[:, cols].astype(F32)[0:1], 0.0)
        pad_ref[7:8, :] = prev_row
        pad_ref[8:8 + t, :] = main_ref[:, cols].astype(F32)
        pad_ref[8 + t:9 + t, :] = next_row
        uc = (pad_ref[7:7 + t, :] * cw_ref[0:1, cols] + pad_ref[8:8 + t, :] * cw_ref[1:2, cols]
              + pad_ref[9:9 + t, :] * cw_ref[2:3, cols] + cb_ref[:, cols])
        if part == 2:
            uc = uc * gate
        outs[part][...] = uc.astype(BF16)


def _hy_pre(proj_a, conv_w, conv_b):
    b, l, n = proj_a.shape
    da = n // 4
    t, halo = 512, 16
    nh = t // halo
    tok = pl.BlockSpec((None, t, da), lambda i, j: (i, j, 0))
    shp = jax.ShapeDtypeStruct((b, l, da), BF16)
    return pl.pallas_call(
        functools.partial(_hy_pre_body, da=da),
        grid=(b, l // t),
        in_specs=[
            pl.BlockSpec((None, t, n), lambda i, j: (i, j, 0)),
            pl.BlockSpec((None, halo, n), lambda i, j: (i, jnp.maximum(j * nh - 1, 0), 0)),
            pl.BlockSpec((None, halo, n), lambda i, j: (i, jnp.minimum((j + 1) * nh, l // halo - 1), 0)),
            pl.BlockSpec((3, 3 * da), lambda i, j: (0, 0)),
            pl.BlockSpec((1, 3 * da), lambda i, j: (0, 0)),
        ],
        out_specs=[tok, tok, tok],
        out_shape=[shp, shp, shp],
        scratch_shapes=[pltpu.VMEM((t + 16, da), F32)],
        compiler_params=_params(("parallel", "parallel")),
        name="hy_pre",
    )(proj_a, proj_a, proj_a, conv_w, conv_b.reshape(1, 3 * da))


def _filter_body(feat_ref, w1_ref, b1_ref, f0_ref, w2_ref, b2_ref, f1_ref, w3_ref, b3_ref, dec_ref,
                 hf_ref, sum_ref):
    i = pl.program_id(0)
    hp = lax.Precision.HIGHEST
    feat = feat_ref[...]
    t = feat[:, 0:1]
    h = jnp.sin(f0_ref[...] * (jnp.dot(feat, w1_ref[...], preferred_element_type=F32, precision=hp)
                               + b1_ref[...]))
    h = jnp.sin(f1_ref[...] * (jnp.dot(h, w2_ref[...], preferred_element_type=F32, precision=hp)
                               + b2_ref[...]))
    h = jnp.dot(h.astype(BF16), w3_ref[...], preferred_element_type=F32) + b3_ref[...]
    h = h * jnp.exp(-t * jnp.abs(dec_ref[...]))
    rows = lax.broadcasted_iota(jnp.int32, h.shape, 0) + i * h.shape[0]
    cols = lax.broadcasted_iota(jnp.int32, h.shape, 1)
    h = jnp.where((rows == 0) & (cols >= h.shape[1] // 2), 0.0, h)
    hf_ref[...] = h.astype(BF16)

    @pl.when(i == 0)
    def _():
        sum_ref[...] = jnp.zeros_like(sum_ref)

    sum_ref[...] += jnp.sum(jnp.abs(h), 0, keepdims=True)


def _filter_taps(l, w1, b1, freq, w2, b2, w3, b3, decay):
    fo = w1.shape[1]
    n = w3.shape[1]
    da = decay.shape[0]
    t = jnp.arange(l, dtype=F32) / l
    bands = jnp.arange(1, HYENA_BANDS + 1, dtype=F32)
    ang = 2.0 * math.pi * t[:, None] * bands[None, :]
    feat = jnp.concatenate([t[:, None], jnp.cos(ang), jnp.sin(ang)], -1)
    feat = jnp.pad(feat, ((0, 0), (0, FEAT_PAD - HYENA_EMB)))
    w1p = jnp.pad(w1, ((0, FEAT_PAD - HYENA_EMB), (0, 0)))
    dec = jnp.tile(decay, n // da).reshape(1, n)
    tt = 512
    const = lambda shape: pl.BlockSpec(shape, lambda i: (0,) * len(shape))
    return pl.pallas_call(
        _filter_body,
        grid=(l // tt,),
        in_specs=[
            pl.BlockSpec((tt, FEAT_PAD), lambda i: (i, 0)),
            const((FEAT_PAD, fo)), const((1, fo)), const((1, fo)),
            const((fo, fo)), const((1, fo)), const((1, fo)),
            const((fo, n)), const((1, n)), const((1, n)),
        ],
        out_specs=[pl.BlockSpec((tt, n), lambda i: (i, 0)), const((1, n))],
        out_shape=[jax.ShapeDtypeStruct((l, n), BF16), jax.ShapeDtypeStruct((1, n), F32)],
        compiler_params=_params(("arbitrary",)),
        name="filter_taps",
    )(feat, w1p, b1.reshape(1, fo), freq[0].reshape(1, fo), w2, b2.reshape(1, fo),
      freq[1].reshape(1, fo), w3.astype(BF16), b3.reshape(1, n), dec)


def _dft_tables(l):
    n = 2 * l
    n1 = n // DFT_N2
    kk = jnp.arange(n1 // 2, dtype=jnp.int32)
    nn = jnp.arange(n1 // 2, dtype=jnp.int32)
    th = (2.0 * math.pi / (2 * n1)) * (((2 * kk[:, None] + 1) * nn[None, :]) % (2 * n1)).astype(F32)
    eye2 = jnp.eye(2, dtype=F32)
    f1 = jnp.kron(jnp.concatenate([jnp.cos(th), -jnp.sin(th)], 0), eye2).astype(BF16)
    g = jnp.kron(jnp.concatenate([jnp.cos(th).T, -jnp.sin(th).T], 1) * (2.0 / n), eye2).astype(BF16)
    k2 = jnp.arange(DFT_N2, dtype=jnp.int32)
    n2 = jnp.arange(DFT_N2, dtype=jnp.int32)
    ph = (n2[None, None, :] * (k2[None, :, None] * (2 * n1) + 2 * kk[:, None, None] + 1)) % (2 * n)
    ang = (2.0 * math.pi / (2 * n)) * ph.astype(F32)
    c, s = jnp.cos(ang), jnp.sin(ang)
    mf = jnp.concatenate([jnp.concatenate([c, s], 2), jnp.concatenate([-s, c], 2)], 1).astype(BF16)
    mi = jnp.swapaxes(mf, 1, 2)
    return f1, g, mf, mi


def _load_pair_group(ref, lead, g):
    start = pl.multiple_of(g * PAIR_GROUP, PAIR_GROUP)
    words = ref.bitcast(jnp.uint32)[(*lead, slice(None), pl.ds(start, PAIR_GROUP), slice(None))]
    words = jnp.swapaxes(words, 0, 1)
    return [pltpu.bitcast(words[i], BF16) for i in range(PAIR_GROUP)]


def _store_pair_group(ref, lead, g, vals):
    start = pl.multiple_of(g * PAIR_GROUP, PAIR_GROUP)
    words = jnp.stack([pltpu.bitcast(v, jnp.uint32) for v in vals], 0)
    ref.bitcast(jnp.uint32)[(*lead, slice(None), pl.ds(start, PAIR_GROUP), slice(None))] = (
        jnp.swapaxes(words, 0, 1))


def _dft_in_body(f_ref, z_ref, a_ref):
    n1 = a_ref.shape[2]
    cb = a_ref.shape[-1]

    def group(g, carry):
        zcat = jnp.concatenate(_load_pair_group(z_ref, (0,), g), 1)
        r = jnp.dot(f_ref[...], zcat, preferred_element_type=F32).astype(BF16)
        cols = [r[:, i * cb:(i + 1) * cb] for i in range(PAIR_GROUP)]
        _store_pair_group(a_ref, (0, 0), g, [c[:2 * n1] for c in cols])
        _store_pair_group(a_ref, (0, 1), g, [c[2 * n1:] for c in cols])
        return carry

    lax.fori_loop(0, z_ref.shape[2] // (2 * PAIR_GROUP), group, 0)


def _dft_in(f1p, z):
    b, half, n2, c = z.shape
    n1 = half
    cb, rs = V7X_LANES, DFT_STEP_ROWS
    return pl.pallas_call(
        _dft_in_body,
        grid=(b, c // cb, n2 // rs),
        in_specs=[
            pl.BlockSpec(f1p.shape, lambda i, j, s: (0, 0)),
            pl.BlockSpec((1, half, rs, cb), lambda i, j, s: (i, 0, s, j)),
        ],
        out_specs=pl.BlockSpec((1, 2, n1, rs, cb), lambda i, j, s: (i, 0, 0, s, j)),
        out_shape=jax.ShapeDtypeStruct((b, 2, n1, n2, c), BF16),
        compiler_params=_params(("parallel", "parallel", "parallel")),
        name="dft_in",
    )(f1p, z)


def _dft_out_body(g_ref, b_ref, x_ref, z_ref, skip_ref, o_ref):
    cb = o_ref.shape[-1]

    def group(g, carry):
        re, im = _load_pair_group(b_ref, (0, 0), g), _load_pair_group(b_ref, (0, 1), g)
        bcat = jnp.concatenate([jnp.concatenate([r, i], 0) for r, i in zip(re, im)], 1)
        y = jnp.dot(g_ref[...], bcat, preferred_element_type=F32)
        xs, zs = _load_pair_group(x_ref, (0,), g), _load_pair_group(z_ref, (0,), g)
        outs = [(xs[i].astype(F32) * (y[:, i * cb:(i + 1) * cb] + skip_ref[...] * zs[i].astype(F32))
                 ).astype(BF16) for i in range(PAIR_GROUP)]
        _store_pair_group(o_ref, (0,), g, outs)
        return carry

    lax.fori_loop(0, x_ref.shape[2] // (2 * PAIR_GROUP), group, 0)


def _dft_out(gp, bb, x, z, skip):
    b, half, n2, c = z.shape
    n1 = half
    cb, rs = V7X_LANES, DFT_STEP_ROWS
    slab = pl.BlockSpec((1, half, rs, cb), lambda i, j, s: (i, 0, s, j))
    return pl.pallas_call(
        _dft_out_body,
        grid=(b, c // cb, n2 // rs),
        in_specs=[
            pl.BlockSpec(gp.shape, lambda i, j, s: (0, 0)),
            pl.BlockSpec((1, 2, n1, rs, cb), lambda i, j, s: (i, 0, 0, s, j)),
            slab, slab,
            pl.BlockSpec((1, cb), lambda i, j, s: (0, j)),
        ],
        out_specs=slab,
        out_shape=jax.ShapeDtypeStruct(z.shape, BF16),
        compiler_params=_params(("parallel", "parallel", "parallel")),
        name="dft_out",
    )(gp, bb, x, z, skip.astype(F32).reshape(1, c))


def _slab_conv_body(mf_ref, mi_ref, k_ref, a_ref, o_ref):
    nb, _, ks, n2, c = a_ref.shape
    for kk in range(ks):
        kr, ki = k_ref[kk, 0].astype(F32), k_ref[kk, 1].astype(F32)
        for b in range(nb):
            a = a_ref[b, :, kk].reshape(2 * n2, c)
            x = jnp.dot(mf_ref[kk], a, preferred_element_type=F32)
            xr, xi = x[:n2], x[n2:]
            y = jnp.concatenate([xr * kr - xi * ki, xr * ki + xi * kr], 0).astype(BF16)
            out = jnp.dot(mi_ref[kk], y, preferred_element_type=F32)
            o_ref[b, :, kk] = out.astype(BF16).reshape(2, n2, c)


def _slab_conv(mf, mi, kspec, a, order):
    b, _, n1, n2, c = a.shape
    m = 2 * n2
    ks = max(1, SLABS_PER_STEP // b)
    return pl.pallas_call(
        _slab_conv_body,
        grid=(n1 // ks,),
        in_specs=[
            pl.BlockSpec((ks, m, m), lambda k: (k, 0, 0)),
            pl.BlockSpec((ks, m, m), lambda k: (k, 0, 0)),
            pl.BlockSpec((ks, 2, n2, c), lambda k: (k, 0, 0, order)),
            pl.BlockSpec((b, 2, ks, n2, c), lambda k: (0, 0, k, 0, 0)),
        ],
        out_specs=pl.BlockSpec((b, 2, ks, n2, c), lambda k: (0, 0, k, 0, 0)),
        out_shape=jax.ShapeDtypeStruct(a.shape, BF16),
        compiler_params=_params(("parallel",)),
        name="slab_conv",
    )(mf, mi, kspec, a)


def _slab_spec_body(mf_ref, sum_ref, a_ref, k_ref):
    n2, c = a_ref.shape[1], a_ref.shape[2]
    half = c // 2
    x = jnp.dot(mf_ref[...], a_ref[...].reshape(2 * n2, c), preferred_element_type=F32)
    inv = 1.0 / (sum_ref[:, :half] + sum_ref[:, half:] + 1e-6)
    k_ref[0] = ((x[:n2, :half] + x[:n2, half:]) * inv).astype(BF16)
    k_ref[1] = ((x[n2:, :half] - x[n2:, half:]) * inv).astype(BF16)


def _slab_spec(mf, sums, a):
    _, _, n1, n2, c = a.shape
    m = 2 * n2
    return pl.pallas_call(
        _slab_spec_body,
        grid=(n1,),
        in_specs=[
            pl.BlockSpec((None, m, m), lambda k: (k, 0, 0)),
            pl.BlockSpec((1, c), lambda k: (0, 0)),
            pl.BlockSpec((None, 2, None, n2, c), lambda k: (0, 0, k, 0, 0)),
        ],
        out_specs=pl.BlockSpec((None, 2, n2, c // 2), lambda k: (k, 0, 0, 0)),
        out_shape=jax.ShapeDtypeStruct((n1, 2, n2, c // 2), BF16),
        compiler_params=_params(("parallel",)),
        name="slab_spec",
    )(mf, sums, a)


def _hyena_spectrum(l, tables, w1, b1, freq, w2, b2, w3, b3, decay):
    f1p, _, mf, _ = tables
    taps, sums = _filter_taps(l, w1, b1, freq, w2, b2, w3, b3, decay)
    a = _dft_in(f1p, taps.reshape(1, l // DFT_N2, DFT_N2, taps.shape[1]))
    return _slab_spec(mf, sums, a)


def _hyena(v, x1, x2g, kspec, skip, tables):
    f1p, gp, mf, mi = tables
    b, l, c = v.shape
    slabs = lambda u: u.reshape(b, l // DFT_N2, DFT_N2, c)
    z = slabs(v)
    for order, xg in enumerate((slabs(x1), slabs(x2g))):
        bb = _slab_conv(mf, mi, kspec, _dft_in(f1p, z), order)
        z = _dft_out(gp, bb, xg, z, skip[order])
    return z.reshape(b, l, c)


def _head_masks(rows, heads):
    lane = lax.broadcasted_iota(jnp.int32, (rows, heads * HEAD_DIM), 1)
    return [(lane >= h * HEAD_DIM) & (lane < (h + 1) * HEAD_DIM) for h in range(heads)]


def _stack_heads(q, masks):
    zero = jnp.zeros_like(q)
    return jnp.concatenate([jnp.where(m, q, zero) for m in masks], 0)


def _unstack_heads(res, masks, rows):
    out = jnp.where(masks[0], res[:rows], 0.0)
    for h in range(1, len(masks)):
        out = out + jnp.where(masks[h], res[h * rows:(h + 1) * rows], 0.0)
    return out


def _na_bias_table(rpb):
    heads = rpb.shape[0]
    c = jnp.arange(GRID_W)
    col_start = jnp.clip(c - NA_KC // 2, 0, GRID_W - NA_KC)
    col_ok = (c[None, :] >= col_start[:, None]) & (c[None, :] < col_start[:, None] + NA_KC)
    dc = jnp.clip(c[None, :] - c[:, None], -(NA_KC - 1), NA_KC - 1) + NA_KC - 1
    onehot = (dc[:, :, None] == jnp.arange(2 * NA_KC - 1)[None, None, :]).astype(F32)
    tcol = jnp.einsum("hrj,qkj->hrqk", rpb.astype(F32), onehot, precision=lax.Precision.HIGHEST)
    per_idx = [tcol[:, NA_KR - 1 - idx:2 * NA_KR - 1 - idx] for idx in range(NA_KR)]
    bias = jnp.transpose(jnp.stack(per_idx, 0), (0, 1, 3, 2, 4))
    bias = jnp.where(col_ok[None, None, :, None, :], bias, NEG_INF)
    return bias.reshape(NA_KR, heads * GRID_W, NA_KR * GRID_W)


def _na_body(q_ref, k_ref, v_ref, z_ref, bias_ref, o_ref, *, rows, rb, heads):
    blk = pl.program_id(1)
    masks = _head_masks(GRID_W, heads)
    span = NA_KR * GRID_W

    def one_row(rr, carry):
        r = blk * rb + rr
        start = jnp.clip(r - NA_KR // 2, 0, rows - NA_KR)
        idx = r - start
        koff = pl.multiple_of(start * GRID_W, GRID_W)
        qoff = pl.multiple_of(rr * GRID_W, GRID_W)
        q = q_ref[pl.ds(qoff, GRID_W), :] * jnp.asarray(HEAD_DIM ** -0.5, BF16)
        kk = k_ref[pl.ds(koff, span), :]
        vv = v_ref[pl.ds(koff, span), :]
        s = lax.dot_general(_stack_heads(q, masks), kk, (((1,), (1,)), ((), ())),
                            preferred_element_type=F32)
        s = s + bias_ref[idx]
        m = jnp.max(s, -1, keepdims=True)
        e = jnp.exp(s - m)
        p = (e / jnp.sum(e, -1, keepdims=True)).astype(BF16)
        o = _unstack_heads(jnp.dot(p, vv, preferred_element_type=F32), masks, GRID_W)
        z = z_ref[pl.ds(qoff, GRID_W), :].astype(F32)
        o_ref[pl.ds(qoff, GRID_W), :] = (o * _silu(z)).astype(BF16)
        return carry

    lax.fori_loop(0, rb, one_row, 0, unroll=NA_ROW_UNROLL)


def _na(proj_b, bias):
    b, l, n = proj_b.shape
    db = n // 4
    heads = db // HEAD_DIM
    rows = l // GRID_W
    rb = 8
    t = rb * GRID_W
    return pl.pallas_call(
        functools.partial(_na_body, rows=rows, rb=rb, heads=heads),
        grid=(b, rows // rb),
        in_specs=[
            pl.BlockSpec((None, t, db), lambda i, j: (i, j, 0)),
            pl.BlockSpec((None, l, db), lambda i, j: (i, 0, 1)),
            pl.BlockSpec((None, l, db), lambda i, j: (i, 0, 2)),
            pl.BlockSpec((None, t, db), lambda i, j: (i, j, 3)),
            pl.BlockSpec(bias.shape, lambda i, j: (0, 0, 0)),
        ],
        out_specs=pl.BlockSpec((None, t, db), lambda i, j: (i, j, 0)),
        out_shape=jax.ShapeDtypeStruct((b, l, db), BF16),
        compiler_params=_params(("parallel", "parallel")),
        name="na",
    )(proj_b, proj_b, proj_b, proj_b, bias)


def _dil_body(q_ref, kp_ref, kc_ref, kn_ref, vp_ref, vc_ref, vn_ref, o_ref, lse_ref, kbuf, vbuf, *,
              heads, qb, nb):
    step = pl.program_id(2)
    blk = DIL_BLK
    main = qb * blk
    masks = _head_masks(blk, heads)
    kbuf[0:blk] = kp_ref[...]
    kbuf[blk:blk + main] = kc_ref[...]
    kbuf[blk + main:] = kn_ref[...]
    vbuf[0:blk] = vp_ref[...]
    vbuf[blk:blk + main] = vc_ref[...]
    vbuf[blk + main:] = vn_ref[...]
    qi = lax.broadcasted_iota(jnp.int32, (heads * blk, 3 * blk), 0) % blk
    ki = lax.broadcasted_iota(jnp.int32, (heads * blk, 3 * blk), 1)
    band = jnp.abs(ki - blk - qi) <= blk

    def one_block(i, carry):
        n = step * qb + i
        off = pl.multiple_of(i * blk, blk)
        s = lax.dot_general(_stack_heads(q_ref[pl.ds(off, blk), :], masks), kbuf[pl.ds(off, 3 * blk), :],
                            (((1,), (1,)), ((), ())), preferred_element_type=F32)
        valid = band & ((ki >= blk) | (n > 0)) & ((ki < 2 * blk) | (n < nb - 1))
        s = jnp.where(valid, s, NEG_INF)
        m = jnp.max(s, -1, keepdims=True)
        e = jnp.exp(s - m)
        l = jnp.sum(e, -1, keepdims=True)
        p = (e / l).astype(BF16)
        o = _unstack_heads(jnp.dot(p, vbuf[pl.ds(off, 3 * blk), :], preferred_element_type=F32), masks, blk)
        lse = jnp.broadcast_to(m + jnp.log(l), (heads * blk, heads * HEAD_DIM))
        o_ref[pl.ds(off, blk), :] = o.astype(BF16)
        lse_ref[pl.ds(off, blk), :] = _unstack_heads(lse, masks, blk)
        return carry

    lax.fori_loop(0, qb, one_block, 0, unroll=True)


def _dilated_pattern(q, k, v, dilation):
    b, l, dc = q.shape
    heads = dc // HEAD_DIM
    ld = l // dilation
    nb = ld // DIL_BLK
    qb = math.gcd(DIL_BLOCKS_PER_STEP, nb)
    view = lambda u: u.reshape(b, ld, dilation * dc)
    main = pl.BlockSpec((None, qb * DIL_BLK, dc), lambda i, j, n: (i, n, j))
    prev = pl.BlockSpec((None, DIL_BLK, dc), lambda i, j, n: (i, jnp.maximum(n * qb - 1, 0), j))
    nxt = pl.BlockSpec((None, DIL_BLK, dc), lambda i, j, n: (i, jnp.minimum((n + 1) * qb, nb - 1), j))
    halo_rows = (qb + 2) * DIL_BLK
    o, lse = pl.pallas_call(
        functools.partial(_dil_body, heads=heads, qb=qb, nb=nb),
        grid=(b, dilation, nb // qb),
        in_specs=[main, prev, main, nxt, prev, main, nxt],
        out_specs=[main, main],
        out_shape=[jax.ShapeDtypeStruct((b, ld, dilation * dc), BF16),
                   jax.ShapeDtypeStruct((b, ld, dilation * dc), F32)],
        scratch_shapes=[pltpu.VMEM((halo_rows, dc), BF16), pltpu.VMEM((halo_rows, dc), BF16)],
        compiler_params=_params(("parallel", "parallel", "parallel")),
        name=f"dil_d{dilation}",
    )(view(q), view(k), view(k), view(k), view(v), view(v), view(v))
    return o.reshape(b, l, dc), lse.reshape(b, l, dc)


def _tail_body(x_ref, ya_ref, yb_ref, o1_ref, o2_ref, o3_ref, l1_ref, l2_ref, l3_ref, cz_ref, g_ref,
               gate_ref, wa_ref, wb_ref, wc_ref, wo_ref, lng_ref, lnb_ref, out_ref, *, alpha):
    d = x_ref.shape[-1]
    la, lb, lc = l1_ref[...], l2_ref[...], l3_ref[...]
    m = jnp.maximum(jnp.maximum(la, lb), lc)
    ea, eb, ec = jnp.exp(la - m), jnp.exp(lb - m), jnp.exp(lc - m)
    den = ea + eb + ec
    o = ((ea / den) * o1_ref[...].astype(F32) + (eb / den) * o2_ref[...].astype(F32)
         + (ec / den) * o3_ref[...].astype(F32))
    yc = (o * _silu(cz_ref[...].astype(F32))).astype(BF16)
    pa = jnp.dot(ya_ref[...], wa_ref[...], preferred_element_type=F32)
    pb = jnp.dot(yb_ref[...], wb_ref[...], preferred_element_type=F32)
    pc = jnp.dot(yc, wc_ref[...], preferred_element_type=F32)
    g = _sigmoid(g_ref[...].astype(F32))
    merged = g[:, :d] * pa + g[:, d:2 * d] * pb + g[:, 2 * d:] * pc
    sub = jnp.dot(merged.astype(BF16), wo_ref[...], preferred_element_type=F32) * gate_ref[...]
    res = alpha * x_ref[...] + sub
    out_ref[...] = _layernorm(res) * lng_ref[...] + lnb_ref[...]


def _tail(x, ya, yb, dil, cz, g_all, gate, wa, wb, wc, wo, ln_g, ln_b, alpha):
    b, l, d = x.shape
    t = 512
    tok = lambda w: pl.BlockSpec((None, t, w), lambda i, j: (i, j, 0))
    const = lambda a: pl.BlockSpec(a.shape, lambda i, j: (0,) * a.ndim)
    (o1, l1), (o2, l2), (o3, l3) = dil
    dc = cz.shape[-1]
    ln_g, ln_b = ln_g.reshape(1, d), ln_b.reshape(1, d)
    return pl.pallas_call(
        functools.partial(_tail_body, alpha=alpha),
        grid=(b, l // t),
        in_specs=[tok(d), tok(ya.shape[-1]), tok(yb.shape[-1]), tok(dc), tok(dc), tok(dc),
                  tok(dc), tok(dc), tok(dc), tok(dc), tok(3 * d),
                  pl.BlockSpec((None, 1, d), lambda i, j: (i, 0, 0)),
                  const(wa), const(wb), const(wc), const(wo), const(ln_g), const(ln_b)],
        out_specs=tok(d),
        out_shape=jax.ShapeDtypeStruct((b, l, d), F32),
        compiler_params=_params(("parallel", "parallel")),
        name="tail",
    )(x, ya, yb, o1, o2, o3, l1, l2, l3, cz, g_all, gate.reshape(b, 1, d), wa, wb, wc, wo, ln_g, ln_b)


def _rope_tables(l, heads):
    half = HEAD_DIM // 2
    inv = ROPE_THETA ** (-jnp.arange(half, dtype=F32) / half)
    ang = jnp.arange(l, dtype=F32)[:, None] * inv[None, :]
    cos, sin = jnp.cos(ang), jnp.sin(ang)
    return (jnp.tile(jnp.concatenate([cos, cos], -1), (1, heads)),
            jnp.tile(jnp.concatenate([-sin, sin], -1), (1, heads)))


def _layer(x, ada, lw, consts, alpha):
    d = x.shape[-1]
    shift, scale, gate = ada[:, :d], ada[:, d:2 * d], ada[:, 2 * d:]
    h = _ln_mod(x, scale, shift)
    w_in, b_in = lw["w_in"], lw["b_in"]
    proj_a = _proj(h, w_in[:, :2 * d], b_in[:2 * d])
    proj_b = _proj(h, w_in[:, 2 * d:3 * d], b_in[2 * d:3 * d])
    cq, ck, cv, cz = _proj_rope(h, w_in[:, 3 * d:4 * d], b_in[3 * d:4 * d], *consts["rope"])
    g_all = _proj(h, w_in[:, 4 * d:], b_in[4 * d:])

    v, x1, x2g = _hy_pre(proj_a, lw["conv_w"], lw["conv_b"])
    ya = _hyena(v, x1, x2g, consts["kspec"], lw["skip"], consts["dft"])
    yb = _na(proj_b, lw["na_bias"])
    dil = [_dilated_pattern(cq, ck, cv, dil_d) for _, dil_d in DIL_PATTERNS]
    return _tail(x, ya, yb, dil, cz, g_all, gate, lw["wa"], lw["wb"], lw["wc"], lw["wo"],
                 lw["ln_g"], lw["ln_b"], alpha)


def kernel(x_prompt, x_sample, c_prompt, c_sample, w_ada, b_ada, w_in, b_in, hy_conv_w, hy_conv_b, hy_w1, hy_b1, hy_freq, hy_w2, hy_b2, hy_w3, hy_b3, hy_decay, hy_skip, na_rpb, w_branch_a, w_branch_b, w_branch_c, w_out, ln_g, ln_b):
    depth, d, _ = w_in.shape
    heads_c = (d // 4) // HEAD_DIM
    alpha = (2 * depth) ** 0.25
    groups = [(x_prompt, c_prompt), (x_sample, c_sample)]

    nb_p = c_prompt.shape[0]
    c_all = jnp.concatenate([c_prompt, c_sample], 0)
    pad_rows = -c_all.shape[0] % 8
    ada_all = _ada(jnp.pad(c_all, ((0, pad_rows), (0, 0))), w_ada, b_ada)
    adas = [ada_all[:, :nb_p], ada_all[:, nb_p:nb_p + c_sample.shape[0]]]

    shared = {}
    for x, _ in groups:
        l = x.shape[1]
        if l not in shared:
            shared[l] = {"dft": _dft_tables(l), "rope": _rope_tables(l, heads_c)}

    ys = [x for x, _ in groups]
    for layer in range(depth):
        lw = {
            "w_in": w_in[layer].astype(BF16), "b_in": b_in[layer],
            "conv_w": hy_conv_w[layer], "conv_b": hy_conv_b[layer], "skip": hy_skip[layer],
            "na_bias": _na_bias_table(na_rpb[layer]),
            "wa": w_branch_a[layer].astype(BF16), "wb": w_branch_b[layer].astype(BF16),
            "wc": w_branch_c[layer].astype(BF16), "wo": w_out[layer].astype(BF16),
            "ln_g": ln_g[layer], "ln_b": ln_b[layer],
        }
        kspecs = {}
        for gi in range(len(groups)):
            l = ys[gi].shape[1]
            if l not in kspecs:
                kspecs[l] = _hyena_spectrum(l, shared[l]["dft"], hy_w1[layer], hy_b1[layer], hy_freq[layer],
                                            hy_w2[layer], hy_b2[layer], hy_w3[layer], hy_b3[layer],
                                            hy_decay[layer])
            consts = dict(shared[l], kspec=kspecs[l])
            ys[gi] = _layer(ys[gi], adas[gi][layer], lw, consts, alpha)
    return tuple(ys)
```

```python
import functools
import math

import jax
import jax.numpy as jnp
from jax import lax
from jax.experimental import pallas as pl
from jax.experimental.pallas import tpu as pltpu

F32 = jnp.float32
BF16 = jnp.bfloat16

GRID_W = 64
HEAD_DIM = 64
HYENA_BANDS = 16
HYENA_EMB = 2 * HYENA_BANDS + 1
NA_KR = 8
NA_KC = 16
DIL_PATTERNS = ((128, 1), (512, 4), (2048, 16))
DIL_BLK = 64
DIL_SPLIT = 16
NA_ROW_UNROLL = 4
ROPE_THETA = 10000.0
LN_EPS = 1e-5
NEG_INF = -1e30

V7X_LANES = 128
V7X_SUBLANES = 8
V7X_VMEM_LIMIT_BYTES = 56 * 1024 * 1024

DFT_N2 = V7X_LANES
PAIR_GROUP = V7X_SUBLANES
DFT_STEP_ROWS = 64
SLABS_PER_STEP = 4
FEAT_PAD = V7X_LANES


def _params(sem):
    return pltpu.CompilerParams(dimension_semantics=sem, vmem_limit_bytes=V7X_VMEM_LIMIT_BYTES)


def _sigmoid(x):
    return 1.0 / (1.0 + jnp.exp(-x))


def _silu(x):
    return x * _sigmoid(x)


def _ada_body(c_ref, w_ref, b_ref, o_ref):
    s = _silu(c_ref[...])
    o_ref[...] = jnp.dot(s, w_ref[...], preferred_element_type=F32,
                         precision=lax.Precision.HIGHEST) + b_ref[...]


def _ada(c_all, w_ada, b_ada):
    depth, d, n = w_ada.shape
    rows = c_all.shape[0]
    tn = 1024
    return pl.pallas_call(
        _ada_body,
        grid=(depth, n // tn),
        in_specs=[
            pl.BlockSpec((rows, d), lambda l, j: (0, 0)),
            pl.BlockSpec((None, d, tn), lambda l, j: (l, 0, j)),
            pl.BlockSpec((None, 1, tn), lambda l, j: (l, 0, j)),
        ],
        out_specs=pl.BlockSpec((None, rows, tn), lambda l, j: (l, 0, j)),
        out_shape=jax.ShapeDtypeStruct((depth, rows, n), F32),
        compiler_params=_params(("parallel", "parallel")),
        name="ada",
    )(c_all, w_ada, b_ada.reshape(depth, 1, n))


def _layernorm(x):
    mu = jnp.mean(x, -1, keepdims=True)
    xc = x - mu
    var = jnp.mean(xc * xc, -1, keepdims=True)
    return xc * lax.rsqrt(var + LN_EPS)


def _ln_mod_body(x_ref, sc_ref, sh_ref, o_ref):
    h = _layernorm(x_ref[...]) * (1.0 + sc_ref[...]) + sh_ref[...]
    o_ref[...] = h.astype(BF16)


def _ln_mod(x, scale, shift):
    b, l, d = x.shape
    t = 512
    return pl.pallas_call(
        _ln_mod_body,
        grid=(b, l // t),
        in_specs=[
            pl.BlockSpec((None, t, d), lambda i, j: (i, j, 0)),
            pl.BlockSpec((None, 1, d), lambda i, j: (i, 0, 0)),
            pl.BlockSpec((None, 1, d), lambda i, j: (i, 0, 0)),
        ],
        out_specs=pl.BlockSpec((None, t, d), lambda i, j: (i, j, 0)),
        out_shape=jax.ShapeDtypeStruct((b, l, d), BF16),
        compiler_params=_params(("parallel", "parallel")),
        name="ln_mod",
    )(x, scale.reshape(b, 1, d), shift.reshape(b, 1, d))


def _proj_body(h_ref, w_ref, b_ref, o_ref):
    acc = jnp.dot(h_ref[...], w_ref[...], preferred_element_type=F32) + b_ref[...]
    o_ref[...] = acc.astype(o_ref.dtype)


def _proj(h, w, bias):
    b, l, d = h.shape
    n = w.shape[1]
    t, tn = 1024, 1024
    return pl.pallas_call(
        _proj_body,
        grid=(n // tn, b, l // t),
        in_specs=[
            pl.BlockSpec((None, t, d), lambda j, i, k: (i, k, 0)),
            pl.BlockSpec((d, tn), lambda j, i, k: (0, j)),
            pl.BlockSpec((1, tn), lambda j, i, k: (0, j)),
        ],
        out_specs=pl.BlockSpec((None, t, tn), lambda j, i, k: (i, k, j)),
        out_shape=jax.ShapeDtypeStruct((b, l, n), BF16),
        compiler_params=_params(("parallel", "parallel", "parallel")),
        name="proj",
    )(h, w, bias.reshape(1, n))


def _rope_lanes(x, cos, sin_signed):
    outs = []
    lane = lax.broadcasted_iota(jnp.int32, (x.shape[0], V7X_LANES), 1)
    first_half = (lane % HEAD_DIM) < (HEAD_DIM // 2)
    for c0 in range(0, x.shape[1], V7X_LANES):
        xc = x[:, c0:c0 + V7X_LANES]
        partner = jnp.where(first_half,
                            pltpu.roll(xc, V7X_LANES - HEAD_DIM // 2, 1),
                            pltpu.roll(xc, HEAD_DIM // 2, 1))
        outs.append(xc * cos[:, c0:c0 + V7X_LANES] + partner * sin_signed[:, c0:c0 + V7X_LANES])
    return jnp.concatenate(outs, 1)


def _store_split(ref, x):
    t, dc = x.shape
    parts = jnp.swapaxes(x.reshape(t // DIL_SPLIT, DIL_SPLIT, dc), 0, 1)
    for r in range(DIL_SPLIT):
        ref[:, r * dc:(r + 1) * dc] = parts[r].astype(BF16)


def _proj_rope_body(h_ref, w_ref, b_ref, cos_ref, sin_ref, q_ref, k_ref, v_ref, z_ref,
                    qs_ref, ks_ref, vs_ref):
    dc = q_ref.shape[-1]
    acc = jnp.dot(h_ref[...], w_ref[...], preferred_element_type=F32) + b_ref[...]
    cos, sin = cos_ref[...], sin_ref[...]
    q = _rope_lanes(acc[:, :dc], cos, sin) * (HEAD_DIM ** -0.5)
    k = _rope_lanes(acc[:, dc:2 * dc], cos, sin)
    v = acc[:, 2 * dc:3 * dc]
    for ref, split_ref, val in ((q_ref, qs_ref, q), (k_ref, ks_ref, k), (v_ref, vs_ref, v)):
        ref[...] = val.astype(BF16)
        _store_split(split_ref, val)
    z_ref[...] = acc[:, 3 * dc:].astype(BF16)


def _proj_rope(h, w, bias, cos_t, sin_t):
    b, l, d = h.shape
    n = w.shape[1]
    dc = n // 4
    t = 512
    tok = pl.BlockSpec((None, t, dc), lambda i, k: (i, k, 0))
    shp = jax.ShapeDtypeStruct((b, l, dc), BF16)
    spl = pl.BlockSpec((None, t // DIL_SPLIT, DIL_SPLIT * dc), lambda i, k: (i, k, 0))
    spl_shp = jax.ShapeDtypeStruct((b, l // DIL_SPLIT, DIL_SPLIT * dc), BF16)
    return pl.pallas_call(
        _proj_rope_body,
        grid=(b, l // t),
        in_specs=[
            pl.BlockSpec((None, t, d), lambda i, k: (i, k, 0)),
            pl.BlockSpec((d, n), lambda i, k: (0, 0)),
            pl.BlockSpec((1, n), lambda i, k: (0, 0)),
            pl.BlockSpec((t, dc), lambda i, k: (k, 0)),
            pl.BlockSpec((t, dc), lambda i, k: (k, 0)),
        ],
        out_specs=[tok, tok, tok, tok, spl, spl, spl],
        out_shape=[shp, shp, shp, shp, spl_shp, spl_shp, spl_shp],
        compiler_params=_params(("parallel", "parallel")),
        name="proj_rope",
    )(h, w, bias.reshape(1, n), cos_t, sin_t)


def _hy_pre_body(main_ref, prev_ref, next_ref, cw_ref, cb_ref, v_ref, x1_ref, x2_ref, pad_ref, *, da):
    i = pl.program_id(1)
    last = pl.num_programs(1) - 1
    t = main_ref.shape[0]
    halo = prev_ref.shape[0]
    outs = (v_ref, x1_ref, x2_ref)
    az = main_ref[:, 3 * da:].astype(F32)
    gate = _silu(az)
    for part in range(3):
        cols = slice(part * da, (part + 1) * da)
        prev_row = jnp.where(i > 0, prev_ref[:, cols].astype(F32)[halo - 1:halo], 0.0)
        next_row = jnp.where(i < last, next_ref[:, cols].astype(F32)[0:1], 0.0)
        pad_ref[7:8, :] = prev_row
        pad_ref[8:8 + t, :] = main_ref[:, cols].astype(F32)
        pad_ref[8 + t:9 + t, :] = next_row
        uc = (pad_ref[7:7 + t, :] * cw_ref[0:1, cols] + pad_ref[8:8 + t, :] * cw_ref[1:2, cols]
              + pad_ref[9:9 + t, :] * cw_ref[2:3, cols] + cb_ref[:, cols])
        if part == 2:
            uc = uc * gate
        outs[part][...] = uc.astype(BF16)


def _hy_pre(proj_a, conv_w, conv_b):
    b, l, n = proj_a.shape
    da = n // 4
    t, halo = 512, 16
    nh = t // halo
    tok = pl.BlockSpec((None, t, da), lambda i, j: (i, j, 0))
    shp = jax.ShapeDtypeStruct((b, l, da), BF16)
    return pl.pallas_call(
        functools.partial(_hy_pre_body, da=da),
        grid=(b, l // t),
        in_specs=[
            pl.BlockSpec((None, t, n), lambda i, j: (i, j, 0)),
            pl.BlockSpec((None, halo, n), lambda i, j: (i, jnp.maximum(j * nh - 1, 0), 0)),
            pl.BlockSpec((None, halo, n), lambda i, j: (i, jnp.minimum((j + 1) * nh, l // halo - 1), 0)),
            pl.BlockSpec((3, 3 * da), lambda i, j: (0, 0)),
            pl.BlockSpec((1, 3 * da), lambda i, j: (0, 0)),
        ],
        out_specs=[tok, tok, tok],
        out_shape=[shp, shp, shp],
        scratch_shapes=[pltpu.VMEM((t + 16, da), F32)],
        compiler_params=_params(("parallel", "parallel")),
        name="hy_pre",
    )(proj_a, proj_a, proj_a, conv_w, conv_b.reshape(1, 3 * da))


def _filter_body(feat_ref, w1_ref, b1_ref, f0_ref, w2_ref, b2_ref, f1_ref, w3_ref, b3_ref, dec_ref,
                 hf_ref, sum_ref):
    i = pl.program_id(0)
    hp = lax.Precision.HIGHEST
    feat = feat_ref[...]
    t = feat[:, 0:1]
    h = jnp.sin(f0_ref[...] * (jnp.dot(feat, w1_ref[...], preferred_element_type=F32, precision=hp)
                               + b1_ref[...]))
    h = jnp.sin(f1_ref[...] * (jnp.dot(h, w2_ref[...], preferred_element_type=F32, precision=hp)
                               + b2_ref[...]))
    h = jnp.dot(h.astype(BF16), w3_ref[...], preferred_element_type=F32) + b3_ref[...]
    h = h * jnp.exp(-t * jnp.abs(dec_ref[...]))
    rows = lax.broadcasted_iota(jnp.int32, h.shape, 0) + i * h.shape[0]
    cols = lax.broadcasted_iota(jnp.int32, h.shape, 1)
    h = jnp.where((rows == 0) & (cols >= h.shape[1] // 2), 0.0, h)
    hf_ref[...] = h.astype(BF16)

    @pl.when(i == 0)
    def _():
        sum_ref[...] = jnp.zeros_like(sum_ref)

    sum_ref[...] += jnp.sum(jnp.abs(h), 0, keepdims=True)


def _filter_taps(l, w1, b1, freq, w2, b2, w3, b3, decay):
    fo = w1.shape[1]
    n = w3.shape[1]
    da = decay.shape[0]
    t = jnp.arange(l, dtype=F32) / l
    bands = jnp.arange(1, HYENA_BANDS + 1, dtype=F32)
    ang = 2.0 * math.pi * t[:, None] * bands[None, :]
    feat = jnp.concatenate([t[:, None], jnp.cos(ang), jnp.sin(ang)], -1)
    feat = jnp.pad(feat, ((0, 0), (0, FEAT_PAD - HYENA_EMB)))
    w1p = jnp.pad(w1, ((0, FEAT_PAD - HYENA_EMB), (0, 0)))
    dec = jnp.tile(decay, n // da).reshape(1, n)
    tt = 512
    const = lambda shape: pl.BlockSpec(shape, lambda i: (0,) * len(shape))
    return pl.pallas_call(
        _filter_body,
        grid=(l // tt,),
        in_specs=[
            pl.BlockSpec((tt, FEAT_PAD), lambda i: (i, 0)),
            const((FEAT_PAD, fo)), const((1, fo)), const((1, fo)),
            const((fo, fo)), const((1, fo)), const((1, fo)),
            const((fo, n)), const((1, n)), const((1, n)),
        ],
        out_specs=[pl.BlockSpec((tt, n), lambda i: (i, 0)), const((1, n))],
        out_shape=[jax.ShapeDtypeStruct((l, n), BF16), jax.ShapeDtypeStruct((1, n), F32)],
        compiler_params=_params(("arbitrary",)),
        name="filter_taps",
    )(feat, w1p, b1.reshape(1, fo), freq[0].reshape(1, fo), w2, b2.reshape(1, fo),
      freq[1].reshape(1, fo), w3.astype(BF16), b3.reshape(1, n), dec)


def _dft_tables(l):
    n = 2 * l
    n1 = n // DFT_N2
    kk = jnp.arange(n1 // 2, dtype=jnp.int32)
    nn = jnp.arange(n1 // 2, dtype=jnp.int32)
    th = (2.0 * math.pi / (2 * n1)) * (((2 * kk[:, None] + 1) * nn[None, :]) % (2 * n1)).astype(F32)
    eye2 = jnp.eye(2, dtype=F32)
    f1 = jnp.kron(jnp.concatenate([jnp.cos(th), -jnp.sin(th)], 0), eye2).astype(BF16)
    g = jnp.kron(jnp.concatenate([jnp.cos(th).T, -jnp.sin(th).T], 1) * (2.0 / n), eye2).astype(BF16)
    k2 = jnp.arange(DFT_N2, dtype=jnp.int32)
    n2 = jnp.arange(DFT_N2, dtype=jnp.int32)
    ph = (n2[None, None, :] * (k2[None, :, None] * (2 * n1) + 2 * kk[:, None, None] + 1)) % (2 * n)
    ang = (2.0 * math.pi / (2 * n)) * ph.astype(F32)
    c, s = jnp.cos(ang), jnp.sin(ang)
    mf = jnp.concatenate([jnp.concatenate([c, s], 2), jnp.concatenate([-s, c], 2)], 1).astype(BF16)
    mi = jnp.swapaxes(mf, 1, 2)
    return f1, g, mf, mi


def _load_pair_group(ref, lead, g):
    start = pl.multiple_of(g * PAIR_GROUP, PAIR_GROUP)
    words = ref.bitcast(jnp.uint32)[(*lead, slice(None), pl.ds(start, PAIR_GROUP), slice(None))]
    words = jnp.swapaxes(words, 0, 1)
    return [pltpu.bitcast(words[i], BF16) for i in range(PAIR_GROUP)]


def _store_pair_group(ref, lead, g, vals):
    start = pl.multiple_of(g * PAIR_GROUP, PAIR_GROUP)
    words = jnp.stack([pltpu.bitcast(v, jnp.uint32) for v in vals], 0)
    ref.bitcast(jnp.uint32)[(*lead, slice(None), pl.ds(start, PAIR_GROUP), slice(None))] = (
        jnp.swapaxes(words, 0, 1))


def _dft_in_body(f_ref, z_ref, a_ref):
    n1 = a_ref.shape[2]
    cb = a_ref.shape[-1]

    def group(g, carry):
        zcat = jnp.concatenate(_load_pair_group(z_ref, (0,), g), 1)
        r = jnp.dot(f_ref[...], zcat, preferred_element_type=F32).astype(BF16)
        cols = [r[:, i * cb:(i + 1) * cb] for i in range(PAIR_GROUP)]
        _store_pair_group(a_ref, (0, 0), g, [c[:2 * n1] for c in cols])
        _store_pair_group(a_ref, (0, 1), g, [c[2 * n1:] for c in cols])
        return carry

    lax.fori_loop(0, z_ref.shape[2] // (2 * PAIR_GROUP), group, 0)


def _dft_in(f1p, z):
    b, half, n2, c = z.shape
    n1 = half
    cb, rs = V7X_LANES, DFT_STEP_ROWS
    return pl.pallas_call(
        _dft_in_body,
        grid=(b, c // cb, n2 // rs),
        in_specs=[
            pl.BlockSpec(f1p.shape, lambda i, j, s: (0, 0)),
            pl.BlockSpec((1, half, rs, cb), lambda i, j, s: (i, 0, s, j)),
        ],
        out_specs=pl.BlockSpec((1, 2, n1, rs, cb), lambda i, j, s: (i, 0, 0, s, j)),
        out_shape=jax.ShapeDtypeStruct((b, 2, n1, n2, c), BF16),
        compiler_params=_params(("parallel", "parallel", "parallel")),
        name="dft_in",
    )(f1p, z)


def _dft_out_body(g_ref, b_ref, x_ref, z_ref, skip_ref, o_ref):
    cb = o_ref.shape[-1]

    def group(g, carry):
        re, im = _load_pair_group(b_ref, (0, 0), g), _load_pair_group(b_ref, (0, 1), g)
        bcat = jnp.concatenate([jnp.concatenate([r, i], 0) for r, i in zip(re, im)], 1)
        y = jnp.dot(g_ref[...], bcat, preferred_element_type=F32)
        xs, zs = _load_pair_group(x_ref, (0,), g), _load_pair_group(z_ref, (0,), g)
        outs = [(xs[i].astype(F32) * (y[:, i * cb:(i + 1) * cb] + skip_ref[...] * zs[i].astype(F32))
                 ).astype(BF16) for i in range(PAIR_GROUP)]
        _store_pair_group(o_ref, (0,), g, outs)
        return carry

    lax.fori_loop(0, x_ref.shape[2] // (2 * PAIR_GROUP), group, 0)


def _dft_out(gp, bb, x, z, skip):
    b, half, n2, c = z.shape
    n1 = half
    cb, rs = V7X_LANES, DFT_STEP_ROWS
    slab = pl.BlockSpec((1, half, rs, cb), lambda i, j, s: (i, 0, s, j))
    return pl.pallas_call(
        _dft_out_body,
        grid=(b, c // cb, n2 // rs),
        in_specs=[
            pl.BlockSpec(gp.shape, lambda i, j, s: (0, 0)),
            pl.BlockSpec((1, 2, n1, rs, cb), lambda i, j, s: (i, 0, 0, s, j)),
            slab, slab,
            pl.BlockSpec((1, cb), lambda i, j, s: (0, j)),
        ],
        out_specs=slab,
        out_shape=jax.ShapeDtypeStruct(z.shape, BF16),
        compiler_params=_params(("parallel", "parallel", "parallel")),
        name="dft_out",
    )(gp, bb, x, z, skip.astype(F32).reshape(1, c))


def _slab_conv_body(mf_ref, mi_ref, k_ref, a_ref, o_ref):
    nb, _, ks, n2, c = a_ref.shape
    for kk in range(ks):
        kr, ki = k_ref[kk, 0].astype(F32), k_ref[kk, 1].astype(F32)
        for b in range(nb):
            a = a_ref[b, :, kk].reshape(2 * n2, c)
            x = jnp.dot(mf_ref[kk], a, preferred_element_type=F32)
            xr, xi = x[:n2], x[n2:]
            y = jnp.concatenate([xr * kr - xi * ki, xr * ki + xi * kr], 0).astype(BF16)
            out = jnp.dot(mi_ref[kk], y, preferred_element_type=F32)
            o_ref[b, :, kk] = out.astype(BF16).reshape(2, n2, c)


def _slab_conv(mf, mi, kspec, a, order):
    b, _, n1, n2, c = a.shape
    m = 2 * n2
    ks = max(1, SLABS_PER_STEP // b)
    return pl.pallas_call(
        _slab_conv_body,
        grid=(n1 // ks,),
        in_specs=[
            pl.BlockSpec((ks, m, m), lambda k: (k, 0, 0)),
            pl.BlockSpec((ks, m, m), lambda k: (k, 0, 0)),
            pl.BlockSpec((ks, 2, n2, c), lambda k: (k, 0, 0, order)),
            pl.BlockSpec((b, 2, ks, n2, c), lambda k: (0, 0, k, 0, 0)),
        ],
        out_specs=pl.BlockSpec((b, 2, ks, n2, c), lambda k: (0, 0, k, 0, 0)),
        out_shape=jax.ShapeDtypeStruct(a.shape, BF16),
        compiler_params=_params(("parallel",)),
        name="slab_conv",
    )(mf, mi, kspec, a)


def _slab_spec_body(mf_ref, sum_ref, a_ref, k_ref):
    n2, c = a_ref.shape[1], a_ref.shape[2]
    half = c // 2
    x = jnp.dot(mf_ref[...], a_ref[...].reshape(2 * n2, c), preferred_element_type=F32)
    inv = 1.0 / (sum_ref[:, :half] + sum_ref[:, half:] + 1e-6)
    k_ref[0] = ((x[:n2, :half] + x[:n2, half:]) * inv).astype(BF16)
    k_ref[1] = ((x[n2:, :half] - x[n2:, half:]) * inv).astype(BF16)


def _slab_spec(mf, sums, a):
    _, _, n1, n2, c = a.shape
    m = 2 * n2
    return pl.pallas_call(
        _slab_spec_body,
        grid=(n1,),
        in_specs=[
            pl.BlockSpec((None, m, m), lambda k: (k, 0, 0)),
            pl.BlockSpec((1, c), lambda k: (0, 0)),
            pl.BlockSpec((None, 2, None, n2, c), lambda k: (0, 0, k, 0, 0)),
        ],
        out_specs=pl.BlockSpec((None, 2, n2, c // 2), lambda k: (k, 0, 0, 0)),
        out_shape=jax.ShapeDtypeStruct((n1, 2, n2, c // 2), BF16),
        compiler_params=_params(("parallel",)),
        name="slab_spec",
    )(mf, sums, a)


def _hyena_spectrum(l, tables, w1, b1, freq, w2, b2, w3, b3, decay):
    f1p, _, mf, _ = tables
    taps, sums = _filter_taps(l, w1, b1, freq, w2, b2, w3, b3, decay)
    a = _dft_in(f1p, taps.reshape(1, l // DFT_N2, DFT_N2, taps.shape[1]))
    return _slab_spec(mf, sums, a)


def _hyena(v, x1, x2g, kspec, skip, tables):
    f1p, gp, mf, mi = tables
    b, l, c = v.shape
    slabs = lambda u: u.reshape(b, l // DFT_N2, DFT_N2, c)
    z = slabs(v)
    for order, xg in enumerate((slabs(x1), slabs(x2g))):
        bb = _slab_conv(mf, mi, kspec, _dft_in(f1p, z), order)
        z = _dft_out(gp, bb, xg, z, skip[order])
    return z.reshape(b, l, c)


def _head_masks(rows, heads):
    lane = lax.broadcasted_iota(jnp.int32, (rows, heads * HEAD_DIM), 1)
    return [(lane >= h * HEAD_DIM) & (lane < (h + 1) * HEAD_DIM) for h in range(heads)]


def _stack_heads(q, masks):
    zero = jnp.zeros_like(q)
    return jnp.concatenate([jnp.where(m, q, zero) for m in masks], 0)


def _unstack_heads(res, masks, rows):
    out = jnp.where(masks[0], res[:rows], 0.0)
    for h in range(1, len(masks)):
        out = out + jnp.where(masks[h], res[h * rows:(h + 1) * rows], 0.0)
    return out


def _na_bias_table(rpb):
    heads = rpb.shape[0]
    c = jnp.arange(GRID_W)
    col_start = jnp.clip(c - NA_KC // 2, 0, GRID_W - NA_KC)
    col_ok = (c[None, :] >= col_start[:, None]) & (c[None, :] < col_start[:, None] + NA_KC)
    dc = jnp.clip(c[None, :] - c[:, None], -(NA_KC - 1), NA_KC - 1) + NA_KC - 1
    onehot = (dc[:, :, None] == jnp.arange(2 * NA_KC - 1)[None, None, :]).astype(F32)
    tcol = jnp.einsum("hrj,qkj->hrqk", rpb.astype(F32), onehot, precision=lax.Precision.HIGHEST)
    per_idx = [tcol[:, NA_KR - 1 - idx:2 * NA_KR - 1 - idx] for idx in range(NA_KR)]
    bias = jnp.transpose(jnp.stack(per_idx, 0), (0, 1, 3, 2, 4))
    bias = jnp.where(col_ok[None, None, :, None, :], bias, NEG_INF)
    return bias.reshape(NA_KR, heads * GRID_W, NA_KR * GRID_W)


def _na_body(q_ref, k_ref, v_ref, z_ref, bias_ref, o_ref, *, rows, rb, heads):
    blk = pl.program_id(1)
    masks = _head_masks(GRID_W, heads)
    span = NA_KR * GRID_W

    def one_row(rr, carry):
        r = blk * rb + rr
        start = jnp.clip(r - NA_KR // 2, 0, rows - NA_KR)
        idx = r - start
        koff = pl.multiple_of(start * GRID_W, GRID_W)
        qoff = pl.multiple_of(rr * GRID_W, GRID_W)
        q = q_ref[pl.ds(qoff, GRID_W), :] * jnp.asarray(HEAD_DIM ** -0.5, BF16)
        kk = k_ref[pl.ds(koff, span), :]
        vv = v_ref[pl.ds(koff, span), :]
        s = lax.dot_general(_stack_heads(q, masks), kk, (((1,), (1,)), ((), ())),
                            preferred_element_type=F32)
        s = s + bias_ref[idx]
        m = jnp.max(s, -1, keepdims=True)
        e = jnp.exp(s - m)
        p = (e / jnp.sum(e, -1, keepdims=True)).astype(BF16)
        o = _unstack_heads(jnp.dot(p, vv, preferred_element_type=F32), masks, GRID_W)
        z = z_ref[pl.ds(qoff, GRID_W), :].astype(F32)
        o_ref[pl.ds(qoff, GRID_W), :] = (o * _silu(z)).astype(BF16)
        return carry

    lax.fori_loop(0, rb, one_row, 0, unroll=NA_ROW_UNROLL)


def _na(proj_b, bias):
    b, l, n = proj_b.shape
    db = n // 4
    heads = db // HEAD_DIM
    rows = l // GRID_W
    rb = 8
    t = rb * GRID_W
    return pl.pallas_call(
        functools.partial(_na_body, rows=rows, rb=rb, heads=heads),
        grid=(b, rows // rb),
        in_specs=[
            pl.BlockSpec((None, t, db), lambda i, j: (i, j, 0)),
            pl.BlockSpec((None, l, db), lambda i, j: (i, 0, 1)),
            pl.BlockSpec((None, l, db), lambda i, j: (i, 0, 2)),
            pl.BlockSpec((None, t, db), lambda i, j: (i, j, 3)),
            pl.BlockSpec(bias.shape, lambda i, j: (0, 0, 0)),
        ],
        out_specs=pl.BlockSpec((None, t, db), lambda i, j: (i, j, 0)),
        out_shape=jax.ShapeDtypeStruct((b, l, db), BF16),
        compiler_params=_params(("parallel", "parallel")),
        name="na",
    )(proj_b, proj_b, proj_b, proj_b, bias)


def _attend(q, kk, vv, valid, masks):
    rows = q.shape[0]
    s = lax.dot_general(_stack_heads(q, masks), kk, (((1,), (1,)), ((), ())), preferred_element_type=F32)
    s = jnp.where(valid, s, NEG_INF)
    m = jnp.max(s, -1, keepdims=True)
    e = jnp.exp(s - m)
    l = jnp.sum(e, -1, keepdims=True)
    p = (e / l).astype(BF16)
    o = _unstack_heads(jnp.dot(p, vv, preferred_element_type=F32), masks, rows)
    lse = _unstack_heads(jnp.broadcast_to(m + jnp.log(l), (s.shape[0], q.shape[1])), masks, rows)
    return o, lse


def _split_pattern(dilation, n, nchunks, q_ref, kbuf, vbuf, o_split, l_split, masks, heads):
    blk = DIL_BLK
    dc = heads * HEAD_DIM
    phases = DIL_SPLIT // dilation
    mq = blk // phases
    row = lax.broadcasted_iota(jnp.int32, (heads * blk, 3 * blk), 0) % blk
    col = lax.broadcasted_iota(jnp.int32, (heads * blk, 3 * blk), 1)
    qa, qm = row // mq, row % mq
    ka = sum((col >= a * 3 * mq).astype(jnp.int32) for a in range(1, phases)) if phases > 1 else 0
    km = col - ka * (3 * mq)
    band = jnp.abs(phases * (km - mq - qm) + (ka - qa)) <= blk
    for m0 in range(0, blk, mq):
        gm = km + (n * blk + m0 - mq)
        valid = band & (gm >= 0) & (gm < nchunks * blk)
        for j in range(dilation):
            lanes = [slice((j + dilation * a) * dc, (j + dilation * a + 1) * dc) for a in range(phases)]
            q = jnp.concatenate([q_ref[m0:m0 + mq, ln] for ln in lanes], 0)
            krows = slice(blk + m0 - mq, blk + m0 + 2 * mq)
            kk = jnp.concatenate([kbuf[krows, ln] for ln in lanes], 0)
            vv = jnp.concatenate([vbuf[krows, ln] for ln in lanes], 0)
            o, lse = _attend(q, kk, vv, valid, masks)
            for a in range(phases):
                o_split[j + dilation * a, m0:m0 + mq, :] = o[a * mq:(a + 1) * mq]
                l_split[j + dilation * a, m0:m0 + mq, :] = lse[a * mq:(a + 1) * mq]


def _dil_body(qn_ref, knp_ref, kn_ref, knn_ref, vnp_ref, vn_ref, vnn_ref,
              qs_ref, ksp_ref, ks_ref, ksn_ref, vsp_ref, vs_ref, vsn_ref, cz_ref, y_ref,
              kbn, vbn, kbs, vbs, o_split, l_split, o_all, l_all, *, heads, nchunks):
    n = pl.program_id(1)
    blk = DIL_BLK
    chunk = qn_ref.shape[0]
    per = chunk // blk
    masks = _head_masks(blk, heads)
    for buf, prev, cur, nxt, halo in ((kbn, knp_ref, kn_ref, knn_ref, blk), (vbn, vnp_ref, vn_ref, vnn_ref, blk),
                                      (kbs, ksp_ref, ks_ref, ksn_ref, blk), (vbs, vsp_ref, vs_ref, vsn_ref, blk)):
        main = cur.shape[0]
        buf[0:halo] = prev[...]
        buf[halo:halo + main] = cur[...]
        buf[halo + main:] = nxt[...]

    qi = lax.broadcasted_iota(jnp.int32, (heads * blk, 3 * blk), 0) % blk
    ki = lax.broadcasted_iota(jnp.int32, (heads * blk, 3 * blk), 1)
    band = jnp.abs(ki - blk - qi) <= blk

    def token_block(i, carry):
        g = n * per + i
        off = pl.multiple_of(i * blk, blk)
        valid = band & ((ki >= blk) | (g > 0)) & ((ki < 2 * blk) | (g < nchunks * per - 1))
        o, lse = _attend(qn_ref[pl.ds(off, blk), :], kbn[pl.ds(off, 3 * blk), :], vbn[pl.ds(off, 3 * blk), :],
                         valid, masks)
        o_all[0, pl.ds(off, blk), :] = o
        l_all[0, pl.ds(off, blk), :] = lse
        return carry

    lax.fori_loop(0, per, token_block, 0, unroll=8)

    for g, (_, dilation) in enumerate(DIL_PATTERNS):
        if dilation == 1:
            continue
        _split_pattern(dilation, n, nchunks, qs_ref, kbs, vbs, o_split, l_split, masks, heads)
        o_all[g] = jnp.swapaxes(o_split[...], 0, 1).reshape(chunk, heads * HEAD_DIM)
        l_all[g] = jnp.swapaxes(l_split[...], 0, 1).reshape(chunk, heads * HEAD_DIM)

    ls = [l_all[g] for g in range(len(DIL_PATTERNS))]
    m = functools.reduce(jnp.maximum, ls)
    es = [jnp.exp(l - m) for l in ls]
    den = functools.reduce(lambda a, b: a + b, es)
    o = functools.reduce(lambda a, b: a + b, [(e / den) * o_all[g] for g, e in enumerate(es)])
    y_ref[...] = (o * _silu(cz_ref[...].astype(F32))).astype(BF16)


def _dilated_mixture(qn, kn, vn, qs, ks, vs, cz):
    b, l, dc = qn.shape
    heads = dc // HEAD_DIM
    blk = DIL_BLK
    assert DIL_PATTERNS[0][1] == 1 and all(w == 2 * blk * d and DIL_SPLIT % d == 0 for w, d in DIL_PATTERNS)
    chunk = DIL_SPLIT * blk
    nchunks = l // chunk
    per = chunk // blk
    nat = pl.BlockSpec((None, chunk, dc), lambda i, n: (i, n, 0))
    nat_prev = pl.BlockSpec((None, blk, dc), lambda i, n: (i, jnp.maximum(n * per - 1, 0), 0))
    nat_next = pl.BlockSpec((None, blk, dc), lambda i, n: (i, jnp.minimum((n + 1) * per, nchunks * per - 1), 0))
    spl = pl.BlockSpec((None, blk, DIL_SPLIT * dc), lambda i, n: (i, n, 0))
    spl_prev = pl.BlockSpec((None, blk, DIL_SPLIT * dc), lambda i, n: (i, jnp.maximum(n - 1, 0), 0))
    spl_next = pl.BlockSpec((None, blk, DIL_SPLIT * dc), lambda i, n: (i, jnp.minimum(n + 1, nchunks - 1), 0))
    return pl.pallas_call(
        functools.partial(_dil_body, heads=heads, nchunks=nchunks),
        grid=(b, nchunks),
        in_specs=[nat, nat_prev, nat, nat_next, nat_prev, nat, nat_next,
                  spl, spl_prev, spl, spl_next, spl_prev, spl, spl_next, nat],
        out_specs=nat,
        out_shape=jax.ShapeDtypeStruct((b, l, dc), BF16),
        scratch_shapes=[pltpu.VMEM((chunk + 2 * blk, dc), BF16), pltpu.VMEM((chunk + 2 * blk, dc), BF16),
                        pltpu.VMEM((3 * blk, DIL_SPLIT * dc), BF16), pltpu.VMEM((3 * blk, DIL_SPLIT * dc), BF16),
                        pltpu.VMEM((DIL_SPLIT, blk, dc), F32), pltpu.VMEM((DIL_SPLIT, blk, dc), F32),
                        pltpu.VMEM((len(DIL_PATTERNS), chunk, dc), F32),
                        pltpu.VMEM((len(DIL_PATTERNS), chunk, dc), F32)],
        compiler_params=_params(("parallel", "parallel")),
        name="dilated",
    )(qn, kn, kn, kn, vn, vn, vn, qs, ks, ks, ks, vs, vs, vs, cz)


def _tail_body(x_ref, ya_ref, yb_ref, yc_ref, g_ref, gate_ref, wa_ref, wb_ref, wc_ref, wo_ref, lng_ref, lnb_ref,
               out_ref, *, alpha):
    d = x_ref.shape[-1]
    pa = jnp.dot(ya_ref[...], wa_ref[...], preferred_element_type=F32)
    pb = jnp.dot(yb_ref[...], wb_ref[...], preferred_element_type=F32)
    pc = jnp.dot(yc_ref[...], wc_ref[...], preferred_element_type=F32)
    g = _sigmoid(g_ref[...].astype(F32))
    merged = g[:, :d] * pa + g[:, d:2 * d] * pb + g[:, 2 * d:] * pc
    sub = jnp.dot(merged.astype(BF16), wo_ref[...], preferred_element_type=F32) * gate_ref[...]
    res = alpha * x_ref[...] + sub
    out_ref[...] = _layernorm(res) * lng_ref[...] + lnb_ref[...]


def _tail(x, ya, yb, yc, g_all, gate, wa, wb, wc, wo, ln_g, ln_b, alpha):
    b, l, d = x.shape
    t = 512
    tok = lambda w: pl.BlockSpec((None, t, w), lambda i, j: (i, j, 0))
    const = lambda a: pl.BlockSpec(a.shape, lambda i, j: (0,) * a.ndim)
    ln_g, ln_b = ln_g.reshape(1, d), ln_b.reshape(1, d)
    return pl.pallas_call(
        functools.partial(_tail_body, alpha=alpha),
        grid=(b, l // t),
        in_specs=[tok(d), tok(ya.shape[-1]), tok(yb.shape[-1]), tok(yc.shape[-1]), tok(3 * d),
                  pl.BlockSpec((None, 1, d), lambda i, j: (i, 0, 0)),
                  const(wa), const(wb), const(wc), const(wo), const(ln_g), const(ln_b)],
        out_specs=tok(d),
        out_shape=jax.ShapeDtypeStruct((b, l, d), F32),
        compiler_params=_params(("parallel", "parallel")),
        name="tail",
    )(x, ya, yb, yc, g_all, gate.reshape(b, 1, d), wa, wb, wc, wo, ln_g, ln_b)


def _rope_tables(l, heads):
    half = HEAD_DIM // 2
    inv = ROPE_THETA ** (-jnp.arange(half, dtype=F32) / half)
    ang = jnp.arange(l, dtype=F32)[:, None] * inv[None, :]
    cos, sin = jnp.cos(ang), jnp.sin(ang)
    return (jnp.tile(jnp.concatenate([cos, cos], -1), (1, heads)),
            jnp.tile(jnp.concatenate([-sin, sin], -1), (1, heads)))


def _layer(x, ada, lw, consts, alpha):
    d = x.shape[-1]
    shift, scale, gate = ada[:, :d], ada[:, d:2 * d], ada[:, 2 * d:]
    h = _ln_mod(x, scale, shift)
    w_in, b_in = lw["w_in"], lw["b_in"]
    proj_a = _proj(h, w_in[:, :2 * d], b_in[:2 * d])
    proj_b = _proj(h, w_in[:, 2 * d:3 * d], b_in[2 * d:3 * d])
    cq, ck, cv, cz, cqs, cks, cvs = _proj_rope(h, w_in[:, 3 * d:4 * d], b_in[3 * d:4 * d], *consts["rope"])
    g_all = _proj(h, w_in[:, 4 * d:], b_in[4 * d:])

    v, x1, x2g = _hy_pre(proj_a, lw["conv_w"], lw["conv_b"])
    ya = _hyena(v, x1, x2g, consts["kspec"], lw["skip"], consts["dft"])
    yb = _na(proj_b, lw["na_bias"])
    yc = _dilated_mixture(cq, ck, cv, cqs, cks, cvs, cz)
    return _tail(x, ya, yb, yc, g_all, gate, lw["wa"], lw["wb"], lw["wc"], lw["wo"],
                 lw["ln_g"], lw["ln_b"], alpha)


def kernel(x_prompt, x_sample, c_prompt, c_sample, w_ada, b_ada, w_in, b_in, hy_conv_w, hy_conv_b, hy_w1, hy_b1, hy_freq, hy_w2, hy_b2, hy_w3, hy_b3, hy_decay, hy_skip, na_rpb, w_branch_a, w_branch_b, w_branch_c, w_out, ln_g, ln_b):
    depth, d, _ = w_in.shape
    heads_c = (d // 4) // HEAD_DIM
    alpha = (2 * depth) ** 0.25
    groups = [(x_prompt, c_prompt), (x_sample, c_sample)]

    nb_p = c_prompt.shape[0]
    c_all = jnp.concatenate([c_prompt, c_sample], 0)
    pad_rows = -c_all.shape[0] % 8
    ada_all = _ada(jnp.pad(c_all, ((0, pad_rows), (0, 0))), w_ada, b_ada)
    adas = [ada_all[:, :nb_p], ada_all[:, nb_p:nb_p + c_sample.shape[0]]]

    shared = {}
    for x, _ in groups:
        l = x.shape[1]
        if l not in shared:
            shared[l] = {"dft": _dft_tables(l), "rope": _rope_tables(l, heads_c)}

    ys = [x for x, _ in groups]
    for layer in range(depth):
        lw = {
            "w_in": w_in[layer].astype(BF16), "b_in": b_in[layer],
            "conv_w": hy_conv_w[layer], "conv_b": hy_conv_b[layer], "skip": hy_skip[layer],
            "na_bias": _na_bias_table(na_rpb[layer]),
            "wa": w_branch_a[layer].astype(BF16), "wb": w_branch_b[layer].astype(BF16),
            "wc": w_branch_c[layer].astype(BF16), "wo": w_out[layer].astype(BF16),
            "ln_g": ln_g[layer], "ln_b": ln_b[layer],
        }
        kspecs = {}
        for gi in range(len(groups)):
            l = ys[gi].shape[1]
            if l not in kspecs:
                kspecs[l] = _hyena_spectrum(l, shared[l]["dft"], hy_w1[layer], hy_b1[layer], hy_freq[layer],
                                            hy_w2[layer], hy_b2[layer], hy_w3[layer], hy_b3[layer],
                                            hy_decay[layer])
            consts = dict(shared[l], kspec=kspecs[l])
            ys[gi] = _layer(ys[gi], adas[gi][layer], lw, consts, alpha)
    return tuple(ys)
```

```python
import functools
import math

import jax
import jax.numpy as jnp
from jax import lax
from jax.experimental import pallas as pl
from jax.experimental.pallas import tpu as pltpu

F32 = jnp.float32
BF16 = jnp.bfloat16

GRID_W = 64
HEAD_DIM = 64
HYENA_BANDS = 16
HYENA_EMB = 2 * HYENA_BANDS + 1
NA_KR = 8
NA_KC = 16
DIL_PATTERNS = ((128, 1), (512, 4), (2048, 16))
DIL_BLK = 64
DIL_SPLIT = 16
NA_ROW_UNROLL = 4
ROPE_THETA = 10000.0
LN_EPS = 1e-5
NEG_INF = -1e30

V7X_LANES = 128
V7X_SUBLANES = 8
V7X_VMEM_LIMIT_BYTES = 56 * 1024 * 1024

DFT_N2 = V7X_LANES
PAIR_GROUP = V7X_SUBLANES
DFT_STEP_ROWS = 64
SLABS_PER_STEP = 8
FEAT_PAD = V7X_LANES


def _params(sem):
    return pltpu.CompilerParams(dimension_semantics=sem, vmem_limit_bytes=V7X_VMEM_LIMIT_BYTES)


def _sigmoid(x):
    return 1.0 / (1.0 + jnp.exp(-x))


def _silu(x):
    return x * _sigmoid(x)


def _ada_body(c_ref, w_ref, b_ref, o_ref):
    s = _silu(c_ref[...])
    o_ref[...] = jnp.dot(s, w_ref[...], preferred_element_type=F32,
                         precision=lax.Precision.HIGHEST) + b_ref[...]


def _ada(c_all, w_ada, b_ada):
    depth, d, n = w_ada.shape
    rows = c_all.shape[0]
    tn = 1024
    return pl.pallas_call(
        _ada_body,
        grid=(depth, n // tn),
        in_specs=[
            pl.BlockSpec((rows, d), lambda l, j: (0, 0)),
            pl.BlockSpec((None, d, tn), lambda l, j: (l, 0, j)),
            pl.BlockSpec((None, 1, tn), lambda l, j: (l, 0, j)),
        ],
        out_specs=pl.BlockSpec((None, rows, tn), lambda l, j: (l, 0, j)),
        out_shape=jax.ShapeDtypeStruct((depth, rows, n), F32),
        compiler_params=_params(("parallel", "parallel")),
        name="ada",
    )(c_all, w_ada, b_ada.reshape(depth, 1, n))


def _layernorm(x):
    mu = jnp.mean(x, -1, keepdims=True)
    xc = x - mu
    var = jnp.mean(xc * xc, -1, keepdims=True)
    return xc * lax.rsqrt(var + LN_EPS)


def _ln_mod_body(x_ref, sc_ref, sh_ref, o_ref):
    h = _layernorm(x_ref[...]) * (1.0 + sc_ref[...]) + sh_ref[...]
    o_ref[...] = h.astype(BF16)


def _ln_mod(x, scale, shift):
    b, l, d = x.shape
    t = 512
    return pl.pallas_call(
        _ln_mod_body,
        grid=(b, l // t),
        in_specs=[
            pl.BlockSpec((None, t, d), lambda i, j: (i, j, 0)),
            pl.BlockSpec((None, 1, d), lambda i, j: (i, 0, 0)),
            pl.BlockSpec((None, 1, d), lambda i, j: (i, 0, 0)),
        ],
        out_specs=pl.BlockSpec((None, t, d), lambda i, j: (i, j, 0)),
        out_shape=jax.ShapeDtypeStruct((b, l, d), BF16),
        compiler_params=_params(("parallel", "parallel")),
        name="ln_mod",
    )(x, scale.reshape(b, 1, d), shift.reshape(b, 1, d))


def _proj_body(h_ref, w_ref, b_ref, o_ref, *, sigmoid):
    acc = jnp.dot(h_ref[...], w_ref[...], preferred_element_type=F32) + b_ref[...]
    if sigmoid:
        acc = _sigmoid(acc)
    o_ref[...] = acc.astype(o_ref.dtype)


def _proj(h, w, bias, sigmoid=False):
    b, l, d = h.shape
    n = w.shape[1]
    t, tn = 1024, 1024
    return pl.pallas_call(
        functools.partial(_proj_body, sigmoid=sigmoid),
        grid=(n // tn, b, l // t),
        in_specs=[
            pl.BlockSpec((None, t, d), lambda j, i, k: (i, k, 0)),
            pl.BlockSpec((d, tn), lambda j, i, k: (0, j)),
            pl.BlockSpec((1, tn), lambda j, i, k: (0, j)),
        ],
        out_specs=pl.BlockSpec((None, t, tn), lambda j, i, k: (i, k, j)),
        out_shape=jax.ShapeDtypeStruct((b, l, n), BF16),
        compiler_params=_params(("parallel", "parallel", "parallel")),
        name="proj",
    )(h, w, bias.reshape(1, n))


def _rope_lanes(x, cos, sin_signed):
    outs = []
    lane = lax.broadcasted_iota(jnp.int32, (x.shape[0], V7X_LANES), 1)
    first_half = (lane % HEAD_DIM) < (HEAD_DIM // 2)
    for c0 in range(0, x.shape[1], V7X_LANES):
        xc = x[:, c0:c0 + V7X_LANES]
        partner = jnp.where(first_half,
                            pltpu.roll(xc, V7X_LANES - HEAD_DIM // 2, 1),
                            pltpu.roll(xc, HEAD_DIM // 2, 1))
        outs.append(xc * cos[:, c0:c0 + V7X_LANES] + partner * sin_signed[:, c0:c0 + V7X_LANES])
    return jnp.concatenate(outs, 1)


def _store_split(ref, x):
    t, dc = x.shape
    parts = jnp.swapaxes(x.reshape(t // DIL_SPLIT, DIL_SPLIT, dc), 0, 1)
    for r in range(DIL_SPLIT):
        ref[:, r * dc:(r + 1) * dc] = parts[r].astype(BF16)


def _proj_rope_body(h_ref, w_ref, b_ref, cos_ref, sin_ref, q_ref, k_ref, v_ref, z_ref,
                    qs_ref, ks_ref, vs_ref):
    dc = q_ref.shape[-1]
    acc = jnp.dot(h_ref[...], w_ref[...], preferred_element_type=F32) + b_ref[...]
    cos, sin = cos_ref[...], sin_ref[...]
    q = _rope_lanes(acc[:, :dc], cos, sin) * (HEAD_DIM ** -0.5)
    k = _rope_lanes(acc[:, dc:2 * dc], cos, sin)
    v = acc[:, 2 * dc:3 * dc]
    for ref, split_ref, val in ((q_ref, qs_ref, q), (k_ref, ks_ref, k), (v_ref, vs_ref, v)):
        ref[...] = val.astype(BF16)
        _store_split(split_ref, val)
    z_ref[...] = _silu(acc[:, 3 * dc:]).astype(BF16)


def _proj_rope(h, w, bias, cos_t, sin_t):
    b, l, d = h.shape
    n = w.shape[1]
    dc = n // 4
    t = 512
    tok = pl.BlockSpec((None, t, dc), lambda i, k: (i, k, 0))
    shp = jax.ShapeDtypeStruct((b, l, dc), BF16)
    spl = pl.BlockSpec((None, t // DIL_SPLIT, DIL_SPLIT * dc), lambda i, k: (i, k, 0))
    spl_shp = jax.ShapeDtypeStruct((b, l // DIL_SPLIT, DIL_SPLIT * dc), BF16)
    return pl.pallas_call(
        _proj_rope_body,
        grid=(b, l // t),
        in_specs=[
            pl.BlockSpec((None, t, d), lambda i, k: (i, k, 0)),
            pl.BlockSpec((d, n), lambda i, k: (0, 0)),
            pl.BlockSpec((1, n), lambda i, k: (0, 0)),
            pl.BlockSpec((t, dc), lambda i, k: (k, 0)),
            pl.BlockSpec((t, dc), lambda i, k: (k, 0)),
        ],
        out_specs=[tok, tok, tok, tok, spl, spl, spl],
        out_shape=[shp, shp, shp, shp, spl_shp, spl_shp, spl_shp],
        compiler_params=_params(("parallel", "parallel")),
        name="proj_rope",
    )(h, w, bias.reshape(1, n), cos_t, sin_t)


def _proj_hyena_body(h_ref, hp_ref, hn_ref, w_ref, b_ref, wz_ref, bz_ref, cw_ref, cb_ref, o_ref, pad_ref):
    part = pl.program_id(0)
    i = pl.program_id(2)
    last = pl.num_programs(2) - 1
    t = h_ref.shape[0]
    halo = hp_ref.shape[0]
    w = w_ref[...]
    acc = jnp.dot(h_ref[...], w, preferred_element_type=F32) + b_ref[...]
    prev = jnp.dot(hp_ref[...], w, preferred_element_type=F32)[halo - 1:halo] + b_ref[...]
    nxt = jnp.dot(hn_ref[...], w, preferred_element_type=F32)[0:1] + b_ref[...]
    pad_ref[7:8, :] = jnp.where(i > 0, prev, 0.0)
    pad_ref[8:8 + t, :] = acc
    pad_ref[8 + t:9 + t, :] = jnp.where(i < last, nxt, 0.0)
    uc = (pad_ref[7:7 + t, :] * cw_ref[0:1, :] + acc * cw_ref[1:2, :] + pad_ref[9:9 + t, :] * cw_ref[2:3, :]
          + cb_ref[...])

    @pl.when(part < 2)
    def _():
        o_ref[...] = uc.astype(BF16)

    @pl.when(part == 2)
    def _():
        az = jnp.dot(h_ref[...], wz_ref[...], preferred_element_type=F32) + bz_ref[...]
        o_ref[...] = (uc * _silu(az)).astype(BF16)


def _proj_hyena(h, w, bias, conv_w, conv_b):
    b, l, d = h.shape
    da = w.shape[1] // 4
    t, halo = 1024, 16
    nh = t // halo
    w3 = jnp.transpose(w[:, :3 * da].reshape(d, 3, da), (1, 0, 2))
    cw = jnp.transpose(conv_w.reshape(3, 3, da), (1, 0, 2))
    return pl.pallas_call(
        _proj_hyena_body,
        grid=(3, b, l // t),
        in_specs=[
            pl.BlockSpec((None, t, d), lambda j, i, k: (i, k, 0)),
            pl.BlockSpec((None, halo, d), lambda j, i, k: (i, jnp.maximum(k * nh - 1, 0), 0)),
            pl.BlockSpec((None, halo, d), lambda j, i, k: (i, jnp.minimum((k + 1) * nh, l // halo - 1), 0)),
            pl.BlockSpec((None, d, da), lambda j, i, k: (j, 0, 0)),
            pl.BlockSpec((None, 1, da), lambda j, i, k: (j, 0, 0)),
            pl.BlockSpec((d, da), lambda j, i, k: (0, 0)),
            pl.BlockSpec((1, da), lambda j, i, k: (0, 0)),
            pl.BlockSpec((None, 3, da), lambda j, i, k: (j, 0, 0)),
            pl.BlockSpec((None, 1, da), lambda j, i, k: (j, 0, 0)),
        ],
        out_specs=pl.BlockSpec((None, None, t, da), lambda j, i, k: (j, i, k, 0)),
        out_shape=jax.ShapeDtypeStruct((3, b, l, da), BF16),
        scratch_shapes=[pltpu.VMEM((t + 16, da), F32)],
        compiler_params=_params(("parallel", "parallel", "parallel")),
        name="proj_hyena",
    )(h, h, h, w3, bias[:3 * da].reshape(3, 1, da), w[:, 3 * da:], bias[3 * da:].reshape(1, da),
      cw, conv_b.reshape(3, 1, da))


def _filter_body(feat_ref, w1_ref, b1_ref, f0_ref, w2_ref, b2_ref, f1_ref, w3_ref, b3_ref, dec_ref,
                 hf_ref, sum_ref):
    i = pl.program_id(0)
    hp = lax.Precision.HIGHEST
    feat = feat_ref[...]
    t = feat[:, 0:1]
    h = jnp.sin(f0_ref[...] * (jnp.dot(feat, w1_ref[...], preferred_element_type=F32, precision=hp)
                               + b1_ref[...]))
    h = jnp.sin(f1_ref[...] * (jnp.dot(h, w2_ref[...], preferred_element_type=F32, precision=hp)
                               + b2_ref[...]))
    h = jnp.dot(h.astype(BF16), w3_ref[...], preferred_element_type=F32) + b3_ref[...]
    h = h * jnp.exp(-t * jnp.abs(dec_ref[...]))
    rows = lax.broadcasted_iota(jnp.int32, h.shape, 0) + i * h.shape[0]
    cols = lax.broadcasted_iota(jnp.int32, h.shape, 1)
    h = jnp.where((rows == 0) & (cols >= h.shape[1] // 2), 0.0, h)
    hf_ref[...] = h.astype(BF16)

    @pl.when(i == 0)
    def _():
        sum_ref[...] = jnp.zeros_like(sum_ref)

    sum_ref[...] += jnp.sum(jnp.abs(h), 0, keepdims=True)


def _filter_taps(l, w1, b1, freq, w2, b2, w3, b3, decay):
    fo = w1.shape[1]
    n = w3.shape[1]
    da = decay.shape[0]
    t = jnp.arange(l, dtype=F32) / l
    bands = jnp.arange(1, HYENA_BANDS + 1, dtype=F32)
    ang = 2.0 * math.pi * t[:, None] * bands[None, :]
    feat = jnp.concatenate([t[:, None], jnp.cos(ang), jnp.sin(ang)], -1)
    feat = jnp.pad(feat, ((0, 0), (0, FEAT_PAD - HYENA_EMB)))
    w1p = jnp.pad(w1, ((0, FEAT_PAD - HYENA_EMB), (0, 0)))
    dec = jnp.tile(decay, n // da).reshape(1, n)
    tt = 512
    const = lambda shape: pl.BlockSpec(shape, lambda i: (0,) * len(shape))
    return pl.pallas_call(
        _filter_body,
        grid=(l // tt,),
        in_specs=[
            pl.BlockSpec((tt, FEAT_PAD), lambda i: (i, 0)),
            const((FEAT_PAD, fo)), const((1, fo)), const((1, fo)),
            const((fo, fo)), const((1, fo)), const((1, fo)),
            const((fo, n)), const((1, n)), const((1, n)),
        ],
        out_specs=[pl.BlockSpec((tt, n), lambda i: (i, 0)), const((1, n))],
        out_shape=[jax.ShapeDtypeStruct((l, n), BF16), jax.ShapeDtypeStruct((1, n), F32)],
        compiler_params=_params(("arbitrary",)),
        name="filter_taps",
    )(feat, w1p, b1.reshape(1, fo), freq[0].reshape(1, fo), w2, b2.reshape(1, fo),
      freq[1].reshape(1, fo), w3.astype(BF16), b3.reshape(1, n), dec)


def _dft_tables(l):
    n = 2 * l
    n1 = n // DFT_N2
    kk = jnp.arange(n1 // 2, dtype=jnp.int32)
    nn = jnp.arange(n1 // 2, dtype=jnp.int32)
    th = (2.0 * math.pi / (2 * n1)) * (((2 * kk[:, None] + 1) * nn[None, :]) % (2 * n1)).astype(F32)
    eye2 = jnp.eye(2, dtype=F32)
    f1 = jnp.kron(jnp.concatenate([jnp.cos(th), -jnp.sin(th)], 0), eye2).astype(BF16)
    g = jnp.kron(jnp.concatenate([jnp.cos(th).T, -jnp.sin(th).T], 1) * (2.0 / n), eye2).astype(BF16)
    k2 = jnp.arange(DFT_N2, dtype=jnp.int32)
    n2 = jnp.arange(DFT_N2, dtype=jnp.int32)
    ph = (n2[None, None, :] * (k2[None, :, None] * (2 * n1) + 2 * kk[:, None, None] + 1)) % (2 * n)
    ang = (2.0 * math.pi / (2 * n)) * ph.astype(F32)
    c, s = jnp.cos(ang), jnp.sin(ang)
    mf = jnp.concatenate([jnp.concatenate([c, s], 2), jnp.concatenate([-s, c], 2)], 1).astype(BF16)
    mi = jnp.swapaxes(mf, 1, 2)
    return f1, g, mf, mi


def _load_pair_group(ref, lead, g):
    start = pl.multiple_of(g * PAIR_GROUP, PAIR_GROUP)
    words = ref.bitcast(jnp.uint32)[(*lead, slice(None), pl.ds(start, PAIR_GROUP), slice(None))]
    words = jnp.swapaxes(words, 0, 1)
    return [pltpu.bitcast(words[i], BF16) for i in range(PAIR_GROUP)]


def _store_pair_group(ref, lead, g, vals):
    start = pl.multiple_of(g * PAIR_GROUP, PAIR_GROUP)
    words = jnp.stack([pltpu.bitcast(v, jnp.uint32) for v in vals], 0)
    ref.bitcast(jnp.uint32)[(*lead, slice(None), pl.ds(start, PAIR_GROUP), slice(None))] = (
        jnp.swapaxes(words, 0, 1))


def _dft_in_body(f_ref, z_ref, a_ref):
    n1 = a_ref.shape[2]
    cb = a_ref.shape[-1]

    def group(g, carry):
        zcat = jnp.concatenate(_load_pair_group(z_ref, (0,), g), 1)
        r = jnp.dot(f_ref[...], zcat, preferred_element_type=F32).astype(BF16)
        cols = [r[:, i * cb:(i + 1) * cb] for i in range(PAIR_GROUP)]
        _store_pair_group(a_ref, (0, 0), g, [c[:2 * n1] for c in cols])
        _store_pair_group(a_ref, (0, 1), g, [c[2 * n1:] for c in cols])
        return carry

    lax.fori_loop(0, z_ref.shape[2] // (2 * PAIR_GROUP), group, 0)


def _dft_in(f1p, z):
    b, half, n2, c = z.shape
    n1 = half
    cb, rs = V7X_LANES, DFT_STEP_ROWS
    return pl.pallas_call(
        _dft_in_body,
        grid=(b, c // cb, n2 // rs),
        in_specs=[
            pl.BlockSpec(f1p.shape, lambda i, j, s: (0, 0)),
            pl.BlockSpec((1, half, rs, cb), lambda i, j, s: (i, 0, s, j)),
        ],
        out_specs=pl.BlockSpec((1, 2, n1, rs, cb), lambda i, j, s: (i, 0, 0, s, j)),
        out_shape=jax.ShapeDtypeStruct((b, 2, n1, n2, c), BF16),
        compiler_params=_params(("parallel", "parallel", "parallel")),
        name="dft_in",
    )(f1p, z)


def _dft_out_body(g_ref, b_ref, x_ref, z_ref, skip_ref, o_ref):
    cb = o_ref.shape[-1]

    def group(g, carry):
        re, im = _load_pair_group(b_ref, (0, 0), g), _load_pair_group(b_ref, (0, 1), g)
        bcat = jnp.concatenate([jnp.concatenate([r, i], 0) for r, i in zip(re, im)], 1)
        y = jnp.dot(g_ref[...], bcat, preferred_element_type=F32)
        xs, zs = _load_pair_group(x_ref, (0,), g), _load_pair_group(z_ref, (0,), g)
        outs = [(xs[i].astype(F32) * (y[:, i * cb:(i + 1) * cb] + skip_ref[...] * zs[i].astype(F32))
                 ).astype(BF16) for i in range(PAIR_GROUP)]
        _store_pair_group(o_ref, (0,), g, outs)
        return carry

    lax.fori_loop(0, x_ref.shape[2] // (2 * PAIR_GROUP), group, 0)


def _dft_out(gp, bb, x, z, skip):
    b, half, n2, c = z.shape
    n1 = half
    cb, rs = V7X_LANES, DFT_STEP_ROWS
    slab = pl.BlockSpec((1, half, rs, cb), lambda i, j, s: (i, 0, s, j))
    return pl.pallas_call(
        _dft_out_body,
        grid=(b, c // cb, n2 // rs),
        in_specs=[
            pl.BlockSpec(gp.shape, lambda i, j, s: (0, 0)),
            pl.BlockSpec((1, 2, n1, rs, cb), lambda i, j, s: (i, 0, 0, s, j)),
            slab, slab,
            pl.BlockSpec((1, cb), lambda i, j, s: (0, j)),
        ],
        out_specs=slab,
        out_shape=jax.ShapeDtypeStruct(z.shape, BF16),
        compiler_params=_params(("parallel", "parallel", "parallel")),
        name="dft_out",
    )(gp, bb, x, z, skip.astype(F32).reshape(1, c))


def _slab_conv_body(mf_ref, mi_ref, k_ref, a_ref, o_ref):
    nb, _, ks, n2, c = a_ref.shape
    for kk in range(ks):
        kr, ki = k_ref[kk, 0].astype(F32), k_ref[kk, 1].astype(F32)
        for b in range(nb):
            a = a_ref[b, :, kk].reshape(2 * n2, c)
            x = jnp.dot(mf_ref[kk], a, preferred_element_type=F32)
            xr, xi = x[:n2], x[n2:]
            y = jnp.concatenate([xr * kr - xi * ki, xr * ki + xi * kr], 0).astype(BF16)
            out = jnp.dot(mi_ref[kk], y, preferred_element_type=F32)
            o_ref[b, :, kk] = out.astype(BF16).reshape(2, n2, c)


def _slab_conv(mf, mi, kspec, a, order):
    b, _, n1, n2, c = a.shape
    m = 2 * n2
    ks = max(1, SLABS_PER_STEP // b)
    return pl.pallas_call(
        _slab_conv_body,
        grid=(n1 // ks,),
        in_specs=[
            pl.BlockSpec((ks, m, m), lambda k: (k, 0, 0)),
            pl.BlockSpec((ks, m, m), lambda k: (k, 0, 0)),
            pl.BlockSpec((ks, 2, n2, c), lambda k: (k, 0, 0, order)),
            pl.BlockSpec((b, 2, ks, n2, c), lambda k: (0, 0, k, 0, 0)),
        ],
        out_specs=pl.BlockSpec((b, 2, ks, n2, c), lambda k: (0, 0, k, 0, 0)),
        out_shape=jax.ShapeDtypeStruct(a.shape, BF16),
        compiler_params=_params(("parallel",)),
        name="slab_conv",
    )(mf, mi, kspec, a)


def _slab_spec_body(mf_ref, sum_ref, a_ref, k_ref):
    n2, c = a_ref.shape[1], a_ref.shape[2]
    half = c // 2
    x = jnp.dot(mf_ref[...], a_ref[...].reshape(2 * n2, c), preferred_element_type=F32)
    inv = 1.0 / (sum_ref[:, :half] + sum_ref[:, half:] + 1e-6)
    k_ref[0] = ((x[:n2, :half] + x[:n2, half:]) * inv).astype(BF16)
    k_ref[1] = ((x[n2:, :half] - x[n2:, half:]) * inv).astype(BF16)


def _slab_spec(mf, sums, a):
    _, _, n1, n2, c = a.shape
    m = 2 * n2
    return pl.pallas_call(
        _slab_spec_body,
        grid=(n1,),
        in_specs=[
            pl.BlockSpec((None, m, m), lambda k: (k, 0, 0)),
            pl.BlockSpec((1, c), lambda k: (0, 0)),
            pl.BlockSpec((None, 2, None, n2, c), lambda k: (0, 0, k, 0, 0)),
        ],
        out_specs=pl.BlockSpec((None, 2, n2, c // 2), lambda k: (k, 0, 0, 0)),
        out_shape=jax.ShapeDtypeStruct((n1, 2, n2, c // 2), BF16),
        compiler_params=_params(("parallel",)),
        name="slab_spec",
    )(mf, sums, a)


def _hyena_spectrum(l, tables, w1, b1, freq, w2, b2, w3, b3, decay):
    f1p, _, mf, _ = tables
    taps, sums = _filter_taps(l, w1, b1, freq, w2, b2, w3, b3, decay)
    a = _dft_in(f1p, taps.reshape(1, l // DFT_N2, DFT_N2, taps.shape[1]))
    return _slab_spec(mf, sums, a)


def _hyena(v, x1, x2g, kspec, skip, tables):
    f1p, gp, mf, mi = tables
    b, l, c = v.shape
    slabs = lambda u: u.reshape(b, l // DFT_N2, DFT_N2, c)
    z = slabs(v)
    for order, xg in enumerate((slabs(x1), slabs(x2g))):
        bb = _slab_conv(mf, mi, kspec, _dft_in(f1p, z), order)
        z = _dft_out(gp, bb, xg, z, skip[order])
    return z.reshape(b, l, c)


def _head_masks(rows, heads):
    lane = lax.broadcasted_iota(jnp.int32, (rows, heads * HEAD_DIM), 1)
    return [(lane >= h * HEAD_DIM) & (lane < (h + 1) * HEAD_DIM) for h in range(heads)]


def _stack_heads(q, masks):
    zero = jnp.zeros_like(q)
    return jnp.concatenate([jnp.where(m, q, zero) for m in masks], 0)


def _unstack_heads(res, masks, rows):
    out = jnp.where(masks[0], res[:rows], 0.0)
    for h in range(1, len(masks)):
        out = out + jnp.where(masks[h], res[h * rows:(h + 1) * rows], 0.0)
    return out


def _na_bias_table(rpb):
    heads = rpb.shape[0]
    c = jnp.arange(GRID_W)
    col_start = jnp.clip(c - NA_KC // 2, 0, GRID_W - NA_KC)
    col_ok = (c[None, :] >= col_start[:, None]) & (c[None, :] < col_start[:, None] + NA_KC)
    dc = jnp.clip(c[None, :] - c[:, None], -(NA_KC - 1), NA_KC - 1) + NA_KC - 1
    onehot = (dc[:, :, None] == jnp.arange(2 * NA_KC - 1)[None, None, :]).astype(F32)
    tcol = jnp.einsum("hrj,qkj->hrqk", rpb.astype(F32), onehot, precision=lax.Precision.HIGHEST)
    per_idx = [tcol[:, NA_KR - 1 - idx:2 * NA_KR - 1 - idx] for idx in range(NA_KR)]
    bias = jnp.transpose(jnp.stack(per_idx, 0), (0, 1, 3, 2, 4))
    bias = jnp.where(col_ok[None, None, :, None, :], bias, NEG_INF)
    return bias.reshape(NA_KR, heads * GRID_W, NA_KR * GRID_W)


def _na_body(q_ref, k_ref, v_ref, z_ref, bias_ref, o_ref, *, rows, rb, heads):
    blk = pl.program_id(1)
    masks = _head_masks(GRID_W, heads)
    span = NA_KR * GRID_W

    def one_row(rr, carry):
        r = blk * rb + rr
        start = jnp.clip(r - NA_KR // 2, 0, rows - NA_KR)
        idx = r - start
        koff = pl.multiple_of(start * GRID_W, GRID_W)
        qoff = pl.multiple_of(rr * GRID_W, GRID_W)
        q = q_ref[pl.ds(qoff, GRID_W), :] * jnp.asarray(HEAD_DIM ** -0.5, BF16)
        kk = k_ref[pl.ds(koff, span), :]
        vv = v_ref[pl.ds(koff, span), :]
        s = lax.dot_general(_stack_heads(q, masks), kk, (((1,), (1,)), ((), ())),
                            preferred_element_type=F32)
        s = s + bias_ref[idx]
        m = jnp.max(s, -1, keepdims=True)
        e = jnp.exp(s - m)
        p = (e / jnp.sum(e, -1, keepdims=True)).astype(BF16)
        o = _unstack_heads(jnp.dot(p, vv, preferred_element_type=F32), masks, GRID_W)
        z = z_ref[pl.ds(qoff, GRID_W), :].astype(F32)
        o_ref[pl.ds(qoff, GRID_W), :] = (o * _silu(z)).astype(BF16)
        return carry

    lax.fori_loop(0, rb, one_row, 0, unroll=NA_ROW_UNROLL)


def _na(proj_b, bias):
    b, l, n = proj_b.shape
    db = n // 4
    heads = db // HEAD_DIM
    rows = l // GRID_W
    rb = 8
    t = rb * GRID_W
    return pl.pallas_call(
        functools.partial(_na_body, rows=rows, rb=rb, heads=heads),
        grid=(b, rows // rb),
        in_specs=[
            pl.BlockSpec((None, t, db), lambda i, j: (i, j, 0)),
            pl.BlockSpec((None, l, db), lambda i, j: (i, 0, 1)),
            pl.BlockSpec((None, l, db), lambda i, j: (i, 0, 2)),
            pl.BlockSpec((None, t, db), lambda i, j: (i, j, 3)),
            pl.BlockSpec(bias.shape, lambda i, j: (0, 0, 0)),
        ],
        out_specs=pl.BlockSpec((None, t, db), lambda i, j: (i, j, 0)),
        out_shape=jax.ShapeDtypeStruct((b, l, db), BF16),
        compiler_params=_params(("parallel", "parallel")),
        name="na",
    )(proj_b, proj_b, proj_b, proj_b, bias)


def _attend(q, kk, vv, valid, masks):
    rows = q.shape[0]
    s = lax.dot_general(_stack_heads(q, masks), kk, (((1,), (1,)), ((), ())), preferred_element_type=F32)
    s = jnp.where(valid, s, NEG_INF)
    m = jnp.max(s, -1, keepdims=True)
    e = jnp.exp(s - m)
    l = jnp.sum(e, -1, keepdims=True)
    p = (e / l).astype(BF16)
    o = _unstack_heads(jnp.dot(p, vv, preferred_element_type=F32), masks, rows)
    lse = _unstack_heads(jnp.broadcast_to(m + jnp.log(l), (s.shape[0], q.shape[1])), masks, rows)
    return o, lse


def _split_pattern(dilation, n, nchunks, q_ref, kbuf, vbuf, o_split, l_split, masks, heads):
    blk = DIL_BLK
    dc = heads * HEAD_DIM
    phases = DIL_SPLIT // dilation
    mq = blk // phases
    row = lax.broadcasted_iota(jnp.int32, (heads * blk, 3 * blk), 0) % blk
    col = lax.broadcasted_iota(jnp.int32, (heads * blk, 3 * blk), 1)
    qa, qm = row // mq, row % mq
    ka = sum((col >= a * 3 * mq).astype(jnp.int32) for a in range(1, phases)) if phases > 1 else 0
    km = col - ka * (3 * mq)
    band = jnp.abs(phases * (km - mq - qm) + (ka - qa)) <= blk
    for m0 in range(0, blk, mq):
        gm = km + (n * blk + m0 - mq)
        valid = band & (gm >= 0) & (gm < nchunks * blk)
        for j in range(dilation):
            lanes = [slice((j + dilation * a) * dc, (j + dilation * a + 1) * dc) for a in range(phases)]
            q = jnp.concatenate([q_ref[m0:m0 + mq, ln] for ln in lanes], 0)
            krows = slice(blk + m0 - mq, blk + m0 + 2 * mq)
            kk = jnp.concatenate([kbuf[krows, ln] for ln in lanes], 0)
            vv = jnp.concatenate([vbuf[krows, ln] for ln in lanes], 0)
            o, lse = _attend(q, kk, vv, valid, masks)
            for a in range(phases):
                o_split[j + dilation * a, m0:m0 + mq, :] = o[a * mq:(a + 1) * mq]
                l_split[j + dilation * a, m0:m0 + mq, :] = lse[a * mq:(a + 1) * mq]


def _dil_body(qn_ref, knp_ref, kn_ref, knn_ref, vnp_ref, vn_ref, vnn_ref,
              qs_ref, ksp_ref, ks_ref, ksn_ref, vsp_ref, vs_ref, vsn_ref, cz_ref, y_ref,
              kbn, vbn, kbs, vbs, o_split, l_split, o_all, l_all, *, heads, nchunks):
    n = pl.program_id(1)
    blk = DIL_BLK
    chunk = qn_ref.shape[0]
    per = chunk // blk
    masks = _head_masks(blk, heads)
    for buf, prev, cur, nxt, halo in ((kbn, knp_ref, kn_ref, knn_ref, blk), (vbn, vnp_ref, vn_ref, vnn_ref, blk),
                                      (kbs, ksp_ref, ks_ref, ksn_ref, blk), (vbs, vsp_ref, vs_ref, vsn_ref, blk)):
        main = cur.shape[0]
        buf[0:halo] = prev[...]
        buf[halo:halo + main] = cur[...]
        buf[halo + main:] = nxt[...]

    qi = lax.broadcasted_iota(jnp.int32, (heads * blk, 3 * blk), 0) % blk
    ki = lax.broadcasted_iota(jnp.int32, (heads * blk, 3 * blk), 1)
    band = jnp.abs(ki - blk - qi) <= blk

    def token_block(i, carry):
        g = n * per + i
        off = pl.multiple_of(i * blk, blk)
        valid = band & ((ki >= blk) | (g > 0)) & ((ki < 2 * blk) | (g < nchunks * per - 1))
        o, lse = _attend(qn_ref[pl.ds(off, blk), :], kbn[pl.ds(off, 3 * blk), :], vbn[pl.ds(off, 3 * blk), :],
                         valid, masks)
        o_all[0, pl.ds(off, blk), :] = o
        l_all[0, pl.ds(off, blk), :] = lse
        return carry

    lax.fori_loop(0, per, token_block, 0, unroll=8)

    for g, (_, dilation) in enumerate(DIL_PATTERNS):
        if dilation == 1:
            continue
        _split_pattern(dilation, n, nchunks, qs_ref, kbs, vbs, o_split, l_split, masks, heads)
        o_all[g] = jnp.swapaxes(o_split[...], 0, 1).reshape(chunk, heads * HEAD_DIM)
        l_all[g] = jnp.swapaxes(l_split[...], 0, 1).reshape(chunk, heads * HEAD_DIM)

    ls = [l_all[g] for g in range(len(DIL_PATTERNS))]
    m = functools.reduce(jnp.maximum, ls)
    es = [jnp.exp(l - m) for l in ls]
    den = functools.reduce(lambda a, b: a + b, es)
    o = functools.reduce(lambda a, b: a + b, [(e / den) * o_all[g] for g, e in enumerate(es)])
    y_ref[...] = (o * cz_ref[...].astype(F32)).astype(BF16)


def _dilated_mixture(qn, kn, vn, qs, ks, vs, cz):
    b, l, dc = qn.shape
    heads = dc // HEAD_DIM
    blk = DIL_BLK
    assert DIL_PATTERNS[0][1] == 1 and all(w == 2 * blk * d and DIL_SPLIT % d == 0 for w, d in DIL_PATTERNS)
    chunk = DIL_SPLIT * blk
    nchunks = l // chunk
    per = chunk // blk
    nat = pl.BlockSpec((None, chunk, dc), lambda i, n: (i, n, 0))
    nat_prev = pl.BlockSpec((None, blk, dc), lambda i, n: (i, jnp.maximum(n * per - 1, 0), 0))
    nat_next = pl.BlockSpec((None, blk, dc), lambda i, n: (i, jnp.minimum((n + 1) * per, nchunks * per - 1), 0))
    spl = pl.BlockSpec((None, blk, DIL_SPLIT * dc), lambda i, n: (i, n, 0))
    spl_prev = pl.BlockSpec((None, blk, DIL_SPLIT * dc), lambda i, n: (i, jnp.maximum(n - 1, 0), 0))
    spl_next = pl.BlockSpec((None, blk, DIL_SPLIT * dc), lambda i, n: (i, jnp.minimum(n + 1, nchunks - 1), 0))
    return pl.pallas_call(
        functools.partial(_dil_body, heads=heads, nchunks=nchunks),
        grid=(b, nchunks),
        in_specs=[nat, nat_prev, nat, nat_next, nat_prev, nat, nat_next,
                  spl, spl_prev, spl, spl_next, spl_prev, spl, spl_next, nat],
        out_specs=nat,
        out_shape=jax.ShapeDtypeStruct((b, l, dc), BF16),
        scratch_shapes=[pltpu.VMEM((chunk + 2 * blk, dc), BF16), pltpu.VMEM((chunk + 2 * blk, dc), BF16),
                        pltpu.VMEM((3 * blk, DIL_SPLIT * dc), BF16), pltpu.VMEM((3 * blk, DIL_SPLIT * dc), BF16),
                        pltpu.VMEM((DIL_SPLIT, blk, dc), F32), pltpu.VMEM((DIL_SPLIT, blk, dc), F32),
                        pltpu.VMEM((len(DIL_PATTERNS), chunk, dc), F32),
                        pltpu.VMEM((len(DIL_PATTERNS), chunk, dc), F32)],
        compiler_params=_params(("parallel", "parallel")),
        name="dilated",
    )(qn, kn, kn, kn, vn, vn, vn, qs, ks, ks, ks, vs, vs, vs, cz)


def _tail_body(x_ref, ya_ref, yb_ref, yc_ref, g_ref, gate_ref, wa_ref, wb_ref, wc_ref, wo_ref, lng_ref, lnb_ref,
               out_ref, *, alpha):
    d = x_ref.shape[-1]
    pa = jnp.dot(ya_ref[...], wa_ref[...], preferred_element_type=F32)
    pb = jnp.dot(yb_ref[...], wb_ref[...], preferred_element_type=F32)
    pc = jnp.dot(yc_ref[...], wc_ref[...], preferred_element_type=F32)
    g = g_ref[...].astype(F32)
    merged = g[:, :d] * pa + g[:, d:2 * d] * pb + g[:, 2 * d:] * pc
    sub = jnp.dot(merged.astype(BF16), wo_ref[...], preferred_element_type=F32) * gate_ref[...]
    res = alpha * x_ref[...] + sub
    out_ref[...] = _layernorm(res) * lng_ref[...] + lnb_ref[...]


def _tail(x, ya, yb, yc, g_all, gate, wa, wb, wc, wo, ln_g, ln_b, alpha):
    b, l, d = x.shape
    t = 512
    tok = lambda w: pl.BlockSpec((None, t, w), lambda i, j: (i, j, 0))
    const = lambda a: pl.BlockSpec(a.shape, lambda i, j: (0,) * a.ndim)
    ln_g, ln_b = ln_g.reshape(1, d), ln_b.reshape(1, d)
    return pl.pallas_call(
        functools.partial(_tail_body, alpha=alpha),
        grid=(b, l // t),
        in_specs=[tok(d), tok(ya.shape[-1]), tok(yb.shape[-1]), tok(yc.shape[-1]), tok(3 * d),
                  pl.BlockSpec((None, 1, d), lambda i, j: (i, 0, 0)),
                  const(wa), const(wb), const(wc), const(wo), const(ln_g), const(ln_b)],
        out_specs=tok(d),
        out_shape=jax.ShapeDtypeStruct((b, l, d), F32),
        compiler_params=_params(("parallel", "parallel")),
        name="tail",
    )(x, ya, yb, yc, g_all, gate.reshape(b, 1, d), wa, wb, wc, wo, ln_g, ln_b)


def _rope_tables(l, heads):
    half = HEAD_DIM // 2
    inv = ROPE_THETA ** (-jnp.arange(half, dtype=F32) / half)
    ang = jnp.arange(l, dtype=F32)[:, None] * inv[None, :]
    cos, sin = jnp.cos(ang), jnp.sin(ang)
    return (jnp.tile(jnp.concatenate([cos, cos], -1), (1, heads)),
            jnp.tile(jnp.concatenate([-sin, sin], -1), (1, heads)))


def _layer(x, ada, lw, consts, alpha):
    d = x.shape[-1]
    shift, scale, gate = ada[:, :d], ada[:, d:2 * d], ada[:, 2 * d:]
    h = _ln_mod(x, scale, shift)
    w_in, b_in = lw["w_in"], lw["b_in"]
    v, x1, x2g = _proj_hyena(h, w_in[:, :2 * d], b_in[:2 * d], lw["conv_w"], lw["conv_b"])
    proj_b = _proj(h, w_in[:, 2 * d:3 * d], b_in[2 * d:3 * d])
    cq, ck, cv, cz, cqs, cks, cvs = _proj_rope(h, w_in[:, 3 * d:4 * d], b_in[3 * d:4 * d], *consts["rope"])
    gates = _proj(h, w_in[:, 4 * d:], b_in[4 * d:], sigmoid=True)

    ya = _hyena(v, x1, x2g, consts["kspec"], lw["skip"], consts["dft"])
    yb = _na(proj_b, lw["na_bias"])
    yc = _dilated_mixture(cq, ck, cv, cqs, cks, cvs, cz)
    return _tail(x, ya, yb, yc, gates, gate, lw["wa"], lw["wb"], lw["wc"], lw["wo"],
                 lw["ln_g"], lw["ln_b"], alpha)


def kernel(x_prompt, x_sample, c_prompt, c_sample, w_ada, b_ada, w_in, b_in, hy_conv_w, hy_conv_b, hy_w1, hy_b1, hy_freq, hy_w2, hy_b2, hy_w3, hy_b3, hy_decay, hy_skip, na_rpb, w_branch_a, w_branch_b, w_branch_c, w_out, ln_g, ln_b):
    depth, d, _ = w_in.shape
    heads_c = (d // 4) // HEAD_DIM
    alpha = (2 * depth) ** 0.25
    groups = [(x_prompt, c_prompt), (x_sample, c_sample)]

    nb_p = c_prompt.shape[0]
    c_all = jnp.concatenate([c_prompt, c_sample], 0)
    pad_rows = -c_all.shape[0] % 8
    ada_all = _ada(jnp.pad(c_all, ((0, pad_rows), (0, 0))), w_ada, b_ada)
    adas = [ada_all[:, :nb_p], ada_all[:, nb_p:nb_p + c_sample.shape[0]]]

    shared = {}
    for x, _ in groups:
        l = x.shape[1]
        if l not in shared:
            shared[l] = {"dft": _dft_tables(l), "rope": _rope_tables(l, heads_c)}

    ys = [x for x, _ in groups]
    for layer in range(depth):
        lw = {
            "w_in": w_in[layer].astype(BF16), "b_in": b_in[layer],
            "conv_w": hy_conv_w[layer], "conv_b": hy_conv_b[layer], "skip": hy_skip[layer],
            "na_bias": _na_bias_table(na_rpb[layer]),
            "wa": w_branch_a[layer].astype(BF16), "wb": w_branch_b[layer].astype(BF16),
            "wc": w_branch_c[layer].astype(BF16), "wo": w_out[layer].astype(BF16),
            "ln_g": ln_g[layer], "ln_b": ln_b[layer],
        }
        kspecs = {}
        for gi in range(len(groups)):
            l = ys[gi].shape[1]
            if l not in kspecs:
                kspecs[l] = _hyena_spectrum(l, shared[l]["dft"], hy_w1[layer], hy_b1[layer], hy_freq[layer],
                                            hy_w2[layer], hy_b2[layer], hy_w3[layer], hy_b3[layer],
                                            hy_decay[layer])
            consts = dict(shared[l], kspec=kspecs[l])
            ys[gi] = _layer(ys[gi], adas[gi][layer], lw, consts, alpha)
    return tuple(ys)
```

```python
import functools
import math

import jax
import jax.numpy as jnp
from jax import lax
from jax.experimental import pallas as pl
from jax.experimental.pallas import tpu as pltpu

F32 = jnp.float32
BF16 = jnp.bfloat16

GRID_W = 64
HEAD_DIM = 64
HYENA_BANDS = 16
HYENA_EMB = 2 * HYENA_BANDS + 1
NA_KR = 8
NA_KC = 16
DIL_PATTERNS = ((128, 1), (512, 4), (2048, 16))
DIL_BLK = 64
DIL_SPLIT = 16
NA_ROW_UNROLL = 4
ROPE_THETA = 10000.0
LN_EPS = 1e-5
NEG_INF = -1e30

V7X_LANES = 128
V7X_SUBLANES = 8
V7X_VMEM_LIMIT_BYTES = 56 * 1024 * 1024

DFT_N2 = V7X_LANES
PAIR_GROUP = V7X_SUBLANES
DFT_STEP_ROWS = 64
SLABS_PER_STEP = 8
FEAT_PAD = V7X_LANES


def _params(sem):
    return pltpu.CompilerParams(dimension_semantics=sem, vmem_limit_bytes=V7X_VMEM_LIMIT_BYTES)


def _sigmoid(x):
    return 1.0 / (1.0 + jnp.exp(-x))


def _silu(x):
    return x * _sigmoid(x)


def _ada_body(c_ref, w_ref, b_ref, o_ref):
    s = _silu(c_ref[...])
    o_ref[...] = jnp.dot(s, w_ref[...], preferred_element_type=F32,
                         precision=lax.Precision.HIGHEST) + b_ref[...]


def _ada(c_all, w_ada, b_ada):
    depth, d, n = w_ada.shape
    rows = c_all.shape[0]
    tn = 1024
    return pl.pallas_call(
        _ada_body,
        grid=(depth, n // tn),
        in_specs=[
            pl.BlockSpec((rows, d), lambda l, j: (0, 0)),
            pl.BlockSpec((None, d, tn), lambda l, j: (l, 0, j)),
            pl.BlockSpec((None, 1, tn), lambda l, j: (l, 0, j)),
        ],
        out_specs=pl.BlockSpec((None, rows, tn), lambda l, j: (l, 0, j)),
        out_shape=jax.ShapeDtypeStruct((depth, rows, n), F32),
        compiler_params=_params(("parallel", "parallel")),
        name="ada",
    )(c_all, w_ada, b_ada.reshape(depth, 1, n))


def _layernorm(x):
    mu = jnp.mean(x, -1, keepdims=True)
    xc = x - mu
    var = jnp.mean(xc * xc, -1, keepdims=True)
    return xc * lax.rsqrt(var + LN_EPS)


def _ln_mod_body(x_ref, sc_ref, sh_ref, o_ref):
    h = _layernorm(x_ref[...]) * (1.0 + sc_ref[...]) + sh_ref[...]
    o_ref[...] = h.astype(BF16)


def _ln_mod(x, scale, shift):
    b, l, d = x.shape
    t = 1024
    return pl.pallas_call(
        _ln_mod_body,
        grid=(b, l // t),
        in_specs=[
            pl.BlockSpec((None, t, d), lambda i, j: (i, j, 0)),
            pl.BlockSpec((None, 1, d), lambda i, j: (i, 0, 0)),
            pl.BlockSpec((None, 1, d), lambda i, j: (i, 0, 0)),
        ],
        out_specs=pl.BlockSpec((None, t, d), lambda i, j: (i, j, 0)),
        out_shape=jax.ShapeDtypeStruct((b, l, d), BF16),
        compiler_params=_params(("parallel", "parallel")),
        name="ln_mod",
    )(x, scale.reshape(b, 1, d), shift.reshape(b, 1, d))


def _proj_body(h_ref, w_ref, b_ref, o_ref, *, sigmoid):
    acc = jnp.dot(h_ref[...], w_ref[...], preferred_element_type=F32) + b_ref[...]
    if sigmoid:
        acc = _sigmoid(acc)
    o_ref[...] = acc.astype(o_ref.dtype)


def _proj(h, w, bias, sigmoid=False):
    b, l, d = h.shape
    n = w.shape[1]
    t, tn = 2048, 1024
    return pl.pallas_call(
        functools.partial(_proj_body, sigmoid=sigmoid),
        grid=(n // tn, b, l // t),
        in_specs=[
            pl.BlockSpec((None, t, d), lambda j, i, k: (i, k, 0)),
            pl.BlockSpec((d, tn), lambda j, i, k: (0, j)),
            pl.BlockSpec((1, tn), lambda j, i, k: (0, j)),
        ],
        out_specs=pl.BlockSpec((None, t, tn), lambda j, i, k: (i, k, j)),
        out_shape=jax.ShapeDtypeStruct((b, l, n), BF16),
        compiler_params=_params(("parallel", "parallel", "parallel")),
        name="proj",
    )(h, w, bias.reshape(1, n))


def _rope_lanes(x, cos, sin_signed):
    outs = []
    lane = lax.broadcasted_iota(jnp.int32, (x.shape[0], V7X_LANES), 1)
    first_half = (lane % HEAD_DIM) < (HEAD_DIM // 2)
    for c0 in range(0, x.shape[1], V7X_LANES):
        xc = x[:, c0:c0 + V7X_LANES]
        partner = jnp.where(first_half,
                            pltpu.roll(xc, V7X_LANES - HEAD_DIM // 2, 1),
                            pltpu.roll(xc, HEAD_DIM // 2, 1))
        outs.append(xc * cos[:, c0:c0 + V7X_LANES] + partner * sin_signed[:, c0:c0 + V7X_LANES])
    return jnp.concatenate(outs, 1)


def _store_split(ref, x):
    t, dc = x.shape
    parts = jnp.swapaxes(x.reshape(t // DIL_SPLIT, DIL_SPLIT, dc), 0, 1)
    for r in range(DIL_SPLIT):
        ref[:, r * dc:(r + 1) * dc] = parts[r].astype(BF16)


def _proj_rope_body(h_ref, w_ref, b_ref, cos_ref, sin_ref, q_ref, k_ref, v_ref, z_ref,
                    qs_ref, ks_ref, vs_ref):
    dc = q_ref.shape[-1]
    acc = jnp.dot(h_ref[...], w_ref[...], preferred_element_type=F32) + b_ref[...]
    cos, sin = cos_ref[...], sin_ref[...]
    q = _rope_lanes(acc[:, :dc], cos, sin) * (HEAD_DIM ** -0.5)
    k = _rope_lanes(acc[:, dc:2 * dc], cos, sin)
    v = acc[:, 2 * dc:3 * dc]
    for ref, split_ref, val in ((q_ref, qs_ref, q), (k_ref, ks_ref, k), (v_ref, vs_ref, v)):
        ref[...] = val.astype(BF16)
        _store_split(split_ref, val)
    z_ref[...] = _silu(acc[:, 3 * dc:]).astype(BF16)


def _proj_rope(h, w, bias, cos_t, sin_t):
    b, l, d = h.shape
    n = w.shape[1]
    dc = n // 4
    t = 512
    tok = pl.BlockSpec((None, t, dc), lambda i, k: (i, k, 0))
    shp = jax.ShapeDtypeStruct((b, l, dc), BF16)
    spl = pl.BlockSpec((None, t // DIL_SPLIT, DIL_SPLIT * dc), lambda i, k: (i, k, 0))
    spl_shp = jax.ShapeDtypeStruct((b, l // DIL_SPLIT, DIL_SPLIT * dc), BF16)
    return pl.pallas_call(
        _proj_rope_body,
        grid=(b, l // t),
        in_specs=[
            pl.BlockSpec((None, t, d), lambda i, k: (i, k, 0)),
            pl.BlockSpec((d, n), lambda i, k: (0, 0)),
            pl.BlockSpec((1, n), lambda i, k: (0, 0)),
            pl.BlockSpec((t, dc), lambda i, k: (k, 0)),
            pl.BlockSpec((t, dc), lambda i, k: (k, 0)),
        ],
        out_specs=[tok, tok, tok, tok, spl, spl, spl],
        out_shape=[shp, shp, shp, shp, spl_shp, spl_shp, spl_shp],
        compiler_params=_params(("parallel", "parallel")),
        name="proj_rope",
    )(h, w, bias.reshape(1, n), cos_t, sin_t)


def _hy_pre_body(main_ref, prev_ref, next_ref, cw_ref, cb_ref, v_ref, x1_ref, x2_ref, pad_ref, *, da):
    i = pl.program_id(1)
    last = pl.num_programs(1) - 1
    t = main_ref.shape[0]
    halo = prev_ref.shape[0]
    outs = (v_ref, x1_ref, x2_ref)
    az = main_ref[:, 3 * da:].astype(F32)
    gate = _silu(az)
    for part in range(3):
        cols = slice(part * da, (part + 1) * da)
        prev_row = jnp.where(i > 0, prev_ref[:, cols].astype(F32)[halo - 1:halo], 0.0)
        next_row = jnp.where(i < last, next_ref[:, cols].astype(F32)[0:1], 0.0)
        pad_ref[7:8, :] = prev_row
        pad_ref[8:8 + t, :] = main_ref[:, cols].astype(F32)
        pad_ref[8 + t:9 + t, :] = next_row
        uc = (pad_ref[7:7 + t, :] * cw_ref[0:1, cols] + pad_ref[8:8 + t, :] * cw_ref[1:2, cols]
              + pad_ref[9:9 + t, :] * cw_ref[2:3, cols] + cb_ref[:, cols])
        if part == 2:
            uc = uc * gate
        outs[part][...] = uc.astype(BF16)


def _hy_pre(proj_a, conv_w, conv_b):
    b, l, n = proj_a.shape
    da = n // 4
    t, halo = 1024, 16
    nh = t // halo
    tok = pl.BlockSpec((None, t, da), lambda i, j: (i, j, 0))
    shp = jax.ShapeDtypeStruct((b, l, da), BF16)
    return pl.pallas_call(
        functools.partial(_hy_pre_body, da=da),
        grid=(b, l // t),
        in_specs=[
            pl.BlockSpec((None, t, n), lambda i, j: (i, j, 0)),
            pl.BlockSpec((None, halo, n), lambda i, j: (i, jnp.maximum(j * nh - 1, 0), 0)),
            pl.BlockSpec((None, halo, n), lambda i, j: (i, jnp.minimum((j + 1) * nh, l // halo - 1), 0)),
            pl.BlockSpec((3, 3 * da), lambda i, j: (0, 0)),
            pl.BlockSpec((1, 3 * da), lambda i, j: (0, 0)),
        ],
        out_specs=[tok, tok, tok],
        out_shape=[shp, shp, shp],
        scratch_shapes=[pltpu.VMEM((t + 16, da), F32)],
        compiler_params=_params(("parallel", "parallel")),
        name="hy_pre",
    )(proj_a, proj_a, proj_a, conv_w, conv_b.reshape(1, 3 * da))


def _filter_body(feat_ref, w1_ref, b1_ref, f0_ref, w2_ref, b2_ref, f1_ref, w3_ref, b3_ref, dec_ref,
                 hf_ref, sum_ref):
    i = pl.program_id(0)
    hp = lax.Precision.HIGHEST
    feat = feat_ref[...]
    t = feat[:, 0:1]
    h = jnp.sin(f0_ref[...] * (jnp.dot(feat, w1_ref[...], preferred_element_type=F32, precision=hp)
                               + b1_ref[...]))
    h = jnp.sin(f1_ref[...] * (jnp.dot(h, w2_ref[...], preferred_element_type=F32, precision=hp)
                               + b2_ref[...]))
    h = jnp.dot(h.astype(BF16), w3_ref[...], preferred_element_type=F32) + b3_ref[...]
    h = h * jnp.exp(-t * jnp.abs(dec_ref[...]))
    rows = lax.broadcasted_iota(jnp.int32, h.shape, 0) + i * h.shape[0]
    cols = lax.broadcasted_iota(jnp.int32, h.shape, 1)
    h = jnp.where((rows == 0) & (cols >= h.shape[1] // 2), 0.0, h)
    hf_ref[...] = h.astype(BF16)

    @pl.when(i == 0)
    def _():
        sum_ref[...] = jnp.zeros_like(sum_ref)

    sum_ref[...] += jnp.sum(jnp.abs(h), 0, keepdims=True)


def _filter_taps(l, w1, b1, freq, w2, b2, w3, b3, decay):
    fo = w1.shape[1]
    n = w3.shape[1]
    da = decay.shape[0]
    t = jnp.arange(l, dtype=F32) / l
    bands = jnp.arange(1, HYENA_BANDS + 1, dtype=F32)
    ang = 2.0 * math.pi * t[:, None] * bands[None, :]
    feat = jnp.concatenate([t[:, None], jnp.cos(ang), jnp.sin(ang)], -1)
    feat = jnp.pad(feat, ((0, 0), (0, FEAT_PAD - HYENA_EMB)))
    w1p = jnp.pad(w1, ((0, FEAT_PAD - HYENA_EMB), (0, 0)))
    dec = jnp.tile(decay, n // da).reshape(1, n)
    tt = 512
    const = lambda shape: pl.BlockSpec(shape, lambda i: (0,) * len(shape))
    return pl.pallas_call(
        _filter_body,
        grid=(l // tt,),
        in_specs=[
            pl.BlockSpec((tt, FEAT_PAD), lambda i: (i, 0)),
            const((FEAT_PAD, fo)), const((1, fo)), const((1, fo)),
            const((fo, fo)), const((1, fo)), const((1, fo)),
            const((fo, n)), const((1, n)), const((1, n)),
        ],
        out_specs=[pl.BlockSpec((tt, n), lambda i: (i, 0)), const((1, n))],
        out_shape=[jax.ShapeDtypeStruct((l, n), BF16), jax.ShapeDtypeStruct((1, n), F32)],
        compiler_params=_params(("arbitrary",)),
        name="filter_taps",
    )(feat, w1p, b1.reshape(1, fo), freq[0].reshape(1, fo), w2, b2.reshape(1, fo),
      freq[1].reshape(1, fo), w3.astype(BF16), b3.reshape(1, n), dec)


def _dft_tables(l):
    n = 2 * l
    n1 = n // DFT_N2
    kk = jnp.arange(n1 // 2, dtype=jnp.int32)
    nn = jnp.arange(n1 // 2, dtype=jnp.int32)
    th = (2.0 * math.pi / (2 * n1)) * (((2 * kk[:, None] + 1) * nn[None, :]) % (2 * n1)).astype(F32)
    eye2 = jnp.eye(2, dtype=F32)
    f1 = jnp.kron(jnp.concatenate([jnp.cos(th), -jnp.sin(th)], 0), eye2).astype(BF16)
    g = jnp.kron(jnp.concatenate([jnp.cos(th).T, -jnp.sin(th).T], 1) * (2.0 / n), eye2).astype(BF16)
    k2 = jnp.arange(DFT_N2, dtype=jnp.int32)
    n2 = jnp.arange(DFT_N2, dtype=jnp.int32)
    ph = (n2[None, None, :] * (k2[None, :, None] * (2 * n1) + 2 * kk[:, None, None] + 1)) % (2 * n)
    ang = (2.0 * math.pi / (2 * n)) * ph.astype(F32)
    c, s = jnp.cos(ang), jnp.sin(ang)
    mf = jnp.concatenate([jnp.concatenate([c, s], 2), jnp.concatenate([-s, c], 2)], 1).astype(BF16)
    mi = jnp.swapaxes(mf, 1, 2)
    return f1, g, mf, mi


def _load_pair_group(ref, lead, g):
    start = pl.multiple_of(g * PAIR_GROUP, PAIR_GROUP)
    words = ref.bitcast(jnp.uint32)[(*lead, slice(None), pl.ds(start, PAIR_GROUP), slice(None))]
    words = jnp.swapaxes(words, 0, 1)
    return [pltpu.bitcast(words[i], BF16) for i in range(PAIR_GROUP)]


def _store_pair_group(ref, lead, g, vals):
    start = pl.multiple_of(g * PAIR_GROUP, PAIR_GROUP)
    words = jnp.stack([pltpu.bitcast(v, jnp.uint32) for v in vals], 0)
    ref.bitcast(jnp.uint32)[(*lead, slice(None), pl.ds(start, PAIR_GROUP), slice(None))] = (
        jnp.swapaxes(words, 0, 1))


def _dft_in_body(f_ref, z_ref, a_ref):
    n1 = a_ref.shape[2]
    cb = a_ref.shape[-1]

    def group(g, carry):
        zcat = jnp.concatenate(_load_pair_group(z_ref, (0,), g), 1)
        r = jnp.dot(f_ref[...], zcat, preferred_element_type=F32).astype(BF16)
        cols = [r[:, i * cb:(i + 1) * cb] for i in range(PAIR_GROUP)]
        _store_pair_group(a_ref, (0, 0), g, [c[:2 * n1] for c in cols])
        _store_pair_group(a_ref, (0, 1), g, [c[2 * n1:] for c in cols])
        return carry

    lax.fori_loop(0, z_ref.shape[2] // (2 * PAIR_GROUP), group, 0)


def _dft_in(f1p, z):
    b, half, n2, c = z.shape
    n1 = half
    cb, rs = V7X_LANES, DFT_STEP_ROWS
    return pl.pallas_call(
        _dft_in_body,
        grid=(b, c // cb, n2 // rs),
        in_specs=[
            pl.BlockSpec(f1p.shape, lambda i, j, s: (0, 0)),
            pl.BlockSpec((1, half, rs, cb), lambda i, j, s: (i, 0, s, j)),
        ],
        out_specs=pl.BlockSpec((1, 2, n1, rs, cb), lambda i, j, s: (i, 0, 0, s, j)),
        out_shape=jax.ShapeDtypeStruct((b, 2, n1, n2, c), BF16),
        compiler_params=_params(("parallel", "parallel", "parallel")),
        name="dft_in",
    )(f1p, z)


def _dft_out_body(g_ref, b_ref, x_ref, z_ref, skip_ref, o_ref):
    cb = o_ref.shape[-1]

    def group(g, carry):
        re, im = _load_pair_group(b_ref, (0, 0), g), _load_pair_group(b_ref, (0, 1), g)
        bcat = jnp.concatenate([jnp.concatenate([r, i], 0) for r, i in zip(re, im)], 1)
        y = jnp.dot(g_ref[...], bcat, preferred_element_type=F32)
        xs, zs = _load_pair_group(x_ref, (0,), g), _load_pair_group(z_ref, (0,), g)
        outs = [(xs[i].astype(F32) * (y[:, i * cb:(i + 1) * cb] + skip_ref[...] * zs[i].astype(F32))
                 ).astype(BF16) for i in range(PAIR_GROUP)]
        _store_pair_group(o_ref, (0,), g, outs)
        return carry

    lax.fori_loop(0, x_ref.shape[2] // (2 * PAIR_GROUP), group, 0)


def _dft_out(gp, bb, x, z, skip):
    b, half, n2, c = z.shape
    n1 = half
    cb, rs = V7X_LANES, DFT_STEP_ROWS
    slab = pl.BlockSpec((1, half, rs, cb), lambda i, j, s: (i, 0, s, j))
    return pl.pallas_call(
        _dft_out_body,
        grid=(b, c // cb, n2 // rs),
        in_specs=[
            pl.BlockSpec(gp.shape, lambda i, j, s: (0, 0)),
            pl.BlockSpec((1, 2, n1, rs, cb), lambda i, j, s: (i, 0, 0, s, j)),
            slab, slab,
            pl.BlockSpec((1, cb), lambda i, j, s: (0, j)),
        ],
        out_specs=slab,
        out_shape=jax.ShapeDtypeStruct(z.shape, BF16),
        compiler_params=_params(("parallel", "parallel", "parallel")),
        name="dft_out",
    )(gp, bb, x, z, skip.astype(F32).reshape(1, c))


def _slab_conv_body(mf_ref, mi_ref, k_ref, a_ref, o_ref):
    nb, _, ks, n2, c = a_ref.shape
    for kk in range(ks):
        kr, ki = k_ref[kk, 0].astype(F32), k_ref[kk, 1].astype(F32)
        for b in range(nb):
            a = a_ref[b, :, kk].reshape(2 * n2, c)
            x = jnp.dot(mf_ref[kk], a, preferred_element_type=F32)
            xr, xi = x[:n2], x[n2:]
            y = jnp.concatenate([xr * kr - xi * ki, xr * ki + xi * kr], 0).astype(BF16)
            out = jnp.dot(mi_ref[kk], y, preferred_element_type=F32)
            o_ref[b, :, kk] = out.astype(BF16).reshape(2, n2, c)


def _slab_conv(mf, mi, kspec, a, order):
    b, _, n1, n2, c = a.shape
    m = 2 * n2
    ks = max(1, SLABS_PER_STEP // b)
    return pl.pallas_call(
        _slab_conv_body,
        grid=(n1 // ks,),
        in_specs=[
            pl.BlockSpec((ks, m, m), lambda k: (k, 0, 0)),
            pl.BlockSpec((ks, m, m), lambda k: (k, 0, 0)),
            pl.BlockSpec((ks, 2, n2, c), lambda k: (k, 0, 0, order)),
            pl.BlockSpec((b, 2, ks, n2, c), lambda k: (0, 0, k, 0, 0)),
        ],
        out_specs=pl.BlockSpec((b, 2, ks, n2, c), lambda k: (0, 0, k, 0, 0)),
        out_shape=jax.ShapeDtypeStruct(a.shape, BF16),
        compiler_params=_params(("parallel",)),
        name="slab_conv",
    )(mf, mi, kspec, a)


def _slab_spec_body(mf_ref, sum_ref, a_ref, k_ref):
    n2, c = a_ref.shape[1], a_ref.shape[2]
    half = c // 2
    x = jnp.dot(mf_ref[...], a_ref[...].reshape(2 * n2, c), preferred_element_type=F32)
    inv = 1.0 / (sum_ref[:, :half] + sum_ref[:, half:] + 1e-6)
    k_ref[0] = ((x[:n2, :half] + x[:n2, half:]) * inv).astype(BF16)
    k_ref[1] = ((x[n2:, :half] - x[n2:, half:]) * inv).astype(BF16)


def _slab_spec(mf, sums, a):
    _, _, n1, n2, c = a.shape
    m = 2 * n2
    return pl.pallas_call(
        _slab_spec_body,
        grid=(n1,),
        in_specs=[
            pl.BlockSpec((None, m, m), lambda k: (k, 0, 0)),
            pl.BlockSpec((1, c), lambda k: (0, 0)),
            pl.BlockSpec((None, 2, None, n2, c), lambda k: (0, 0, k, 0, 0)),
        ],
        out_specs=pl.BlockSpec((None, 2, n2, c // 2), lambda k: (k, 0, 0, 0)),
        out_shape=jax.ShapeDtypeStruct((n1, 2, n2, c // 2), BF16),
        compiler_params=_params(("parallel",)),
        name="slab_spec",
    )(mf, sums, a)


def _hyena_spectrum(l, tables, w1, b1, freq, w2, b2, w3, b3, decay):
    f1p, _, mf, _ = tables
    taps, sums = _filter_taps(l, w1, b1, freq, w2, b2, w3, b3, decay)
    a = _dft_in(f1p, taps.reshape(1, l // DFT_N2, DFT_N2, taps.shape[1]))
    return _slab_spec(mf, sums, a)


def _hyena(v, x1, x2g, kspec, skip, tables):
    f1p, gp, mf, mi = tables
    b, l, c = v.shape
    slabs = lambda u: u.reshape(b, l // DFT_N2, DFT_N2, c)
    z = slabs(v)
    for order, xg in enumerate((slabs(x1), slabs(x2g))):
        bb = _slab_conv(mf, mi, kspec, _dft_in(f1p, z), order)
        z = _dft_out(gp, bb, xg, z, skip[order])
    return z.reshape(b, l, c)


def _head_masks(rows, heads):
    lane = lax.broadcasted_iota(jnp.int32, (rows, heads * HEAD_DIM), 1)
    return [(lane >= h * HEAD_DIM) & (lane < (h + 1) * HEAD_DIM) for h in range(heads)]


def _stack_heads(q, masks):
    zero = jnp.zeros_like(q)
    return jnp.concatenate([jnp.where(m, q, zero) for m in masks], 0)


def _unstack_heads(res, masks, rows):
    out = jnp.where(masks[0], res[:rows], 0.0)
    for h in range(1, len(masks)):
        out = out + jnp.where(masks[h], res[h * rows:(h + 1) * rows], 0.0)
    return out


def _na_bias_table(rpb):
    heads = rpb.shape[0]
    c = jnp.arange(GRID_W)
    col_start = jnp.clip(c - NA_KC // 2, 0, GRID_W - NA_KC)
    col_ok = (c[None, :] >= col_start[:, None]) & (c[None, :] < col_start[:, None] + NA_KC)
    dc = jnp.clip(c[None, :] - c[:, None], -(NA_KC - 1), NA_KC - 1) + NA_KC - 1
    onehot = (dc[:, :, None] == jnp.arange(2 * NA_KC - 1)[None, None, :]).astype(F32)
    tcol = jnp.einsum("hrj,qkj->hrqk", rpb.astype(F32), onehot, precision=lax.Precision.HIGHEST)
    per_idx = [tcol[:, NA_KR - 1 - idx:2 * NA_KR - 1 - idx] for idx in range(NA_KR)]
    bias = jnp.transpose(jnp.stack(per_idx, 0), (0, 1, 3, 2, 4))
    bias = jnp.where(col_ok[None, None, :, None, :], bias, NEG_INF)
    return bias.reshape(NA_KR, heads * GRID_W, NA_KR * GRID_W)


def _na_body(q_ref, k_ref, v_ref, z_ref, bias_ref, o_ref, *, rows, rb, heads):
    blk = pl.program_id(1)
    masks = _head_masks(GRID_W, heads)
    span = NA_KR * GRID_W

    def one_row(rr, carry):
        r = blk * rb + rr
        start = jnp.clip(r - NA_KR // 2, 0, rows - NA_KR)
        idx = r - start
        koff = pl.multiple_of(start * GRID_W, GRID_W)
        qoff = pl.multiple_of(rr * GRID_W, GRID_W)
        q = q_ref[pl.ds(qoff, GRID_W), :] * jnp.asarray(HEAD_DIM ** -0.5, BF16)
        kk = k_ref[pl.ds(koff, span), :]
        vv = v_ref[pl.ds(koff, span), :]
        s = lax.dot_general(_stack_heads(q, masks), kk, (((1,), (1,)), ((), ())),
                            preferred_element_type=F32)
        s = s + bias_ref[idx]
        m = jnp.max(s, -1, keepdims=True)
        e = jnp.exp(s - m)
        p = (e / jnp.sum(e, -1, keepdims=True)).astype(BF16)
        o = _unstack_heads(jnp.dot(p, vv, preferred_element_type=F32), masks, GRID_W)
        z = z_ref[pl.ds(qoff, GRID_W), :].astype(F32)
        o_ref[pl.ds(qoff, GRID_W), :] = (o * _silu(z)).astype(BF16)
        return carry

    lax.fori_loop(0, rb, one_row, 0, unroll=NA_ROW_UNROLL)


def _na(proj_b, bias):
    b, l, n = proj_b.shape
    db = n // 4
    heads = db // HEAD_DIM
    rows = l // GRID_W
    rb = 16
    t = rb * GRID_W
    return pl.pallas_call(
        functools.partial(_na_body, rows=rows, rb=rb, heads=heads),
        grid=(b, rows // rb),
        in_specs=[
            pl.BlockSpec((None, t, db), lambda i, j: (i, j, 0)),
            pl.BlockSpec((None, l, db), lambda i, j: (i, 0, 1)),
            pl.BlockSpec((None, l, db), lambda i, j: (i, 0, 2)),
            pl.BlockSpec((None, t, db), lambda i, j: (i, j, 3)),
            pl.BlockSpec(bias.shape, lambda i, j: (0, 0, 0)),
        ],
        out_specs=pl.BlockSpec((None, t, db), lambda i, j: (i, j, 0)),
        out_shape=jax.ShapeDtypeStruct((b, l, db), BF16),
        compiler_params=_params(("parallel", "parallel")),
        name="na",
    )(proj_b, proj_b, proj_b, proj_b, bias)


def _attend(q, kk, vv, valid, masks):
    rows = q.shape[0]
    s = lax.dot_general(_stack_heads(q, masks), kk, (((1,), (1,)), ((), ())), preferred_element_type=F32)
    s = jnp.where(valid, s, NEG_INF)
    m = jnp.max(s, -1, keepdims=True)
    e = jnp.exp(s - m)
    l = jnp.sum(e, -1, keepdims=True)
    p = (e / l).astype(BF16)
    o = _unstack_heads(jnp.dot(p, vv, preferred_element_type=F32), masks, rows)
    lse = _unstack_heads(jnp.broadcast_to(m + jnp.log(l), (s.shape[0], q.shape[1])), masks, rows)
    return o, lse


def _split_pattern(dilation, n, nchunks, q_ref, kbuf, vbuf, o_split, l_split, masks, heads):
    blk = DIL_BLK
    dc = heads * HEAD_DIM
    phases = DIL_SPLIT // dilation
    mq = blk // phases
    row = lax.broadcasted_iota(jnp.int32, (heads * blk, 3 * blk), 0) % blk
    col = lax.broadcasted_iota(jnp.int32, (heads * blk, 3 * blk), 1)
    qa, qm = row // mq, row % mq
    ka = sum((col >= a * 3 * mq).astype(jnp.int32) for a in range(1, phases)) if phases > 1 else 0
    km = col - ka * (3 * mq)
    band = jnp.abs(phases * (km - mq - qm) + (ka - qa)) <= blk
    for m0 in range(0, blk, mq):
        gm = km + (n * blk + m0 - mq)
        valid = band & (gm >= 0) & (gm < nchunks * blk)
        for j in range(dilation):
            lanes = [slice((j + dilation * a) * dc, (j + dilation * a + 1) * dc) for a in range(phases)]
            q = jnp.concatenate([q_ref[m0:m0 + mq, ln] for ln in lanes], 0)
            krows = slice(blk + m0 - mq, blk + m0 + 2 * mq)
            kk = jnp.concatenate([kbuf[krows, ln] for ln in lanes], 0)
            vv = jnp.concatenate([vbuf[krows, ln] for ln in lanes], 0)
            o, lse = _attend(q, kk, vv, valid, masks)
            for a in range(phases):
                o_split[j + dilation * a, m0:m0 + mq, :] = o[a * mq:(a + 1) * mq]
                l_split[j + dilation * a, m0:m0 + mq, :] = lse[a * mq:(a + 1) * mq]


def _dil_body(qn_ref, knp_ref, kn_ref, knn_ref, vnp_ref, vn_ref, vnn_ref,
              qs_ref, ksp_ref, ks_ref, ksn_ref, vsp_ref, vs_ref, vsn_ref, cz_ref, y_ref,
              kbn, vbn, kbs, vbs, o_split, l_split, o_all, l_all, *, heads, nchunks):
    n = pl.program_id(1)
    blk = DIL_BLK
    chunk = qn_ref.shape[0]
    per = chunk // blk
    masks = _head_masks(blk, heads)
    for buf, prev, cur, nxt, halo in ((kbn, knp_ref, kn_ref, knn_ref, blk), (vbn, vnp_ref, vn_ref, vnn_ref, blk),
                                      (kbs, ksp_ref, ks_ref, ksn_ref, blk), (vbs, vsp_ref, vs_ref, vsn_ref, blk)):
        main = cur.shape[0]
        buf[0:halo] = prev[...]
        buf[halo:halo + main] = cur[...]
        buf[halo + main:] = nxt[...]

    qi = lax.broadcasted_iota(jnp.int32, (heads * blk, 3 * blk), 0) % blk
    ki = lax.broadcasted_iota(jnp.int32, (heads * blk, 3 * blk), 1)
    band = jnp.abs(ki - blk - qi) <= blk

    def token_block(i, carry):
        g = n * per + i
        off = pl.multiple_of(i * blk, blk)
        valid = band & ((ki >= blk) | (g > 0)) & ((ki < 2 * blk) | (g < nchunks * per - 1))
        o, lse = _attend(qn_ref[pl.ds(off, blk), :], kbn[pl.ds(off, 3 * blk), :], vbn[pl.ds(off, 3 * blk), :],
                         valid, masks)
        o_all[0, pl.ds(off, blk), :] = o
        l_all[0, pl.ds(off, blk), :] = lse
        return carry

    lax.fori_loop(0, per, token_block, 0, unroll=8)

    for g, (_, dilation) in enumerate(DIL_PATTERNS):
        if dilation == 1:
            continue
        _split_pattern(dilation, n, nchunks, qs_ref, kbs, vbs, o_split, l_split, masks, heads)
        o_all[g] = jnp.swapaxes(o_split[...], 0, 1).reshape(chunk, heads * HEAD_DIM)
        l_all[g] = jnp.swapaxes(l_split[...], 0, 1).reshape(chunk, heads * HEAD_DIM)

    ls = [l_all[g] for g in range(len(DIL_PATTERNS))]
    m = functools.reduce(jnp.maximum, ls)
    es = [jnp.exp(l - m) for l in ls]
    den = functools.reduce(lambda a, b: a + b, es)
    o = functools.reduce(lambda a, b: a + b, [(e / den) * o_all[g] for g, e in enumerate(es)])
    y_ref[...] = (o * cz_ref[...].astype(F32)).astype(BF16)


def _dilated_mixture(qn, kn, vn, qs, ks, vs, cz):
    b, l, dc = qn.shape
    heads = dc // HEAD_DIM
    blk = DIL_BLK
    assert DIL_PATTERNS[0][1] == 1 and all(w == 2 * blk * d and DIL_SPLIT % d == 0 for w, d in DIL_PATTERNS)
    chunk = DIL_SPLIT * blk
    nchunks = l // chunk
    per = chunk // blk
    nat = pl.BlockSpec((None, chunk, dc), lambda i, n: (i, n, 0))
    nat_prev = pl.BlockSpec((None, blk, dc), lambda i, n: (i, jnp.maximum(n * per - 1, 0), 0))
    nat_next = pl.BlockSpec((None, blk, dc), lambda i, n: (i, jnp.minimum((n + 1) * per, nchunks * per - 1), 0))
    spl = pl.BlockSpec((None, blk, DIL_SPLIT * dc), lambda i, n: (i, n, 0))
    spl_prev = pl.BlockSpec((None, blk, DIL_SPLIT * dc), lambda i, n: (i, jnp.maximum(n - 1, 0), 0))
    spl_next = pl.BlockSpec((None, blk, DIL_SPLIT * dc), lambda i, n: (i, jnp.minimum(n + 1, nchunks - 1), 0))
    return pl.pallas_call(
        functools.partial(_dil_body, heads=heads, nchunks=nchunks),
        grid=(b, nchunks),
        in_specs=[nat, nat_prev, nat, nat_next, nat_prev, nat, nat_next,
                  spl, spl_prev, spl, spl_next, spl_prev, spl, spl_next, nat],
        out_specs=nat,
        out_shape=jax.ShapeDtypeStruct((b, l, dc), BF16),
        scratch_shapes=[pltpu.VMEM((chunk + 2 * blk, dc), BF16), pltpu.VMEM((chunk + 2 * blk, dc), BF16),
                        pltpu.VMEM((3 * blk, DIL_SPLIT * dc), BF16), pltpu.VMEM((3 * blk, DIL_SPLIT * dc), BF16),
                        pltpu.VMEM((DIL_SPLIT, blk, dc), F32), pltpu.VMEM((DIL_SPLIT, blk, dc), F32),
                        pltpu.VMEM((len(DIL_PATTERNS), chunk, dc), F32),
                        pltpu.VMEM((len(DIL_PATTERNS), chunk, dc), F32)],
        compiler_params=_params(("parallel", "parallel")),
        name="dilated",
    )(qn, kn, kn, kn, vn, vn, vn, qs, ks, ks, ks, vs, vs, vs, cz)


def _tail_body(x_ref, ya_ref, yb_ref, yc_ref, g_ref, gate_ref, wa_ref, wb_ref, wc_ref, wo_ref, lng_ref, lnb_ref,
               out_ref, *, alpha):
    d = x_ref.shape[-1]
    pa = jnp.dot(ya_ref[...], wa_ref[...], preferred_element_type=F32)
    pb = jnp.dot(yb_ref[...], wb_ref[...], preferred_element_type=F32)
    pc = jnp.dot(yc_ref[...], wc_ref[...], preferred_element_type=F32)
    g = g_ref[...].astype(F32)
    merged = g[:, :d] * pa + g[:, d:2 * d] * pb + g[:, 2 * d:] * pc
    sub = jnp.dot(merged.astype(BF16), wo_ref[...], preferred_element_type=F32) * gate_ref[...]
    res = alpha * x_ref[...] + sub
    out_ref[...] = _layernorm(res) * lng_ref[...] + lnb_ref[...]


def _tail(x, ya, yb, yc, g_all, gate, wa, wb, wc, wo, ln_g, ln_b, alpha):
    b, l, d = x.shape
    t = 1024
    tok = lambda w: pl.BlockSpec((None, t, w), lambda i, j: (i, j, 0))
    const = lambda a: pl.BlockSpec(a.shape, lambda i, j: (0,) * a.ndim)
    ln_g, ln_b = ln_g.reshape(1, d), ln_b.reshape(1, d)
    return pl.pallas_call(
        functools.partial(_tail_body, alpha=alpha),
        grid=(b, l // t),
        in_specs=[tok(d), tok(ya.shape[-1]), tok(yb.shape[-1]), tok(yc.shape[-1]), tok(3 * d),
                  pl.BlockSpec((None, 1, d), lambda i, j: (i, 0, 0)),
                  const(wa), const(wb), const(wc), const(wo), const(ln_g), const(ln_b)],
        out_specs=tok(d),
        out_shape=jax.ShapeDtypeStruct((b, l, d), F32),
        compiler_params=_params(("parallel", "parallel")),
        name="tail",
    )(x, ya, yb, yc, g_all, gate.reshape(b, 1, d), wa, wb, wc, wo, ln_g, ln_b)


def _rope_tables(l, heads):
    half = HEAD_DIM // 2
    inv = ROPE_THETA ** (-jnp.arange(half, dtype=F32) / half)
    ang = jnp.arange(l, dtype=F32)[:, None] * inv[None, :]
    cos, sin = jnp.cos(ang), jnp.sin(ang)
    return (jnp.tile(jnp.concatenate([cos, cos], -1), (1, heads)),
            jnp.tile(jnp.concatenate([-sin, sin], -1), (1, heads)))


def _layer(x, ada, lw, consts, alpha):
    d = x.shape[-1]
    shift, scale, gate = ada[:, :d], ada[:, d:2 * d], ada[:, 2 * d:]
    h = _ln_mod(x, scale, shift)
    w_in, b_in = lw["w_in"], lw["b_in"]
    proj_a = _proj(h, w_in[:, :2 * d], b_in[:2 * d])
    proj_b = _proj(h, w_in[:, 2 * d:3 * d], b_in[2 * d:3 * d])
    cq, ck, cv, cz, cqs, cks, cvs = _proj_rope(h, w_in[:, 3 * d:4 * d], b_in[3 * d:4 * d], *consts["rope"])
    gates = _proj(h, w_in[:, 4 * d:], b_in[4 * d:], sigmoid=True)

    v, x1, x2g = _hy_pre(proj_a, lw["conv_w"], lw["conv_b"])
    ya = _hyena(v, x1, x2g, consts["kspec"], lw["skip"], consts["dft"])
    yb = _na(proj_b, lw["na_bias"])
    yc = _dilated_mixture(cq, ck, cv, cqs, cks, cvs, cz)
    return _tail(x, ya, yb, yc, gates, gate, lw["wa"], lw["wb"], lw["wc"], lw["wo"],
                 lw["ln_g"], lw["ln_b"], alpha)


def kernel(x_prompt, x_sample, c_prompt, c_sample, w_ada, b_ada, w_in, b_in, hy_conv_w, hy_conv_b, hy_w1, hy_b1, hy_freq, hy_w2, hy_b2, hy_w3, hy_b3, hy_decay, hy_skip, na_rpb, w_branch_a, w_branch_b, w_branch_c, w_out, ln_g, ln_b):
    depth, d, _ = w_in.shape
    heads_c = (d // 4) // HEAD_DIM
    alpha = (2 * depth) ** 0.25
    groups = [(x_prompt, c_prompt), (x_sample, c_sample)]

    nb_p = c_prompt.shape[0]
    c_all = jnp.concatenate([c_prompt, c_sample], 0)
    pad_rows = -c_all.shape[0] % 8
    ada_all = _ada(jnp.pad(c_all, ((0, pad_rows), (0, 0))), w_ada, b_ada)
    adas = [ada_all[:, :nb_p], ada_all[:, nb_p:nb_p + c_sample.shape[0]]]

    shared = {}
    for x, _ in groups:
        l = x.shape[1]
        if l not in shared:
            shared[l] = {"dft": _dft_tables(l), "rope": _rope_tables(l, heads_c)}

    ys = [x for x, _ in groups]
    for layer in range(depth):
        lw = {
            "w_in": w_in[layer].astype(BF16), "b_in": b_in[layer],
            "conv_w": hy_conv_w[layer], "conv_b": hy_conv_b[layer], "skip": hy_skip[layer],
            "na_bias": _na_bias_table(na_rpb[layer]),
            "wa": w_branch_a[layer].astype(BF16), "wb": w_branch_b[layer].astype(BF16),
            "wc": w_branch_c[layer].astype(BF16), "wo": w_out[layer].astype(BF16),
            "ln_g": ln_g[layer], "ln_b": ln_b[layer],
        }
        kspecs = {}
        for gi in range(len(groups)):
            l = ys[gi].shape[1]
            if l not in kspecs:
                kspecs[l] = _hyena_spectrum(l, shared[l]["dft"], hy_w1[layer], hy_b1[layer], hy_freq[layer],
                                            hy_w2[layer], hy_b2[layer], hy_w3[layer], hy_b3[layer],
                                            hy_decay[layer])
            consts = dict(shared[l], kspec=kspecs[l])
            ys[gi] = _layer(ys[gi], adas[gi][layer], lw, consts, alpha)
    return tuple(ys)
```

```python
import functools
import math

import jax
import jax.numpy as jnp
from jax import lax
from jax.experimental import pallas as pl
from jax.experimental.pallas import tpu as pltpu

F32 = jnp.float32
BF16 = jnp.bfloat16

GRID_W = 64
HEAD_DIM = 64
HYENA_BANDS = 16
HYENA_EMB = 2 * HYENA_BANDS + 1
NA_KR = 8
NA_KC = 16
DIL_PATTERNS = ((128, 1), (512, 4), (2048, 16))
DIL_BLK = 64
DIL_SPLIT = 16
NA_ROW_UNROLL = 4
ROPE_THETA = 10000.0
LN_EPS = 1e-5
NEG_INF = -1e30
LOG2_E = math.log2(math.e)
QK_SCALE_LOG2 = HEAD_DIM ** -0.5 * LOG2_E

V7X_LANES = 128
V7X_SUBLANES = 8
V7X_VMEM_LIMIT_BYTES = 56 * 1024 * 1024

DFT_N2 = V7X_LANES
PAIR_GROUP = V7X_SUBLANES
DFT_STEP_ROWS = 128
SLABS_PER_STEP = 8
FEAT_PAD = V7X_LANES


def _params(sem):
    return pltpu.CompilerParams(dimension_semantics=sem, vmem_limit_bytes=V7X_VMEM_LIMIT_BYTES)


def _sigmoid(x):
    return 1.0 / (1.0 + jnp.exp(-x))


def _silu(x):
    return x * _sigmoid(x)


def _ada_body(c_ref, w_ref, b_ref, o_ref):
    s = _silu(c_ref[...])
    o_ref[...] = jnp.dot(s, w_ref[...], preferred_element_type=F32,
                         precision=lax.Precision.HIGHEST) + b_ref[...]


def _ada(c_all, w_ada, b_ada):
    depth, d, n = w_ada.shape
    rows = c_all.shape[0]
    tn = 1024
    return pl.pallas_call(
        _ada_body,
        grid=(depth, n // tn),
        in_specs=[
            pl.BlockSpec((rows, d), lambda l, j: (0, 0)),
            pl.BlockSpec((None, d, tn), lambda l, j: (l, 0, j)),
            pl.BlockSpec((None, 1, tn), lambda l, j: (l, 0, j)),
        ],
        out_specs=pl.BlockSpec((None, rows, tn), lambda l, j: (l, 0, j)),
        out_shape=jax.ShapeDtypeStruct((depth, rows, n), F32),
        compiler_params=_params(("parallel", "parallel")),
        name="ada",
    )(c_all, w_ada, b_ada.reshape(depth, 1, n))


def _layernorm(x):
    mu = jnp.mean(x, -1, keepdims=True)
    xc = x - mu
    var = jnp.mean(xc * xc, -1, keepdims=True)
    return xc * lax.rsqrt(var + LN_EPS)


def _ln_mod_body(x_ref, sc_ref, sh_ref, o_ref):
    h = _layernorm(x_ref[...]) * (1.0 + sc_ref[...]) + sh_ref[...]
    o_ref[...] = h.astype(BF16)


def _ln_mod(x, scale, shift):
    b, l, d = x.shape
    t = 1024
    return pl.pallas_call(
        _ln_mod_body,
        grid=(b, l // t),
        in_specs=[
            pl.BlockSpec((None, t, d), lambda i, j: (i, j, 0)),
            pl.BlockSpec((None, 1, d), lambda i, j: (i, 0, 0)),
            pl.BlockSpec((None, 1, d), lambda i, j: (i, 0, 0)),
        ],
        out_specs=pl.BlockSpec((None, t, d), lambda i, j: (i, j, 0)),
        out_shape=jax.ShapeDtypeStruct((b, l, d), BF16),
        compiler_params=_params(("parallel", "parallel")),
        name="ln_mod",
    )(x, scale.reshape(b, 1, d), shift.reshape(b, 1, d))


def _proj_body(h_ref, w_ref, b_ref, sc_ref, o_ref, *, sigmoid):
    acc = (jnp.dot(h_ref[...], w_ref[...], preferred_element_type=F32) + b_ref[...]) * sc_ref[...]
    if sigmoid:
        acc = _sigmoid(acc)
    o_ref[...] = acc.astype(o_ref.dtype)


def _proj(h, w, bias, col_scale=None, sigmoid=False):
    b, l, d = h.shape
    n = w.shape[1]
    t, tn = 1024, 1024
    if col_scale is None:
        col_scale = jnp.ones((n,), F32)
    return pl.pallas_call(
        functools.partial(_proj_body, sigmoid=sigmoid),
        grid=(n // tn, b, l // t),
        in_specs=[
            pl.BlockSpec((None, t, d), lambda j, i, k: (i, k, 0)),
            pl.BlockSpec((d, tn), lambda j, i, k: (0, j)),
            pl.BlockSpec((1, tn), lambda j, i, k: (0, j)),
            pl.BlockSpec((1, tn), lambda j, i, k: (0, j)),
        ],
        out_specs=pl.BlockSpec((None, t, tn), lambda j, i, k: (i, k, j)),
        out_shape=jax.ShapeDtypeStruct((b, l, n), BF16),
        compiler_params=_params(("parallel", "parallel", "parallel")),
        name="proj",
    )(h, w, bias.reshape(1, n), col_scale.reshape(1, n))


def _rope_lanes(x, cos, sin_signed):
    outs = []
    lane = lax.broadcasted_iota(jnp.int32, (x.shape[0], V7X_LANES), 1)
    first_half = (lane % HEAD_DIM) < (HEAD_DIM // 2)
    for c0 in range(0, x.shape[1], V7X_LANES):
        xc = x[:, c0:c0 + V7X_LANES]
        partner = jnp.where(first_half,
                            pltpu.roll(xc, V7X_LANES - HEAD_DIM // 2, 1),
                            pltpu.roll(xc, HEAD_DIM // 2, 1))
        outs.append(xc * cos[:, c0:c0 + V7X_LANES] + partner * sin_signed[:, c0:c0 + V7X_LANES])
    return jnp.concatenate(outs, 1)


def _store_split(ref, x):
    t, dc = x.shape
    parts = jnp.swapaxes(x.reshape(t // DIL_SPLIT, DIL_SPLIT, dc), 0, 1)
    for r in range(DIL_SPLIT):
        ref[:, r * dc:(r + 1) * dc] = parts[r].astype(BF16)


def _proj_rope_body(h_ref, w_ref, b_ref, cos_ref, sin_ref, q_ref, k_ref, v_ref, z_ref,
                    qs_ref, ks_ref, vs_ref):
    dc = q_ref.shape[-1]
    acc = jnp.dot(h_ref[...], w_ref[...], preferred_element_type=F32) + b_ref[...]
    cos, sin = cos_ref[...], sin_ref[...]
    q = _rope_lanes(acc[:, :dc], cos, sin) * QK_SCALE_LOG2
    k = _rope_lanes(acc[:, dc:2 * dc], cos, sin)
    v = acc[:, 2 * dc:3 * dc]
    for ref, split_ref, val in ((q_ref, qs_ref, q), (k_ref, ks_ref, k), (v_ref, vs_ref, v)):
        ref[...] = val.astype(BF16)
        _store_split(split_ref, val)
    z_ref[...] = _silu(acc[:, 3 * dc:]).astype(BF16)


def _proj_rope(h, w, bias, cos_t, sin_t):
    b, l, d = h.shape
    n = w.shape[1]
    dc = n // 4
    t = 512
    tok = pl.BlockSpec((None, t, dc), lambda i, k: (i, k, 0))
    shp = jax.ShapeDtypeStruct((b, l, dc), BF16)
    spl = pl.BlockSpec((None, t // DIL_SPLIT, DIL_SPLIT * dc), lambda i, k: (i, k, 0))
    spl_shp = jax.ShapeDtypeStruct((b, l // DIL_SPLIT, DIL_SPLIT * dc), BF16)
    return pl.pallas_call(
        _proj_rope_body,
        grid=(b, l // t),
        in_specs=[
            pl.BlockSpec((None, t, d), lambda i, k: (i, k, 0)),
            pl.BlockSpec((d, n), lambda i, k: (0, 0)),
            pl.BlockSpec((1, n), lambda i, k: (0, 0)),
            pl.BlockSpec((t, dc), lambda i, k: (k, 0)),
            pl.BlockSpec((t, dc), lambda i, k: (k, 0)),
        ],
        out_specs=[tok, tok, tok, tok, spl, spl, spl],
        out_shape=[shp, shp, shp, shp, spl_shp, spl_shp, spl_shp],
        compiler_params=_params(("parallel", "parallel")),
        name="proj_rope",
    )(h, w, bias.reshape(1, n), cos_t, sin_t)


def _hy_pre_body(main_ref, prev_ref, next_ref, cw_ref, cb_ref, v_ref, x1_ref, x2_ref, pad_ref, *, da):
    i = pl.program_id(1)
    last = pl.num_programs(1) - 1
    t = main_ref.shape[0]
    halo = prev_ref.shape[0]
    outs = (v_ref, x1_ref, x2_ref)
    az = main_ref[:, 3 * da:].astype(F32)
    gate = _silu(az)
    for part in range(3):
        cols = slice(part * da, (part + 1) * da)
        prev_row = jnp.where(i > 0, prev_ref[:, cols].astype(F32)[halo - 1:halo], 0.0)
        next_row = jnp.where(i < last, next_ref[:, cols].astype(F32)[0:1], 0.0)
        pad_ref[7:8, :] = prev_row
        pad_ref[8:8 + t, :] = main_ref[:, cols].astype(F32)
        pad_ref[8 + t:9 + t, :] = next_row
        uc = (pad_ref[7:7 + t, :] * cw_ref[0:1, cols] + pad_ref[8:8 + t, :] * cw_ref[1:2, cols]
              + pad_ref[9:9 + t, :] * cw_ref[2:3, cols] + cb_ref[:, cols])
        if part == 2:
            uc = uc * gate
        outs[part][...] = uc.astype(BF16)


def _hy_pre(proj_a, conv_w, conv_b):
    b, l, n = proj_a.shape
    da = n // 4
    t, halo = 1024, 16
    nh = t // halo
    tok = pl.BlockSpec((None, t, da), lambda i, j: (i, j, 0))
    shp = jax.ShapeDtypeStruct((b, l, da), BF16)
    return pl.pallas_call(
        functools.partial(_hy_pre_body, da=da),
        grid=(b, l // t),
        in_specs=[
            pl.BlockSpec((None, t, n), lambda i, j: (i, j, 0)),
            pl.BlockSpec((None, halo, n), lambda i, j: (i, jnp.maximum(j * nh - 1, 0), 0)),
            pl.BlockSpec((None, halo, n), lambda i, j: (i, jnp.minimum((j + 1) * nh, l // halo - 1), 0)),
            pl.BlockSpec((3, 3 * da), lambda i, j: (0, 0)),
            pl.BlockSpec((1, 3 * da), lambda i, j: (0, 0)),
        ],
        out_specs=[tok, tok, tok],
        out_shape=[shp, shp, shp],
        scratch_shapes=[pltpu.VMEM((t + 16, da), F32)],
        compiler_params=_params(("parallel", "parallel")),
        name="hy_pre",
    )(proj_a, proj_a, proj_a, conv_w, conv_b.reshape(1, 3 * da))


def _filter_body(feat_ref, w1_ref, b1_ref, f0_ref, w2_ref, b2_ref, f1_ref, w3_ref, b3_ref, dec_ref,
                 hf_ref, sum_ref):
    i = pl.program_id(0)
    hp = lax.Precision.HIGHEST
    feat = feat_ref[...]
    t = feat[:, 0:1]
    h = jnp.sin(f0_ref[...] * (jnp.dot(feat, w1_ref[...], preferred_element_type=F32, precision=hp)
                               + b1_ref[...]))
    h = jnp.sin(f1_ref[...] * (jnp.dot(h, w2_ref[...], preferred_element_type=F32, precision=hp)
                               + b2_ref[...]))
    h = jnp.dot(h.astype(BF16), w3_ref[...], preferred_element_type=F32) + b3_ref[...]
    h = h * jnp.exp(-t * jnp.abs(dec_ref[...]))
    rows = lax.broadcasted_iota(jnp.int32, h.shape, 0) + i * h.shape[0]
    cols = lax.broadcasted_iota(jnp.int32, h.shape, 1)
    h = jnp.where((rows == 0) & (cols >= h.shape[1] // 2), 0.0, h)
    hf_ref[...] = h.astype(BF16)

    @pl.when(i == 0)
    def _():
        sum_ref[...] = jnp.zeros_like(sum_ref)

    sum_ref[...] += jnp.sum(jnp.abs(h), 0, keepdims=True)


def _filter_taps(l, w1, b1, freq, w2, b2, w3, b3, decay):
    fo = w1.shape[1]
    n = w3.shape[1]
    da = decay.shape[0]
    t = jnp.arange(l, dtype=F32) / l
    bands = jnp.arange(1, HYENA_BANDS + 1, dtype=F32)
    ang = 2.0 * math.pi * t[:, None] * bands[None, :]
    feat = jnp.concatenate([t[:, None], jnp.cos(ang), jnp.sin(ang)], -1)
    feat = jnp.pad(feat, ((0, 0), (0, FEAT_PAD - HYENA_EMB)))
    w1p = jnp.pad(w1, ((0, FEAT_PAD - HYENA_EMB), (0, 0)))
    dec = jnp.tile(decay, n // da).reshape(1, n)
    tt = 512
    const = lambda shape: pl.BlockSpec(shape, lambda i: (0,) * len(shape))
    return pl.pallas_call(
        _filter_body,
        grid=(l // tt,),
        in_specs=[
            pl.BlockSpec((tt, FEAT_PAD), lambda i: (i, 0)),
            const((FEAT_PAD, fo)), const((1, fo)), const((1, fo)),
            const((fo, fo)), const((1, fo)), const((1, fo)),
            const((fo, n)), const((1, n)), const((1, n)),
        ],
        out_specs=[pl.BlockSpec((tt, n), lambda i: (i, 0)), const((1, n))],
        out_shape=[jax.ShapeDtypeStruct((l, n), BF16), jax.ShapeDtypeStruct((1, n), F32)],
        compiler_params=_params(("arbitrary",)),
        name="filter_taps",
    )(feat, w1p, b1.reshape(1, fo), freq[0].reshape(1, fo), w2, b2.reshape(1, fo),
      freq[1].reshape(1, fo), w3.astype(BF16), b3.reshape(1, n), dec)


def _dft_tables(l):
    n = 2 * l
    n1 = n // DFT_N2
    kk = jnp.arange(n1 // 2, dtype=jnp.int32)
    nn = jnp.arange(n1 // 2, dtype=jnp.int32)
    th = (2.0 * math.pi / (2 * n1)) * (((2 * kk[:, None] + 1) * nn[None, :]) % (2 * n1)).astype(F32)
    eye2 = jnp.eye(2, dtype=F32)
    f1 = jnp.kron(jnp.concatenate([jnp.cos(th), -jnp.sin(th)], 0), eye2).astype(BF16)
    g = jnp.kron(jnp.concatenate([jnp.cos(th).T, -jnp.sin(th).T], 1) * (2.0 / n), eye2).astype(BF16)
    k2 = jnp.arange(DFT_N2, dtype=jnp.int32)
    n2 = jnp.arange(DFT_N2, dtype=jnp.int32)
    ph = (n2[None, None, :] * (k2[None, :, None] * (2 * n1) + 2 * kk[:, None, None] + 1)) % (2 * n)
    ang = (2.0 * math.pi / (2 * n)) * ph.astype(F32)
    c, s = jnp.cos(ang), jnp.sin(ang)
    mf = jnp.concatenate([jnp.concatenate([c, s], 2), jnp.concatenate([-s, c], 2)], 1).astype(BF16)
    mi = jnp.swapaxes(mf, 1, 2)
    return f1, g, mf, mi


def _load_pair_group(ref, lead, g):
    start = pl.multiple_of(g * PAIR_GROUP, PAIR_GROUP)
    words = ref.bitcast(jnp.uint32)[(*lead, slice(None), pl.ds(start, PAIR_GROUP), slice(None))]
    words = jnp.swapaxes(words, 0, 1)
    return [pltpu.bitcast(words[i], BF16) for i in range(PAIR_GROUP)]


def _store_pair_group(ref, lead, g, vals):
    start = pl.multiple_of(g * PAIR_GROUP, PAIR_GROUP)
    words = jnp.stack([pltpu.bitcast(v, jnp.uint32) for v in vals], 0)
    ref.bitcast(jnp.uint32)[(*lead, slice(None), pl.ds(start, PAIR_GROUP), slice(None))] = (
        jnp.swapaxes(words, 0, 1))


def _dft_in_body(f_ref, z_ref, a_ref):
    n1 = a_ref.shape[2]
    cb = a_ref.shape[-1]

    def group(g, carry):
        zcat = jnp.concatenate(_load_pair_group(z_ref, (0,), g), 1)
        r = jnp.dot(f_ref[...], zcat, preferred_element_type=F32).astype(BF16)
        cols = [r[:, i * cb:(i + 1) * cb] for i in range(PAIR_GROUP)]
        _store_pair_group(a_ref, (0, 0), g, [c[:2 * n1] for c in cols])
        _store_pair_group(a_ref, (0, 1), g, [c[2 * n1:] for c in cols])
        return carry

    lax.fori_loop(0, z_ref.shape[2] // (2 * PAIR_GROUP), group, 0)


def _dft_in(f1p, z):
    b, half, n2, c = z.shape
    n1 = half
    cb, rs = V7X_LANES, DFT_STEP_ROWS
    return pl.pallas_call(
        _dft_in_body,
        grid=(b, c // cb, n2 // rs),
        in_specs=[
            pl.BlockSpec(f1p.shape, lambda i, j, s: (0, 0)),
            pl.BlockSpec((1, half, rs, cb), lambda i, j, s: (i, 0, s, j)),
        ],
        out_specs=pl.BlockSpec((1, 2, n1, rs, cb), lambda i, j, s: (i, 0, 0, s, j)),
        out_shape=jax.ShapeDtypeStruct((b, 2, n1, n2, c), BF16),
        compiler_params=_params(("parallel", "parallel", "parallel")),
        name="dft_in",
    )(f1p, z)


def _dft_out_body(g_ref, b_ref, x_ref, z_ref, skip_ref, o_ref):
    cb = o_ref.shape[-1]

    def group(g, carry):
        re, im = _load_pair_group(b_ref, (0, 0), g), _load_pair_group(b_ref, (0, 1), g)
        bcat = jnp.concatenate([jnp.concatenate([r, i], 0) for r, i in zip(re, im)], 1)
        y = jnp.dot(g_ref[...], bcat, preferred_element_type=F32)
        xs, zs = _load_pair_group(x_ref, (0,), g), _load_pair_group(z_ref, (0,), g)
        outs = [(xs[i].astype(F32) * (y[:, i * cb:(i + 1) * cb] + skip_ref[...] * zs[i].astype(F32))
                 ).astype(BF16) for i in range(PAIR_GROUP)]
        _store_pair_group(o_ref, (0,), g, outs)
        return carry

    lax.fori_loop(0, x_ref.shape[2] // (2 * PAIR_GROUP), group, 0)


def _dft_out(gp, bb, x, z, skip):
    b, half, n2, c = z.shape
    n1 = half
    cb, rs = V7X_LANES, DFT_STEP_ROWS
    slab = pl.BlockSpec((1, half, rs, cb), lambda i, j, s: (i, 0, s, j))
    return pl.pallas_call(
        _dft_out_body,
        grid=(b, c // cb, n2 // rs),
        in_specs=[
            pl.BlockSpec(gp.shape, lambda i, j, s: (0, 0)),
            pl.BlockSpec((1, 2, n1, rs, cb), lambda i, j, s: (i, 0, 0, s, j)),
            slab, slab,
            pl.BlockSpec((1, cb), lambda i, j, s: (0, j)),
        ],
        out_specs=slab,
        out_shape=jax.ShapeDtypeStruct(z.shape, BF16),
        compiler_params=_params(("parallel", "parallel", "parallel")),
        name="dft_out",
    )(gp, bb, x, z, skip.astype(F32).reshape(1, c))


def _slab_conv_body(mf_ref, mi_ref, k_ref, a_ref, o_ref):
    nb, _, ks, n2, c = a_ref.shape
    for kk in range(ks):
        kr, ki = k_ref[kk, 0].astype(F32), k_ref[kk, 1].astype(F32)
        for b in range(nb):
            a = a_ref[b, :, kk].reshape(2 * n2, c)
            x = jnp.dot(mf_ref[kk], a, preferred_element_type=F32)
            xr, xi = x[:n2], x[n2:]
            y = jnp.concatenate([xr * kr - xi * ki, xr * ki + xi * kr], 0).astype(BF16)
            out = jnp.dot(mi_ref[kk], y, preferred_element_type=F32)
            o_ref[b, :, kk] = out.astype(BF16).reshape(2, n2, c)


def _slab_conv(mf, mi, kspec, a, order):
    b, _, n1, n2, c = a.shape
    m = 2 * n2
    ks = max(1, SLABS_PER_STEP // b)
    return pl.pallas_call(
        _slab_conv_body,
        grid=(n1 // ks,),
        in_specs=[
            pl.BlockSpec((ks, m, m), lambda k: (k, 0, 0)),
            pl.BlockSpec((ks, m, m), lambda k: (k, 0, 0)),
            pl.BlockSpec((ks, 2, n2, c), lambda k: (k, 0, 0, order)),
            pl.BlockSpec((b, 2, ks, n2, c), lambda k: (0, 0, k, 0, 0)),
        ],
        out_specs=pl.BlockSpec((b, 2, ks, n2, c), lambda k: (0, 0, k, 0, 0)),
        out_shape=jax.ShapeDtypeStruct(a.shape, BF16),
        compiler_params=_params(("parallel",)),
        name="slab_conv",
    )(mf, mi, kspec, a)


def _slab_spec_body(mf_ref, sum_ref, a_ref, k_ref):
    n2, c = a_ref.shape[1], a_ref.shape[2]
    half = c // 2
    x = jnp.dot(mf_ref[...], a_ref[...].reshape(2 * n2, c), preferred_element_type=F32)
    inv = 1.0 / (sum_ref[:, :half] + sum_ref[:, half:] + 1e-6)
    k_ref[0] = ((x[:n2, :half] + x[:n2, half:]) * inv).astype(BF16)
    k_ref[1] = ((x[n2:, :half] - x[n2:, half:]) * inv).astype(BF16)


def _slab_spec(mf, sums, a):
    _, _, n1, n2, c = a.shape
    m = 2 * n2
    return pl.pallas_call(
        _slab_spec_body,
        grid=(n1,),
        in_specs=[
            pl.BlockSpec((None, m, m), lambda k: (k, 0, 0)),
            pl.BlockSpec((1, c), lambda k: (0, 0)),
            pl.BlockSpec((None, 2, None, n2, c), lambda k: (0, 0, k, 0, 0)),
        ],
        out_specs=pl.BlockSpec((None, 2, n2, c // 2), lambda k: (k, 0, 0, 0)),
        out_shape=jax.ShapeDtypeStruct((n1, 2, n2, c // 2), BF16),
        compiler_params=_params(("parallel",)),
        name="slab_spec",
    )(mf, sums, a)


def _hyena_spectrum(l, tables, w1, b1, freq, w2, b2, w3, b3, decay):
    f1p, _, mf, _ = tables
    taps, sums = _filter_taps(l, w1, b1, freq, w2, b2, w3, b3, decay)
    a = _dft_in(f1p, taps.reshape(1, l // DFT_N2, DFT_N2, taps.shape[1]))
    return _slab_spec(mf, sums, a)


def _hyena(v, x1, x2g, kspec, skip, tables):
    f1p, gp, mf, mi = tables
    b, l, c = v.shape
    slabs = lambda u: u.reshape(b, l // DFT_N2, DFT_N2, c)
    z = slabs(v)
    for order, xg in enumerate((slabs(x1), slabs(x2g))):
        bb = _slab_conv(mf, mi, kspec, _dft_in(f1p, z), order)
        z = _dft_out(gp, bb, xg, z, skip[order])
    return z.reshape(b, l, c)


def _head_masks(rows, heads):
    lane = lax.broadcasted_iota(jnp.int32, (rows, heads * HEAD_DIM), 1)
    return [(lane >= h * HEAD_DIM) & (lane < (h + 1) * HEAD_DIM) for h in range(heads)]


def _stack_heads(q, masks):
    zero = jnp.zeros_like(q)
    return jnp.concatenate([jnp.where(m, q, zero) for m in masks], 0)


def _unstack_heads(res, masks, rows):
    out = jnp.where(masks[0], res[:rows], 0.0)
    for h in range(1, len(masks)):
        out = out + jnp.where(masks[h], res[h * rows:(h + 1) * rows], 0.0)
    return out


def _na_bias_table(rpb):
    heads = rpb.shape[0]
    c = jnp.arange(GRID_W)
    col_start = jnp.clip(c - NA_KC // 2, 0, GRID_W - NA_KC)
    col_ok = (c[None, :] >= col_start[:, None]) & (c[None, :] < col_start[:, None] + NA_KC)
    dc = jnp.clip(c[None, :] - c[:, None], -(NA_KC - 1), NA_KC - 1) + NA_KC - 1
    onehot = (dc[:, :, None] == jnp.arange(2 * NA_KC - 1)[None, None, :]).astype(F32)
    tcol = jnp.einsum("hrj,qkj->hrqk", rpb.astype(F32), onehot, precision=lax.Precision.HIGHEST)
    per_idx = [tcol[:, NA_KR - 1 - idx:2 * NA_KR - 1 - idx] for idx in range(NA_KR)]
    bias = jnp.transpose(jnp.stack(per_idx, 0), (0, 1, 3, 2, 4))
    bias = jnp.where(col_ok[None, None, :, None, :], bias * LOG2_E, NEG_INF)
    return bias.reshape(NA_KR, heads * GRID_W, NA_KR * GRID_W)


def _na_body(q_ref, k_ref, v_ref, z_ref, bias_ref, o_ref, *, rows, rb, heads):
    blk = pl.program_id(1)
    masks = _head_masks(GRID_W, heads)
    span = NA_KR * GRID_W

    def one_row(rr, carry):
        r = blk * rb + rr
        start = jnp.clip(r - NA_KR // 2, 0, rows - NA_KR)
        idx = r - start
        koff = pl.multiple_of(start * GRID_W, GRID_W)
        qoff = pl.multiple_of(rr * GRID_W, GRID_W)
        q = q_ref[pl.ds(qoff, GRID_W), :]
        kk = k_ref[pl.ds(koff, span), :]
        vv = v_ref[pl.ds(koff, span), :]
        s = lax.dot_general(_stack_heads(q, masks), kk, (((1,), (1,)), ((), ())),
                            preferred_element_type=F32)
        s = s + bias_ref[idx]
        m = jnp.max(s, -1, keepdims=True)
        e = jnp.exp2(s - m)
        p = (e / jnp.sum(e, -1, keepdims=True)).astype(BF16)
        o = _unstack_heads(jnp.dot(p, vv, preferred_element_type=F32), masks, GRID_W)
        z = z_ref[pl.ds(qoff, GRID_W), :].astype(F32)
        o_ref[pl.ds(qoff, GRID_W), :] = (o * _silu(z)).astype(BF16)
        return carry

    lax.fori_loop(0, rb, one_row, 0, unroll=NA_ROW_UNROLL)


def _na(proj_b, bias):
    b, l, n = proj_b.shape
    db = n // 4
    heads = db // HEAD_DIM
    rows = l // GRID_W
    rb = 16
    t = rb * GRID_W
    return pl.pallas_call(
        functools.partial(_na_body, rows=rows, rb=rb, heads=heads),
        grid=(b, rows // rb),
        in_specs=[
            pl.BlockSpec((None, t, db), lambda i, j: (i, j, 0)),
            pl.BlockSpec((None, l, db), lambda i, j: (i, 0, 1)),
            pl.BlockSpec((None, l, db), lambda i, j: (i, 0, 2)),
            pl.BlockSpec((None, t, db), lambda i, j: (i, j, 3)),
            pl.BlockSpec(bias.shape, lambda i, j: (0, 0, 0)),
        ],
        out_specs=pl.BlockSpec((None, t, db), lambda i, j: (i, j, 0)),
        out_shape=jax.ShapeDtypeStruct((b, l, db), BF16),
        compiler_params=_params(("parallel", "parallel")),
        name="na",
    )(proj_b, proj_b, proj_b, proj_b, bias)


def _attend(q, kk, vv, valid, masks):
    rows = q.shape[0]
    s = lax.dot_general(_stack_heads(q, masks), kk, (((1,), (1,)), ((), ())), preferred_element_type=F32)
    s = jnp.where(valid, s, NEG_INF)
    m = jnp.max(s, -1, keepdims=True)
    e = jnp.exp2(s - m)
    l = jnp.sum(e, -1, keepdims=True)
    p = (e / l).astype(BF16)
    o = _unstack_heads(jnp.dot(p, vv, preferred_element_type=F32), masks, rows)
    lse = _unstack_heads(jnp.broadcast_to(m + jnp.log2(l), (s.shape[0], q.shape[1])), masks, rows)
    return o, lse


def _split_pattern(dilation, n, nchunks, q_ref, kbuf, vbuf, o_split, l_split, masks, heads):
    blk = DIL_BLK
    dc = heads * HEAD_DIM
    phases = DIL_SPLIT // dilation
    mq = blk // phases
    row = lax.broadcasted_iota(jnp.int32, (heads * blk, 3 * blk), 0) % blk
    col = lax.broadcasted_iota(jnp.int32, (heads * blk, 3 * blk), 1)
    qa, qm = row // mq, row % mq
    ka = sum((col >= a * 3 * mq).astype(jnp.int32) for a in range(1, phases)) if phases > 1 else 0
    km = col - ka * (3 * mq)
    band = jnp.abs(phases * (km - mq - qm) + (ka - qa)) <= blk
    for m0 in range(0, blk, mq):
        gm = km + (n * blk + m0 - mq)
        valid = band & (gm >= 0) & (gm < nchunks * blk)
        for j in range(dilation):
            lanes = [slice((j + dilation * a) * dc, (j + dilation * a + 1) * dc) for a in range(phases)]
            q = jnp.concatenate([q_ref[m0:m0 + mq, ln] for ln in lanes], 0)
            krows = slice(blk + m0 - mq, blk + m0 + 2 * mq)
            kk = jnp.concatenate([kbuf[krows, ln] for ln in lanes], 0)
            vv = jnp.concatenate([vbuf[krows, ln] for ln in lanes], 0)
            o, lse = _attend(q, kk, vv, valid, masks)
            for a in range(phases):
                o_split[j + dilation * a, m0:m0 + mq, :] = o[a * mq:(a + 1) * mq]
                l_split[j + dilation * a, m0:m0 + mq, :] = lse[a * mq:(a + 1) * mq]


def _dil_body(qn_ref, knp_ref, kn_ref, knn_ref, vnp_ref, vn_ref, vnn_ref,
              qs_ref, ksp_ref, ks_ref, ksn_ref, vsp_ref, vs_ref, vsn_ref, cz_ref, y_ref,
              kbn, vbn, kbs, vbs, o_split, l_split, o_all, l_all, *, heads, nchunks):
    n = pl.program_id(1)
    blk = DIL_BLK
    chunk = qn_ref.shape[0]
    per = chunk // blk
    masks = _head_masks(blk, heads)
    for buf, prev, cur, nxt, halo in ((kbn, knp_ref, kn_ref, knn_ref, blk), (vbn, vnp_ref, vn_ref, vnn_ref, blk),
                                      (kbs, ksp_ref, ks_ref, ksn_ref, blk), (vbs, vsp_ref, vs_ref, vsn_ref, blk)):
        main = cur.shape[0]
        buf[0:halo] = prev[...]
        buf[halo:halo + main] = cur[...]
        buf[halo + main:] = nxt[...]

    qi = lax.broadcasted_iota(jnp.int32, (heads * blk, 3 * blk), 0) % blk
    ki = lax.broadcasted_iota(jnp.int32, (heads * blk, 3 * blk), 1)
    band = jnp.abs(ki - blk - qi) <= blk

    def token_block(i, carry):
        g = n * per + i
        off = pl.multiple_of(i * blk, blk)
        valid = band & ((ki >= blk) | (g > 0)) & ((ki < 2 * blk) | (g < nchunks * per - 1))
        o, lse = _attend(qn_ref[pl.ds(off, blk), :], kbn[pl.ds(off, 3 * blk), :], vbn[pl.ds(off, 3 * blk), :],
                         valid, masks)
        o_all[0, pl.ds(off, blk), :] = o
        l_all[0, pl.ds(off, blk), :] = lse
        return carry

    lax.fori_loop(0, per, token_block, 0, unroll=8)

    for g, (_, dilation) in enumerate(DIL_PATTERNS):
        if dilation == 1:
            continue
        _split_pattern(dilation, n, nchunks, qs_ref, kbs, vbs, o_split, l_split, masks, heads)
        o_all[g] = jnp.swapaxes(o_split[...], 0, 1).reshape(chunk, heads * HEAD_DIM)
        l_all[g] = jnp.swapaxes(l_split[...], 0, 1).reshape(chunk, heads * HEAD_DIM)

    ls = [l_all[g] for g in range(len(DIL_PATTERNS))]
    m = functools.reduce(jnp.maximum, ls)
    es = [jnp.exp2(l - m) for l in ls]
    den = functools.reduce(lambda a, b: a + b, es)
    o = functools.reduce(lambda a, b: a + b, [(e / den) * o_all[g] for g, e in enumerate(es)])
    y_ref[...] = (o * cz_ref[...].astype(F32)).astype(BF16)


def _dilated_mixture(qn, kn, vn, qs, ks, vs, cz):
    b, l, dc = qn.shape
    heads = dc // HEAD_DIM
    blk = DIL_BLK
    assert DIL_PATTERNS[0][1] == 1 and all(w == 2 * blk * d and DIL_SPLIT % d == 0 for w, d in DIL_PATTERNS)
    chunk = DIL_SPLIT * blk
    nchunks = l // chunk
    per = chunk // blk
    nat = pl.BlockSpec((None, chunk, dc), lambda i, n: (i, n, 0))
    nat_prev = pl.BlockSpec((None, blk, dc), lambda i, n: (i, jnp.maximum(n * per - 1, 0), 0))
    nat_next = pl.BlockSpec((None, blk, dc), lambda i, n: (i, jnp.minimum((n + 1) * per, nchunks * per - 1), 0))
    spl = pl.BlockSpec((None, blk, DIL_SPLIT * dc), lambda i, n: (i, n, 0))
    spl_prev = pl.BlockSpec((None, blk, DIL_SPLIT * dc), lambda i, n: (i, jnp.maximum(n - 1, 0), 0))
    spl_next = pl.BlockSpec((None, blk, DIL_SPLIT * dc), lambda i, n: (i, jnp.minimum(n + 1, nchunks - 1), 0))
    return pl.pallas_call(
        functools.partial(_dil_body, heads=heads, nchunks=nchunks),
        grid=(b, nchunks),
        in_specs=[nat, nat_prev, nat, nat_next, nat_prev, nat, nat_next,
                  spl, spl_prev, spl, spl_next, spl_prev, spl, spl_next, nat],
        out_specs=nat,
        out_shape=jax.ShapeDtypeStruct((b, l, dc), BF16),
        scratch_shapes=[pltpu.VMEM((chunk + 2 * blk, dc), BF16), pltpu.VMEM((chunk + 2 * blk, dc), BF16),
                        pltpu.VMEM((3 * blk, DIL_SPLIT * dc), BF16), pltpu.VMEM((3 * blk, DIL_SPLIT * dc), BF16),
                        pltpu.VMEM((DIL_SPLIT, blk, dc), F32), pltpu.VMEM((DIL_SPLIT, blk, dc), F32),
                        pltpu.VMEM((len(DIL_PATTERNS), chunk, dc), F32),
                        pltpu.VMEM((len(DIL_PATTERNS), chunk, dc), F32)],
        compiler_params=_params(("parallel", "parallel")),
        name="dilated",
    )(qn, kn, kn, kn, vn, vn, vn, qs, ks, ks, ks, vs, vs, vs, cz)


def _tail_body(x_ref, ya_ref, yb_ref, yc_ref, g_ref, gate_ref, wa_ref, wb_ref, wc_ref, wo_ref, lng_ref, lnb_ref,
               out_ref, *, alpha):
    d = x_ref.shape[-1]
    pa = jnp.dot(ya_ref[...], wa_ref[...], preferred_element_type=F32)
    pb = jnp.dot(yb_ref[...], wb_ref[...], preferred_element_type=F32)
    pc = jnp.dot(yc_ref[...], wc_ref[...], preferred_element_type=F32)
    g = g_ref[...].astype(F32)
    merged = g[:, :d] * pa + g[:, d:2 * d] * pb + g[:, 2 * d:] * pc
    sub = jnp.dot(merged.astype(BF16), wo_ref[...], preferred_element_type=F32) * gate_ref[...]
    res = alpha * x_ref[...] + sub
    out_ref[...] = _layernorm(res) * lng_ref[...] + lnb_ref[...]


def _tail(x, ya, yb, yc, g_all, gate, wa, wb, wc, wo, ln_g, ln_b, alpha):
    b, l, d = x.shape
    t = 1024
    tok = lambda w: pl.BlockSpec((None, t, w), lambda i, j: (i, j, 0))
    const = lambda a: pl.BlockSpec(a.shape, lambda i, j: (0,) * a.ndim)
    ln_g, ln_b = ln_g.reshape(1, d), ln_b.reshape(1, d)
    return pl.pallas_call(
        functools.partial(_tail_body, alpha=alpha),
        grid=(b, l // t),
        in_specs=[tok(d), tok(ya.shape[-1]), tok(yb.shape[-1]), tok(yc.shape[-1]), tok(3 * d),
                  pl.BlockSpec((None, 1, d), lambda i, j: (i, 0, 0)),
                  const(wa), const(wb), const(wc), const(wo), const(ln_g), const(ln_b)],
        out_specs=tok(d),
        out_shape=jax.ShapeDtypeStruct((b, l, d), F32),
        compiler_params=_params(("parallel", "parallel")),
        name="tail",
    )(x, ya, yb, yc, g_all, gate.reshape(b, 1, d), wa, wb, wc, wo, ln_g, ln_b)


def _rope_tables(l, heads):
    half = HEAD_DIM // 2
    inv = ROPE_THETA ** (-jnp.arange(half, dtype=F32) / half)
    ang = jnp.arange(l, dtype=F32)[:, None] * inv[None, :]
    cos, sin = jnp.cos(ang), jnp.sin(ang)
    return (jnp.tile(jnp.concatenate([cos, cos], -1), (1, heads)),
            jnp.tile(jnp.concatenate([-sin, sin], -1), (1, heads)))


def _layer(x, ada, lw, consts, alpha):
    d = x.shape[-1]
    shift, scale, gate = ada[:, :d], ada[:, d:2 * d], ada[:, 2 * d:]
    h = _ln_mod(x, scale, shift)
    w_in, b_in = lw["w_in"], lw["b_in"]
    proj_a = _proj(h, w_in[:, :2 * d], b_in[:2 * d])
    q_cols = jnp.arange(d) < d // 4
    proj_b = _proj(h, w_in[:, 2 * d:3 * d], b_in[2 * d:3 * d], col_scale=jnp.where(q_cols, QK_SCALE_LOG2, 1.0))
    cq, ck, cv, cz, cqs, cks, cvs = _proj_rope(h, w_in[:, 3 * d:4 * d], b_in[3 * d:4 * d], *consts["rope"])
    gates = _proj(h, w_in[:, 4 * d:], b_in[4 * d:], sigmoid=True)

    v, x1, x2g = _hy_pre(proj_a, lw["conv_w"], lw["conv_b"])
    ya = _hyena(v, x1, x2g, consts["kspec"], lw["skip"], consts["dft"])
    yb = _na(proj_b, lw["na_bias"])
    yc = _dilated_mixture(cq, ck, cv, cqs, cks, cvs, cz)
    return _tail(x, ya, yb, yc, gates, gate, lw["wa"], lw["wb"], lw["wc"], lw["wo"],
                 lw["ln_g"], lw["ln_b"], alpha)


def kernel(x_prompt, x_sample, c_prompt, c_sample, w_ada, b_ada, w_in, b_in, hy_conv_w, hy_conv_b, hy_w1, hy_b1, hy_freq, hy_w2, hy_b2, hy_w3, hy_b3, hy_decay, hy_skip, na_rpb, w_branch_a, w_branch_b, w_branch_c, w_out, ln_g, ln_b):
    depth, d, _ = w_in.shape
    heads_c = (d // 4) // HEAD_DIM
    alpha = (2 * depth) ** 0.25
    groups = [(x_prompt, c_prompt), (x_sample, c_sample)]

    nb_p = c_prompt.shape[0]
    c_all = jnp.concatenate([c_prompt, c_sample], 0)
    pad_rows = -c_all.shape[0] % 8
    ada_all = _ada(jnp.pad(c_all, ((0, pad_rows), (0, 0))), w_ada, b_ada)
    adas = [ada_all[:, :nb_p], ada_all[:, nb_p:nb_p + c_sample.shape[0]]]

    shared = {}
    for x, _ in groups:
        l = x.shape[1]
        if l not in shared:
            shared[l] = {"dft": _dft_tables(l), "rope": _rope_tables(l, heads_c)}

    ys = [x for x, _ in groups]
    for layer in range(depth):
        lw = {
            "w_in": w_in[layer].astype(BF16), "b_in": b_in[layer],
            "conv_w": hy_conv_w[layer], "conv_b": hy_conv_b[layer], "skip": hy_skip[layer],
            "na_bias": _na_bias_table(na_rpb[layer]),
            "wa": w_branch_a[layer].astype(BF16), "wb": w_branch_b[layer].astype(BF16),
            "wc": w_branch_c[layer].astype(BF16), "wo": w_out[layer].astype(BF16),
            "ln_g": ln_g[layer], "ln_b": ln_b[layer],
        }
        kspecs = {}
        for gi in range(len(groups)):
            l = ys[gi].shape[1]
            if l not in kspecs:
                kspecs[l] = _hyena_spectrum(l, shared[l]["dft"], hy_w1[layer], hy_b1[layer], hy_freq[layer],
                                            hy_w2[layer], hy_b2[layer], hy_w3[layer], hy_b3[layer],
                                            hy_decay[layer])
            consts = dict(shared[l], kspec=kspecs[l])
            ys[gi] = _layer(ys[gi], adas[gi][layer], lw, consts, alpha)
    return tuple(ys)
```

```python
import functools
import math

import jax
import jax.numpy as jnp
from jax import lax
from jax.experimental import pallas as pl
from jax.experimental.pallas import tpu as pltpu

F32 = jnp.float32
BF16 = jnp.bfloat16

GRID_W = 64
HEAD_DIM = 64
HYENA_BANDS = 16
HYENA_EMB = 2 * HYENA_BANDS + 1
NA_KR = 8
NA_KC = 16
DIL_PATTERNS = ((128, 1), (512, 4), (2048, 16))
DIL_BLK = 64
DIL_SPLIT = 16
NA_ROW_UNROLL = 4
ROPE_THETA = 10000.0
LN_EPS = 1e-5
NEG_INF = -1e30
LOG2_E = math.log2(math.e)
QK_SCALE_LOG2 = HEAD_DIM ** -0.5 * LOG2_E

V7X_LANES = 128
V7X_SUBLANES = 8
V7X_VMEM_LIMIT_BYTES = 56 * 1024 * 1024

DFT_N2 = V7X_LANES
PAIR_GROUP = V7X_SUBLANES
DFT_STEP_ROWS = 128
SLABS_PER_STEP = 8
FEAT_PAD = V7X_LANES
FILTER_HEAD_ROWS = 2 * V7X_SUBLANES


def _params(sem):
    return pltpu.CompilerParams(dimension_semantics=sem, vmem_limit_bytes=V7X_VMEM_LIMIT_BYTES)


def _sigmoid(x):
    return 1.0 / (1.0 + jnp.exp(-x))


def _silu(x):
    return x * _sigmoid(x)


def _ada_body(c_ref, w_ref, b_ref, o_ref):
    s = _silu(c_ref[...])
    o_ref[...] = jnp.dot(s, w_ref[...], preferred_element_type=F32,
                         precision=lax.Precision.HIGHEST) + b_ref[...]


def _ada(c_all, w_ada, b_ada):
    depth, d, n = w_ada.shape
    rows = c_all.shape[0]
    tn = 1024
    return pl.pallas_call(
        _ada_body,
        grid=(depth, n // tn),
        in_specs=[
            pl.BlockSpec((rows, d), lambda l, j: (0, 0)),
            pl.BlockSpec((None, d, tn), lambda l, j: (l, 0, j)),
            pl.BlockSpec((None, 1, tn), lambda l, j: (l, 0, j)),
        ],
        out_specs=pl.BlockSpec((None, rows, tn), lambda l, j: (l, 0, j)),
        out_shape=jax.ShapeDtypeStruct((depth, rows, n), F32),
        compiler_params=_params(("parallel", "parallel")),
        name="ada",
    )(c_all, w_ada, b_ada.reshape(depth, 1, n))


def _layernorm(x):
    mu = jnp.mean(x, -1, keepdims=True)
    xc = x - mu
    var = jnp.mean(xc * xc, -1, keepdims=True)
    return xc * lax.rsqrt(var + LN_EPS)


def _ln_mod_body(x_ref, sc_ref, sh_ref, o_ref):
    h = _layernorm(x_ref[...]) * (1.0 + sc_ref[...]) + sh_ref[...]
    o_ref[...] = h.astype(BF16)


def _ln_mod(x, scale, shift):
    b, l, d = x.shape
    t = 1024
    return pl.pallas_call(
        _ln_mod_body,
        grid=(b, l // t),
        in_specs=[
            pl.BlockSpec((None, t, d), lambda i, j: (i, j, 0)),
            pl.BlockSpec((None, 1, d), lambda i, j: (i, 0, 0)),
            pl.BlockSpec((None, 1, d), lambda i, j: (i, 0, 0)),
        ],
        out_specs=pl.BlockSpec((None, t, d), lambda i, j: (i, j, 0)),
        out_shape=jax.ShapeDtypeStruct((b, l, d), BF16),
        compiler_params=_params(("parallel", "parallel")),
        name="ln_mod",
    )(x, scale.reshape(b, 1, d), shift.reshape(b, 1, d))


def _proj_body(h_ref, w_ref, b_ref, *rest, sigmoid, scaled):
    o_ref = rest[-1]
    acc = jnp.dot(h_ref[...], w_ref[...], preferred_element_type=F32) + b_ref[...]
    if scaled:
        acc = acc * rest[0][...]
    if sigmoid:
        acc = _sigmoid(acc)
    o_ref[...] = acc.astype(o_ref.dtype)


def _proj(h, w, bias, col_scale=None, sigmoid=False):
    b, l, d = h.shape
    n = w.shape[1]
    t, tn = 1024, 1024
    row = pl.BlockSpec((1, tn), lambda j, i, k: (0, j))
    scale_args = [] if col_scale is None else [col_scale.reshape(1, n)]
    return pl.pallas_call(
        functools.partial(_proj_body, sigmoid=sigmoid, scaled=col_scale is not None),
        grid=(n // tn, b, l // t),
        in_specs=[
            pl.BlockSpec((None, t, d), lambda j, i, k: (i, k, 0)),
            pl.BlockSpec((d, tn), lambda j, i, k: (0, j)),
            row,
        ] + [row] * len(scale_args),
        out_specs=pl.BlockSpec((None, t, tn), lambda j, i, k: (i, k, j)),
        out_shape=jax.ShapeDtypeStruct((b, l, n), BF16),
        compiler_params=_params(("parallel", "parallel", "parallel")),
        name="proj",
    )(h, w, bias.reshape(1, n), *scale_args)


def _rope_lanes(x, cos, sin_signed):
    outs = []
    lane = lax.broadcasted_iota(jnp.int32, (x.shape[0], V7X_LANES), 1)
    first_half = (lane % HEAD_DIM) < (HEAD_DIM // 2)
    for c0 in range(0, x.shape[1], V7X_LANES):
        xc = x[:, c0:c0 + V7X_LANES]
        partner = jnp.where(first_half,
                            pltpu.roll(xc, V7X_LANES - HEAD_DIM // 2, 1),
                            pltpu.roll(xc, HEAD_DIM // 2, 1))
        outs.append(xc * cos[:, c0:c0 + V7X_LANES] + partner * sin_signed[:, c0:c0 + V7X_LANES])
    return jnp.concatenate(outs, 1)


def _store_split(ref, x):
    t, dc = x.shape
    parts = jnp.swapaxes(x.reshape(t // DIL_SPLIT, DIL_SPLIT, dc), 0, 1)
    for r in range(DIL_SPLIT):
        ref[:, r * dc:(r + 1) * dc] = parts[r].astype(BF16)


def _proj_rope_body(h_ref, w_ref, b_ref, cos_ref, sin_ref, q_ref, k_ref, v_ref, z_ref,
                    qs_ref, ks_ref, vs_ref):
    dc = q_ref.shape[-1]
    acc = jnp.dot(h_ref[...], w_ref[...], preferred_element_type=F32) + b_ref[...]
    cos, sin = cos_ref[...], sin_ref[...]
    q = _rope_lanes(acc[:, :dc], cos, sin) * QK_SCALE_LOG2
    k = _rope_lanes(acc[:, dc:2 * dc], cos, sin)
    v = acc[:, 2 * dc:3 * dc]
    for ref, split_ref, val in ((q_ref, qs_ref, q), (k_ref, ks_ref, k), (v_ref, vs_ref, v)):
        ref[...] = val.astype(BF16)
        _store_split(split_ref, val)
    z_ref[...] = _silu(acc[:, 3 * dc:]).astype(BF16)


def _proj_rope(h, w, bias, cos_t, sin_t):
    b, l, d = h.shape
    n = w.shape[1]
    dc = n // 4
    t = 512
    tok = pl.BlockSpec((None, t, dc), lambda i, k: (i, k, 0))
    shp = jax.ShapeDtypeStruct((b, l, dc), BF16)
    spl = pl.BlockSpec((None, t // DIL_SPLIT, DIL_SPLIT * dc), lambda i, k: (i, k, 0))
    spl_shp = jax.ShapeDtypeStruct((b, l // DIL_SPLIT, DIL_SPLIT * dc), BF16)
    return pl.pallas_call(
        _proj_rope_body,
        grid=(b, l // t),
        in_specs=[
            pl.BlockSpec((None, t, d), lambda i, k: (i, k, 0)),
            pl.BlockSpec((d, n), lambda i, k: (0, 0)),
            pl.BlockSpec((1, n), lambda i, k: (0, 0)),
            pl.BlockSpec((t, dc), lambda i, k: (k, 0)),
            pl.BlockSpec((t, dc), lambda i, k: (k, 0)),
        ],
        out_specs=[tok, tok, tok, tok, spl, spl, spl],
        out_shape=[shp, shp, shp, shp, spl_shp, spl_shp, spl_shp],
        compiler_params=_params(("parallel", "parallel")),
        name="proj_rope",
    )(h, w, bias.reshape(1, n), cos_t, sin_t)


def _hy_pre_body(main_ref, prev_ref, next_ref, cw_ref, cb_ref, v_ref, x1_ref, x2_ref, pad_ref, *, da):
    i = pl.program_id(1)
    last = pl.num_programs(1) - 1
    t = main_ref.shape[0]
    halo = prev_ref.shape[0]
    outs = (v_ref, x1_ref, x2_ref)
    az = main_ref[:, 3 * da:].astype(F32)
    gate = _silu(az)
    for part in range(3):
        cols = slice(part * da, (part + 1) * da)
        prev_row = jnp.where(i > 0, prev_ref[:, cols].astype(F32)[halo - 1:halo], 0.0)
        next_row = jnp.where(i < last, next_ref[:, cols].astype(F32)[0:1], 0.0)
        pad_ref[7:8, :] = prev_row
        pad_ref[8:8 + t, :] = main_ref[:, cols].astype(F32)
        pad_ref[8 + t:9 + t, :] = next_row
        uc = (pad_ref[7:7 + t, :] * cw_ref[0:1, cols] + pad_ref[8:8 + t, :] * cw_ref[1:2, cols]
              + pad_ref[9:9 + t, :] * cw_ref[2:3, cols] + cb_ref[:, cols])
        if part == 2:
            uc = uc * gate
        outs[part][...] = uc.astype(BF16)


def _hy_pre(proj_a, conv_w, conv_b):
    b, l, n = proj_a.shape
    da = n // 4
    t, halo = 1024, 16
    nh = t // halo
    tok = pl.BlockSpec((None, t, da), lambda i, j: (i, j, 0))
    shp = jax.ShapeDtypeStruct((b, l, da), BF16)
    return pl.pallas_call(
        functools.partial(_hy_pre_body, da=da),
        grid=(b, l // t),
        in_specs=[
            pl.BlockSpec((None, t, n), lambda i, j: (i, j, 0)),
            pl.BlockSpec((None, halo, n), lambda i, j: (i, jnp.maximum(j * nh - 1, 0), 0)),
            pl.BlockSpec((None, halo, n), lambda i, j: (i, jnp.minimum((j + 1) * nh, l // halo - 1), 0)),
            pl.BlockSpec((3, 3 * da), lambda i, j: (0, 0)),
            pl.BlockSpec((1, 3 * da), lambda i, j: (0, 0)),
        ],
        out_specs=[tok, tok, tok],
        out_shape=[shp, shp, shp],
        scratch_shapes=[pltpu.VMEM((t + 16, da), F32)],
        compiler_params=_params(("parallel", "parallel")),
        name="hy_pre",
    )(proj_a, proj_a, proj_a, conv_w, conv_b.reshape(1, 3 * da))


def _filter_body(feat_ref, w1_ref, b1_ref, f0_ref, w2_ref, b2_ref, f1_ref, w3_ref, b3_ref, dec_ref,
                 hf_ref, sum_ref):
    i = pl.program_id(0)
    hp = lax.Precision.HIGHEST
    feat = feat_ref[...]
    t = feat[:, 0:1]
    h = jnp.sin(f0_ref[...] * (jnp.dot(feat, w1_ref[...], preferred_element_type=F32, precision=hp)
                               + b1_ref[...]))
    h = jnp.sin(f1_ref[...] * (jnp.dot(h, w2_ref[...], preferred_element_type=F32, precision=hp)
                               + b2_ref[...]))
    h = jnp.dot(h.astype(BF16), w3_ref[...], preferred_element_type=F32) + b3_ref[...]
    h = h * jnp.exp(-t * jnp.abs(dec_ref[...]))
    hf_ref[...] = h.astype(BF16)
    abs_sum = jnp.sum(jnp.abs(h), 0, keepdims=True)

    @pl.when(i > 0)
    def _():
        sum_ref[...] += abs_sum

    @pl.when(i == 0)
    def _():
        head = h[0:FILTER_HEAD_ROWS]
        rows = lax.broadcasted_iota(jnp.int32, head.shape, 0)
        cols = lax.broadcasted_iota(jnp.int32, head.shape, 1)
        drop = (rows == 0) & (cols >= h.shape[1] // 2)
        hf_ref[0:FILTER_HEAD_ROWS, :] = jnp.where(drop, 0.0, head).astype(BF16)
        sum_ref[...] = abs_sum - jnp.sum(jnp.where(drop, jnp.abs(head), 0.0), 0, keepdims=True)


def _filter_taps(l, w1, b1, freq, w2, b2, w3, b3, decay):
    fo = w1.shape[1]
    n = w3.shape[1]
    da = decay.shape[0]
    t = jnp.arange(l, dtype=F32) / l
    bands = jnp.arange(1, HYENA_BANDS + 1, dtype=F32)
    ang = 2.0 * math.pi * t[:, None] * bands[None, :]
    feat = jnp.concatenate([t[:, None], jnp.cos(ang), jnp.sin(ang)], -1)
    feat = jnp.pad(feat, ((0, 0), (0, FEAT_PAD - HYENA_EMB)))
    w1p = jnp.pad(w1, ((0, FEAT_PAD - HYENA_EMB), (0, 0)))
    dec = jnp.tile(decay, n // da).reshape(1, n)
    tt = 512
    const = lambda shape: pl.BlockSpec(shape, lambda i: (0,) * len(shape))
    return pl.pallas_call(
        _filter_body,
        grid=(l // tt,),
        in_specs=[
            pl.BlockSpec((tt, FEAT_PAD), lambda i: (i, 0)),
            const((FEAT_PAD, fo)), const((1, fo)), const((1, fo)),
            const((fo, fo)), const((1, fo)), const((1, fo)),
            const((fo, n)), const((1, n)), const((1, n)),
        ],
        out_specs=[pl.BlockSpec((tt, n), lambda i: (i, 0)), const((1, n))],
        out_shape=[jax.ShapeDtypeStruct((l, n), BF16), jax.ShapeDtypeStruct((1, n), F32)],
        compiler_params=_params(("arbitrary",)),
        name="filter_taps",
    )(feat, w1p, b1.reshape(1, fo), freq[0].reshape(1, fo), w2, b2.reshape(1, fo),
      freq[1].reshape(1, fo), w3.astype(BF16), b3.reshape(1, n), dec)


def _dft_tables(l):
    n = 2 * l
    n1 = n // DFT_N2
    kk = jnp.arange(n1 // 2, dtype=jnp.int32)
    nn = jnp.arange(n1 // 2, dtype=jnp.int32)
    th = (2.0 * math.pi / (2 * n1)) * (((2 * kk[:, None] + 1) * nn[None, :]) % (2 * n1)).astype(F32)
    eye2 = jnp.eye(2, dtype=F32)
    f1 = jnp.kron(jnp.concatenate([jnp.cos(th), -jnp.sin(th)], 0), eye2).astype(BF16)
    g = jnp.kron(jnp.concatenate([jnp.cos(th).T, -jnp.sin(th).T], 1) * (2.0 / n), eye2).astype(BF16)
    k2 = jnp.arange(DFT_N2, dtype=jnp.int32)
    n2 = jnp.arange(DFT_N2, dtype=jnp.int32)
    ph = (n2[None, None, :] * (k2[None, :, None] * (2 * n1) + 2 * kk[:, None, None] + 1)) % (2 * n)
    ang = (2.0 * math.pi / (2 * n)) * ph.astype(F32)
    c, s = jnp.cos(ang), jnp.sin(ang)
    mf = jnp.concatenate([jnp.concatenate([c, s], 2), jnp.concatenate([-s, c], 2)], 1).astype(BF16)
    mi = jnp.swapaxes(mf, 1, 2)
    return f1, g, mf, mi


def _load_pair_group(ref, lead, g):
    start = pl.multiple_of(g * PAIR_GROUP, PAIR_GROUP)
    words = ref.bitcast(jnp.uint32)[(*lead, slice(None), pl.ds(start, PAIR_GROUP), slice(None))]
    words = jnp.swapaxes(words, 0, 1)
    return [pltpu.bitcast(words[i], BF16) for i in range(PAIR_GROUP)]


def _store_pair_group(ref, lead, g, vals):
    start = pl.multiple_of(g * PAIR_GROUP, PAIR_GROUP)
    words = jnp.stack([pltpu.bitcast(v, jnp.uint32) for v in vals], 0)
    ref.bitcast(jnp.uint32)[(*lead, slice(None), pl.ds(start, PAIR_GROUP), slice(None))] = (
        jnp.swapaxes(words, 0, 1))


def _dft_in_body(f_ref, z_ref, a_ref):
    n1 = a_ref.shape[2]
    cb = a_ref.shape[-1]

    def group(g, carry):
        zcat = jnp.concatenate(_load_pair_group(z_ref, (0,), g), 1)
        r = jnp.dot(f_ref[...], zcat, preferred_element_type=F32).astype(BF16)
        cols = [r[:, i * cb:(i + 1) * cb] for i in range(PAIR_GROUP)]
        _store_pair_group(a_ref, (0, 0), g, [c[:2 * n1] for c in cols])
        _store_pair_group(a_ref, (0, 1), g, [c[2 * n1:] for c in cols])
        return carry

    lax.fori_loop(0, z_ref.shape[2] // (2 * PAIR_GROUP), group, 0)


def _dft_in(f1p, z):
    b, half, n2, c = z.shape
    n1 = half
    cb, rs = V7X_LANES, DFT_STEP_ROWS
    return pl.pallas_call(
        _dft_in_body,
        grid=(b, c // cb, n2 // rs),
        in_specs=[
            pl.BlockSpec(f1p.shape, lambda i, j, s: (0, 0)),
            pl.BlockSpec((1, half, rs, cb), lambda i, j, s: (i, 0, s, j)),
        ],
        out_specs=pl.BlockSpec((1, 2, n1, rs, cb), lambda i, j, s: (i, 0, 0, s, j)),
        out_shape=jax.ShapeDtypeStruct((b, 2, n1, n2, c), BF16),
        compiler_params=_params(("parallel", "parallel", "parallel")),
        name="dft_in",
    )(f1p, z)


def _dft_out_body(g_ref, b_ref, x_ref, z_ref, skip_ref, o_ref):
    cb = o_ref.shape[-1]

    def group(g, carry):
        re, im = _load_pair_group(b_ref, (0, 0), g), _load_pair_group(b_ref, (0, 1), g)
        bcat = jnp.concatenate([jnp.concatenate([r, i], 0) for r, i in zip(re, im)], 1)
        y = jnp.dot(g_ref[...], bcat, preferred_element_type=F32)
        xs, zs = _load_pair_group(x_ref, (0,), g), _load_pair_group(z_ref, (0,), g)
        outs = [(xs[i].astype(F32) * (y[:, i * cb:(i + 1) * cb] + skip_ref[...] * zs[i].astype(F32))
                 ).astype(BF16) for i in range(PAIR_GROUP)]
        _store_pair_group(o_ref, (0,), g, outs)
        return carry

    lax.fori_loop(0, x_ref.shape[2] // (2 * PAIR_GROUP), group, 0)


def _dft_out(gp, bb, x, z, skip):
    b, half, n2, c = z.shape
    n1 = half
    cb, rs = V7X_LANES, DFT_STEP_ROWS
    slab = pl.BlockSpec((1, half, rs, cb), lambda i, j, s: (i, 0, s, j))
    return pl.pallas_call(
        _dft_out_body,
        grid=(b, c // cb, n2 // rs),
        in_specs=[
            pl.BlockSpec(gp.shape, lambda i, j, s: (0, 0)),
            pl.BlockSpec((1, 2, n1, rs, cb), lambda i, j, s: (i, 0, 0, s, j)),
            slab, slab,
            pl.BlockSpec((1, cb), lambda i, j, s: (0, j)),
        ],
        out_specs=slab,
        out_shape=jax.ShapeDtypeStruct(z.shape, BF16),
        compiler_params=_params(("parallel", "parallel", "parallel")),
        name="dft_out",
    )(gp, bb, x, z, skip.astype(F32).reshape(1, c))


def _slab_conv_body(mf_ref, mi_ref, k_ref, a_ref, o_ref):
    nb, _, ks, n2, c = a_ref.shape
    for kk in range(ks):
        kr, ki = k_ref[kk, 0].astype(F32), k_ref[kk, 1].astype(F32)
        for b in range(nb):
            a = a_ref[b, :, kk].reshape(2 * n2, c)
            x = jnp.dot(mf_ref[kk], a, preferred_element_type=F32)
            xr, xi = x[:n2], x[n2:]
            y = jnp.concatenate([xr * kr - xi * ki, xr * ki + xi * kr], 0).astype(BF16)
            out = jnp.dot(mi_ref[kk], y, preferred_element_type=F32)
            o_ref[b, :, kk] = out.astype(BF16).reshape(2, n2, c)


def _slab_conv(mf, mi, kspec, a, order):
    b, _, n1, n2, c = a.shape
    m = 2 * n2
    ks = max(1, SLABS_PER_STEP // b)
    return pl.pallas_call(
        _slab_conv_body,
        grid=(n1 // ks,),
        in_specs=[
            pl.BlockSpec((ks, m, m), lambda k: (k, 0, 0)),
            pl.BlockSpec((ks, m, m), lambda k: (k, 0, 0)),
            pl.BlockSpec((ks, 2, n2, c), lambda k: (k, 0, 0, order)),
            pl.BlockSpec((b, 2, ks, n2, c), lambda k: (0, 0, k, 0, 0)),
        ],
        out_specs=pl.BlockSpec((b, 2, ks, n2, c), lambda k: (0, 0, k, 0, 0)),
        out_shape=jax.ShapeDtypeStruct(a.shape, BF16),
        compiler_params=_params(("parallel",)),
        name="slab_conv",
    )(mf, mi, kspec, a)


def _slab_spec_body(mf_ref, sum_ref, a_ref, k_ref):
    _, ks, n2, c = a_ref.shape
    half = c // 2
    inv = 1.0 / (sum_ref[:, :half] + sum_ref[:, half:] + 1e-6)
    for kk in range(ks):
        x = jnp.dot(mf_ref[kk], a_ref[:, kk].reshape(2 * n2, c), preferred_element_type=F32)
        k_ref[kk, 0] = ((x[:n2, :half] + x[:n2, half:]) * inv).astype(BF16)
        k_ref[kk, 1] = ((x[n2:, :half] - x[n2:, half:]) * inv).astype(BF16)


def _slab_spec(mf, sums, a):
    _, _, n1, n2, c = a.shape
    m = 2 * n2
    ks = 2
    return pl.pallas_call(
        _slab_spec_body,
        grid=(n1 // ks,),
        in_specs=[
            pl.BlockSpec((ks, m, m), lambda k: (k, 0, 0)),
            pl.BlockSpec((1, c), lambda k: (0, 0)),
            pl.BlockSpec((None, 2, ks, n2, c), lambda k: (0, 0, k, 0, 0)),
        ],
        out_specs=pl.BlockSpec((ks, 2, n2, c // 2), lambda k: (k, 0, 0, 0)),
        out_shape=jax.ShapeDtypeStruct((n1, 2, n2, c // 2), BF16),
        compiler_params=_params(("parallel",)),
        name="slab_spec",
    )(mf, sums, a)


def _hyena_spectrum(l, tables, w1, b1, freq, w2, b2, w3, b3, decay):
    f1p, _, mf, _ = tables
    taps, sums = _filter_taps(l, w1, b1, freq, w2, b2, w3, b3, decay)
    a = _dft_in(f1p, taps.reshape(1, l // DFT_N2, DFT_N2, taps.shape[1]))
    return _slab_spec(mf, sums, a)


def _hyena(v, x1, x2g, kspec, skip, tables):
    f1p, gp, mf, mi = tables
    b, l, c = v.shape
    slabs = lambda u: u.reshape(b, l // DFT_N2, DFT_N2, c)
    z = slabs(v)
    for order, xg in enumerate((slabs(x1), slabs(x2g))):
        bb = _slab_conv(mf, mi, kspec, _dft_in(f1p, z), order)
        z = _dft_out(gp, bb, xg, z, skip[order])
    return z.reshape(b, l, c)


def _head_masks(rows, heads):
    lane = lax.broadcasted_iota(jnp.int32, (rows, heads * HEAD_DIM), 1)
    return [(lane >= h * HEAD_DIM) & (lane < (h + 1) * HEAD_DIM) for h in range(heads)]


def _stack_heads(q, masks):
    zero = jnp.zeros_like(q)
    return jnp.concatenate([jnp.where(m, q, zero) for m in masks], 0)


def _unstack_heads(res, masks, rows):
    out = jnp.where(masks[0], res[:rows], 0.0)
    for h in range(1, len(masks)):
        out = out + jnp.where(masks[h], res[h * rows:(h + 1) * rows], 0.0)
    return out


def _na_bias_table(rpb):
    heads = rpb.shape[0]
    c = jnp.arange(GRID_W)
    col_start = jnp.clip(c - NA_KC // 2, 0, GRID_W - NA_KC)
    col_ok = (c[None, :] >= col_start[:, None]) & (c[None, :] < col_start[:, None] + NA_KC)
    dc = jnp.clip(c[None, :] - c[:, None], -(NA_KC - 1), NA_KC - 1) + NA_KC - 1
    onehot = (dc[:, :, None] == jnp.arange(2 * NA_KC - 1)[None, None, :]).astype(F32)
    tcol = jnp.einsum("hrj,qkj->hrqk", rpb.astype(F32), onehot, precision=lax.Precision.HIGHEST)
    per_idx = [tcol[:, NA_KR - 1 - idx:2 * NA_KR - 1 - idx] for idx in range(NA_KR)]
    bias = jnp.transpose(jnp.stack(per_idx, 0), (0, 1, 3, 2, 4))
    bias = jnp.where(col_ok[None, None, :, None, :], bias * LOG2_E, NEG_INF)
    return bias.reshape(NA_KR, heads * GRID_W, NA_KR * GRID_W)


def _na_body(q_ref, k_ref, v_ref, z_ref, bias_ref, o_ref, *, rows, rb, heads):
    blk = pl.program_id(1)
    masks = _head_masks(GRID_W, heads)
    span = NA_KR * GRID_W

    def one_row(rr, carry):
        r = blk * rb + rr
        start = jnp.clip(r - NA_KR // 2, 0, rows - NA_KR)
        idx = r - start
        koff = pl.multiple_of(start * GRID_W, GRID_W)
        qoff = pl.multiple_of(rr * GRID_W, GRID_W)
        q = q_ref[pl.ds(qoff, GRID_W), :]
        kk = k_ref[pl.ds(koff, span), :]
        vv = v_ref[pl.ds(koff, span), :]
        s = lax.dot_general(_stack_heads(q, masks), kk, (((1,), (1,)), ((), ())),
                            preferred_element_type=F32)
        s = s + bias_ref[idx]
        m = jnp.max(s, -1, keepdims=True)
        e = jnp.exp2(s - m)
        p = (e / jnp.sum(e, -1, keepdims=True)).astype(BF16)
        o = _unstack_heads(jnp.dot(p, vv, preferred_element_type=F32), masks, GRID_W)
        z = z_ref[pl.ds(qoff, GRID_W), :].astype(F32)
        o_ref[pl.ds(qoff, GRID_W), :] = (o * _silu(z)).astype(BF16)
        return carry

    lax.fori_loop(0, rb, one_row, 0, unroll=NA_ROW_UNROLL)


def _na(proj_b, bias):
    b, l, n = proj_b.shape
    db = n // 4
    heads = db // HEAD_DIM
    rows = l // GRID_W
    rb = 16
    t = rb * GRID_W
    return pl.pallas_call(
        functools.partial(_na_body, rows=rows, rb=rb, heads=heads),
        grid=(b, rows // rb),
        in_specs=[
            pl.BlockSpec((None, t, db), lambda i, j: (i, j, 0)),
            pl.BlockSpec((None, l, db), lambda i, j: (i, 0, 1)),
            pl.BlockSpec((None, l, db), lambda i, j: (i, 0, 2)),
            pl.BlockSpec((None, t, db), lambda i, j: (i, j, 3)),
            pl.BlockSpec(bias.shape, lambda i, j: (0, 0, 0)),
        ],
        out_specs=pl.BlockSpec((None, t, db), lambda i, j: (i, j, 0)),
        out_shape=jax.ShapeDtypeStruct((b, l, db), BF16),
        compiler_params=_params(("parallel", "parallel")),
        name="na",
    )(proj_b, proj_b, proj_b, proj_b, bias)


def _attend(q, kk, vv, valid, masks):
    rows = q.shape[0]
    s = lax.dot_general(_stack_heads(q, masks), kk, (((1,), (1,)), ((), ())), preferred_element_type=F32)
    s = jnp.where(valid, s, NEG_INF)
    m = jnp.max(s, -1, keepdims=True)
    e = jnp.exp2(s - m)
    l = jnp.sum(e, -1, keepdims=True)
    p = (e / l).astype(BF16)
    o = _unstack_heads(jnp.dot(p, vv, preferred_element_type=F32), masks, rows)
    lse = _unstack_heads(jnp.broadcast_to(m + jnp.log2(l), (s.shape[0], q.shape[1])), masks, rows)
    return o, lse


def _split_pattern(dilation, n, nchunks, q_ref, kbuf, vbuf, o_split, l_split, masks, heads):
    blk = DIL_BLK
    dc = heads * HEAD_DIM
    phases = DIL_SPLIT // dilation
    mq = blk // phases
    row = lax.broadcasted_iota(jnp.int32, (heads * blk, 3 * blk), 0) % blk
    col = lax.broadcasted_iota(jnp.int32, (heads * blk, 3 * blk), 1)
    qa, qm = row // mq, row % mq
    ka = sum((col >= a * 3 * mq).astype(jnp.int32) for a in range(1, phases)) if phases > 1 else 0
    km = col - ka * (3 * mq)
    band = jnp.abs(phases * (km - mq - qm) + (ka - qa)) <= blk
    for m0 in range(0, blk, mq):
        gm = km + (n * blk + m0 - mq)
        valid = band & (gm >= 0) & (gm < nchunks * blk)
        for j in range(dilation):
            lanes = [slice((j + dilation * a) * dc, (j + dilation * a + 1) * dc) for a in range(phases)]
            q = jnp.concatenate([q_ref[m0:m0 + mq, ln] for ln in lanes], 0)
            krows = slice(blk + m0 - mq, blk + m0 + 2 * mq)
            kk = jnp.concatenate([kbuf[krows, ln] for ln in lanes], 0)
            vv = jnp.concatenate([vbuf[krows, ln] for ln in lanes], 0)
            o, lse = _attend(q, kk, vv, valid, masks)
            for a in range(phases):
                o_split[j + dilation * a, m0:m0 + mq, :] = o[a * mq:(a + 1) * mq]
                l_split[j + dilation * a, m0:m0 + mq, :] = lse[a * mq:(a + 1) * mq]


def _dil_body(qn_ref, knp_ref, kn_ref, knn_ref, vnp_ref, vn_ref, vnn_ref,
              qs_ref, ksp_ref, ks_ref, ksn_ref, vsp_ref, vs_ref, vsn_ref, cz_ref, y_ref,
              kbn, vbn, kbs, vbs, o_split, l_split, o_all, l_all, *, heads, nchunks):
    n = pl.program_id(1)
    blk = DIL_BLK
    chunk = qn_ref.shape[0]
    per = chunk // blk
    masks = _head_masks(blk, heads)
    for buf, prev, cur, nxt, halo in ((kbn, knp_ref, kn_ref, knn_ref, blk), (vbn, vnp_ref, vn_ref, vnn_ref, blk),
                                      (kbs, ksp_ref, ks_ref, ksn_ref, blk), (vbs, vsp_ref, vs_ref, vsn_ref, blk)):
        main = cur.shape[0]
        buf[0:halo] = prev[...]
        buf[halo:halo + main] = cur[...]
        buf[halo + main:] = nxt[...]

    qi = lax.broadcasted_iota(jnp.int32, (heads * blk, 3 * blk), 0) % blk
    ki = lax.broadcasted_iota(jnp.int32, (heads * blk, 3 * blk), 1)
    band = jnp.abs(ki - blk - qi) <= blk

    def token_block(i, carry):
        g = n * per + i
        off = pl.multiple_of(i * blk, blk)
        valid = band & ((ki >= blk) | (g > 0)) & ((ki < 2 * blk) | (g < nchunks * per - 1))
        o, lse = _attend(qn_ref[pl.ds(off, blk), :], kbn[pl.ds(off, 3 * blk), :], vbn[pl.ds(off, 3 * blk), :],
                         valid, masks)
        o_all[0, pl.ds(off, blk), :] = o
        l_all[0, pl.ds(off, blk), :] = lse
        return carry

    lax.fori_loop(0, per, token_block, 0, unroll=8)

    for g, (_, dilation) in enumerate(DIL_PATTERNS):
        if dilation == 1:
            continue
        _split_pattern(dilation, n, nchunks, qs_ref, kbs, vbs, o_split, l_split, masks, heads)
        o_all[g] = jnp.swapaxes(o_split[...], 0, 1).reshape(chunk, heads * HEAD_DIM)
        l_all[g] = jnp.swapaxes(l_split[...], 0, 1).reshape(chunk, heads * HEAD_DIM)

    ls = [l_all[g] for g in range(len(DIL_PATTERNS))]
    m = functools.reduce(jnp.maximum, ls)
    es = [jnp.exp2(l - m) for l in ls]
    den = functools.reduce(lambda a, b: a + b, es)
    o = functools.reduce(lambda a, b: a + b, [(e / den) * o_all[g] for g, e in enumerate(es)])
    y_ref[...] = (o * cz_ref[...].astype(F32)).astype(BF16)


def _dilated_mixture(qn, kn, vn, qs, ks, vs, cz):
    b, l, dc = qn.shape
    heads = dc // HEAD_DIM
    blk = DIL_BLK
    assert DIL_PATTERNS[0][1] == 1 and all(w == 2 * blk * d and DIL_SPLIT % d == 0 for w, d in DIL_PATTERNS)
    chunk = DIL_SPLIT * blk
    nchunks = l // chunk
    per = chunk // blk
    nat = pl.BlockSpec((None, chunk, dc), lambda i, n: (i, n, 0))
    nat_prev = pl.BlockSpec((None, blk, dc), lambda i, n: (i, jnp.maximum(n * per - 1, 0), 0))
    nat_next = pl.BlockSpec((None, blk, dc), lambda i, n: (i, jnp.minimum((n + 1) * per, nchunks * per - 1), 0))
    spl = pl.BlockSpec((None, blk, DIL_SPLIT * dc), lambda i, n: (i, n, 0))
    spl_prev = pl.BlockSpec((None, blk, DIL_SPLIT * dc), lambda i, n: (i, jnp.maximum(n - 1, 0), 0))
    spl_next = pl.BlockSpec((None, blk, DIL_SPLIT * dc), lambda i, n: (i, jnp.minimum(n + 1, nchunks - 1), 0))
    return pl.pallas_call(
        functools.partial(_dil_body, heads=heads, nchunks=nchunks),
        grid=(b, nchunks),
        in_specs=[nat, nat_prev, nat, nat_next, nat_prev, nat, nat_next,
                  spl, spl_prev, spl, spl_next, spl_prev, spl, spl_next, nat],
        out_specs=nat,
        out_shape=jax.ShapeDtypeStruct((b, l, dc), BF16),
        scratch_shapes=[pltpu.VMEM((chunk + 2 * blk, dc), BF16), pltpu.VMEM((chunk + 2 * blk, dc), BF16),
                        pltpu.VMEM((3 * blk, DIL_SPLIT * dc), BF16), pltpu.VMEM((3 * blk, DIL_SPLIT * dc), BF16),
                        pltpu.VMEM((DIL_SPLIT, blk, dc), F32), pltpu.VMEM((DIL_SPLIT, blk, dc), F32),
                        pltpu.VMEM((len(DIL_PATTERNS), chunk, dc), F32),
                        pltpu.VMEM((len(DIL_PATTERNS), chunk, dc), F32)],
        compiler_params=_params(("parallel", "parallel")),
        name="dilated",
    )(qn, kn, kn, kn, vn, vn, vn, qs, ks, ks, ks, vs, vs, vs, cz)


def _tail_body(x_ref, ya_ref, yb_ref, yc_ref, g_ref, gate_ref, wa_ref, wb_ref, wc_ref, wo_ref, lng_ref, lnb_ref,
               out_ref, *, alpha):
    d = x_ref.shape[-1]
    pa = jnp.dot(ya_ref[...], wa_ref[...], preferred_element_type=F32)
    pb = jnp.dot(yb_ref[...], wb_ref[...], preferred_element_type=F32)
    pc = jnp.dot(yc_ref[...], wc_ref[...], preferred_element_type=F32)
    g = g_ref[...].astype(F32)
    merged = g[:, :d] * pa + g[:, d:2 * d] * pb + g[:, 2 * d:] * pc
    sub = jnp.dot(merged.astype(BF16), wo_ref[...], preferred_element_type=F32) * gate_ref[...]
    res = alpha * x_ref[...] + sub
    out_ref[...] = _layernorm(res) * lng_ref[...] + lnb_ref[...]


def _tail(x, ya, yb, yc, g_all, gate, wa, wb, wc, wo, ln_g, ln_b, alpha):
    b, l, d = x.shape
    t = 1024
    tok = lambda w: pl.BlockSpec((None, t, w), lambda i, j: (i, j, 0))
    const = lambda a: pl.BlockSpec(a.shape, lambda i, j: (0,) * a.ndim)
    ln_g, ln_b = ln_g.reshape(1, d), ln_b.reshape(1, d)
    return pl.pallas_call(
        functools.partial(_tail_body, alpha=alpha),
        grid=(b, l // t),
        in_specs=[tok(d), tok(ya.shape[-1]), tok(yb.shape[-1]), tok(yc.shape[-1]), tok(3 * d),
                  pl.BlockSpec((None, 1, d), lambda i, j: (i, 0, 0)),
                  const(wa), const(wb), const(wc), const(wo), const(ln_g), const(ln_b)],
        out_specs=tok(d),
        out_shape=jax.ShapeDtypeStruct((b, l, d), F32),
        compiler_params=_params(("parallel", "parallel")),
        name="tail",
    )(x, ya, yb, yc, g_all, gate.reshape(b, 1, d), wa, wb, wc, wo, ln_g, ln_b)


def _rope_tables(l, heads):
    half = HEAD_DIM // 2
    inv = ROPE_THETA ** (-jnp.arange(half, dtype=F32) / half)
    ang = jnp.arange(l, dtype=F32)[:, None] * inv[None, :]
    cos, sin = jnp.cos(ang), jnp.sin(ang)
    return (jnp.tile(jnp.concatenate([cos, cos], -1), (1, heads)),
            jnp.tile(jnp.concatenate([-sin, sin], -1), (1, heads)))


def _layer(x, ada, lw, consts, alpha):
    d = x.shape[-1]
    shift, scale, gate = ada[:, :d], ada[:, d:2 * d], ada[:, 2 * d:]
    h = _ln_mod(x, scale, shift)
    w_in, b_in = lw["w_in"], lw["b_in"]
    proj_a = _proj(h, w_in[:, :2 * d], b_in[:2 * d])
    q_cols = jnp.arange(d) < d // 4
    proj_b = _proj(h, w_in[:, 2 * d:3 * d], b_in[2 * d:3 * d], col_scale=jnp.where(q_cols, QK_SCALE_LOG2, 1.0))
    cq, ck, cv, cz, cqs, cks, cvs = _proj_rope(h, w_in[:, 3 * d:4 * d], b_in[3 * d:4 * d], *consts["rope"])
    gates = _proj(h, w_in[:, 4 * d:], b_in[4 * d:], sigmoid=True)

    v, x1, x2g = _hy_pre(proj_a, lw["conv_w"], lw["conv_b"])
    ya = _hyena(v, x1, x2g, consts["kspec"], lw["skip"], consts["dft"])
    yb = _na(proj_b, lw["na_bias"])
    yc = _dilated_mixture(cq, ck, cv, cqs, cks, cvs, cz)
    return _tail(x, ya, yb, yc, gates, gate, lw["wa"], lw["wb"], lw["wc"], lw["wo"],
                 lw["ln_g"], lw["ln_b"], alpha)


def kernel(x_prompt, x_sample, c_prompt, c_sample, w_ada, b_ada, w_in, b_in, hy_conv_w, hy_conv_b, hy_w1, hy_b1, hy_freq, hy_w2, hy_b2, hy_w3, hy_b3, hy_decay, hy_skip, na_rpb, w_branch_a, w_branch_b, w_branch_c, w_out, ln_g, ln_b):
    depth, d, _ = w_in.shape
    heads_c = (d // 4) // HEAD_DIM
    alpha = (2 * depth) ** 0.25
    groups = [(x_prompt, c_prompt), (x_sample, c_sample)]

    nb_p = c_prompt.shape[0]
    c_all = jnp.concatenate([c_prompt, c_sample], 0)
    pad_rows = -c_all.shape[0] % 8
    ada_all = _ada(jnp.pad(c_all, ((0, pad_rows), (0, 0))), w_ada, b_ada)
    adas = [ada_all[:, :nb_p], ada_all[:, nb_p:nb_p + c_sample.shape[0]]]

    shared = {}
    for x, _ in groups:
        l = x.shape[1]
        if l not in shared:
            shared[l] = {"dft": _dft_tables(l), "rope": _rope_tables(l, heads_c)}

    ys = [x for x, _ in groups]
    for layer in range(depth):
        lw = {
            "w_in": w_in[layer].astype(BF16), "b_in": b_in[layer],
            "conv_w": hy_conv_w[layer], "conv_b": hy_conv_b[layer], "skip": hy_skip[layer],
            "na_bias": _na_bias_table(na_rpb[layer]),
            "wa": w_branch_a[layer].astype(BF16), "wb": w_branch_b[layer].astype(BF16),
            "wc": w_branch_c[layer].astype(BF16), "wo": w_out[layer].astype(BF16),
            "ln_g": ln_g[layer], "ln_b": ln_b[layer],
        }
        kspecs = {}
        for gi in range(len(groups)):
            l = ys[gi].shape[1]
            if l not in kspecs:
                kspecs[l] = _hyena_spectrum(l, shared[l]["dft"], hy_w1[layer], hy_b1[layer], hy_freq[layer],
                                            hy_w2[layer], hy_b2[layer], hy_w3[layer], hy_b3[layer],
                                            hy_decay[layer])
            consts = dict(shared[l], kspec=kspecs[l])
            ys[gi] = _layer(ys[gi], adas[gi][layer], lw, consts, alpha)
    return tuple(ys)
```

```python
import functools
import math

import jax
import jax.numpy as jnp
from jax import lax
from jax.experimental import pallas as pl
from jax.experimental.pallas import tpu as pltpu

F32 = jnp.float32
BF16 = jnp.bfloat16

GRID_W = 64
HEAD_DIM = 64
HYENA_BANDS = 16
HYENA_EMB = 2 * HYENA_BANDS + 1
NA_KR = 8
NA_KC = 16
DIL_PATTERNS = ((128, 1), (512, 4), (2048, 16))
DIL_BLK = 64
DIL_SPLIT = 16
NA_ROW_UNROLL = 16
ROPE_THETA = 10000.0
LN_EPS = 1e-5
NEG_INF = -1e30
LOG2_E = math.log2(math.e)
QK_SCALE_LOG2 = HEAD_DIM ** -0.5 * LOG2_E

V7X_LANES = 128
V7X_SUBLANES = 8
V7X_VMEM_LIMIT_BYTES = 56 * 1024 * 1024

DFT_N2 = V7X_LANES
PAIR_GROUP = V7X_SUBLANES
DFT_STEP_ROWS = 128
SLABS_PER_STEP = 8
FEAT_PAD = V7X_LANES
FILTER_HEAD_ROWS = 2 * V7X_SUBLANES


def _params(sem):
    return pltpu.CompilerParams(dimension_semantics=sem, vmem_limit_bytes=V7X_VMEM_LIMIT_BYTES)


def _sigmoid(x):
    return 1.0 / (1.0 + jnp.exp(-x))


def _silu(x):
    return x * _sigmoid(x)


def _ada_body(c_ref, w_ref, b_ref, o_ref):
    s = _silu(c_ref[...])
    o_ref[...] = jnp.dot(s, w_ref[...], preferred_element_type=F32,
                         precision=lax.Precision.HIGHEST) + b_ref[...]


def _ada(c_all, w_ada, b_ada):
    depth, d, n = w_ada.shape
    rows = c_all.shape[0]
    tn = 1024
    return pl.pallas_call(
        _ada_body,
        grid=(depth, n // tn),
        in_specs=[
            pl.BlockSpec((rows, d), lambda l, j: (0, 0)),
            pl.BlockSpec((None, d, tn), lambda l, j: (l, 0, j)),
            pl.BlockSpec((None, 1, tn), lambda l, j: (l, 0, j)),
        ],
        out_specs=pl.BlockSpec((None, rows, tn), lambda l, j: (l, 0, j)),
        out_shape=jax.ShapeDtypeStruct((depth, rows, n), F32),
        compiler_params=_params(("parallel", "parallel")),
        name="ada",
    )(c_all, w_ada, b_ada.reshape(depth, 1, n))


def _layernorm(x):
    mu = jnp.mean(x, -1, keepdims=True)
    xc = x - mu
    var = jnp.mean(xc * xc, -1, keepdims=True)
    return xc * lax.rsqrt(var + LN_EPS)


def _ln_mod_body(x_ref, sc_ref, sh_ref, o_ref):
    h = _layernorm(x_ref[...]) * (1.0 + sc_ref[...]) + sh_ref[...]
    o_ref[...] = h.astype(BF16)


def _ln_mod(x, scale, shift):
    b, l, d = x.shape
    t = 1024
    return pl.pallas_call(
        _ln_mod_body,
        grid=(b, l // t),
        in_specs=[
            pl.BlockSpec((None, t, d), lambda i, j: (i, j, 0)),
            pl.BlockSpec((None, 1, d), lambda i, j: (i, 0, 0)),
            pl.BlockSpec((None, 1, d), lambda i, j: (i, 0, 0)),
        ],
        out_specs=pl.BlockSpec((None, t, d), lambda i, j: (i, j, 0)),
        out_shape=jax.ShapeDtypeStruct((b, l, d), BF16),
        compiler_params=_params(("parallel", "parallel")),
        name="ln_mod",
    )(x, scale.reshape(b, 1, d), shift.reshape(b, 1, d))


def _proj_body(h_ref, w_ref, b_ref, *rest, sigmoid, scaled):
    o_ref = rest[-1]
    acc = jnp.dot(h_ref[...], w_ref[...], preferred_element_type=F32) + b_ref[...]
    if scaled:
        acc = acc * rest[0][...]
    if sigmoid:
        acc = _sigmoid(acc)
    o_ref[...] = acc.astype(o_ref.dtype)


def _proj(h, w, bias, col_scale=None, sigmoid=False):
    b, l, d = h.shape
    n = w.shape[1]
    t, tn = 1024, 1024
    row = pl.BlockSpec((1, tn), lambda j, i, k: (0, j))
    scale_args = [] if col_scale is None else [col_scale.reshape(1, n)]
    return pl.pallas_call(
        functools.partial(_proj_body, sigmoid=sigmoid, scaled=col_scale is not None),
        grid=(n // tn, b, l // t),
        in_specs=[
            pl.BlockSpec((None, t, d), lambda j, i, k: (i, k, 0)),
            pl.BlockSpec((d, tn), lambda j, i, k: (0, j)),
            row,
        ] + [row] * len(scale_args),
        out_specs=pl.BlockSpec((None, t, tn), lambda j, i, k: (i, k, j)),
        out_shape=jax.ShapeDtypeStruct((b, l, n), BF16),
        compiler_params=_params(("parallel", "parallel", "parallel")),
        name="proj",
    )(h, w, bias.reshape(1, n), *scale_args)


def _rope_lanes(x, cos, sin_signed):
    outs = []
    lane = lax.broadcasted_iota(jnp.int32, (x.shape[0], V7X_LANES), 1)
    first_half = (lane % HEAD_DIM) < (HEAD_DIM // 2)
    for c0 in range(0, x.shape[1], V7X_LANES):
        xc = x[:, c0:c0 + V7X_LANES]
        partner = jnp.where(first_half,
                            pltpu.roll(xc, V7X_LANES - HEAD_DIM // 2, 1),
                            pltpu.roll(xc, HEAD_DIM // 2, 1))
        outs.append(xc * cos[:, c0:c0 + V7X_LANES] + partner * sin_signed[:, c0:c0 + V7X_LANES])
    return jnp.concatenate(outs, 1)


def _store_split(ref, x):
    t, dc = x.shape
    parts = jnp.swapaxes(x.reshape(t // DIL_SPLIT, DIL_SPLIT, dc), 0, 1)
    for r in range(DIL_SPLIT):
        ref[:, r * dc:(r + 1) * dc] = parts[r].astype(BF16)


def _proj_rope_body(h_ref, w_ref, b_ref, cos_ref, sin_ref, q_ref, k_ref, v_ref, z_ref,
                    qs_ref, ks_ref, vs_ref):
    dc = q_ref.shape[-1]
    acc = jnp.dot(h_ref[...], w_ref[...], preferred_element_type=F32) + b_ref[...]
    cos, sin = cos_ref[...], sin_ref[...]
    q = _rope_lanes(acc[:, :dc], cos, sin) * QK_SCALE_LOG2
    k = _rope_lanes(acc[:, dc:2 * dc], cos, sin)
    v = acc[:, 2 * dc:3 * dc]
    for ref, split_ref, val in ((q_ref, qs_ref, q), (k_ref, ks_ref, k), (v_ref, vs_ref, v)):
        ref[...] = val.astype(BF16)
        _store_split(split_ref, val)
    z_ref[...] = _silu(acc[:, 3 * dc:]).astype(BF16)


def _proj_rope(h, w, bias, cos_t, sin_t):
    b, l, d = h.shape
    n = w.shape[1]
    dc = n // 4
    t = 512
    tok = pl.BlockSpec((None, t, dc), lambda i, k: (i, k, 0))
    shp = jax.ShapeDtypeStruct((b, l, dc), BF16)
    spl = pl.BlockSpec((None, t // DIL_SPLIT, DIL_SPLIT * dc), lambda i, k: (i, k, 0))
    spl_shp = jax.ShapeDtypeStruct((b, l // DIL_SPLIT, DIL_SPLIT * dc), BF16)
    return pl.pallas_call(
        _proj_rope_body,
        grid=(b, l // t),
        in_specs=[
            pl.BlockSpec((None, t, d), lambda i, k: (i, k, 0)),
            pl.BlockSpec((d, n), lambda i, k: (0, 0)),
            pl.BlockSpec((1, n), lambda i, k: (0, 0)),
            pl.BlockSpec((t, dc), lambda i, k: (k, 0)),
            pl.BlockSpec((t, dc), lambda i, k: (k, 0)),
        ],
        out_specs=[tok, tok, tok, tok, spl, spl, spl],
        out_shape=[shp, shp, shp, shp, spl_shp, spl_shp, spl_shp],
        compiler_params=_params(("parallel", "parallel")),
        name="proj_rope",
    )(h, w, bias.reshape(1, n), cos_t, sin_t)


def _hy_pre_body(main_ref, prev_ref, next_ref, cw_ref, cb_ref, v_ref, x1_ref, x2_ref, pad_ref, *, da):
    i = pl.program_id(1)
    last = pl.num_programs(1) - 1
    t = main_ref.shape[0]
    halo = prev_ref.shape[0]
    outs = (v_ref, x1_ref, x2_ref)
    az = main_ref[:, 3 * da:].astype(F32)
    gate = _silu(az)
    for part in range(3):
        cols = slice(part * da, (part + 1) * da)
        prev_row = jnp.where(i > 0, prev_ref[:, cols].astype(F32)[halo - 1:halo], 0.0)
        next_row = jnp.where(i < last, next_ref[:, cols].astype(F32)[0:1], 0.0)
        pad_ref[7:8, :] = prev_row
        pad_ref[8:8 + t, :] = main_ref[:, cols].astype(F32)
        pad_ref[8 + t:9 + t, :] = next_row
        uc = (pad_ref[7:7 + t, :] * cw_ref[0:1, cols] + pad_ref[8:8 + t, :] * cw_ref[1:2, cols]
              + pad_ref[9:9 + t, :] * cw_ref[2:3, cols] + cb_ref[:, cols])
        if part == 2:
            uc = uc * gate
        outs[part][...] = uc.astype(BF16)


def _hy_pre(proj_a, conv_w, conv_b):
    b, l, n = proj_a.shape
    da = n // 4
    t, halo = 1024, 16
    nh = t // halo
    tok = pl.BlockSpec((None, t, da), lambda i, j: (i, j, 0))
    shp = jax.ShapeDtypeStruct((b, l, da), BF16)
    return pl.pallas_call(
        functools.partial(_hy_pre_body, da=da),
        grid=(b, l // t),
        in_specs=[
            pl.BlockSpec((None, t, n), lambda i, j: (i, j, 0)),
            pl.BlockSpec((None, halo, n), lambda i, j: (i, jnp.maximum(j * nh - 1, 0), 0)),
            pl.BlockSpec((None, halo, n), lambda i, j: (i, jnp.minimum((j + 1) * nh, l // halo - 1), 0)),
            pl.BlockSpec((3, 3 * da), lambda i, j: (0, 0)),
            pl.BlockSpec((1, 3 * da), lambda i, j: (0, 0)),
        ],
        out_specs=[tok, tok, tok],
        out_shape=[shp, shp, shp],
        scratch_shapes=[pltpu.VMEM((t + 16, da), F32)],
        compiler_params=_params(("parallel", "parallel")),
        name="hy_pre",
    )(proj_a, proj_a, proj_a, conv_w, conv_b.reshape(1, 3 * da))


def _filter_body(feat_ref, w1_ref, b1_ref, f0_ref, w2_ref, b2_ref, f1_ref, w3_ref, b3_ref, dec_ref,
                 hf_ref, sum_ref):
    i = pl.program_id(0)
    hp = lax.Precision.HIGHEST
    feat = feat_ref[...]
    t = feat[:, 0:1]
    h = jnp.sin(f0_ref[...] * (jnp.dot(feat, w1_ref[...], preferred_element_type=F32, precision=hp)
                               + b1_ref[...]))
    h = jnp.sin(f1_ref[...] * (jnp.dot(h, w2_ref[...], preferred_element_type=F32, precision=hp)
                               + b2_ref[...]))
    h = jnp.dot(h.astype(BF16), w3_ref[...], preferred_element_type=F32) + b3_ref[...]
    h = h * jnp.exp(-t * jnp.abs(dec_ref[...]))
    hf_ref[...] = h.astype(BF16)
    abs_sum = jnp.sum(jnp.abs(h), 0, keepdims=True)

    @pl.when(i > 0)
    def _():
        sum_ref[...] += abs_sum

    @pl.when(i == 0)
    def _():
        head = h[0:FILTER_HEAD_ROWS]
        rows = lax.broadcasted_iota(jnp.int32, head.shape, 0)
        cols = lax.broadcasted_iota(jnp.int32, head.shape, 1)
        drop = (rows == 0) & (cols >= h.shape[1] // 2)
        hf_ref[0:FILTER_HEAD_ROWS, :] = jnp.where(drop, 0.0, head).astype(BF16)
        sum_ref[...] = abs_sum - jnp.sum(jnp.where(drop, jnp.abs(head), 0.0), 0, keepdims=True)


def _filter_taps(l, w1, b1, freq, w2, b2, w3, b3, decay):
    fo = w1.shape[1]
    n = w3.shape[1]
    da = decay.shape[0]
    t = jnp.arange(l, dtype=F32) / l
    bands = jnp.arange(1, HYENA_BANDS + 1, dtype=F32)
    ang = 2.0 * math.pi * t[:, None] * bands[None, :]
    feat = jnp.concatenate([t[:, None], jnp.cos(ang), jnp.sin(ang)], -1)
    feat = jnp.pad(feat, ((0, 0), (0, FEAT_PAD - HYENA_EMB)))
    w1p = jnp.pad(w1, ((0, FEAT_PAD - HYENA_EMB), (0, 0)))
    dec = jnp.tile(decay, n // da).reshape(1, n)
    tt = 512
    const = lambda shape: pl.BlockSpec(shape, lambda i: (0,) * len(shape))
    return pl.pallas_call(
        _filter_body,
        grid=(l // tt,),
        in_specs=[
            pl.BlockSpec((tt, FEAT_PAD), lambda i: (i, 0)),
            const((FEAT_PAD, fo)), const((1, fo)), const((1, fo)),
            const((fo, fo)), const((1, fo)), const((1, fo)),
            const((fo, n)), const((1, n)), const((1, n)),
        ],
        out_specs=[pl.BlockSpec((tt, n), lambda i: (i, 0)), const((1, n))],
        out_shape=[jax.ShapeDtypeStruct((l, n), BF16), jax.ShapeDtypeStruct((1, n), F32)],
        compiler_params=_params(("arbitrary",)),
        name="filter_taps",
    )(feat, w1p, b1.reshape(1, fo), freq[0].reshape(1, fo), w2, b2.reshape(1, fo),
      freq[1].reshape(1, fo), w3.astype(BF16), b3.reshape(1, n), dec)


def _dft_tables(l):
    n = 2 * l
    n1 = n // DFT_N2
    kk = jnp.arange(n1 // 2, dtype=jnp.int32)
    nn = jnp.arange(n1 // 2, dtype=jnp.int32)
    th = (2.0 * math.pi / (2 * n1)) * (((2 * kk[:, None] + 1) * nn[None, :]) % (2 * n1)).astype(F32)
    eye2 = jnp.eye(2, dtype=F32)
    f1 = jnp.kron(jnp.concatenate([jnp.cos(th), -jnp.sin(th)], 0), eye2).astype(BF16)
    g = jnp.kron(jnp.concatenate([jnp.cos(th).T, -jnp.sin(th).T], 1) * (2.0 / n), eye2).astype(BF16)
    k2 = jnp.arange(DFT_N2, dtype=jnp.int32)
    n2 = jnp.arange(DFT_N2, dtype=jnp.int32)
    ph = (n2[None, None, :] * (k2[None, :, None] * (2 * n1) + 2 * kk[:, None, None] + 1)) % (2 * n)
    ang = (2.0 * math.pi / (2 * n)) * ph.astype(F32)
    c, s = jnp.cos(ang), jnp.sin(ang)
    mf = jnp.concatenate([jnp.concatenate([c, s], 2), jnp.concatenate([-s, c], 2)], 1).astype(BF16)
    mi = jnp.swapaxes(mf, 1, 2)
    return f1, g, mf, mi


def _load_pair_group(ref, lead, g):
    start = pl.multiple_of(g * PAIR_GROUP, PAIR_GROUP)
    words = ref.bitcast(jnp.uint32)[(*lead, slice(None), pl.ds(start, PAIR_GROUP), slice(None))]
    words = jnp.swapaxes(words, 0, 1)
    return [pltpu.bitcast(words[i], BF16) for i in range(PAIR_GROUP)]


def _store_pair_group(ref, lead, g, vals):
    start = pl.multiple_of(g * PAIR_GROUP, PAIR_GROUP)
    words = jnp.stack([pltpu.bitcast(v, jnp.uint32) for v in vals], 0)
    ref.bitcast(jnp.uint32)[(*lead, slice(None), pl.ds(start, PAIR_GROUP), slice(None))] = (
        jnp.swapaxes(words, 0, 1))


def _dft_in_body(f_ref, z_ref, a_ref):
    n1 = a_ref.shape[2]
    cb = a_ref.shape[-1]

    def group(g, carry):
        zcat = jnp.concatenate(_load_pair_group(z_ref, (0,), g), 1)
        r = jnp.dot(f_ref[...], zcat, preferred_element_type=F32).astype(BF16)
        cols = [r[:, i * cb:(i + 1) * cb] for i in range(PAIR_GROUP)]
        _store_pair_group(a_ref, (0, 0), g, [c[:2 * n1] for c in cols])
        _store_pair_group(a_ref, (0, 1), g, [c[2 * n1:] for c in cols])
        return carry

    lax.fori_loop(0, z_ref.shape[2] // (2 * PAIR_GROUP), group, 0, unroll=4)


def _dft_in(f1p, z):
    b, half, n2, c = z.shape
    n1 = half
    cb, rs = V7X_LANES, DFT_STEP_ROWS
    return pl.pallas_call(
        _dft_in_body,
        grid=(b, c // cb, n2 // rs),
        in_specs=[
            pl.BlockSpec(f1p.shape, lambda i, j, s: (0, 0)),
            pl.BlockSpec((1, half, rs, cb), lambda i, j, s: (i, 0, s, j)),
        ],
        out_specs=pl.BlockSpec((1, 2, n1, rs, cb), lambda i, j, s: (i, 0, 0, s, j)),
        out_shape=jax.ShapeDtypeStruct((b, 2, n1, n2, c), BF16),
        compiler_params=_params(("parallel", "parallel", "parallel")),
        name="dft_in",
    )(f1p, z)


def _dft_out_body(g_ref, b_ref, x_ref, z_ref, skip_ref, o_ref):
    cb = o_ref.shape[-1]

    def group(g, carry):
        re, im = _load_pair_group(b_ref, (0, 0), g), _load_pair_group(b_ref, (0, 1), g)
        bcat = jnp.concatenate([jnp.concatenate([r, i], 0) for r, i in zip(re, im)], 1)
        y = jnp.dot(g_ref[...], bcat, preferred_element_type=F32)
        xs, zs = _load_pair_group(x_ref, (0,), g), _load_pair_group(z_ref, (0,), g)
        outs = [(xs[i].astype(F32) * (y[:, i * cb:(i + 1) * cb] + skip_ref[...] * zs[i].astype(F32))
                 ).astype(BF16) for i in range(PAIR_GROUP)]
        _store_pair_group(o_ref, (0,), g, outs)
        return carry

    lax.fori_loop(0, x_ref.shape[2] // (2 * PAIR_GROUP), group, 0)


def _dft_out(gp, bb, x, z, skip):
    b, half, n2, c = z.shape
    n1 = half
    cb, rs = V7X_LANES, DFT_STEP_ROWS
    slab = pl.BlockSpec((1, half, rs, cb), lambda i, j, s: (i, 0, s, j))
    return pl.pallas_call(
        _dft_out_body,
        grid=(b, c // cb, n2 // rs),
        in_specs=[
            pl.BlockSpec(gp.shape, lambda i, j, s: (0, 0)),
            pl.BlockSpec((1, 2, n1, rs, cb), lambda i, j, s: (i, 0, 0, s, j)),
            slab, slab,
            pl.BlockSpec((1, cb), lambda i, j, s: (0, j)),
        ],
        out_specs=slab,
        out_shape=jax.ShapeDtypeStruct(z.shape, BF16),
        compiler_params=_params(("parallel", "parallel", "parallel")),
        name="dft_out",
    )(gp, bb, x, z, skip.astype(F32).reshape(1, c))


def _slab_conv_body(mf_ref, mi_ref, k_ref, a_ref, o_ref):
    nb, _, ks, n2, c = a_ref.shape
    for kk in range(ks):
        kr, ki = k_ref[kk, 0].astype(F32), k_ref[kk, 1].astype(F32)
        for b in range(nb):
            a = a_ref[b, :, kk].reshape(2 * n2, c)
            x = jnp.dot(mf_ref[kk], a, preferred_element_type=F32)
            xr, xi = x[:n2], x[n2:]
            y = jnp.concatenate([xr * kr - xi * ki, xr * ki + xi * kr], 0).astype(BF16)
            out = jnp.dot(mi_ref[kk], y, preferred_element_type=F32)
            o_ref[b, :, kk] = out.astype(BF16).reshape(2, n2, c)


def _slab_conv(mf, mi, kspec, a, order):
    b, _, n1, n2, c = a.shape
    m = 2 * n2
    ks = max(1, SLABS_PER_STEP // b)
    return pl.pallas_call(
        _slab_conv_body,
        grid=(n1 // ks,),
        in_specs=[
            pl.BlockSpec((ks, m, m), lambda k: (k, 0, 0)),
            pl.BlockSpec((ks, m, m), lambda k: (k, 0, 0)),
            pl.BlockSpec((ks, 2, n2, c), lambda k: (k, 0, 0, order)),
            pl.BlockSpec((b, 2, ks, n2, c), lambda k: (0, 0, k, 0, 0)),
        ],
        out_specs=pl.BlockSpec((b, 2, ks, n2, c), lambda k: (0, 0, k, 0, 0)),
        out_shape=jax.ShapeDtypeStruct(a.shape, BF16),
        compiler_params=_params(("parallel",)),
        name="slab_conv",
    )(mf, mi, kspec, a)


def _slab_spec_body(mf_ref, sum_ref, a_ref, k_ref):
    _, ks, n2, c = a_ref.shape
    half = c // 2
    inv = 1.0 / (sum_ref[:, :half] + sum_ref[:, half:] + 1e-6)
    for kk in range(ks):
        x = jnp.dot(mf_ref[kk], a_ref[:, kk].reshape(2 * n2, c), preferred_element_type=F32)
        k_ref[kk, 0] = ((x[:n2, :half] + x[:n2, half:]) * inv).astype(BF16)
        k_ref[kk, 1] = ((x[n2:, :half] - x[n2:, half:]) * inv).astype(BF16)


def _slab_spec(mf, sums, a):
    _, _, n1, n2, c = a.shape
    m = 2 * n2
    ks = 2
    return pl.pallas_call(
        _slab_spec_body,
        grid=(n1 // ks,),
        in_specs=[
            pl.BlockSpec((ks, m, m), lambda k: (k, 0, 0)),
            pl.BlockSpec((1, c), lambda k: (0, 0)),
            pl.BlockSpec((None, 2, ks, n2, c), lambda k: (0, 0, k, 0, 0)),
        ],
        out_specs=pl.BlockSpec((ks, 2, n2, c // 2), lambda k: (k, 0, 0, 0)),
        out_shape=jax.ShapeDtypeStruct((n1, 2, n2, c // 2), BF16),
        compiler_params=_params(("parallel",)),
        name="slab_spec",
    )(mf, sums, a)


def _hyena_spectrum(l, tables, w1, b1, freq, w2, b2, w3, b3, decay):
    f1p, _, mf, _ = tables
    taps, sums = _filter_taps(l, w1, b1, freq, w2, b2, w3, b3, decay)
    a = _dft_in(f1p, taps.reshape(1, l // DFT_N2, DFT_N2, taps.shape[1]))
    return _slab_spec(mf, sums, a)


def _hyena(v, x1, x2g, kspec, skip, tables):
    f1p, gp, mf, mi = tables
    b, l, c = v.shape
    slabs = lambda u: u.reshape(b, l // DFT_N2, DFT_N2, c)
    z = slabs(v)
    for order, xg in enumerate((slabs(x1), slabs(x2g))):
        bb = _slab_conv(mf, mi, kspec, _dft_in(f1p, z), order)
        z = _dft_out(gp, bb, xg, z, skip[order])
    return z.reshape(b, l, c)


def _head_masks(rows, heads):
    lane = lax.broadcasted_iota(jnp.int32, (rows, heads * HEAD_DIM), 1)
    return [(lane >= h * HEAD_DIM) & (lane < (h + 1) * HEAD_DIM) for h in range(heads)]


def _stack_heads(q, masks):
    zero = jnp.zeros_like(q)
    return jnp.concatenate([jnp.where(m, q, zero) for m in masks], 0)


def _unstack_heads(res, masks, rows):
    out = jnp.where(masks[0], res[:rows], 0.0)
    for h in range(1, len(masks)):
        out = out + jnp.where(masks[h], res[h * rows:(h + 1) * rows], 0.0)
    return out


def _na_bias_table(rpb):
    heads = rpb.shape[0]
    c = jnp.arange(GRID_W)
    col_start = jnp.clip(c - NA_KC // 2, 0, GRID_W - NA_KC)
    col_ok = (c[None, :] >= col_start[:, None]) & (c[None, :] < col_start[:, None] + NA_KC)
    dc = jnp.clip(c[None, :] - c[:, None], -(NA_KC - 1), NA_KC - 1) + NA_KC - 1
    onehot = (dc[:, :, None] == jnp.arange(2 * NA_KC - 1)[None, None, :]).astype(F32)
    tcol = jnp.einsum("hrj,qkj->hrqk", rpb.astype(F32), onehot, precision=lax.Precision.HIGHEST)
    per_idx = [tcol[:, NA_KR - 1 - idx:2 * NA_KR - 1 - idx] for idx in range(NA_KR)]
    bias = jnp.transpose(jnp.stack(per_idx, 0), (0, 1, 3, 2, 4))
    bias = jnp.where(col_ok[None, None, :, None, :], bias * LOG2_E, NEG_INF)
    return bias.reshape(NA_KR, heads * GRID_W, NA_KR * GRID_W)


def _na_body(q_ref, k_ref, v_ref, z_ref, bias_ref, o_ref, *, rows, rb, heads):
    blk = pl.program_id(1)
    masks = _head_masks(GRID_W, heads)
    span = NA_KR * GRID_W

    def one_row(rr, carry):
        r = blk * rb + rr
        start = jnp.clip(r - NA_KR // 2, 0, rows - NA_KR)
        idx = r - start
        koff = pl.multiple_of(start * GRID_W, GRID_W)
        qoff = pl.multiple_of(rr * GRID_W, GRID_W)
        q = q_ref[pl.ds(qoff, GRID_W), :]
        kk = k_ref[pl.ds(koff, span), :]
        vv = v_ref[pl.ds(koff, span), :]
        s = lax.dot_general(_stack_heads(q, masks), kk, (((1,), (1,)), ((), ())),
                            preferred_element_type=F32)
        s = s + bias_ref[idx]
        m = jnp.max(s, -1, keepdims=True)
        e = jnp.exp2(s - m)
        p = (e / jnp.sum(e, -1, keepdims=True)).astype(BF16)
        o = _unstack_heads(jnp.dot(p, vv, preferred_element_type=F32), masks, GRID_W)
        z = z_ref[pl.ds(qoff, GRID_W), :].astype(F32)
        o_ref[pl.ds(qoff, GRID_W), :] = (o * _silu(z)).astype(BF16)
        return carry

    lax.fori_loop(0, rb, one_row, 0, unroll=NA_ROW_UNROLL)


def _na(proj_b, bias):
    b, l, n = proj_b.shape
    db = n // 4
    heads = db // HEAD_DIM
    rows = l // GRID_W
    rb = 16
    t = rb * GRID_W
    return pl.pallas_call(
        functools.partial(_na_body, rows=rows, rb=rb, heads=heads),
        grid=(b, rows // rb),
        in_specs=[
            pl.BlockSpec((None, t, db), lambda i, j: (i, j, 0)),
            pl.BlockSpec((None, l, db), lambda i, j: (i, 0, 1)),
            pl.BlockSpec((None, l, db), lambda i, j: (i, 0, 2)),
            pl.BlockSpec((None, t, db), lambda i, j: (i, j, 3)),
            pl.BlockSpec(bias.shape, lambda i, j: (0, 0, 0)),
        ],
        out_specs=pl.BlockSpec((None, t, db), lambda i, j: (i, j, 0)),
        out_shape=jax.ShapeDtypeStruct((b, l, db), BF16),
        compiler_params=_params(("parallel", "parallel")),
        name="na",
    )(proj_b, proj_b, proj_b, proj_b, bias)


def _attend(q, kk, vv, valid, masks):
    rows = q.shape[0]
    s = lax.dot_general(_stack_heads(q, masks), kk, (((1,), (1,)), ((), ())), preferred_element_type=F32)
    s = jnp.where(valid, s, NEG_INF)
    m = jnp.max(s, -1, keepdims=True)
    e = jnp.exp2(s - m)
    l = jnp.sum(e, -1, keepdims=True)
    p = (e / l).astype(BF16)
    o = _unstack_heads(jnp.dot(p, vv, preferred_element_type=F32), masks, rows)
    lse = _unstack_heads(jnp.broadcast_to(m + jnp.log2(l), (s.shape[0], q.shape[1])), masks, rows)
    return o, lse


def _split_pattern(dilation, n, nchunks, q_ref, kbuf, vbuf, o_split, l_split, masks, heads):
    blk = DIL_BLK
    dc = heads * HEAD_DIM
    phases = DIL_SPLIT // dilation
    mq = blk // phases
    row = lax.broadcasted_iota(jnp.int32, (heads * blk, 3 * blk), 0) % blk
    col = lax.broadcasted_iota(jnp.int32, (heads * blk, 3 * blk), 1)
    qa, qm = row // mq, row % mq
    ka = sum((col >= a * 3 * mq).astype(jnp.int32) for a in range(1, phases)) if phases > 1 else 0
    km = col - ka * (3 * mq)
    band = jnp.abs(phases * (km - mq - qm) + (ka - qa)) <= blk
    for m0 in range(0, blk, mq):
        gm = km + (n * blk + m0 - mq)
        valid = band & (gm >= 0) & (gm < nchunks * blk)
        for j in range(dilation):
            lanes = [slice((j + dilation * a) * dc, (j + dilation * a + 1) * dc) for a in range(phases)]
            q = jnp.concatenate([q_ref[m0:m0 + mq, ln] for ln in lanes], 0)
            krows = slice(blk + m0 - mq, blk + m0 + 2 * mq)
            kk = jnp.concatenate([kbuf[krows, ln] for ln in lanes], 0)
            vv = jnp.concatenate([vbuf[krows, ln] for ln in lanes], 0)
            o, lse = _attend(q, kk, vv, valid, masks)
            for a in range(phases):
                o_split[j + dilation * a, m0:m0 + mq, :] = o[a * mq:(a + 1) * mq]
                l_split[j + dilation * a, m0:m0 + mq, :] = lse[a * mq:(a + 1) * mq]


def _dil_body(qn_ref, knp_ref, kn_ref, knn_ref, vnp_ref, vn_ref, vnn_ref,
              qs_ref, ksp_ref, ks_ref, ksn_ref, vsp_ref, vs_ref, vsn_ref, cz_ref, y_ref,
              kbn, vbn, kbs, vbs, o_split, l_split, o_all, l_all, *, heads, nchunks):
    n = pl.program_id(1)
    blk = DIL_BLK
    chunk = qn_ref.shape[0]
    per = chunk // blk
    masks = _head_masks(blk, heads)
    for buf, prev, cur, nxt, halo in ((kbn, knp_ref, kn_ref, knn_ref, blk), (vbn, vnp_ref, vn_ref, vnn_ref, blk),
                                      (kbs, ksp_ref, ks_ref, ksn_ref, blk), (vbs, vsp_ref, vs_ref, vsn_ref, blk)):
        main = cur.shape[0]
        buf[0:halo] = prev[...]
        buf[halo:halo + main] = cur[...]
        buf[halo + main:] = nxt[...]

    qi = lax.broadcasted_iota(jnp.int32, (heads * blk, 3 * blk), 0) % blk
    ki = lax.broadcasted_iota(jnp.int32, (heads * blk, 3 * blk), 1)
    band = jnp.abs(ki - blk - qi) <= blk

    def token_block(i, carry):
        g = n * per + i
        off = pl.multiple_of(i * blk, blk)
        valid = band & ((ki >= blk) | (g > 0)) & ((ki < 2 * blk) | (g < nchunks * per - 1))
        o, lse = _attend(qn_ref[pl.ds(off, blk), :], kbn[pl.ds(off, 3 * blk), :], vbn[pl.ds(off, 3 * blk), :],
                         valid, masks)
        o_all[0, pl.ds(off, blk), :] = o
        l_all[0, pl.ds(off, blk), :] = lse
        return carry

    lax.fori_loop(0, per, token_block, 0, unroll=True)

    for g, (_, dilation) in enumerate(DIL_PATTERNS):
        if dilation == 1:
            continue
        _split_pattern(dilation, n, nchunks, qs_ref, kbs, vbs, o_split, l_split, masks, heads)
        o_all[g] = jnp.swapaxes(o_split[...], 0, 1).reshape(chunk, heads * HEAD_DIM)
        l_all[g] = jnp.swapaxes(l_split[...], 0, 1).reshape(chunk, heads * HEAD_DIM)

    ls = [l_all[g] for g in range(len(DIL_PATTERNS))]
    m = functools.reduce(jnp.maximum, ls)
    es = [jnp.exp2(l - m) for l in ls]
    den = functools.reduce(lambda a, b: a + b, es)
    o = functools.reduce(lambda a, b: a + b, [(e / den) * o_all[g] for g, e in enumerate(es)])
    y_ref[...] = (o * cz_ref[...].astype(F32)).astype(BF16)


def _dilated_mixture(qn, kn, vn, qs, ks, vs, cz):
    b, l, dc = qn.shape
    heads = dc // HEAD_DIM
    blk = DIL_BLK
    assert DIL_PATTERNS[0][1] == 1 and all(w == 2 * blk * d and DIL_SPLIT % d == 0 for w, d in DIL_PATTERNS)
    chunk = DIL_SPLIT * blk
    nchunks = l // chunk
    per = chunk // blk
    nat = pl.BlockSpec((None, chunk, dc), lambda i, n: (i, n, 0))
    nat_prev = pl.BlockSpec((None, blk, dc), lambda i, n: (i, jnp.maximum(n * per - 1, 0), 0))
    nat_next = pl.BlockSpec((None, blk, dc), lambda i, n: (i, jnp.minimum((n + 1) * per, nchunks * per - 1), 0))
    spl = pl.BlockSpec((None, blk, DIL_SPLIT * dc), lambda i, n: (i, n, 0))
    spl_prev = pl.BlockSpec((None, blk, DIL_SPLIT * dc), lambda i, n: (i, jnp.maximum(n - 1, 0), 0))
    spl_next = pl.BlockSpec((None, blk, DIL_SPLIT * dc), lambda i, n: (i, jnp.minimum(n + 1, nchunks - 1), 0))
    return pl.pallas_call(
        functools.partial(_dil_body, heads=heads, nchunks=nchunks),
        grid=(b, nchunks),
        in_specs=[nat, nat_prev, nat, nat_next, nat_prev, nat, nat_next,
                  spl, spl_prev, spl, spl_next, spl_prev, spl, spl_next, nat],
        out_specs=nat,
        out_shape=jax.ShapeDtypeStruct((b, l, dc), BF16),
        scratch_shapes=[pltpu.VMEM((chunk + 2 * blk, dc), BF16), pltpu.VMEM((chunk + 2 * blk, dc), BF16),
                        pltpu.VMEM((3 * blk, DIL_SPLIT * dc), BF16), pltpu.VMEM((3 * blk, DIL_SPLIT * dc), BF16),
                        pltpu.VMEM((DIL_SPLIT, blk, dc), F32), pltpu.VMEM((DIL_SPLIT, blk, dc), F32),
                        pltpu.VMEM((len(DIL_PATTERNS), chunk, dc), F32),
                        pltpu.VMEM((len(DIL_PATTERNS), chunk, dc), F32)],
        compiler_params=_params(("parallel", "parallel")),
        name="dilated",
    )(qn, kn, kn, kn, vn, vn, vn, qs, ks, ks, ks, vs, vs, vs, cz)


def _tail_body(x_ref, ya_ref, yb_ref, yc_ref, g_ref, gate_ref, wa_ref, wb_ref, wc_ref, wo_ref, lng_ref, lnb_ref,
               out_ref, *, alpha):
    d = x_ref.shape[-1]
    pa = jnp.dot(ya_ref[...], wa_ref[...], preferred_element_type=F32)
    pb = jnp.dot(yb_ref[...], wb_ref[...], preferred_element_type=F32)
    pc = jnp.dot(yc_ref[...], wc_ref[...], preferred_element_type=F32)
    g = g_ref[...].astype(F32)
    merged = g[:, :d] * pa + g[:, d:2 * d] * pb + g[:, 2 * d:] * pc
    sub = jnp.dot(merged.astype(BF16), wo_ref[...], preferred_element_type=F32) * gate_ref[...]
    res = alpha * x_ref[...] + sub
    out_ref[...] = _layernorm(res) * lng_ref[...] + lnb_ref[...]


def _tail(x, ya, yb, yc, g_all, gate, wa, wb, wc, wo, ln_g, ln_b, alpha):
    b, l, d = x.shape
    t = 1024
    tok = lambda w: pl.BlockSpec((None, t, w), lambda i, j: (i, j, 0))
    const = lambda a: pl.BlockSpec(a.shape, lambda i, j: (0,) * a.ndim)
    ln_g, ln_b = ln_g.reshape(1, d), ln_b.reshape(1, d)
    return pl.pallas_call(
        functools.partial(_tail_body, alpha=alpha),
        grid=(b, l // t),
        in_specs=[tok(d), tok(ya.shape[-1]), tok(yb.shape[-1]), tok(yc.shape[-1]), tok(3 * d),
                  pl.BlockSpec((None, 1, d), lambda i, j: (i, 0, 0)),
                  const(wa), const(wb), const(wc), const(wo), const(ln_g), const(ln_b)],
        out_specs=tok(d),
        out_shape=jax.ShapeDtypeStruct((b, l, d), F32),
        compiler_params=_params(("parallel", "parallel")),
        name="tail",
    )(x, ya, yb, yc, g_all, gate.reshape(b, 1, d), wa, wb, wc, wo, ln_g, ln_b)


def _rope_tables(l, heads):
    half = HEAD_DIM // 2
    inv = ROPE_THETA ** (-jnp.arange(half, dtype=F32) / half)
    ang = jnp.arange(l, dtype=F32)[:, None] * inv[None, :]
    cos, sin = jnp.cos(ang), jnp.sin(ang)
    return (jnp.tile(jnp.concatenate([cos, cos], -1), (1, heads)),
            jnp.tile(jnp.concatenate([-sin, sin], -1), (1, heads)))


def _layer(x, ada, lw, consts, alpha):
    d = x.shape[-1]
    shift, scale, gate = ada[:, :d], ada[:, d:2 * d], ada[:, 2 * d:]
    h = _ln_mod(x, scale, shift)
    w_in, b_in = lw["w_in"], lw["b_in"]
    proj_a = _proj(h, w_in[:, :2 * d], b_in[:2 * d])
    q_cols = jnp.arange(d) < d // 4
    proj_b = _proj(h, w_in[:, 2 * d:3 * d], b_in[2 * d:3 * d], col_scale=jnp.where(q_cols, QK_SCALE_LOG2, 1.0))
    cq, ck, cv, cz, cqs, cks, cvs = _proj_rope(h, w_in[:, 3 * d:4 * d], b_in[3 * d:4 * d], *consts["rope"])
    gates = _proj(h, w_in[:, 4 * d:], b_in[4 * d:], sigmoid=True)

    v, x1, x2g = _hy_pre(proj_a, lw["conv_w"], lw["conv_b"])
    ya = _hyena(v, x1, x2g, consts["kspec"], lw["skip"], consts["dft"])
    yb = _na(proj_b, lw["na_bias"])
    yc = _dilated_mixture(cq, ck, cv, cqs, cks, cvs, cz)
    return _tail(x, ya, yb, yc, gates, gate, lw["wa"], lw["wb"], lw["wc"], lw["wo"],
                 lw["ln_g"], lw["ln_b"], alpha)


def kernel(x_prompt, x_sample, c_prompt, c_sample, w_ada, b_ada, w_in, b_in, hy_conv_w, hy_conv_b, hy_w1, hy_b1, hy_freq, hy_w2, hy_b2, hy_w3, hy_b3, hy_decay, hy_skip, na_rpb, w_branch_a, w_branch_b, w_branch_c, w_out, ln_g, ln_b):
    depth, d, _ = w_in.shape
    heads_c = (d // 4) // HEAD_DIM
    alpha = (2 * depth) ** 0.25
    groups = [(x_prompt, c_prompt), (x_sample, c_sample)]

    nb_p = c_prompt.shape[0]
    c_all = jnp.concatenate([c_prompt, c_sample], 0)
    pad_rows = -c_all.shape[0] % 8
    ada_all = _ada(jnp.pad(c_all, ((0, pad_rows), (0, 0))), w_ada, b_ada)
    adas = [ada_all[:, :nb_p], ada_all[:, nb_p:nb_p + c_sample.shape[0]]]

    shared = {}
    for x, _ in groups:
        l = x.shape[1]
        if l not in shared:
            shared[l] = {"dft": _dft_tables(l), "rope": _rope_tables(l, heads_c)}

    ys = [x for x, _ in groups]
    for layer in range(depth):
        lw = {
            "w_in": w_in[layer].astype(BF16), "b_in": b_in[layer],
            "conv_w": hy_conv_w[layer], "conv_b": hy_conv_b[layer], "skip": hy_skip[layer],
            "na_bias": _na_bias_table(na_rpb[layer]),
            "wa": w_branch_a[layer].astype(BF16), "wb": w_branch_b[layer].astype(BF16),
            "wc": w_branch_c[layer].astype(BF16), "wo": w_out[layer].astype(BF16),
            "ln_g": ln_g[layer], "ln_b": ln_b[layer],
        }
        kspecs = {}
        for gi in range(len(groups)):
            l = ys[gi].shape[1]
            if l not in kspecs:
                kspecs[l] = _hyena_spectrum(l, shared[l]["dft"], hy_w1[layer], hy_b1[layer], hy_freq[layer],
                                            hy_w2[layer], hy_b2[layer], hy_w3[layer], hy_b3[layer],
                                            hy_decay[layer])
            consts = dict(shared[l], kspec=kspecs[l])
            ys[gi] = _layer(ys[gi], adas[gi][layer], lw, consts, alpha)
    return tuple(ys)
```

```python
import functools
import math

import jax
import jax.numpy as jnp
from jax import lax
from jax.experimental import pallas as pl
from jax.experimental.pallas import tpu as pltpu

F32 = jnp.float32
BF16 = jnp.bfloat16

GRID_W = 64
HEAD_DIM = 64
HYENA_BANDS = 16
HYENA_EMB = 2 * HYENA_BANDS + 1
NA_KR = 8
NA_KC = 16
DIL_PATTERNS = ((128, 1), (512, 4), (2048, 16))
DIL_BLK = 64
DIL_SPLIT = 16
NA_ROW_UNROLL = 16
ROPE_THETA = 10000.0
LN_EPS = 1e-5
NEG_INF = -1e30
LOG2_E = math.log2(math.e)
QK_SCALE_LOG2 = HEAD_DIM ** -0.5 * LOG2_E

V7X_LANES = 128
V7X_SUBLANES = 8
V7X_VMEM_LIMIT_BYTES = 56 * 1024 * 1024

DFT_N2 = V7X_LANES
PAIR_GROUP = V7X_SUBLANES
DFT_STEP_ROWS = 128
SLABS_PER_STEP = 8
FEAT_PAD = V7X_LANES
FILTER_HEAD_ROWS = 2 * V7X_SUBLANES
BF16_TILE_ROWS = 2 * V7X_SUBLANES

TOKEN_TILE = 1024
PROJ_COL_TILE = 1024
ROPE_TOKEN_TILE = 512
FILTER_ROW_TILE = 512
NA_ROWS_PER_STEP = 16
SPEC_SLABS_PER_STEP = 2


def _params(sem):
    return pltpu.CompilerParams(dimension_semantics=sem, vmem_limit_bytes=V7X_VMEM_LIMIT_BYTES)


def _sigmoid(x):
    return 1.0 / (1.0 + jnp.exp(-x))


def _silu(x):
    return x * _sigmoid(x)


def _ada_body(c_ref, w_ref, b_ref, o_ref):
    s = _silu(c_ref[...])
    o_ref[...] = jnp.dot(s, w_ref[...], preferred_element_type=F32,
                         precision=lax.Precision.HIGHEST) + b_ref[...]


def _ada(c_all, w_ada, b_ada):
    depth, d, n = w_ada.shape
    rows = c_all.shape[0]
    tn = PROJ_COL_TILE
    return pl.pallas_call(
        _ada_body,
        grid=(depth, n // tn),
        in_specs=[
            pl.BlockSpec((rows, d), lambda l, j: (0, 0)),
            pl.BlockSpec((None, d, tn), lambda l, j: (l, 0, j)),
            pl.BlockSpec((None, 1, tn), lambda l, j: (l, 0, j)),
        ],
        out_specs=pl.BlockSpec((None, rows, tn), lambda l, j: (l, 0, j)),
        out_shape=jax.ShapeDtypeStruct((depth, rows, n), F32),
        compiler_params=_params(("parallel", "parallel")),
        name="ada",
    )(c_all, w_ada, b_ada.reshape(depth, 1, n))


def _layernorm(x):
    mu = jnp.mean(x, -1, keepdims=True)
    xc = x - mu
    var = jnp.mean(xc * xc, -1, keepdims=True)
    return xc * lax.rsqrt(var + LN_EPS)


def _ln_mod_body(x_ref, sc_ref, sh_ref, o_ref):
    h = _layernorm(x_ref[...]) * (1.0 + sc_ref[...]) + sh_ref[...]
    o_ref[...] = h.astype(BF16)


def _ln_mod(x, scale, shift):
    b, l, d = x.shape
    t = TOKEN_TILE
    return pl.pallas_call(
        _ln_mod_body,
        grid=(b, l // t),
        in_specs=[
            pl.BlockSpec((None, t, d), lambda i, j: (i, j, 0)),
            pl.BlockSpec((None, 1, d), lambda i, j: (i, 0, 0)),
            pl.BlockSpec((None, 1, d), lambda i, j: (i, 0, 0)),
        ],
        out_specs=pl.BlockSpec((None, t, d), lambda i, j: (i, j, 0)),
        out_shape=jax.ShapeDtypeStruct((b, l, d), BF16),
        compiler_params=_params(("parallel", "parallel")),
        name="ln_mod",
    )(x, scale.reshape(b, 1, d), shift.reshape(b, 1, d))


def _proj_body(h_ref, w_ref, b_ref, *rest, sigmoid, scaled):
    o_ref = rest[-1]
    acc = jnp.dot(h_ref[...], w_ref[...], preferred_element_type=F32) + b_ref[...]
    if scaled:
        acc = acc * rest[0][...]
    if sigmoid:
        acc = _sigmoid(acc)
    o_ref[...] = acc.astype(o_ref.dtype)


def _proj(h, w, bias, col_scale=None, sigmoid=False):
    b, l, d = h.shape
    n = w.shape[1]
    t, tn = TOKEN_TILE, PROJ_COL_TILE
    row = pl.BlockSpec((1, tn), lambda j, i, k: (0, j))
    scale_args = [] if col_scale is None else [col_scale.reshape(1, n)]
    return pl.pallas_call(
        functools.partial(_proj_body, sigmoid=sigmoid, scaled=col_scale is not None),
        grid=(n // tn, b, l // t),
        in_specs=[
            pl.BlockSpec((None, t, d), lambda j, i, k: (i, k, 0)),
            pl.BlockSpec((d, tn), lambda j, i, k: (0, j)),
            row,
        ] + [row] * len(scale_args),
        out_specs=pl.BlockSpec((None, t, tn), lambda j, i, k: (i, k, j)),
        out_shape=jax.ShapeDtypeStruct((b, l, n), BF16),
        compiler_params=_params(("parallel", "parallel", "parallel")),
        name="proj",
    )(h, w, bias.reshape(1, n), *scale_args)


def _rope_lanes(x, cos, sin_signed):
    outs = []
    lane = lax.broadcasted_iota(jnp.int32, (x.shape[0], V7X_LANES), 1)
    first_half = (lane % HEAD_DIM) < (HEAD_DIM // 2)
    for c0 in range(0, x.shape[1], V7X_LANES):
        xc = x[:, c0:c0 + V7X_LANES]
        partner = jnp.where(first_half,
                            pltpu.roll(xc, V7X_LANES - HEAD_DIM // 2, 1),
                            pltpu.roll(xc, HEAD_DIM // 2, 1))
        outs.append(xc * cos[:, c0:c0 + V7X_LANES] + partner * sin_signed[:, c0:c0 + V7X_LANES])
    return jnp.concatenate(outs, 1)


def _store_split(ref, x):
    t, dc = x.shape
    parts = jnp.swapaxes(x.reshape(t // DIL_SPLIT, DIL_SPLIT, dc), 0, 1)
    for r in range(DIL_SPLIT):
        ref[:, r * dc:(r + 1) * dc] = parts[r].astype(BF16)


def _proj_rope_body(h_ref, w_ref, b_ref, cos_ref, sin_ref, q_ref, k_ref, v_ref, z_ref,
                    qs_ref, ks_ref, vs_ref):
    dc = q_ref.shape[-1]
    acc = jnp.dot(h_ref[...], w_ref[...], preferred_element_type=F32) + b_ref[...]
    cos, sin = cos_ref[...], sin_ref[...]
    q = _rope_lanes(acc[:, :dc], cos, sin) * QK_SCALE_LOG2
    k = _rope_lanes(acc[:, dc:2 * dc], cos, sin)
    v = acc[:, 2 * dc:3 * dc]
    for ref, split_ref, val in ((q_ref, qs_ref, q), (k_ref, ks_ref, k), (v_ref, vs_ref, v)):
        ref[...] = val.astype(BF16)
        _store_split(split_ref, val)
    z_ref[...] = _silu(acc[:, 3 * dc:]).astype(BF16)


def _proj_rope(h, w, bias, cos_t, sin_t):
    b, l, d = h.shape
    n = w.shape[1]
    dc = n // 4
    t = ROPE_TOKEN_TILE
    tok = pl.BlockSpec((None, t, dc), lambda i, k: (i, k, 0))
    shp = jax.ShapeDtypeStruct((b, l, dc), BF16)
    spl = pl.BlockSpec((None, t // DIL_SPLIT, DIL_SPLIT * dc), lambda i, k: (i, k, 0))
    spl_shp = jax.ShapeDtypeStruct((b, l // DIL_SPLIT, DIL_SPLIT * dc), BF16)
    return pl.pallas_call(
        _proj_rope_body,
        grid=(b, l // t),
        in_specs=[
            pl.BlockSpec((None, t, d), lambda i, k: (i, k, 0)),
            pl.BlockSpec((d, n), lambda i, k: (0, 0)),
            pl.BlockSpec((1, n), lambda i, k: (0, 0)),
            pl.BlockSpec((t, dc), lambda i, k: (k, 0)),
            pl.BlockSpec((t, dc), lambda i, k: (k, 0)),
        ],
        out_specs=[tok, tok, tok, tok, spl, spl, spl],
        out_shape=[shp, shp, shp, shp, spl_shp, spl_shp, spl_shp],
        compiler_params=_params(("parallel", "parallel")),
        name="proj_rope",
    )(h, w, bias.reshape(1, n), cos_t, sin_t)


def _hy_pre_body(main_ref, prev_ref, next_ref, cw_ref, cb_ref, v_ref, x1_ref, x2_ref, pad_ref, *, da):
    i = pl.program_id(1)
    last = pl.num_programs(1) - 1
    t = main_ref.shape[0]
    halo = prev_ref.shape[0]
    outs = (v_ref, x1_ref, x2_ref)
    az = main_ref[:, 3 * da:].astype(F32)
    gate = _silu(az)
    for part in range(3):
        cols = slice(part * da, (part + 1) * da)
        prev_row = jnp.where(i > 0, prev_ref[:, cols].astype(F32)[halo - 1:halo], 0.0)
        next_row = jnp.where(i < last, next_ref[:, cols].astype(F32)[0:1], 0.0)
        lo = V7X_SUBLANES
        pad_ref[lo - 1:lo, :] = prev_row
        pad_ref[lo:lo + t, :] = main_ref[:, cols].astype(F32)
        pad_ref[lo + t:lo + t + 1, :] = next_row
        uc = (pad_ref[lo - 1:lo - 1 + t, :] * cw_ref[0:1, cols] + pad_ref[lo:lo + t, :] * cw_ref[1:2, cols]
              + pad_ref[lo + 1:lo + 1 + t, :] * cw_ref[2:3, cols] + cb_ref[:, cols])
        if part == 2:
            uc = uc * gate
        outs[part][...] = uc.astype(BF16)


def _hy_pre(proj_a, conv_w, conv_b):
    b, l, n = proj_a.shape
    da = n // 4
    t, halo = TOKEN_TILE, BF16_TILE_ROWS
    nh = t // halo
    tok = pl.BlockSpec((None, t, da), lambda i, j: (i, j, 0))
    shp = jax.ShapeDtypeStruct((b, l, da), BF16)
    return pl.pallas_call(
        functools.partial(_hy_pre_body, da=da),
        grid=(b, l // t),
        in_specs=[
            pl.BlockSpec((None, t, n), lambda i, j: (i, j, 0)),
            pl.BlockSpec((None, halo, n), lambda i, j: (i, jnp.maximum(j * nh - 1, 0), 0)),
            pl.BlockSpec((None, halo, n), lambda i, j: (i, jnp.minimum((j + 1) * nh, l // halo - 1), 0)),
            pl.BlockSpec((3, 3 * da), lambda i, j: (0, 0)),
            pl.BlockSpec((1, 3 * da), lambda i, j: (0, 0)),
        ],
        out_specs=[tok, tok, tok],
        out_shape=[shp, shp, shp],
        scratch_shapes=[pltpu.VMEM((t + 2 * V7X_SUBLANES, da), F32)],
        compiler_params=_params(("parallel", "parallel")),
        name="hy_pre",
    )(proj_a, proj_a, proj_a, conv_w, conv_b.reshape(1, 3 * da))


def _filter_body(feat_ref, w1_ref, b1_ref, f0_ref, w2_ref, b2_ref, f1_ref, w3_ref, b3_ref, dec_ref,
                 hf_ref, sum_ref):
    i = pl.program_id(0)
    hp = lax.Precision.HIGHEST
    feat = feat_ref[...]
    t = feat[:, 0:1]
    h = jnp.sin(f0_ref[...] * (jnp.dot(feat, w1_ref[...], preferred_element_type=F32, precision=hp)
                               + b1_ref[...]))
    h = jnp.sin(f1_ref[...] * (jnp.dot(h, w2_ref[...], preferred_element_type=F32, precision=hp)
                               + b2_ref[...]))
    h = jnp.dot(h.astype(BF16), w3_ref[...], preferred_element_type=F32) + b3_ref[...]
    h = h * jnp.exp(-t * jnp.abs(dec_ref[...]))
    hf_ref[...] = h.astype(BF16)
    abs_sum = jnp.sum(jnp.abs(h), 0, keepdims=True)

    @pl.when(i > 0)
    def _():
        sum_ref[...] += abs_sum

    @pl.when(i == 0)
    def _():
        head = h[0:FILTER_HEAD_ROWS]
        rows = lax.broadcasted_iota(jnp.int32, head.shape, 0)
        cols = lax.broadcasted_iota(jnp.int32, head.shape, 1)
        drop = (rows == 0) & (cols >= h.shape[1] // 2)
        hf_ref[0:FILTER_HEAD_ROWS, :] = jnp.where(drop, 0.0, head).astype(BF16)
        sum_ref[...] = abs_sum - jnp.sum(jnp.where(drop, jnp.abs(head), 0.0), 0, keepdims=True)


def _filter_taps(l, w1, b1, freq, w2, b2, w3, b3, decay):
    fo = w1.shape[1]
    n = w3.shape[1]
    da = decay.shape[0]
    t = jnp.arange(l, dtype=F32) / l
    bands = jnp.arange(1, HYENA_BANDS + 1, dtype=F32)
    ang = 2.0 * math.pi * t[:, None] * bands[None, :]
    feat = jnp.concatenate([t[:, None], jnp.cos(ang), jnp.sin(ang)], -1)
    feat = jnp.pad(feat, ((0, 0), (0, FEAT_PAD - HYENA_EMB)))
    w1p = jnp.pad(w1, ((0, FEAT_PAD - HYENA_EMB), (0, 0)))
    dec = jnp.tile(decay, n // da).reshape(1, n)
    tt = FILTER_ROW_TILE
    const = lambda shape: pl.BlockSpec(shape, lambda i: (0,) * len(shape))
    return pl.pallas_call(
        _filter_body,
        grid=(l // tt,),
        in_specs=[
            pl.BlockSpec((tt, FEAT_PAD), lambda i: (i, 0)),
            const((FEAT_PAD, fo)), const((1, fo)), const((1, fo)),
            const((fo, fo)), const((1, fo)), const((1, fo)),
            const((fo, n)), const((1, n)), const((1, n)),
        ],
        out_specs=[pl.BlockSpec((tt, n), lambda i: (i, 0)), const((1, n))],
        out_shape=[jax.ShapeDtypeStruct((l, n), BF16), jax.ShapeDtypeStruct((1, n), F32)],
        compiler_params=_params(("arbitrary",)),
        name="filter_taps",
    )(feat, w1p, b1.reshape(1, fo), freq[0].reshape(1, fo), w2, b2.reshape(1, fo),
      freq[1].reshape(1, fo), w3.astype(BF16), b3.reshape(1, n), dec)


def _dft_tables(l):
    n = 2 * l
    n1 = n // DFT_N2
    kk = jnp.arange(n1 // 2, dtype=jnp.int32)
    nn = jnp.arange(n1 // 2, dtype=jnp.int32)
    th = (2.0 * math.pi / (2 * n1)) * (((2 * kk[:, None] + 1) * nn[None, :]) % (2 * n1)).astype(F32)
    eye2 = jnp.eye(2, dtype=F32)
    f1 = jnp.kron(jnp.concatenate([jnp.cos(th), -jnp.sin(th)], 0), eye2).astype(BF16)
    g = jnp.kron(jnp.concatenate([jnp.cos(th).T, -jnp.sin(th).T], 1) * (2.0 / n), eye2).astype(BF16)
    k2 = jnp.arange(DFT_N2, dtype=jnp.int32)
    n2 = jnp.arange(DFT_N2, dtype=jnp.int32)
    ph = (n2[None, None, :] * (k2[None, :, None] * (2 * n1) + 2 * kk[:, None, None] + 1)) % (2 * n)
    ang = (2.0 * math.pi / (2 * n)) * ph.astype(F32)
    c, s = jnp.cos(ang), jnp.sin(ang)
    mf = jnp.concatenate([jnp.concatenate([c, s], 2), jnp.concatenate([-s, c], 2)], 1).astype(BF16)
    mi = jnp.swapaxes(mf, 1, 2)
    return f1, g, mf, mi


def _load_pair_group(ref, lead, g):
    start = pl.multiple_of(g * PAIR_GROUP, PAIR_GROUP)
    words = ref.bitcast(jnp.uint32)[(*lead, slice(None), pl.ds(start, PAIR_GROUP), slice(None))]
    words = jnp.swapaxes(words, 0, 1)
    return [pltpu.bitcast(words[i], BF16) for i in range(PAIR_GROUP)]


def _store_pair_group(ref, lead, g, vals):
    start = pl.multiple_of(g * PAIR_GROUP, PAIR_GROUP)
    words = jnp.stack([pltpu.bitcast(v, jnp.uint32) for v in vals], 0)
    ref.bitcast(jnp.uint32)[(*lead, slice(None), pl.ds(start, PAIR_GROUP), slice(None))] = (
        jnp.swapaxes(words, 0, 1))


def _dft_in_body(f_ref, z_ref, a_ref):
    n1 = a_ref.shape[2]
    cb = a_ref.shape[-1]

    def group(g, carry):
        zcat = jnp.concatenate(_load_pair_group(z_ref, (0,), g), 1)
        r = jnp.dot(f_ref[...], zcat, preferred_element_type=F32).astype(BF16)
        cols = [r[:, i * cb:(i + 1) * cb] for i in range(PAIR_GROUP)]
        _store_pair_group(a_ref, (0, 0), g, [c[:2 * n1] for c in cols])
        _store_pair_group(a_ref, (0, 1), g, [c[2 * n1:] for c in cols])
        return carry

    lax.fori_loop(0, z_ref.shape[2] // (2 * PAIR_GROUP), group, 0, unroll=4)


def _dft_in(f1p, z):
    b, half, n2, c = z.shape
    n1 = half
    cb, rs = V7X_LANES, DFT_STEP_ROWS
    return pl.pallas_call(
        _dft_in_body,
        grid=(b, c // cb, n2 // rs),
        in_specs=[
            pl.BlockSpec(f1p.shape, lambda i, j, s: (0, 0)),
            pl.BlockSpec((1, half, rs, cb), lambda i, j, s: (i, 0, s, j)),
        ],
        out_specs=pl.BlockSpec((1, 2, n1, rs, cb), lambda i, j, s: (i, 0, 0, s, j)),
        out_shape=jax.ShapeDtypeStruct((b, 2, n1, n2, c), BF16),
        compiler_params=_params(("parallel", "parallel", "parallel")),
        name="dft_in",
    )(f1p, z)


def _dft_out_body(g_ref, b_ref, x_ref, z_ref, skip_ref, o_ref):
    cb = o_ref.shape[-1]

    def group(g, carry):
        re, im = _load_pair_group(b_ref, (0, 0), g), _load_pair_group(b_ref, (0, 1), g)
        bcat = jnp.concatenate([jnp.concatenate([r, i], 0) for r, i in zip(re, im)], 1)
        y = jnp.dot(g_ref[...], bcat, preferred_element_type=F32)
        xs, zs = _load_pair_group(x_ref, (0,), g), _load_pair_group(z_ref, (0,), g)
        outs = [(xs[i].astype(F32) * (y[:, i * cb:(i + 1) * cb] + skip_ref[...] * zs[i].astype(F32))
                 ).astype(BF16) for i in range(PAIR_GROUP)]
        _store_pair_group(o_ref, (0,), g, outs)
        return carry

    lax.fori_loop(0, x_ref.shape[2] // (2 * PAIR_GROUP), group, 0)


def _dft_out(gp, bb, x, z, skip):
    b, half, n2, c = z.shape
    n1 = half
    cb, rs = V7X_LANES, DFT_STEP_ROWS
    slab = pl.BlockSpec((1, half, rs, cb), lambda i, j, s: (i, 0, s, j))
    return pl.pallas_call(
        _dft_out_body,
        grid=(b, c // cb, n2 // rs),
        in_specs=[
            pl.BlockSpec(gp.shape, lambda i, j, s: (0, 0)),
            pl.BlockSpec((1, 2, n1, rs, cb), lambda i, j, s: (i, 0, 0, s, j)),
            slab, slab,
            pl.BlockSpec((1, cb), lambda i, j, s: (0, j)),
        ],
        out_specs=slab,
        out_shape=jax.ShapeDtypeStruct(z.shape, BF16),
        compiler_params=_params(("parallel", "parallel", "parallel")),
        name="dft_out",
    )(gp, bb, x, z, skip.astype(F32).reshape(1, c))


def _slab_conv_body(mf_ref, mi_ref, k_ref, a_ref, o_ref):
    nb, _, ks, n2, c = a_ref.shape
    for kk in range(ks):
        kr, ki = k_ref[kk, 0].astype(F32), k_ref[kk, 1].astype(F32)
        for b in range(nb):
            a = a_ref[b, :, kk].reshape(2 * n2, c)
            x = jnp.dot(mf_ref[kk], a, preferred_element_type=F32)
            xr, xi = x[:n2], x[n2:]
            y = jnp.concatenate([xr * kr - xi * ki, xr * ki + xi * kr], 0).astype(BF16)
            out = jnp.dot(mi_ref[kk], y, preferred_element_type=F32)
            o_ref[b, :, kk] = out.astype(BF16).reshape(2, n2, c)


def _slab_conv(mf, mi, kspec, a, order):
    b, _, n1, n2, c = a.shape
    m = 2 * n2
    ks = max(1, SLABS_PER_STEP // b)
    return pl.pallas_call(
        _slab_conv_body,
        grid=(n1 // ks,),
        in_specs=[
            pl.BlockSpec((ks, m, m), lambda k: (k, 0, 0)),
            pl.BlockSpec((ks, m, m), lambda k: (k, 0, 0)),
            pl.BlockSpec((ks, 2, n2, c), lambda k: (k, 0, 0, order)),
            pl.BlockSpec((b, 2, ks, n2, c), lambda k: (0, 0, k, 0, 0)),
        ],
        out_specs=pl.BlockSpec((b, 2, ks, n2, c), lambda k: (0, 0, k, 0, 0)),
        out_shape=jax.ShapeDtypeStruct(a.shape, BF16),
        compiler_params=_params(("parallel",)),
        name="slab_conv",
    )(mf, mi, kspec, a)


def _slab_spec_body(mf_ref, sum_ref, a_ref, k_ref):
    _, ks, n2, c = a_ref.shape
    half = c // 2
    inv = 1.0 / (sum_ref[:, :half] + sum_ref[:, half:] + 1e-6)
    for kk in range(ks):
        x = jnp.dot(mf_ref[kk], a_ref[:, kk].reshape(2 * n2, c), preferred_element_type=F32)
        k_ref[kk, 0] = ((x[:n2, :half] + x[:n2, half:]) * inv).astype(BF16)
        k_ref[kk, 1] = ((x[n2:, :half] - x[n2:, half:]) * inv).astype(BF16)


def _slab_spec(mf, sums, a):
    _, _, n1, n2, c = a.shape
    m = 2 * n2
    ks = SPEC_SLABS_PER_STEP
    return pl.pallas_call(
        _slab_spec_body,
        grid=(n1 // ks,),
        in_specs=[
            pl.BlockSpec((ks, m, m), lambda k: (k, 0, 0)),
            pl.BlockSpec((1, c), lambda k: (0, 0)),
            pl.BlockSpec((None, 2, ks, n2, c), lambda k: (0, 0, k, 0, 0)),
        ],
        out_specs=pl.BlockSpec((ks, 2, n2, c // 2), lambda k: (k, 0, 0, 0)),
        out_shape=jax.ShapeDtypeStruct((n1, 2, n2, c // 2), BF16),
        compiler_params=_params(("parallel",)),
        name="slab_spec",
    )(mf, sums, a)


def _hyena_spectrum(l, tables, w1, b1, freq, w2, b2, w3, b3, decay):
    f1p, _, mf, _ = tables
    taps, sums = _filter_taps(l, w1, b1, freq, w2, b2, w3, b3, decay)
    a = _dft_in(f1p, taps.reshape(1, l // DFT_N2, DFT_N2, taps.shape[1]))
    return _slab_spec(mf, sums, a)


def _hyena(v, x1, x2g, kspec, skip, tables):
    f1p, gp, mf, mi = tables
    b, l, c = v.shape
    slabs = lambda u: u.reshape(b, l // DFT_N2, DFT_N2, c)
    z = slabs(v)
    for order, xg in enumerate((slabs(x1), slabs(x2g))):
        bb = _slab_conv(mf, mi, kspec, _dft_in(f1p, z), order)
        z = _dft_out(gp, bb, xg, z, skip[order])
    return z.reshape(b, l, c)


def _head_masks(rows, heads):
    lane = lax.broadcasted_iota(jnp.int32, (rows, heads * HEAD_DIM), 1)
    return [(lane >= h * HEAD_DIM) & (lane < (h + 1) * HEAD_DIM) for h in range(heads)]


def _stack_heads(q, masks):
    zero = jnp.zeros_like(q)
    return jnp.concatenate([jnp.where(m, q, zero) for m in masks], 0)


def _unstack_heads(res, masks, rows):
    out = jnp.where(masks[0], res[:rows], 0.0)
    for h in range(1, len(masks)):
        out = out + jnp.where(masks[h], res[h * rows:(h + 1) * rows], 0.0)
    return out


def _na_bias_table(rpb):
    heads = rpb.shape[0]
    c = jnp.arange(GRID_W)
    col_start = jnp.clip(c - NA_KC // 2, 0, GRID_W - NA_KC)
    col_ok = (c[None, :] >= col_start[:, None]) & (c[None, :] < col_start[:, None] + NA_KC)
    dc = jnp.clip(c[None, :] - c[:, None], -(NA_KC - 1), NA_KC - 1) + NA_KC - 1
    onehot = (dc[:, :, None] == jnp.arange(2 * NA_KC - 1)[None, None, :]).astype(F32)
    tcol = jnp.einsum("hrj,qkj->hrqk", rpb.astype(F32), onehot, precision=lax.Precision.HIGHEST)
    per_idx = [tcol[:, NA_KR - 1 - idx:2 * NA_KR - 1 - idx] for idx in range(NA_KR)]
    bias = jnp.transpose(jnp.stack(per_idx, 0), (0, 1, 3, 2, 4))
    bias = jnp.where(col_ok[None, None, :, None, :], bias * LOG2_E, NEG_INF)
    return bias.reshape(NA_KR, heads * GRID_W, NA_KR * GRID_W)


def _na_body(q_ref, k_ref, v_ref, z_ref, bias_ref, o_ref, *, rows, rb, heads):
    blk = pl.program_id(1)
    masks = _head_masks(GRID_W, heads)
    span = NA_KR * GRID_W

    def one_row(rr, carry):
        r = blk * rb + rr
        start = jnp.clip(r - NA_KR // 2, 0, rows - NA_KR)
        idx = r - start
        koff = pl.multiple_of(start * GRID_W, GRID_W)
        qoff = pl.multiple_of(rr * GRID_W, GRID_W)
        q = q_ref[pl.ds(qoff, GRID_W), :]
        kk = k_ref[pl.ds(koff, span), :]
        vv = v_ref[pl.ds(koff, span), :]
        s = lax.dot_general(_stack_heads(q, masks), kk, (((1,), (1,)), ((), ())),
                            preferred_element_type=F32)
        s = s + bias_ref[idx]
        m = jnp.max(s, -1, keepdims=True)
        e = jnp.exp2(s - m)
        p = (e / jnp.sum(e, -1, keepdims=True)).astype(BF16)
        o = _unstack_heads(jnp.dot(p, vv, preferred_element_type=F32), masks, GRID_W)
        z = z_ref[pl.ds(qoff, GRID_W), :].astype(F32)
        o_ref[pl.ds(qoff, GRID_W), :] = (o * _silu(z)).astype(BF16)
        return carry

    lax.fori_loop(0, rb, one_row, 0, unroll=NA_ROW_UNROLL)


def _na(proj_b, bias):
    b, l, n = proj_b.shape
    db = n // 4
    heads = db // HEAD_DIM
    rows = l // GRID_W
    rb = NA_ROWS_PER_STEP
    t = rb * GRID_W
    return pl.pallas_call(
        functools.partial(_na_body, rows=rows, rb=rb, heads=heads),
        grid=(b, rows // rb),
        in_specs=[
            pl.BlockSpec((None, t, db), lambda i, j: (i, j, 0)),
            pl.BlockSpec((None, l, db), lambda i, j: (i, 0, 1)),
            pl.BlockSpec((None, l, db), lambda i, j: (i, 0, 2)),
            pl.BlockSpec((None, t, db), lambda i, j: (i, j, 3)),
            pl.BlockSpec(bias.shape, lambda i, j: (0, 0, 0)),
        ],
        out_specs=pl.BlockSpec((None, t, db), lambda i, j: (i, j, 0)),
        out_shape=jax.ShapeDtypeStruct((b, l, db), BF16),
        compiler_params=_params(("parallel", "parallel")),
        name="na",
    )(proj_b, proj_b, proj_b, proj_b, bias)


def _attend(q, kk, vv, valid, masks):
    rows = q.shape[0]
    s = lax.dot_general(_stack_heads(q, masks), kk, (((1,), (1,)), ((), ())), preferred_element_type=F32)
    s = jnp.where(valid, s, NEG_INF)
    m = jnp.max(s, -1, keepdims=True)
    e = jnp.exp2(s - m)
    l = jnp.sum(e, -1, keepdims=True)
    p = (e / l).astype(BF16)
    o = _unstack_heads(jnp.dot(p, vv, preferred_element_type=F32), masks, rows)
    lse = _unstack_heads(jnp.broadcast_to(m + jnp.log2(l), (s.shape[0], q.shape[1])), masks, rows)
    return o, lse


def _split_pattern(dilation, n, nchunks, q_ref, kbuf, vbuf, o_split, l_split, masks, heads):
    blk = DIL_BLK
    dc = heads * HEAD_DIM
    phases = DIL_SPLIT // dilation
    mq = blk // phases
    row = lax.broadcasted_iota(jnp.int32, (heads * blk, 3 * blk), 0) % blk
    col = lax.broadcasted_iota(jnp.int32, (heads * blk, 3 * blk), 1)
    qa, qm = row // mq, row % mq
    ka = sum((col >= a * 3 * mq).astype(jnp.int32) for a in range(1, phases)) if phases > 1 else 0
    km = col - ka * (3 * mq)
    band = jnp.abs(phases * (km - mq - qm) + (ka - qa)) <= blk
    for m0 in range(0, blk, mq):
        gm = km + (n * blk + m0 - mq)
        valid = band & (gm >= 0) & (gm < nchunks * blk)
        for j in range(dilation):
            lanes = [slice((j + dilation * a) * dc, (j + dilation * a + 1) * dc) for a in range(phases)]
            q = jnp.concatenate([q_ref[m0:m0 + mq, ln] for ln in lanes], 0)
            krows = slice(blk + m0 - mq, blk + m0 + 2 * mq)
            kk = jnp.concatenate([kbuf[krows, ln] for ln in lanes], 0)
            vv = jnp.concatenate([vbuf[krows, ln] for ln in lanes], 0)
            o, lse = _attend(q, kk, vv, valid, masks)
            for a in range(phases):
                o_split[j + dilation * a, m0:m0 + mq, :] = o[a * mq:(a + 1) * mq]
                l_split[j + dilation * a, m0:m0 + mq, :] = lse[a * mq:(a + 1) * mq]


def _dil_body(qn_ref, knp_ref, kn_ref, knn_ref, vnp_ref, vn_ref, vnn_ref,
              qs_ref, ksp_ref, ks_ref, ksn_ref, vsp_ref, vs_ref, vsn_ref, cz_ref, y_ref,
              kbn, vbn, kbs, vbs, o_split, l_split, o_all, l_all, *, heads, nchunks):
    n = pl.program_id(1)
    blk = DIL_BLK
    chunk = qn_ref.shape[0]
    per = chunk // blk
    masks = _head_masks(blk, heads)
    for buf, prev, cur, nxt, halo in ((kbn, knp_ref, kn_ref, knn_ref, blk), (vbn, vnp_ref, vn_ref, vnn_ref, blk),
                                      (kbs, ksp_ref, ks_ref, ksn_ref, blk), (vbs, vsp_ref, vs_ref, vsn_ref, blk)):
        main = cur.shape[0]
        buf[0:halo] = prev[...]
        buf[halo:halo + main] = cur[...]
        buf[halo + main:] = nxt[...]

    qi = lax.broadcasted_iota(jnp.int32, (heads * blk, 3 * blk), 0) % blk
    ki = lax.broadcasted_iota(jnp.int32, (heads * blk, 3 * blk), 1)
    band = jnp.abs(ki - blk - qi) <= blk

    def token_block(i, carry):
        g = n * per + i
        off = pl.multiple_of(i * blk, blk)
        valid = band & ((ki >= blk) | (g > 0)) & ((ki < 2 * blk) | (g < nchunks * per - 1))
        o, lse = _attend(qn_ref[pl.ds(off, blk), :], kbn[pl.ds(off, 3 * blk), :], vbn[pl.ds(off, 3 * blk), :],
                         valid, masks)
        o_all[0, pl.ds(off, blk), :] = o
        l_all[0, pl.ds(off, blk), :] = lse
        return carry

    lax.fori_loop(0, per, token_block, 0, unroll=True)

    for g, (_, dilation) in enumerate(DIL_PATTERNS):
        if dilation == 1:
            continue
        _split_pattern(dilation, n, nchunks, qs_ref, kbs, vbs, o_split, l_split, masks, heads)
        o_all[g] = jnp.swapaxes(o_split[...], 0, 1).reshape(chunk, heads * HEAD_DIM)
        l_all[g] = jnp.swapaxes(l_split[...], 0, 1).reshape(chunk, heads * HEAD_DIM)

    ls = [l_all[g] for g in range(len(DIL_PATTERNS))]
    m = functools.reduce(jnp.maximum, ls)
    es = [jnp.exp2(l - m) for l in ls]
    den = functools.reduce(lambda a, b: a + b, es)
    o = functools.reduce(lambda a, b: a + b, [(e / den) * o_all[g] for g, e in enumerate(es)])
    y_ref[...] = (o * cz_ref[...].astype(F32)).astype(BF16)


def _dilated_mixture(qn, kn, vn, qs, ks, vs, cz):
    b, l, dc = qn.shape
    heads = dc // HEAD_DIM
    blk = DIL_BLK
    assert DIL_PATTERNS[0][1] == 1 and all(w == 2 * blk * d and DIL_SPLIT % d == 0 for w, d in DIL_PATTERNS)
    chunk = DIL_SPLIT * blk
    nchunks = l // chunk
    per = chunk // blk
    nat = pl.BlockSpec((None, chunk, dc), lambda i, n: (i, n, 0))
    nat_prev = pl.BlockSpec((None, blk, dc), lambda i, n: (i, jnp.maximum(n * per - 1, 0), 0))
    nat_next = pl.BlockSpec((None, blk, dc), lambda i, n: (i, jnp.minimum((n + 1) * per, nchunks * per - 1), 0))
    spl = pl.BlockSpec((None, blk, DIL_SPLIT * dc), lambda i, n: (i, n, 0))
    spl_prev = pl.BlockSpec((None, blk, DIL_SPLIT * dc), lambda i, n: (i, jnp.maximum(n - 1, 0), 0))
    spl_next = pl.BlockSpec((None, blk, DIL_SPLIT * dc), lambda i, n: (i, jnp.minimum(n + 1, nchunks - 1), 0))
    return pl.pallas_call(
        functools.partial(_dil_body, heads=heads, nchunks=nchunks),
        grid=(b, nchunks),
        in_specs=[nat, nat_prev, nat, nat_next, nat_prev, nat, nat_next,
                  spl, spl_prev, spl, spl_next, spl_prev, spl, spl_next, nat],
        out_specs=nat,
        out_shape=jax.ShapeDtypeStruct((b, l, dc), BF16),
        scratch_shapes=[pltpu.VMEM((chunk + 2 * blk, dc), BF16), pltpu.VMEM((chunk + 2 * blk, dc), BF16),
                        pltpu.VMEM((3 * blk, DIL_SPLIT * dc), BF16), pltpu.VMEM((3 * blk, DIL_SPLIT * dc), BF16),
                        pltpu.VMEM((DIL_SPLIT, blk, dc), F32), pltpu.VMEM((DIL_SPLIT, blk, dc), F32),
                        pltpu.VMEM((len(DIL_PATTERNS), chunk, dc), F32),
                        pltpu.VMEM((len(DIL_PATTERNS), chunk, dc), F32)],
        compiler_params=_params(("parallel", "parallel")),
        name="dilated",
    )(qn, kn, kn, kn, vn, vn, vn, qs, ks, ks, ks, vs, vs, vs, cz)


def _tail_body(x_ref, ya_ref, yb_ref, yc_ref, g_ref, gate_ref, wa_ref, wb_ref, wc_ref, wo_ref, lng_ref, lnb_ref,
               out_ref, *, alpha):
    d = x_ref.shape[-1]
    pa = jnp.dot(ya_ref[...], wa_ref[...], preferred_element_type=F32)
    pb = jnp.dot(yb_ref[...], wb_ref[...], preferred_element_type=F32)
    pc = jnp.dot(yc_ref[...], wc_ref[...], preferred_element_type=F32)
    g = g_ref[...].astype(F32)
    merged = g[:, :d] * pa + g[:, d:2 * d] * pb + g[:, 2 * d:] * pc
    sub = jnp.dot(merged.astype(BF16), wo_ref[...], preferred_element_type=F32) * gate_ref[...]
    res = alpha * x_ref[...] + sub
    out_ref[...] = _layernorm(res) * lng_ref[...] + lnb_ref[...]


def _tail(x, ya, yb, yc, g_all, gate, wa, wb, wc, wo, ln_g, ln_b, alpha):
    b, l, d = x.shape
    t = TOKEN_TILE
    tok = lambda w: pl.BlockSpec((None, t, w), lambda i, j: (i, j, 0))
    const = lambda a: pl.BlockSpec(a.shape, lambda i, j: (0,) * a.ndim)
    ln_g, ln_b = ln_g.reshape(1, d), ln_b.reshape(1, d)
    return pl.pallas_call(
        functools.partial(_tail_body, alpha=alpha),
        grid=(b, l // t),
        in_specs=[tok(d), tok(ya.shape[-1]), tok(yb.shape[-1]), tok(yc.shape[-1]), tok(3 * d),
                  pl.BlockSpec((None, 1, d), lambda i, j: (i, 0, 0)),
                  const(wa), const(wb), const(wc), const(wo), const(ln_g), const(ln_b)],
        out_specs=tok(d),
        out_shape=jax.ShapeDtypeStruct((b, l, d), F32),
        compiler_params=_params(("parallel", "parallel")),
        name="tail",
    )(x, ya, yb, yc, g_all, gate.reshape(b, 1, d), wa, wb, wc, wo, ln_g, ln_b)


def _rope_tables(l, heads):
    half = HEAD_DIM // 2
    inv = ROPE_THETA ** (-jnp.arange(half, dtype=F32) / half)
    ang = jnp.arange(l, dtype=F32)[:, None] * inv[None, :]
    cos, sin = jnp.cos(ang), jnp.sin(ang)
    return (jnp.tile(jnp.concatenate([cos, cos], -1), (1, heads)),
            jnp.tile(jnp.concatenate([-sin, sin], -1), (1, heads)))


def _layer(x, ada, lw, consts, alpha):
    d = x.shape[-1]
    shift, scale, gate = ada[:, :d], ada[:, d:2 * d], ada[:, 2 * d:]
    h = _ln_mod(x, scale, shift)
    w_in, b_in = lw["w_in"], lw["b_in"]
    proj_a = _proj(h, w_in[:, :2 * d], b_in[:2 * d])
    q_cols = jnp.arange(d) < d // 4
    proj_b = _proj(h, w_in[:, 2 * d:3 * d], b_in[2 * d:3 * d], col_scale=jnp.where(q_cols, QK_SCALE_LOG2, 1.0))
    cq, ck, cv, cz, cqs, cks, cvs = _proj_rope(h, w_in[:, 3 * d:4 * d], b_in[3 * d:4 * d], *consts["rope"])
    gates = _proj(h, w_in[:, 4 * d:], b_in[4 * d:], sigmoid=True)

    v, x1, x2g = _hy_pre(proj_a, lw["conv_w"], lw["conv_b"])
    ya = _hyena(v, x1, x2g, consts["kspec"], lw["skip"], consts["dft"])
    yb = _na(proj_b, lw["na_bias"])
    yc = _dilated_mixture(cq, ck, cv, cqs, cks, cvs, cz)
    return _tail(x, ya, yb, yc, gates, gate, lw["wa"], lw["wb"], lw["wc"], lw["wo"],
                 lw["ln_g"], lw["ln_b"], alpha)


def kernel(x_prompt, x_sample, c_prompt, c_sample, w_ada, b_ada, w_in, b_in, hy_conv_w, hy_conv_b, hy_w1, hy_b1, hy_freq, hy_w2, hy_b2, hy_w3, hy_b3, hy_decay, hy_skip, na_rpb, w_branch_a, w_branch_b, w_branch_c, w_out, ln_g, ln_b):
    depth, d, _ = w_in.shape
    heads_c = (d // 4) // HEAD_DIM
    alpha = (2 * depth) ** 0.25
    groups = [(x_prompt, c_prompt), (x_sample, c_sample)]

    nb_p = c_prompt.shape[0]
    c_all = jnp.concatenate([c_prompt, c_sample], 0)
    pad_rows = -c_all.shape[0] % 8
    ada_all = _ada(jnp.pad(c_all, ((0, pad_rows), (0, 0))), w_ada, b_ada)
    adas = [ada_all[:, :nb_p], ada_all[:, nb_p:nb_p + c_sample.shape[0]]]

    shared = {}
    for x, _ in groups:
        l = x.shape[1]
        if l not in shared:
            shared[l] = {"dft": _dft_tables(l), "rope": _rope_tables(l, heads_c)}

    ys = [x for x, _ in groups]
    for layer in range(depth):
        lw = {
            "w_in": w_in[layer].astype(BF16), "b_in": b_in[layer],
            "conv_w": hy_conv_w[layer], "conv_b": hy_conv_b[layer], "skip": hy_skip[layer],
            "na_bias": _na_bias_table(na_rpb[layer]),
            "wa": w_branch_a[layer].astype(BF16), "wb": w_branch_b[layer].astype(BF16),
            "wc": w_branch_c[layer].astype(BF16), "wo": w_out[layer].astype(BF16),
            "ln_g": ln_g[layer], "ln_b": ln_b[layer],
        }
        kspecs = {}
        for gi in range(len(groups)):
            l = ys[gi].shape[1]
            if l not in kspecs:
                kspecs[l] = _hyena_spectrum(l, shared[l]["dft"], hy_w1[layer], hy_b1[layer], hy_freq[layer],
                                            hy_w2[layer], hy_b2[layer], hy_w3[layer], hy_b3[layer],
                                            hy_decay[layer])
            consts = dict(shared[l], kspec=kspecs[l])
            ys[gi] = _layer(ys[gi], adas[gi][layer], lw, consts, alpha)
    return tuple(ys)
```

```python
import functools
import math

import jax
import jax.numpy as jnp
from jax import lax
from jax.experimental import pallas as pl
from jax.experimental.pallas import tpu as pltpu

F32 = jnp.float32
BF16 = jnp.bfloat16

GRID_W = 64
HEAD_DIM = 64
HYENA_BANDS = 16
HYENA_EMB = 2 * HYENA_BANDS + 1
NA_KR = 8
NA_KC = 16
DIL_PATTERNS = ((128, 1), (512, 4), (2048, 16))
DIL_BLK = 64
DIL_SPLIT = 16
NA_ROW_UNROLL = 32
ROPE_THETA = 10000.0
LN_EPS = 1e-5
NEG_INF = -1e30
LOG2_E = math.log2(math.e)
QK_SCALE_LOG2 = HEAD_DIM ** -0.5 * LOG2_E

V7X_LANES = 128
V7X_SUBLANES = 8
V7X_VMEM_LIMIT_BYTES = 56 * 1024 * 1024

DFT_N2 = V7X_LANES
PAIR_GROUP = V7X_SUBLANES
DFT_STEP_ROWS = 128
SLABS_PER_STEP = 8
FEAT_PAD = V7X_LANES
FILTER_HEAD_ROWS = 2 * V7X_SUBLANES
BF16_TILE_ROWS = 2 * V7X_SUBLANES

TOKEN_TILE = 1024
PROJ_COL_TILE = 1024
ROPE_TOKEN_TILE = 1024
FILTER_ROW_TILE = 1024
NA_ROWS_PER_STEP = 32
SPEC_SLABS_PER_STEP = 2


def _params(sem):
    return pltpu.CompilerParams(dimension_semantics=sem, vmem_limit_bytes=V7X_VMEM_LIMIT_BYTES)


def _sigmoid(x):
    return 1.0 / (1.0 + jnp.exp(-x))


def _silu(x):
    return x * _sigmoid(x)


def _ada_body(c_ref, w_ref, b_ref, o_ref):
    s = _silu(c_ref[...])
    o_ref[...] = jnp.dot(s, w_ref[...], preferred_element_type=F32,
                         precision=lax.Precision.HIGHEST) + b_ref[...]


def _ada(c_all, w_ada, b_ada):
    depth, d, n = w_ada.shape
    rows = c_all.shape[0]
    tn = PROJ_COL_TILE
    return pl.pallas_call(
        _ada_body,
        grid=(depth, n // tn),
        in_specs=[
            pl.BlockSpec((rows, d), lambda l, j: (0, 0)),
            pl.BlockSpec((None, d, tn), lambda l, j: (l, 0, j)),
            pl.BlockSpec((None, 1, tn), lambda l, j: (l, 0, j)),
        ],
        out_specs=pl.BlockSpec((None, rows, tn), lambda l, j: (l, 0, j)),
        out_shape=jax.ShapeDtypeStruct((depth, rows, n), F32),
        compiler_params=_params(("parallel", "parallel")),
        name="ada",
    )(c_all, w_ada, b_ada.reshape(depth, 1, n))


def _layernorm(x):
    mu = jnp.mean(x, -1, keepdims=True)
    xc = x - mu
    var = jnp.mean(xc * xc, -1, keepdims=True)
    return xc * lax.rsqrt(var + LN_EPS)


def _ln_mod_body(x_ref, sc_ref, sh_ref, o_ref):
    h = _layernorm(x_ref[...]) * (1.0 + sc_ref[...]) + sh_ref[...]
    o_ref[...] = h.astype(BF16)


def _ln_mod(x, scale, shift):
    b, l, d = x.shape
    t = TOKEN_TILE
    return pl.pallas_call(
        _ln_mod_body,
        grid=(b, l // t),
        in_specs=[
            pl.BlockSpec((None, t, d), lambda i, j: (i, j, 0)),
            pl.BlockSpec((None, 1, d), lambda i, j: (i, 0, 0)),
            pl.BlockSpec((None, 1, d), lambda i, j: (i, 0, 0)),
        ],
        out_specs=pl.BlockSpec((None, t, d), lambda i, j: (i, j, 0)),
        out_shape=jax.ShapeDtypeStruct((b, l, d), BF16),
        compiler_params=_params(("parallel", "parallel")),
        name="ln_mod",
    )(x, scale.reshape(b, 1, d), shift.reshape(b, 1, d))


def _proj_body(h_ref, w_ref, b_ref, *rest, sigmoid, scaled):
    o_ref = rest[-1]
    acc = jnp.dot(h_ref[...], w_ref[...], preferred_element_type=F32) + b_ref[...]
    if scaled:
        acc = acc * rest[0][...]
    if sigmoid:
        acc = _sigmoid(acc)
    o_ref[...] = acc.astype(o_ref.dtype)


def _proj(h, w, bias, col_scale=None, sigmoid=False):
    b, l, d = h.shape
    n = w.shape[1]
    t, tn = TOKEN_TILE, PROJ_COL_TILE
    row = pl.BlockSpec((1, tn), lambda j, i, k: (0, j))
    scale_args = [] if col_scale is None else [col_scale.reshape(1, n)]
    return pl.pallas_call(
        functools.partial(_proj_body, sigmoid=sigmoid, scaled=col_scale is not None),
        grid=(n // tn, b, l // t),
        in_specs=[
            pl.BlockSpec((None, t, d), lambda j, i, k: (i, k, 0)),
            pl.BlockSpec((d, tn), lambda j, i, k: (0, j)),
            row,
        ] + [row] * len(scale_args),
        out_specs=pl.BlockSpec((None, t, tn), lambda j, i, k: (i, k, j)),
        out_shape=jax.ShapeDtypeStruct((b, l, n), BF16),
        compiler_params=_params(("parallel", "parallel", "parallel")),
        name="proj",
    )(h, w, bias.reshape(1, n), *scale_args)


def _rope_lanes(x, cos, sin_signed):
    outs = []
    lane = lax.broadcasted_iota(jnp.int32, (x.shape[0], V7X_LANES), 1)
    first_half = (lane % HEAD_DIM) < (HEAD_DIM // 2)
    for c0 in range(0, x.shape[1], V7X_LANES):
        xc = x[:, c0:c0 + V7X_LANES]
        partner = jnp.where(first_half,
                            pltpu.roll(xc, V7X_LANES - HEAD_DIM // 2, 1),
                            pltpu.roll(xc, HEAD_DIM // 2, 1))
        outs.append(xc * cos[:, c0:c0 + V7X_LANES] + partner * sin_signed[:, c0:c0 + V7X_LANES])
    return jnp.concatenate(outs, 1)


def _store_split(ref, x):
    t, dc = x.shape
    parts = jnp.swapaxes(x.reshape(t // DIL_SPLIT, DIL_SPLIT, dc), 0, 1)
    for r in range(DIL_SPLIT):
        ref[:, r * dc:(r + 1) * dc] = parts[r].astype(BF16)


def _proj_rope_body(h_ref, w_ref, b_ref, cos_ref, sin_ref, q_ref, k_ref, v_ref, z_ref,
                    qs_ref, ks_ref, vs_ref):
    dc = q_ref.shape[-1]
    acc = jnp.dot(h_ref[...], w_ref[...], preferred_element_type=F32) + b_ref[...]
    cos, sin = cos_ref[...], sin_ref[...]
    q = _rope_lanes(acc[:, :dc], cos, sin) * QK_SCALE_LOG2
    k = _rope_lanes(acc[:, dc:2 * dc], cos, sin)
    v = acc[:, 2 * dc:3 * dc]
    for ref, split_ref, val in ((q_ref, qs_ref, q), (k_ref, ks_ref, k), (v_ref, vs_ref, v)):
        ref[...] = val.astype(BF16)
        _store_split(split_ref, val)
    z_ref[...] = _silu(acc[:, 3 * dc:]).astype(BF16)


def _proj_rope(h, w, bias, cos_t, sin_t):
    b, l, d = h.shape
    n = w.shape[1]
    dc = n // 4
    t = ROPE_TOKEN_TILE
    tok = pl.BlockSpec((None, t, dc), lambda i, k: (i, k, 0))
    shp = jax.ShapeDtypeStruct((b, l, dc), BF16)
    spl = pl.BlockSpec((None, t // DIL_SPLIT, DIL_SPLIT * dc), lambda i, k: (i, k, 0))
    spl_shp = jax.ShapeDtypeStruct((b, l // DIL_SPLIT, DIL_SPLIT * dc), BF16)
    return pl.pallas_call(
        _proj_rope_body,
        grid=(b, l // t),
        in_specs=[
            pl.BlockSpec((None, t, d), lambda i, k: (i, k, 0)),
            pl.BlockSpec((d, n), lambda i, k: (0, 0)),
            pl.BlockSpec((1, n), lambda i, k: (0, 0)),
            pl.BlockSpec((t, dc), lambda i, k: (k, 0)),
            pl.BlockSpec((t, dc), lambda i, k: (k, 0)),
        ],
        out_specs=[tok, tok, tok, tok, spl, spl, spl],
        out_shape=[shp, shp, shp, shp, spl_shp, spl_shp, spl_shp],
        compiler_params=_params(("parallel", "parallel")),
        name="proj_rope",
    )(h, w, bias.reshape(1, n), cos_t, sin_t)


def _hy_pre_body(main_ref, prev_ref, next_ref, cw_ref, cb_ref, v_ref, x1_ref, x2_ref, pad_ref, *, da):
    i = pl.program_id(1)
    last = pl.num_programs(1) - 1
    t = main_ref.shape[0]
    halo = prev_ref.shape[0]
    outs = (v_ref, x1_ref, x2_ref)
    az = main_ref[:, 3 * da:].astype(F32)
    gate = _silu(az)
    for part in range(3):
        cols = slice(part * da, (part + 1) * da)
        prev_row = jnp.where(i > 0, prev_ref[:, cols].astype(F32)[halo - 1:halo], 0.0)
        next_row = jnp.where(i < last, next_ref[:, cols].astype(F32)[0:1], 0.0)
        lo = V7X_SUBLANES
        pad_ref[lo - 1:lo, :] = prev_row
        pad_ref[lo:lo + t, :] = main_ref[:, cols].astype(F32)
        pad_ref[lo + t:lo + t + 1, :] = next_row
        uc = (pad_ref[lo - 1:lo - 1 + t, :] * cw_ref[0:1, cols] + pad_ref[lo:lo + t, :] * cw_ref[1:2, cols]
              + pad_ref[lo + 1:lo + 1 + t, :] * cw_ref[2:3, cols] + cb_ref[:, cols])
        if part == 2:
            uc = uc * gate
        outs[part][...] = uc.astype(BF16)


def _hy_pre(proj_a, conv_w, conv_b):
    b, l, n = proj_a.shape
    da = n // 4
    t, halo = TOKEN_TILE, BF16_TILE_ROWS
    nh = t // halo
    tok = pl.BlockSpec((None, t, da), lambda i, j: (i, j, 0))
    shp = jax.ShapeDtypeStruct((b, l, da), BF16)
    return pl.pallas_call(
        functools.partial(_hy_pre_body, da=da),
        grid=(b, l // t),
        in_specs=[
            pl.BlockSpec((None, t, n), lambda i, j: (i, j, 0)),
            pl.BlockSpec((None, halo, n), lambda i, j: (i, jnp.maximum(j * nh - 1, 0), 0)),
            pl.BlockSpec((None, halo, n), lambda i, j: (i, jnp.minimum((j + 1) * nh, l // halo - 1), 0)),
            pl.BlockSpec((3, 3 * da), lambda i, j: (0, 0)),
            pl.BlockSpec((1, 3 * da), lambda i, j: (0, 0)),
        ],
        out_specs=[tok, tok, tok],
        out_shape=[shp, shp, shp],
        scratch_shapes=[pltpu.VMEM((t + 2 * V7X_SUBLANES, da), F32)],
        compiler_params=_params(("parallel", "parallel")),
        name="hy_pre",
    )(proj_a, proj_a, proj_a, conv_w, conv_b.reshape(1, 3 * da))


def _filter_body(feat_ref, w1_ref, b1_ref, f0_ref, w2_ref, b2_ref, f1_ref, w3_ref, b3_ref, dec_ref,
                 hf_ref, sum_ref):
    i = pl.program_id(0)
    hp = lax.Precision.HIGHEST
    feat = feat_ref[...]
    t = feat[:, 0:1]
    h = jnp.sin(f0_ref[...] * (jnp.dot(feat, w1_ref[...], preferred_element_type=F32, precision=hp)
                               + b1_ref[...]))
    h = jnp.sin(f1_ref[...] * (jnp.dot(h, w2_ref[...], preferred_element_type=F32, precision=hp)
                               + b2_ref[...]))
    h = jnp.dot(h.astype(BF16), w3_ref[...], preferred_element_type=F32) + b3_ref[...]
    h = h * jnp.exp(-t * jnp.abs(dec_ref[...]))
    hf_ref[...] = h.astype(BF16)
    abs_sum = jnp.sum(jnp.abs(h), 0, keepdims=True)

    @pl.when(i > 0)
    def _():
        sum_ref[...] += abs_sum

    @pl.when(i == 0)
    def _():
        head = h[0:FILTER_HEAD_ROWS]
        rows = lax.broadcasted_iota(jnp.int32, head.shape, 0)
        cols = lax.broadcasted_iota(jnp.int32, head.shape, 1)
        drop = (rows == 0) & (cols >= h.shape[1] // 2)
        hf_ref[0:FILTER_HEAD_ROWS, :] = jnp.where(drop, 0.0, head).astype(BF16)
        sum_ref[...] = abs_sum - jnp.sum(jnp.where(drop, jnp.abs(head), 0.0), 0, keepdims=True)


def _filter_taps(l, w1, b1, freq, w2, b2, w3, b3, decay):
    fo = w1.shape[1]
    n = w3.shape[1]
    da = decay.shape[0]
    t = jnp.arange(l, dtype=F32) / l
    bands = jnp.arange(1, HYENA_BANDS + 1, dtype=F32)
    ang = 2.0 * math.pi * t[:, None] * bands[None, :]
    feat = jnp.concatenate([t[:, None], jnp.cos(ang), jnp.sin(ang)], -1)
    feat = jnp.pad(feat, ((0, 0), (0, FEAT_PAD - HYENA_EMB)))
    w1p = jnp.pad(w1, ((0, FEAT_PAD - HYENA_EMB), (0, 0)))
    dec = jnp.tile(decay, n // da).reshape(1, n)
    tt = FILTER_ROW_TILE
    const = lambda shape: pl.BlockSpec(shape, lambda i: (0,) * len(shape))
    return pl.pallas_call(
        _filter_body,
        grid=(l // tt,),
        in_specs=[
            pl.BlockSpec((tt, FEAT_PAD), lambda i: (i, 0)),
            const((FEAT_PAD, fo)), const((1, fo)), const((1, fo)),
            const((fo, fo)), const((1, fo)), const((1, fo)),
            const((fo, n)), const((1, n)), const((1, n)),
        ],
        out_specs=[pl.BlockSpec((tt, n), lambda i: (i, 0)), const((1, n))],
        out_shape=[jax.ShapeDtypeStruct((l, n), BF16), jax.ShapeDtypeStruct((1, n), F32)],
        compiler_params=_params(("arbitrary",)),
        name="filter_taps",
    )(feat, w1p, b1.reshape(1, fo), freq[0].reshape(1, fo), w2, b2.reshape(1, fo),
      freq[1].reshape(1, fo), w3.astype(BF16), b3.reshape(1, n), dec)


def _dft_tables(l):
    n = 2 * l
    n1 = n // DFT_N2
    kk = jnp.arange(n1 // 2, dtype=jnp.int32)
    nn = jnp.arange(n1 // 2, dtype=jnp.int32)
    th = (2.0 * math.pi / (2 * n1)) * (((2 * kk[:, None] + 1) * nn[None, :]) % (2 * n1)).astype(F32)
    eye2 = jnp.eye(2, dtype=F32)
    f1 = jnp.kron(jnp.concatenate([jnp.cos(th), -jnp.sin(th)], 0), eye2).astype(BF16)
    g = jnp.kron(jnp.concatenate([jnp.cos(th).T, -jnp.sin(th).T], 1) * (2.0 / n), eye2).astype(BF16)
    k2 = jnp.arange(DFT_N2, dtype=jnp.int32)
    n2 = jnp.arange(DFT_N2, dtype=jnp.int32)
    ph = (n2[None, None, :] * (k2[None, :, None] * (2 * n1) + 2 * kk[:, None, None] + 1)) % (2 * n)
    ang = (2.0 * math.pi / (2 * n)) * ph.astype(F32)
    c, s = jnp.cos(ang), jnp.sin(ang)
    mf = jnp.concatenate([jnp.concatenate([c, s], 2), jnp.concatenate([-s, c], 2)], 1).astype(BF16)
    mi = jnp.swapaxes(mf, 1, 2)
    return f1, g, mf, mi


def _load_pair_group(ref, lead, g):
    start = pl.multiple_of(g * PAIR_GROUP, PAIR_GROUP)
    words = ref.bitcast(jnp.uint32)[(*lead, slice(None), pl.ds(start, PAIR_GROUP), slice(None))]
    words = jnp.swapaxes(words, 0, 1)
    return [pltpu.bitcast(words[i], BF16) for i in range(PAIR_GROUP)]


def _store_pair_group(ref, lead, g, vals):
    start = pl.multiple_of(g * PAIR_GROUP, PAIR_GROUP)
    words = jnp.stack([pltpu.bitcast(v, jnp.uint32) for v in vals], 0)
    ref.bitcast(jnp.uint32)[(*lead, slice(None), pl.ds(start, PAIR_GROUP), slice(None))] = (
        jnp.swapaxes(words, 0, 1))


def _dft_in_body(f_ref, z_ref, a_ref):
    n1 = a_ref.shape[2]
    cb = a_ref.shape[-1]

    def group(g, carry):
        zcat = jnp.concatenate(_load_pair_group(z_ref, (0,), g), 1)
        r = jnp.dot(f_ref[...], zcat, preferred_element_type=F32).astype(BF16)
        cols = [r[:, i * cb:(i + 1) * cb] for i in range(PAIR_GROUP)]
        _store_pair_group(a_ref, (0, 0), g, [c[:2 * n1] for c in cols])
        _store_pair_group(a_ref, (0, 1), g, [c[2 * n1:] for c in cols])
        return carry

    lax.fori_loop(0, z_ref.shape[2] // (2 * PAIR_GROUP), group, 0, unroll=4)


def _dft_in(f1p, z):
    b, half, n2, c = z.shape
    n1 = half
    cb, rs = V7X_LANES, DFT_STEP_ROWS
    return pl.pallas_call(
        _dft_in_body,
        grid=(b, c // cb, n2 // rs),
        in_specs=[
            pl.BlockSpec(f1p.shape, lambda i, j, s: (0, 0)),
            pl.BlockSpec((1, half, rs, cb), lambda i, j, s: (i, 0, s, j)),
        ],
        out_specs=pl.BlockSpec((1, 2, n1, rs, cb), lambda i, j, s: (i, 0, 0, s, j)),
        out_shape=jax.ShapeDtypeStruct((b, 2, n1, n2, c), BF16),
        compiler_params=_params(("parallel", "parallel", "parallel")),
        name="dft_in",
    )(f1p, z)


def _dft_out_body(g_ref, b_ref, x_ref, z_ref, skip_ref, o_ref):
    cb = o_ref.shape[-1]

    def group(g, carry):
        re, im = _load_pair_group(b_ref, (0, 0), g), _load_pair_group(b_ref, (0, 1), g)
        bcat = jnp.concatenate([jnp.concatenate([r, i], 0) for r, i in zip(re, im)], 1)
        y = jnp.dot(g_ref[...], bcat, preferred_element_type=F32)
        xs, zs = _load_pair_group(x_ref, (0,), g), _load_pair_group(z_ref, (0,), g)
        outs = [(xs[i].astype(F32) * (y[:, i * cb:(i + 1) * cb] + skip_ref[...] * zs[i].astype(F32))
                 ).astype(BF16) for i in range(PAIR_GROUP)]
        _store_pair_group(o_ref, (0,), g, outs)
        return carry

    lax.fori_loop(0, x_ref.shape[2] // (2 * PAIR_GROUP), group, 0)


def _dft_out(gp, bb, x, z, skip):
    b, half, n2, c = z.shape
    n1 = half
    cb, rs = V7X_LANES, DFT_STEP_ROWS
    slab = pl.BlockSpec((1, half, rs, cb), lambda i, j, s: (i, 0, s, j))
    return pl.pallas_call(
        _dft_out_body,
        grid=(b, c // cb, n2 // rs),
        in_specs=[
            pl.BlockSpec(gp.shape, lambda i, j, s: (0, 0)),
            pl.BlockSpec((1, 2, n1, rs, cb), lambda i, j, s: (i, 0, 0, s, j)),
            slab, slab,
            pl.BlockSpec((1, cb), lambda i, j, s: (0, j)),
        ],
        out_specs=slab,
        out_shape=jax.ShapeDtypeStruct(z.shape, BF16),
        compiler_params=_params(("parallel", "parallel", "parallel")),
        name="dft_out",
    )(gp, bb, x, z, skip.astype(F32).reshape(1, c))


def _slab_conv_body(mf_ref, mi_ref, k_ref, a_ref, o_ref):
    nb, _, ks, n2, c = a_ref.shape
    for kk in range(ks):
        kr, ki = k_ref[kk, 0].astype(F32), k_ref[kk, 1].astype(F32)
        for b in range(nb):
            a = a_ref[b, :, kk].reshape(2 * n2, c)
            x = jnp.dot(mf_ref[kk], a, preferred_element_type=F32)
            xr, xi = x[:n2], x[n2:]
            y = jnp.concatenate([xr * kr - xi * ki, xr * ki + xi * kr], 0).astype(BF16)
            out = jnp.dot(mi_ref[kk], y, preferred_element_type=F32)
            o_ref[b, :, kk] = out.astype(BF16).reshape(2, n2, c)


def _slab_conv(mf, mi, kspec, a, order):
    b, _, n1, n2, c = a.shape
    m = 2 * n2
    ks = max(1, SLABS_PER_STEP // b)
    return pl.pallas_call(
        _slab_conv_body,
        grid=(n1 // ks,),
        in_specs=[
            pl.BlockSpec((ks, m, m), lambda k: (k, 0, 0)),
            pl.BlockSpec((ks, m, m), lambda k: (k, 0, 0)),
            pl.BlockSpec((ks, 2, n2, c), lambda k: (k, 0, 0, order)),
            pl.BlockSpec((b, 2, ks, n2, c), lambda k: (0, 0, k, 0, 0)),
        ],
        out_specs=pl.BlockSpec((b, 2, ks, n2, c), lambda k: (0, 0, k, 0, 0)),
        out_shape=jax.ShapeDtypeStruct(a.shape, BF16),
        compiler_params=_params(("parallel",)),
        name="slab_conv",
    )(mf, mi, kspec, a)


def _slab_spec_body(mf_ref, sum_ref, a_ref, k_ref):
    _, ks, n2, c = a_ref.shape
    half = c // 2
    inv = 1.0 / (sum_ref[:, :half] + sum_ref[:, half:] + 1e-6)
    for kk in range(ks):
        x = jnp.dot(mf_ref[kk], a_ref[:, kk].reshape(2 * n2, c), preferred_element_type=F32)
        k_ref[kk, 0] = ((x[:n2, :half] + x[:n2, half:]) * inv).astype(BF16)
        k_ref[kk, 1] = ((x[n2:, :half] - x[n2:, half:]) * inv).astype(BF16)


def _slab_spec(mf, sums, a):
    _, _, n1, n2, c = a.shape
    m = 2 * n2
    ks = SPEC_SLABS_PER_STEP
    return pl.pallas_call(
        _slab_spec_body,
        grid=(n1 // ks,),
        in_specs=[
            pl.BlockSpec((ks, m, m), lambda k: (k, 0, 0)),
            pl.BlockSpec((1, c), lambda k: (0, 0)),
            pl.BlockSpec((None, 2, ks, n2, c), lambda k: (0, 0, k, 0, 0)),
        ],
        out_specs=pl.BlockSpec((ks, 2, n2, c // 2), lambda k: (k, 0, 0, 0)),
        out_shape=jax.ShapeDtypeStruct((n1, 2, n2, c // 2), BF16),
        compiler_params=_params(("parallel",)),
        name="slab_spec",
    )(mf, sums, a)


def _hyena_spectrum(l, tables, w1, b1, freq, w2, b2, w3, b3, decay):
    f1p, _, mf, _ = tables
    taps, sums = _filter_taps(l, w1, b1, freq, w2, b2, w3, b3, decay)
    a = _dft_in(f1p, taps.reshape(1, l // DFT_N2, DFT_N2, taps.shape[1]))
    return _slab_spec(mf, sums, a)


def _hyena(v, x1, x2g, kspec, skip, tables):
    f1p, gp, mf, mi = tables
    b, l, c = v.shape
    slabs = lambda u: u.reshape(b, l // DFT_N2, DFT_N2, c)
    z = slabs(v)
    for order, xg in enumerate((slabs(x1), slabs(x2g))):
        bb = _slab_conv(mf, mi, kspec, _dft_in(f1p, z), order)
        z = _dft_out(gp, bb, xg, z, skip[order])
    return z.reshape(b, l, c)


def _head_masks(rows, heads):
    lane = lax.broadcasted_iota(jnp.int32, (rows, heads * HEAD_DIM), 1)
    return [(lane >= h * HEAD_DIM) & (lane < (h + 1) * HEAD_DIM) for h in range(heads)]


def _stack_heads(q, masks):
    zero = jnp.zeros_like(q)
    return jnp.concatenate([jnp.where(m, q, zero) for m in masks], 0)


def _unstack_heads(res, masks, rows):
    out = jnp.where(masks[0], res[:rows], 0.0)
    for h in range(1, len(masks)):
        out = out + jnp.where(masks[h], res[h * rows:(h + 1) * rows], 0.0)
    return out


def _na_bias_table(rpb):
    heads = rpb.shape[0]
    c = jnp.arange(GRID_W)
    col_start = jnp.clip(c - NA_KC // 2, 0, GRID_W - NA_KC)
    col_ok = (c[None, :] >= col_start[:, None]) & (c[None, :] < col_start[:, None] + NA_KC)
    dc = jnp.clip(c[None, :] - c[:, None], -(NA_KC - 1), NA_KC - 1) + NA_KC - 1
    onehot = (dc[:, :, None] == jnp.arange(2 * NA_KC - 1)[None, None, :]).astype(F32)
    tcol = jnp.einsum("hrj,qkj->hrqk", rpb.astype(F32), onehot, precision=lax.Precision.HIGHEST)
    per_idx = [tcol[:, NA_KR - 1 - idx:2 * NA_KR - 1 - idx] for idx in range(NA_KR)]
    bias = jnp.transpose(jnp.stack(per_idx, 0), (0, 1, 3, 2, 4))
    bias = jnp.where(col_ok[None, None, :, None, :], bias * LOG2_E, NEG_INF)
    return bias.reshape(NA_KR, heads * GRID_W, NA_KR * GRID_W)


def _na_body(q_ref, k_ref, v_ref, z_ref, bias_ref, o_ref, *, rows, rb, heads):
    blk = pl.program_id(1)
    masks = _head_masks(GRID_W, heads)
    span = NA_KR * GRID_W

    def one_row(rr, carry):
        r = blk * rb + rr
        start = jnp.clip(r - NA_KR // 2, 0, rows - NA_KR)
        idx = r - start
        koff = pl.multiple_of(start * GRID_W, GRID_W)
        qoff = pl.multiple_of(rr * GRID_W, GRID_W)
        q = q_ref[pl.ds(qoff, GRID_W), :]
        kk = k_ref[pl.ds(koff, span), :]
        vv = v_ref[pl.ds(koff, span), :]
        s = lax.dot_general(_stack_heads(q, masks), kk, (((1,), (1,)), ((), ())),
                            preferred_element_type=F32)
        s = s + bias_ref[idx]
        m = jnp.max(s, -1, keepdims=True)
        e = jnp.exp2(s - m)
        p = (e / jnp.sum(e, -1, keepdims=True)).astype(BF16)
        o = _unstack_heads(jnp.dot(p, vv, preferred_element_type=F32), masks, GRID_W)
        z = z_ref[pl.ds(qoff, GRID_W), :].astype(F32)
        o_ref[pl.ds(qoff, GRID_W), :] = (o * _silu(z)).astype(BF16)
        return carry

    lax.fori_loop(0, rb, one_row, 0, unroll=NA_ROW_UNROLL)


def _na(proj_b, bias):
    b, l, n = proj_b.shape
    db = n // 4
    heads = db // HEAD_DIM
    rows = l // GRID_W
    rb = NA_ROWS_PER_STEP
    t = rb * GRID_W
    return pl.pallas_call(
        functools.partial(_na_body, rows=rows, rb=rb, heads=heads),
        grid=(b, rows // rb),
        in_specs=[
            pl.BlockSpec((None, t, db), lambda i, j: (i, j, 0)),
            pl.BlockSpec((None, l, db), lambda i, j: (i, 0, 1)),
            pl.BlockSpec((None, l, db), lambda i, j: (i, 0, 2)),
            pl.BlockSpec((None, t, db), lambda i, j: (i, j, 3)),
            pl.BlockSpec(bias.shape, lambda i, j: (0, 0, 0)),
        ],
        out_specs=pl.BlockSpec((None, t, db), lambda i, j: (i, j, 0)),
        out_shape=jax.ShapeDtypeStruct((b, l, db), BF16),
        compiler_params=_params(("parallel", "parallel")),
        name="na",
    )(proj_b, proj_b, proj_b, proj_b, bias)


def _attend(q, kk, vv, valid, masks):
    rows = q.shape[0]
    s = lax.dot_general(_stack_heads(q, masks), kk, (((1,), (1,)), ((), ())), preferred_element_type=F32)
    s = jnp.where(valid, s, NEG_INF)
    m = jnp.max(s, -1, keepdims=True)
    e = jnp.exp2(s - m)
    l = jnp.sum(e, -1, keepdims=True)
    p = (e / l).astype(BF16)
    o = _unstack_heads(jnp.dot(p, vv, preferred_element_type=F32), masks, rows)
    lse = _unstack_heads(jnp.broadcast_to(m + jnp.log2(l), (s.shape[0], q.shape[1])), masks, rows)
    return o, lse


def _split_pattern(dilation, n, nchunks, q_ref, kbuf, vbuf, o_split, l_split, masks, heads):
    blk = DIL_BLK
    dc = heads * HEAD_DIM
    phases = DIL_SPLIT // dilation
    mq = blk // phases
    row = lax.broadcasted_iota(jnp.int32, (heads * blk, 3 * blk), 0) % blk
    col = lax.broadcasted_iota(jnp.int32, (heads * blk, 3 * blk), 1)
    qa, qm = row // mq, row % mq
    ka = sum((col >= a * 3 * mq).astype(jnp.int32) for a in range(1, phases)) if phases > 1 else 0
    km = col - ka * (3 * mq)
    band = jnp.abs(phases * (km - mq - qm) + (ka - qa)) <= blk
    for m0 in range(0, blk, mq):
        gm = km + (n * blk + m0 - mq)
        valid = band & (gm >= 0) & (gm < nchunks * blk)
        for j in range(dilation):
            lanes = [slice((j + dilation * a) * dc, (j + dilation * a + 1) * dc) for a in range(phases)]
            q = jnp.concatenate([q_ref[m0:m0 + mq, ln] for ln in lanes], 0)
            krows = slice(blk + m0 - mq, blk + m0 + 2 * mq)
            kk = jnp.concatenate([kbuf[krows, ln] for ln in lanes], 0)
            vv = jnp.concatenate([vbuf[krows, ln] for ln in lanes], 0)
            o, lse = _attend(q, kk, vv, valid, masks)
            for a in range(phases):
                o_split[j + dilation * a, m0:m0 + mq, :] = o[a * mq:(a + 1) * mq]
                l_split[j + dilation * a, m0:m0 + mq, :] = lse[a * mq:(a + 1) * mq]


def _dil_body(qn_ref, knp_ref, kn_ref, knn_ref, vnp_ref, vn_ref, vnn_ref,
              qs_ref, ksp_ref, ks_ref, ksn_ref, vsp_ref, vs_ref, vsn_ref, cz_ref, y_ref,
              kbn, vbn, kbs, vbs, o_split, l_split, o_all, l_all, *, heads, nchunks):
    n = pl.program_id(1)
    blk = DIL_BLK
    chunk = qn_ref.shape[0]
    per = chunk // blk
    masks = _head_masks(blk, heads)
    for buf, prev, cur, nxt, halo in ((kbn, knp_ref, kn_ref, knn_ref, blk), (vbn, vnp_ref, vn_ref, vnn_ref, blk),
                                      (kbs, ksp_ref, ks_ref, ksn_ref, blk), (vbs, vsp_ref, vs_ref, vsn_ref, blk)):
        main = cur.shape[0]
        buf[0:halo] = prev[...]
        buf[halo:halo + main] = cur[...]
        buf[halo + main:] = nxt[...]

    qi = lax.broadcasted_iota(jnp.int32, (heads * blk, 3 * blk), 0) % blk
    ki = lax.broadcasted_iota(jnp.int32, (heads * blk, 3 * blk), 1)
    band = jnp.abs(ki - blk - qi) <= blk

    def token_block(i, carry):
        g = n * per + i
        off = pl.multiple_of(i * blk, blk)
        valid = band & ((ki >= blk) | (g > 0)) & ((ki < 2 * blk) | (g < nchunks * per - 1))
        o, lse = _attend(qn_ref[pl.ds(off, blk), :], kbn[pl.ds(off, 3 * blk), :], vbn[pl.ds(off, 3 * blk), :],
                         valid, masks)
        o_all[0, pl.ds(off, blk), :] = o
        l_all[0, pl.ds(off, blk), :] = lse
        return carry

    lax.fori_loop(0, per, token_block, 0, unroll=True)

    for g, (_, dilation) in enumerate(DIL_PATTERNS):
        if dilation == 1:
            continue
        _split_pattern(dilation, n, nchunks, qs_ref, kbs, vbs, o_split, l_split, masks, heads)
        o_all[g] = jnp.swapaxes(o_split[...], 0, 1).reshape(chunk, heads * HEAD_DIM)
        l_all[g] = jnp.swapaxes(l_split[...], 0, 1).reshape(chunk, heads * HEAD_DIM)

    ls = [l_all[g] for g in range(len(DIL_PATTERNS))]
    m = functools.reduce(jnp.maximum, ls)
    es = [jnp.exp2(l - m) for l in ls]
    den = functools.reduce(lambda a, b: a + b, es)
    o = functools.reduce(lambda a, b: a + b, [(e / den) * o_all[g] for g, e in enumerate(es)])
    y_ref[...] = (o * cz_ref[...].astype(F32)).astype(BF16)


def _dilated_mixture(qn, kn, vn, qs, ks, vs, cz):
    b, l, dc = qn.shape
    heads = dc // HEAD_DIM
    blk = DIL_BLK
    assert DIL_PATTERNS[0][1] == 1 and all(w == 2 * blk * d and DIL_SPLIT % d == 0 for w, d in DIL_PATTERNS)
    chunk = DIL_SPLIT * blk
    nchunks = l // chunk
    per = chunk // blk
    nat = pl.BlockSpec((None, chunk, dc), lambda i, n: (i, n, 0))
    nat_prev = pl.BlockSpec((None, blk, dc), lambda i, n: (i, jnp.maximum(n * per - 1, 0), 0))
    nat_next = pl.BlockSpec((None, blk, dc), lambda i, n: (i, jnp.minimum((n + 1) * per, nchunks * per - 1), 0))
    spl = pl.BlockSpec((None, blk, DIL_SPLIT * dc), lambda i, n: (i, n, 0))
    spl_prev = pl.BlockSpec((None, blk, DIL_SPLIT * dc), lambda i, n: (i, jnp.maximum(n - 1, 0), 0))
    spl_next = pl.BlockSpec((None, blk, DIL_SPLIT * dc), lambda i, n: (i, jnp.minimum(n + 1, nchunks - 1), 0))
    return pl.pallas_call(
        functools.partial(_dil_body, heads=heads, nchunks=nchunks),
        grid=(b, nchunks),
        in_specs=[nat, nat_prev, nat, nat_next, nat_prev, nat, nat_next,
                  spl, spl_prev, spl, spl_next, spl_prev, spl, spl_next, nat],
        out_specs=nat,
        out_shape=jax.ShapeDtypeStruct((b, l, dc), BF16),
        scratch_shapes=[pltpu.VMEM((chunk + 2 * blk, dc), BF16), pltpu.VMEM((chunk + 2 * blk, dc), BF16),
                        pltpu.VMEM((3 * blk, DIL_SPLIT * dc), BF16), pltpu.VMEM((3 * blk, DIL_SPLIT * dc), BF16),
                        pltpu.VMEM((DIL_SPLIT, blk, dc), F32), pltpu.VMEM((DIL_SPLIT, blk, dc), F32),
                        pltpu.VMEM((len(DIL_PATTERNS), chunk, dc), F32),
                        pltpu.VMEM((len(DIL_PATTERNS), chunk, dc), F32)],
        compiler_params=_params(("parallel", "parallel")),
        name="dilated",
    )(qn, kn, kn, kn, vn, vn, vn, qs, ks, ks, ks, vs, vs, vs, cz)


def _tail_body(x_ref, ya_ref, yb_ref, yc_ref, g_ref, gate_ref, wa_ref, wb_ref, wc_ref, wo_ref, lng_ref, lnb_ref,
               out_ref, *, alpha):
    d = x_ref.shape[-1]
    pa = jnp.dot(ya_ref[...], wa_ref[...], preferred_element_type=F32)
    pb = jnp.dot(yb_ref[...], wb_ref[...], preferred_element_type=F32)
    pc = jnp.dot(yc_ref[...], wc_ref[...], preferred_element_type=F32)
    g = g_ref[...].astype(F32)
    merged = g[:, :d] * pa + g[:, d:2 * d] * pb + g[:, 2 * d:] * pc
    sub = jnp.dot(merged.astype(BF16), wo_ref[...], preferred_element_type=F32) * gate_ref[...]
    res = alpha * x_ref[...] + sub
    out_ref[...] = _layernorm(res) * lng_ref[...] + lnb_ref[...]


def _tail(x, ya, yb, yc, g_all, gate, wa, wb, wc, wo, ln_g, ln_b, alpha):
    b, l, d = x.shape
    t = TOKEN_TILE
    tok = lambda w: pl.BlockSpec((None, t, w), lambda i, j: (i, j, 0))
    const = lambda a: pl.BlockSpec(a.shape, lambda i, j: (0,) * a.ndim)
    ln_g, ln_b = ln_g.reshape(1, d), ln_b.reshape(1, d)
    return pl.pallas_call(
        functools.partial(_tail_body, alpha=alpha),
        grid=(b, l // t),
        in_specs=[tok(d), tok(ya.shape[-1]), tok(yb.shape[-1]), tok(yc.shape[-1]), tok(3 * d),
                  pl.BlockSpec((None, 1, d), lambda i, j: (i, 0, 0)),
                  const(wa), const(wb), const(wc), const(wo), const(ln_g), const(ln_b)],
        out_specs=tok(d),
        out_shape=jax.ShapeDtypeStruct((b, l, d), F32),
        compiler_params=_params(("parallel", "parallel")),
        name="tail",
    )(x, ya, yb, yc, g_all, gate.reshape(b, 1, d), wa, wb, wc, wo, ln_g, ln_b)


def _rope_tables(l, heads):
    half = HEAD_DIM // 2
    inv = ROPE_THETA ** (-jnp.arange(half, dtype=F32) / half)
    ang = jnp.arange(l, dtype=F32)[:, None] * inv[None, :]
    cos, sin = jnp.cos(ang), jnp.sin(ang)
    return (jnp.tile(jnp.concatenate([cos, cos], -1), (1, heads)),
            jnp.tile(jnp.concatenate([-sin, sin], -1), (1, heads)))


def _layer(x, ada, lw, consts, alpha):
    d = x.shape[-1]
    shift, scale, gate = ada[:, :d], ada[:, d:2 * d], ada[:, 2 * d:]
    h = _ln_mod(x, scale, shift)
    w_in, b_in = lw["w_in"], lw["b_in"]
    proj_a = _proj(h, w_in[:, :2 * d], b_in[:2 * d])
    q_cols = jnp.arange(d) < d // 4
    proj_b = _proj(h, w_in[:, 2 * d:3 * d], b_in[2 * d:3 * d], col_scale=jnp.where(q_cols, QK_SCALE_LOG2, 1.0))
    cq, ck, cv, cz, cqs, cks, cvs = _proj_rope(h, w_in[:, 3 * d:4 * d], b_in[3 * d:4 * d], *consts["rope"])
    gates = _proj(h, w_in[:, 4 * d:], b_in[4 * d:], sigmoid=True)

    v, x1, x2g = _hy_pre(proj_a, lw["conv_w"], lw["conv_b"])
    ya = _hyena(v, x1, x2g, consts["kspec"], lw["skip"], consts["dft"])
    yb = _na(proj_b, lw["na_bias"])
    yc = _dilated_mixture(cq, ck, cv, cqs, cks, cvs, cz)
    return _tail(x, ya, yb, yc, gates, gate, lw["wa"], lw["wb"], lw["wc"], lw["wo"],
                 lw["ln_g"], lw["ln_b"], alpha)


def kernel(x_prompt, x_sample, c_prompt, c_sample, w_ada, b_ada, w_in, b_in, hy_conv_w, hy_conv_b, hy_w1, hy_b1, hy_freq, hy_w2, hy_b2, hy_w3, hy_b3, hy_decay, hy_skip, na_rpb, w_branch_a, w_branch_b, w_branch_c, w_out, ln_g, ln_b):
    depth, d, _ = w_in.shape
    heads_c = (d // 4) // HEAD_DIM
    alpha = (2 * depth) ** 0.25
    groups = [(x_prompt, c_prompt), (x_sample, c_sample)]

    nb_p = c_prompt.shape[0]
    c_all = jnp.concatenate([c_prompt, c_sample], 0)
    pad_rows = -c_all.shape[0] % 8
    ada_all = _ada(jnp.pad(c_all, ((0, pad_rows), (0, 0))), w_ada, b_ada)
    adas = [ada_all[:, :nb_p], ada_all[:, nb_p:nb_p + c_sample.shape[0]]]

    shared = {}
    for x, _ in groups:
        l = x.shape[1]
        if l not in shared:
            shared[l] = {"dft": _dft_tables(l), "rope": _rope_tables(l, heads_c)}

    ys = [x for x, _ in groups]
    for layer in range(depth):
        lw = {
            "w_in": w_in[layer].astype(BF16), "b_in": b_in[layer],
            "conv_w": hy_conv_w[layer], "conv_b": hy_conv_b[layer], "skip": hy_skip[layer],
            "na_bias": _na_bias_table(na_rpb[layer]),
            "wa": w_branch_a[layer].astype(BF16), "wb": w_branch_b[layer].astype(BF16),
            "wc": w_branch_c[layer].astype(BF16), "wo": w_out[layer].astype(BF16),
            "ln_g": ln_g[layer], "ln_b": ln_b[layer],
        }
        kspecs = {}
        for gi in range(len(groups)):
            l = ys[gi].shape[1]
            if l not in kspecs:
                kspecs[l] = _hyena_spectrum(l, shared[l]["dft"], hy_w1[layer], hy_b1[layer], hy_freq[layer],
                                            hy_w2[layer], hy_b2[layer], hy_w3[layer], hy_b3[layer],
                                            hy_decay[layer])
            consts = dict(shared[l], kspec=kspecs[l])
            ys[gi] = _layer(ys[gi], adas[gi][layer], lw, consts, alpha)
    return tuple(ys)
```

```python
import functools
import math

import jax
import jax.numpy as jnp
from jax import lax
from jax.experimental import pallas as pl
from jax.experimental.pallas import tpu as pltpu

F32 = jnp.float32
BF16 = jnp.bfloat16

GRID_W = 64
HEAD_DIM = 64
HYENA_BANDS = 16
HYENA_EMB = 2 * HYENA_BANDS + 1
NA_KR = 8
NA_KC = 16
DIL_PATTERNS = ((128, 1), (512, 4), (2048, 16))
DIL_BLK = 64
DIL_SPLIT = 16
NA_ROW_UNROLL = 32
ROPE_THETA = 10000.0
LN_EPS = 1e-5
NEG_INF = -1e30
LOG2_E = math.log2(math.e)
QK_SCALE_LOG2 = HEAD_DIM ** -0.5 * LOG2_E

V7X_LANES = 128
V7X_SUBLANES = 8
V7X_VMEM_LIMIT_BYTES = 56 * 1024 * 1024

DFT_N2 = V7X_LANES
PAIR_GROUP = V7X_SUBLANES
DFT_STEP_ROWS = 128
SLABS_PER_STEP = 8
FEAT_PAD = V7X_LANES
FILTER_HEAD_ROWS = 2 * V7X_SUBLANES
BF16_TILE_ROWS = 2 * V7X_SUBLANES

TOKEN_TILE = 1024
PROJ_COL_TILE = 1024
ROPE_TOKEN_TILE = 1024
FILTER_ROW_TILE = 1024
NA_ROWS_PER_STEP = 32
SPEC_SLABS_PER_STEP = 2
HY_SHIFT_ROWS = 256


def _params(sem):
    return pltpu.CompilerParams(dimension_semantics=sem, vmem_limit_bytes=V7X_VMEM_LIMIT_BYTES)


def _sigmoid(x):
    return 1.0 / (1.0 + jnp.exp(-x))


def _silu(x):
    return x * _sigmoid(x)


def _ada_body(c_ref, w_ref, b_ref, o_ref):
    s = _silu(c_ref[...])
    o_ref[...] = jnp.dot(s, w_ref[...], preferred_element_type=F32,
                         precision=lax.Precision.HIGHEST) + b_ref[...]


def _ada(c_all, w_ada, b_ada):
    depth, d, n = w_ada.shape
    rows = c_all.shape[0]
    tn = PROJ_COL_TILE
    return pl.pallas_call(
        _ada_body,
        grid=(depth, n // tn),
        in_specs=[
            pl.BlockSpec((rows, d), lambda l, j: (0, 0)),
            pl.BlockSpec((None, d, tn), lambda l, j: (l, 0, j)),
            pl.BlockSpec((None, 1, tn), lambda l, j: (l, 0, j)),
        ],
        out_specs=pl.BlockSpec((None, rows, tn), lambda l, j: (l, 0, j)),
        out_shape=jax.ShapeDtypeStruct((depth, rows, n), F32),
        compiler_params=_params(("parallel", "parallel")),
        name="ada",
    )(c_all, w_ada, b_ada.reshape(depth, 1, n))


def _layernorm(x):
    mu = jnp.mean(x, -1, keepdims=True)
    xc = x - mu
    var = jnp.mean(xc * xc, -1, keepdims=True)
    return xc * lax.rsqrt(var + LN_EPS)


def _ln_mod_body(x_ref, sc_ref, sh_ref, o_ref):
    h = _layernorm(x_ref[...]) * (1.0 + sc_ref[...]) + sh_ref[...]
    o_ref[...] = h.astype(BF16)


def _ln_mod(x, scale, shift):
    b, l, d = x.shape
    t = TOKEN_TILE
    return pl.pallas_call(
        _ln_mod_body,
        grid=(b, l // t),
        in_specs=[
            pl.BlockSpec((None, t, d), lambda i, j: (i, j, 0)),
            pl.BlockSpec((None, 1, d), lambda i, j: (i, 0, 0)),
            pl.BlockSpec((None, 1, d), lambda i, j: (i, 0, 0)),
        ],
        out_specs=pl.BlockSpec((None, t, d), lambda i, j: (i, j, 0)),
        out_shape=jax.ShapeDtypeStruct((b, l, d), BF16),
        compiler_params=_params(("parallel", "parallel")),
        name="ln_mod",
    )(x, scale.reshape(b, 1, d), shift.reshape(b, 1, d))


def _proj_body(h_ref, w_ref, b_ref, *rest, sigmoid, scaled):
    o_ref = rest[-1]
    acc = jnp.dot(h_ref[...], w_ref[...], preferred_element_type=F32) + b_ref[...]
    if scaled:
        acc = acc * rest[0][...]
    if sigmoid:
        acc = _sigmoid(acc)
    o_ref[...] = acc.astype(o_ref.dtype)


def _proj(h, w, bias, col_scale=None, sigmoid=False):
    b, l, d = h.shape
    n = w.shape[1]
    t, tn = TOKEN_TILE, PROJ_COL_TILE
    row = pl.BlockSpec((1, tn), lambda j, i, k: (0, j))
    scale_args = [] if col_scale is None else [col_scale.reshape(1, n)]
    return pl.pallas_call(
        functools.partial(_proj_body, sigmoid=sigmoid, scaled=col_scale is not None),
        grid=(n // tn, b, l // t),
        in_specs=[
            pl.BlockSpec((None, t, d), lambda j, i, k: (i, k, 0)),
            pl.BlockSpec((d, tn), lambda j, i, k: (0, j)),
            row,
        ] + [row] * len(scale_args),
        out_specs=pl.BlockSpec((None, t, tn), lambda j, i, k: (i, k, j)),
        out_shape=jax.ShapeDtypeStruct((b, l, n), BF16),
        compiler_params=_params(("parallel", "parallel", "parallel")),
        name="proj",
    )(h, w, bias.reshape(1, n), *scale_args)


def _rope_lanes(x, cos, sin_signed):
    outs = []
    lane = lax.broadcasted_iota(jnp.int32, (x.shape[0], V7X_LANES), 1)
    first_half = (lane % HEAD_DIM) < (HEAD_DIM // 2)
    for c0 in range(0, x.shape[1], V7X_LANES):
        xc = x[:, c0:c0 + V7X_LANES]
        partner = jnp.where(first_half,
                            pltpu.roll(xc, V7X_LANES - HEAD_DIM // 2, 1),
                            pltpu.roll(xc, HEAD_DIM // 2, 1))
        outs.append(xc * cos[:, c0:c0 + V7X_LANES] + partner * sin_signed[:, c0:c0 + V7X_LANES])
    return jnp.concatenate(outs, 1)


def _store_split(ref, x):
    t, dc = x.shape
    parts = jnp.swapaxes(x.reshape(t // DIL_SPLIT, DIL_SPLIT, dc), 0, 1)
    for r in range(DIL_SPLIT):
        ref[:, r * dc:(r + 1) * dc] = parts[r].astype(BF16)


def _proj_rope_body(h_ref, w_ref, b_ref, cos_ref, sin_ref, q_ref, k_ref, v_ref, z_ref,
                    qs_ref, ks_ref, vs_ref):
    dc = q_ref.shape[-1]
    acc = jnp.dot(h_ref[...], w_ref[...], preferred_element_type=F32) + b_ref[...]
    cos, sin = cos_ref[...], sin_ref[...]
    q = _rope_lanes(acc[:, :dc], cos, sin) * QK_SCALE_LOG2
    k = _rope_lanes(acc[:, dc:2 * dc], cos, sin)
    v = acc[:, 2 * dc:3 * dc]
    for ref, split_ref, val in ((q_ref, qs_ref, q), (k_ref, ks_ref, k), (v_ref, vs_ref, v)):
        ref[...] = val.astype(BF16)
        _store_split(split_ref, val)
    z_ref[...] = _silu(acc[:, 3 * dc:]).astype(BF16)


def _proj_rope(h, w, bias, cos_t, sin_t):
    b, l, d = h.shape
    n = w.shape[1]
    dc = n // 4
    t = ROPE_TOKEN_TILE
    tok = pl.BlockSpec((None, t, dc), lambda i, k: (i, k, 0))
    shp = jax.ShapeDtypeStruct((b, l, dc), BF16)
    spl = pl.BlockSpec((None, t // DIL_SPLIT, DIL_SPLIT * dc), lambda i, k: (i, k, 0))
    spl_shp = jax.ShapeDtypeStruct((b, l // DIL_SPLIT, DIL_SPLIT * dc), BF16)
    return pl.pallas_call(
        _proj_rope_body,
        grid=(b, l // t),
        in_specs=[
            pl.BlockSpec((None, t, d), lambda i, k: (i, k, 0)),
            pl.BlockSpec((d, n), lambda i, k: (0, 0)),
            pl.BlockSpec((1, n), lambda i, k: (0, 0)),
            pl.BlockSpec((t, dc), lambda i, k: (k, 0)),
            pl.BlockSpec((t, dc), lambda i, k: (k, 0)),
        ],
        out_specs=[tok, tok, tok, tok, spl, spl, spl],
        out_shape=[shp, shp, shp, shp, spl_shp, spl_shp, spl_shp],
        compiler_params=_params(("parallel", "parallel")),
        name="proj_rope",
    )(h, w, bias.reshape(1, n), cos_t, sin_t)


def _hy_pre_body(main_ref, prev_ref, next_ref, shift_ref, cw_ref, cb_ref, v_ref, x1_ref, x2_ref, *, da):
    i = pl.program_id(1)
    last = pl.num_programs(1) - 1
    t = main_ref.shape[0]
    halo = prev_ref.shape[0]
    blk = shift_ref.shape[1]
    nc = 3 * da
    outs = (v_ref, x1_ref, x2_ref)
    edge = lax.broadcasted_iota(jnp.int32, (V7X_SUBLANES, nc), 0)
    for r0 in range(0, t, blk):
        xb = main_ref[r0:r0 + blk, :nc]
        sh = jnp.dot(shift_ref[...], xb, preferred_element_type=F32)
        if r0 == 0:
            prev_row = jnp.where(i > 0, prev_ref[:, :nc].astype(F32)[halo - 1:halo], 0.0)
        else:
            prev_row = main_ref[r0 - halo:r0, :nc].astype(F32)[halo - 1:halo]
        if r0 + blk == t:
            next_row = jnp.where(i < last, next_ref[:, :nc].astype(F32)[0:1], 0.0)
        else:
            next_row = main_ref[r0 + blk:r0 + blk + halo, :nc].astype(F32)[0:1]
        up, dn = sh[:blk], sh[blk:]
        up = jnp.concatenate([up[:V7X_SUBLANES] + jnp.where(edge == 0, prev_row, 0.0), up[V7X_SUBLANES:]], 0)
        dn = jnp.concatenate([dn[:blk - V7X_SUBLANES],
                              dn[blk - V7X_SUBLANES:] + jnp.where(edge == V7X_SUBLANES - 1, next_row, 0.0)], 0)
        uc = up * cw_ref[0:1, :] + xb.astype(F32) * cw_ref[1:2, :] + dn * cw_ref[2:3, :] + cb_ref[...]
        for part in range(3):
            val = uc[:, part * da:(part + 1) * da]
            if part == 2:
                val = val * _silu(main_ref[r0:r0 + blk, nc:].astype(F32))
            outs[part][r0:r0 + blk, :] = val.astype(BF16)


def _hy_pre(proj_a, conv_w, conv_b):
    b, l, n = proj_a.shape
    da = n // 4
    t, halo = TOKEN_TILE, BF16_TILE_ROWS
    nh = t // halo
    blk = HY_SHIFT_ROWS
    shift = jnp.concatenate([jnp.eye(blk, k=-1, dtype=BF16), jnp.eye(blk, k=1, dtype=BF16)], 0)
    tok = pl.BlockSpec((None, t, da), lambda i, j: (i, j, 0))
    shp = jax.ShapeDtypeStruct((b, l, da), BF16)
    return pl.pallas_call(
        functools.partial(_hy_pre_body, da=da),
        grid=(b, l // t),
        in_specs=[
            pl.BlockSpec((None, t, n), lambda i, j: (i, j, 0)),
            pl.BlockSpec((None, halo, n), lambda i, j: (i, jnp.maximum(j * nh - 1, 0), 0)),
            pl.BlockSpec((None, halo, n), lambda i, j: (i, jnp.minimum((j + 1) * nh, l // halo - 1), 0)),
            pl.BlockSpec((2 * blk, blk), lambda i, j: (0, 0)),
            pl.BlockSpec((3, 3 * da), lambda i, j: (0, 0)),
            pl.BlockSpec((1, 3 * da), lambda i, j: (0, 0)),
        ],
        out_specs=[tok, tok, tok],
        out_shape=[shp, shp, shp],
        compiler_params=_params(("parallel", "parallel")),
        name="hy_pre",
    )(proj_a, proj_a, proj_a, shift, conv_w, conv_b.reshape(1, 3 * da))


def _filter_body(feat_ref, w1_ref, b1_ref, f0_ref, w2_ref, b2_ref, f1_ref, w3_ref, b3_ref, dec_ref,
                 hf_ref, sum_ref):
    i = pl.program_id(0)
    hp = lax.Precision.HIGHEST
    feat = feat_ref[...]
    t = feat[:, 0:1]
    h = jnp.sin(f0_ref[...] * (jnp.dot(feat, w1_ref[...], preferred_element_type=F32, precision=hp)
                               + b1_ref[...]))
    h = jnp.sin(f1_ref[...] * (jnp.dot(h, w2_ref[...], preferred_element_type=F32, precision=hp)
                               + b2_ref[...]))
    h = jnp.dot(h.astype(BF16), w3_ref[...], preferred_element_type=F32) + b3_ref[...]
    h = h * jnp.exp(-t * jnp.abs(dec_ref[...]))
    hf_ref[...] = h.astype(BF16)
    abs_sum = jnp.sum(jnp.abs(h), 0, keepdims=True)

    @pl.when(i > 0)
    def _():
        sum_ref[...] += abs_sum

    @pl.when(i == 0)
    def _():
        head = h[0:FILTER_HEAD_ROWS]
        rows = lax.broadcasted_iota(jnp.int32, head.shape, 0)
        cols = lax.broadcasted_iota(jnp.int32, head.shape, 1)
        drop = (rows == 0) & (cols >= h.shape[1] // 2)
        hf_ref[0:FILTER_HEAD_ROWS, :] = jnp.where(drop, 0.0, head).astype(BF16)
        sum_ref[...] = abs_sum - jnp.sum(jnp.where(drop, jnp.abs(head), 0.0), 0, keepdims=True)


def _filter_taps(l, w1, b1, freq, w2, b2, w3, b3, decay):
    fo = w1.shape[1]
    n = w3.shape[1]
    da = decay.shape[0]
    t = jnp.arange(l, dtype=F32) / l
    bands = jnp.arange(1, HYENA_BANDS + 1, dtype=F32)
    ang = 2.0 * math.pi * t[:, None] * bands[None, :]
    feat = jnp.concatenate([t[:, None], jnp.cos(ang), jnp.sin(ang)], -1)
    feat = jnp.pad(feat, ((0, 0), (0, FEAT_PAD - HYENA_EMB)))
    w1p = jnp.pad(w1, ((0, FEAT_PAD - HYENA_EMB), (0, 0)))
    dec = jnp.tile(decay, n // da).reshape(1, n)
    tt = FILTER_ROW_TILE
    const = lambda shape: pl.BlockSpec(shape, lambda i: (0,) * len(shape))
    return pl.pallas_call(
        _filter_body,
        grid=(l // tt,),
        in_specs=[
            pl.BlockSpec((tt, FEAT_PAD), lambda i: (i, 0)),
            const((FEAT_PAD, fo)), const((1, fo)), const((1, fo)),
            const((fo, fo)), const((1, fo)), const((1, fo)),
            const((fo, n)), const((1, n)), const((1, n)),
        ],
        out_specs=[pl.BlockSpec((tt, n), lambda i: (i, 0)), const((1, n))],
        out_shape=[jax.ShapeDtypeStruct((l, n), BF16), jax.ShapeDtypeStruct((1, n), F32)],
        compiler_params=_params(("arbitrary",)),
        name="filter_taps",
    )(feat, w1p, b1.reshape(1, fo), freq[0].reshape(1, fo), w2, b2.reshape(1, fo),
      freq[1].reshape(1, fo), w3.astype(BF16), b3.reshape(1, n), dec)


def _dft_tables(l):
    n = 2 * l
    n1 = n // DFT_N2
    kk = jnp.arange(n1 // 2, dtype=jnp.int32)
    nn = jnp.arange(n1 // 2, dtype=jnp.int32)
    th = (2.0 * math.pi / (2 * n1)) * (((2 * kk[:, None] + 1) * nn[None, :]) % (2 * n1)).astype(F32)
    eye2 = jnp.eye(2, dtype=F32)
    f1 = jnp.kron(jnp.concatenate([jnp.cos(th), -jnp.sin(th)], 0), eye2).astype(BF16)
    g = jnp.kron(jnp.concatenate([jnp.cos(th).T, -jnp.sin(th).T], 1) * (2.0 / n), eye2).astype(BF16)
    k2 = jnp.arange(DFT_N2, dtype=jnp.int32)
    n2 = jnp.arange(DFT_N2, dtype=jnp.int32)
    ph = (n2[None, None, :] * (k2[None, :, None] * (2 * n1) + 2 * kk[:, None, None] + 1)) % (2 * n)
    ang = (2.0 * math.pi / (2 * n)) * ph.astype(F32)
    c, s = jnp.cos(ang), jnp.sin(ang)
    mf = jnp.concatenate([jnp.concatenate([c, s], 2), jnp.concatenate([-s, c], 2)], 1).astype(BF16)
    mi = jnp.swapaxes(mf, 1, 2)
    return f1, g, mf, mi


def _load_pair_group(ref, lead, g):
    start = pl.multiple_of(g * PAIR_GROUP, PAIR_GROUP)
    words = ref.bitcast(jnp.uint32)[(*lead, slice(None), pl.ds(start, PAIR_GROUP), slice(None))]
    words = jnp.swapaxes(words, 0, 1)
    return [pltpu.bitcast(words[i], BF16) for i in range(PAIR_GROUP)]


def _store_pair_group(ref, lead, g, vals):
    start = pl.multiple_of(g * PAIR_GROUP, PAIR_GROUP)
    words = jnp.stack([pltpu.bitcast(v, jnp.uint32) for v in vals], 0)
    ref.bitcast(jnp.uint32)[(*lead, slice(None), pl.ds(start, PAIR_GROUP), slice(None))] = (
        jnp.swapaxes(words, 0, 1))


def _dft_in_body(f_ref, z_ref, a_ref):
    n1 = a_ref.shape[2]
    cb = a_ref.shape[-1]

    def group(g, carry):
        zcat = jnp.concatenate(_load_pair_group(z_ref, (0,), g), 1)
        r = jnp.dot(f_ref[...], zcat, preferred_element_type=F32).astype(BF16)
        cols = [r[:, i * cb:(i + 1) * cb] for i in range(PAIR_GROUP)]
        _store_pair_group(a_ref, (0, 0), g, [c[:2 * n1] for c in cols])
        _store_pair_group(a_ref, (0, 1), g, [c[2 * n1:] for c in cols])
        return carry

    lax.fori_loop(0, z_ref.shape[2] // (2 * PAIR_GROUP), group, 0, unroll=4)


def _dft_in(f1p, z):
    b, half, n2, c = z.shape
    n1 = half
    cb, rs = V7X_LANES, DFT_STEP_ROWS
    return pl.pallas_call(
        _dft_in_body,
        grid=(b, c // cb, n2 // rs),
        in_specs=[
            pl.BlockSpec(f1p.shape, lambda i, j, s: (0, 0)),
            pl.BlockSpec((1, half, rs, cb), lambda i, j, s: (i, 0, s, j)),
        ],
        out_specs=pl.BlockSpec((1, 2, n1, rs, cb), lambda i, j, s: (i, 0, 0, s, j)),
        out_shape=jax.ShapeDtypeStruct((b, 2, n1, n2, c), BF16),
        compiler_params=_params(("parallel", "parallel", "parallel")),
        name="dft_in",
    )(f1p, z)


def _dft_out_body(g_ref, b_ref, x_ref, z_ref, skip_ref, o_ref):
    cb = o_ref.shape[-1]

    def group(g, carry):
        re, im = _load_pair_group(b_ref, (0, 0), g), _load_pair_group(b_ref, (0, 1), g)
        bcat = jnp.concatenate([jnp.concatenate([r, i], 0) for r, i in zip(re, im)], 1)
        y = jnp.dot(g_ref[...], bcat, preferred_element_type=F32)
        xs, zs = _load_pair_group(x_ref, (0,), g), _load_pair_group(z_ref, (0,), g)
        outs = [(xs[i].astype(F32) * (y[:, i * cb:(i + 1) * cb] + skip_ref[...] * zs[i].astype(F32))
                 ).astype(BF16) for i in range(PAIR_GROUP)]
        _store_pair_group(o_ref, (0,), g, outs)
        return carry

    lax.fori_loop(0, x_ref.shape[2] // (2 * PAIR_GROUP), group, 0)


def _dft_out(gp, bb, x, z, skip):
    b, half, n2, c = z.shape
    n1 = half
    cb, rs = V7X_LANES, DFT_STEP_ROWS
    slab = pl.BlockSpec((1, half, rs, cb), lambda i, j, s: (i, 0, s, j))
    return pl.pallas_call(
        _dft_out_body,
        grid=(b, c // cb, n2 // rs),
        in_specs=[
            pl.BlockSpec(gp.shape, lambda i, j, s: (0, 0)),
            pl.BlockSpec((1, 2, n1, rs, cb), lambda i, j, s: (i, 0, 0, s, j)),
            slab, slab,
            pl.BlockSpec((1, cb), lambda i, j, s: (0, j)),
        ],
        out_specs=slab,
        out_shape=jax.ShapeDtypeStruct(z.shape, BF16),
        compiler_params=_params(("parallel", "parallel", "parallel")),
        name="dft_out",
    )(gp, bb, x, z, skip.astype(F32).reshape(1, c))


def _slab_conv_body(mf_ref, mi_ref, k_ref, a_ref, o_ref):
    nb, _, ks, n2, c = a_ref.shape
    for kk in range(ks):
        kr, ki = k_ref[kk, 0].astype(F32), k_ref[kk, 1].astype(F32)
        for b in range(nb):
            a = a_ref[b, :, kk].reshape(2 * n2, c)
            x = jnp.dot(mf_ref[kk], a, preferred_element_type=F32)
            xr, xi = x[:n2], x[n2:]
            y = jnp.concatenate([xr * kr - xi * ki, xr * ki + xi * kr], 0).astype(BF16)
            out = jnp.dot(mi_ref[kk], y, preferred_element_type=F32)
            o_ref[b, :, kk] = out.astype(BF16).reshape(2, n2, c)


def _slab_conv(mf, mi, kspec, a, order):
    b, _, n1, n2, c = a.shape
    m = 2 * n2
    ks = max(1, SLABS_PER_STEP // b)
    return pl.pallas_call(
        _slab_conv_body,
        grid=(n1 // ks,),
        in_specs=[
            pl.BlockSpec((ks, m, m), lambda k: (k, 0, 0)),
            pl.BlockSpec((ks, m, m), lambda k: (k, 0, 0)),
            pl.BlockSpec((ks, 2, n2, c), lambda k: (k, 0, 0, order)),
            pl.BlockSpec((b, 2, ks, n2, c), lambda k: (0, 0, k, 0, 0)),
        ],
        out_specs=pl.BlockSpec((b, 2, ks, n2, c), lambda k: (0, 0, k, 0, 0)),
        out_shape=jax.ShapeDtypeStruct(a.shape, BF16),
        compiler_params=_params(("parallel",)),
        name="slab_conv",
    )(mf, mi, kspec, a)


def _slab_spec_body(mf_ref, sum_ref, a_ref, k_ref):
    _, ks, n2, c = a_ref.shape
    half = c // 2
    inv = 1.0 / (sum_ref[:, :half] + sum_ref[:, half:] + 1e-6)
    for kk in range(ks):
        x = jnp.dot(mf_ref[kk], a_ref[:, kk].reshape(2 * n2, c), preferred_element_type=F32)
        k_ref[kk, 0] = ((x[:n2, :half] + x[:n2, half:]) * inv).astype(BF16)
        k_ref[kk, 1] = ((x[n2:, :half] - x[n2:, half:]) * inv).astype(BF16)


def _slab_spec(mf, sums, a):
    _, _, n1, n2, c = a.shape
    m = 2 * n2
    ks = SPEC_SLABS_PER_STEP
    return pl.pallas_call(
        _slab_spec_body,
        grid=(n1 // ks,),
        in_specs=[
            pl.BlockSpec((ks, m, m), lambda k: (k, 0, 0)),
            pl.BlockSpec((1, c), lambda k: (0, 0)),
            pl.BlockSpec((None, 2, ks, n2, c), lambda k: (0, 0, k, 0, 0)),
        ],
        out_specs=pl.BlockSpec((ks, 2, n2, c // 2), lambda k: (k, 0, 0, 0)),
        out_shape=jax.ShapeDtypeStruct((n1, 2, n2, c // 2), BF16),
        compiler_params=_params(("parallel",)),
        name="slab_spec",
    )(mf, sums, a)


def _hyena_spectrum(l, tables, w1, b1, freq, w2, b2, w3, b3, decay):
    f1p, _, mf, _ = tables
    taps, sums = _filter_taps(l, w1, b1, freq, w2, b2, w3, b3, decay)
    a = _dft_in(f1p, taps.reshape(1, l // DFT_N2, DFT_N2, taps.shape[1]))
    return _slab_spec(mf, sums, a)


def _hyena(v, x1, x2g, kspec, skip, tables):
    f1p, gp, mf, mi = tables
    b, l, c = v.shape
    slabs = lambda u: u.reshape(b, l // DFT_N2, DFT_N2, c)
    z = slabs(v)
    for order, xg in enumerate((slabs(x1), slabs(x2g))):
        bb = _slab_conv(mf, mi, kspec, _dft_in(f1p, z), order)
        z = _dft_out(gp, bb, xg, z, skip[order])
    return z.reshape(b, l, c)


def _head_masks(rows, heads):
    lane = lax.broadcasted_iota(jnp.int32, (rows, heads * HEAD_DIM), 1)
    return [(lane >= h * HEAD_DIM) & (lane < (h + 1) * HEAD_DIM) for h in range(heads)]


def _stack_heads(q, masks):
    zero = jnp.zeros_like(q)
    return jnp.concatenate([jnp.where(m, q, zero) for m in masks], 0)


def _unstack_heads(res, masks, rows):
    out = jnp.where(masks[0], res[:rows], 0.0)
    for h in range(1, len(masks)):
        out = out + jnp.where(masks[h], res[h * rows:(h + 1) * rows], 0.0)
    return out


def _na_bias_table(rpb):
    heads = rpb.shape[0]
    c = jnp.arange(GRID_W)
    col_start = jnp.clip(c - NA_KC // 2, 0, GRID_W - NA_KC)
    col_ok = (c[None, :] >= col_start[:, None]) & (c[None, :] < col_start[:, None] + NA_KC)
    dc = jnp.clip(c[None, :] - c[:, None], -(NA_KC - 1), NA_KC - 1) + NA_KC - 1
    onehot = (dc[:, :, None] == jnp.arange(2 * NA_KC - 1)[None, None, :]).astype(F32)
    tcol = jnp.einsum("hrj,qkj->hrqk", rpb.astype(F32), onehot, precision=lax.Precision.HIGHEST)
    per_idx = [tcol[:, NA_KR - 1 - idx:2 * NA_KR - 1 - idx] for idx in range(NA_KR)]
    bias = jnp.transpose(jnp.stack(per_idx, 0), (0, 1, 3, 2, 4))
    bias = jnp.where(col_ok[None, None, :, None, :], bias * LOG2_E, NEG_INF)
    return bias.reshape(NA_KR, heads * GRID_W, NA_KR * GRID_W)


def _na_body(q_ref, k_ref, v_ref, z_ref, bias_ref, o_ref, *, rows, rb, heads):
    blk = pl.program_id(1)
    masks = _head_masks(GRID_W, heads)
    span = NA_KR * GRID_W

    def one_row(rr, carry):
        r = blk * rb + rr
        start = jnp.clip(r - NA_KR // 2, 0, rows - NA_KR)
        idx = r - start
        koff = pl.multiple_of(start * GRID_W, GRID_W)
        qoff = pl.multiple_of(rr * GRID_W, GRID_W)
        q = q_ref[pl.ds(qoff, GRID_W), :]
        kk = k_ref[pl.ds(koff, span), :]
        vv = v_ref[pl.ds(koff, span), :]
        s = lax.dot_general(_stack_heads(q, masks), kk, (((1,), (1,)), ((), ())),
                            preferred_element_type=F32)
        s = s + bias_ref[idx]
        m = jnp.max(s, -1, keepdims=True)
        e = jnp.exp2(s - m)
        p = (e / jnp.sum(e, -1, keepdims=True)).astype(BF16)
        o = _unstack_heads(jnp.dot(p, vv, preferred_element_type=F32), masks, GRID_W)
        z = z_ref[pl.ds(qoff, GRID_W), :].astype(F32)
        o_ref[pl.ds(qoff, GRID_W), :] = (o * _silu(z)).astype(BF16)
        return carry

    lax.fori_loop(0, rb, one_row, 0, unroll=NA_ROW_UNROLL)


def _na(proj_b, bias):
    b, l, n = proj_b.shape
    db = n // 4
    heads = db // HEAD_DIM
    rows = l // GRID_W
    rb = NA_ROWS_PER_STEP
    t = rb * GRID_W
    return pl.pallas_call(
        functools.partial(_na_body, rows=rows, rb=rb, heads=heads),
        grid=(b, rows // rb),
        in_specs=[
            pl.BlockSpec((None, t, db), lambda i, j: (i, j, 0)),
            pl.BlockSpec((None, l, db), lambda i, j: (i, 0, 1)),
            pl.BlockSpec((None, l, db), lambda i, j: (i, 0, 2)),
            pl.BlockSpec((None, t, db), lambda i, j: (i, j, 3)),
            pl.BlockSpec(bias.shape, lambda i, j: (0, 0, 0)),
        ],
        out_specs=pl.BlockSpec((None, t, db), lambda i, j: (i, j, 0)),
        out_shape=jax.ShapeDtypeStruct((b, l, db), BF16),
        compiler_params=_params(("parallel", "parallel")),
        name="na",
    )(proj_b, proj_b, proj_b, proj_b, bias)


def _attend(q, kk, vv, valid, masks):
    rows = q.shape[0]
    s = lax.dot_general(_stack_heads(q, masks), kk, (((1,), (1,)), ((), ())), preferred_element_type=F32)
    s = jnp.where(valid, s, NEG_INF)
    m = jnp.max(s, -1, keepdims=True)
    e = jnp.exp2(s - m)
    l = jnp.sum(e, -1, keepdims=True)
    p = (e / l).astype(BF16)
    o = _unstack_heads(jnp.dot(p, vv, preferred_element_type=F32), masks, rows)
    lse = _unstack_heads(jnp.broadcast_to(m + jnp.log2(l), (s.shape[0], q.shape[1])), masks, rows)
    return o, lse


def _split_pattern(dilation, n, nchunks, q_ref, kbuf, vbuf, o_split, l_split, masks, heads):
    blk = DIL_BLK
    dc = heads * HEAD_DIM
    phases = DIL_SPLIT // dilation
    mq = blk // phases
    row = lax.broadcasted_iota(jnp.int32, (heads * blk, 3 * blk), 0) % blk
    col = lax.broadcasted_iota(jnp.int32, (heads * blk, 3 * blk), 1)
    qa, qm = row // mq, row % mq
    ka = sum((col >= a * 3 * mq).astype(jnp.int32) for a in range(1, phases)) if phases > 1 else 0
    km = col - ka * (3 * mq)
    band = jnp.abs(phases * (km - mq - qm) + (ka - qa)) <= blk
    for m0 in range(0, blk, mq):
        gm = km + (n * blk + m0 - mq)
        valid = band & (gm >= 0) & (gm < nchunks * blk)
        for j in range(dilation):
            lanes = [slice((j + dilation * a) * dc, (j + dilation * a + 1) * dc) for a in range(phases)]
            q = jnp.concatenate([q_ref[m0:m0 + mq, ln] for ln in lanes], 0)
            krows = slice(blk + m0 - mq, blk + m0 + 2 * mq)
            kk = jnp.concatenate([kbuf[krows, ln] for ln in lanes], 0)
            vv = jnp.concatenate([vbuf[krows, ln] for ln in lanes], 0)
            o, lse = _attend(q, kk, vv, valid, masks)
            for a in range(phases):
                o_split[j + dilation * a, m0:m0 + mq, :] = o[a * mq:(a + 1) * mq]
                l_split[j + dilation * a, m0:m0 + mq, :] = lse[a * mq:(a + 1) * mq]


def _dil_body(qn_ref, knp_ref, kn_ref, knn_ref, vnp_ref, vn_ref, vnn_ref,
              qs_ref, ksp_ref, ks_ref, ksn_ref, vsp_ref, vs_ref, vsn_ref, cz_ref, y_ref,
              kbn, vbn, kbs, vbs, o_split, l_split, o_all, l_all, *, heads, nchunks):
    n = pl.program_id(1)
    blk = DIL_BLK
    chunk = qn_ref.shape[0]
    per = chunk // blk
    masks = _head_masks(blk, heads)
    for buf, prev, cur, nxt, halo in ((kbn, knp_ref, kn_ref, knn_ref, blk), (vbn, vnp_ref, vn_ref, vnn_ref, blk),
                                      (kbs, ksp_ref, ks_ref, ksn_ref, blk), (vbs, vsp_ref, vs_ref, vsn_ref, blk)):
        main = cur.shape[0]
        buf[0:halo] = prev[...]
        buf[halo:halo + main] = cur[...]
        buf[halo + main:] = nxt[...]

    qi = lax.broadcasted_iota(jnp.int32, (heads * blk, 3 * blk), 0) % blk
    ki = lax.broadcasted_iota(jnp.int32, (heads * blk, 3 * blk), 1)
    band = jnp.abs(ki - blk - qi) <= blk

    def token_block(i, carry):
        g = n * per + i
        off = pl.multiple_of(i * blk, blk)
        valid = band & ((ki >= blk) | (g > 0)) & ((ki < 2 * blk) | (g < nchunks * per - 1))
        o, lse = _attend(qn_ref[pl.ds(off, blk), :], kbn[pl.ds(off, 3 * blk), :], vbn[pl.ds(off, 3 * blk), :],
                         valid, masks)
        o_all[0, pl.ds(off, blk), :] = o
        l_all[0, pl.ds(off, blk), :] = lse
        return carry

    lax.fori_loop(0, per, token_block, 0, unroll=True)

    for g, (_, dilation) in enumerate(DIL_PATTERNS):
        if dilation == 1:
            continue
        _split_pattern(dilation, n, nchunks, qs_ref, kbs, vbs, o_split, l_split, masks, heads)
        o_all[g] = jnp.swapaxes(o_split[...], 0, 1).reshape(chunk, heads * HEAD_DIM)
        l_all[g] = jnp.swapaxes(l_split[...], 0, 1).reshape(chunk, heads * HEAD_DIM)

    ls = [l_all[g] for g in range(len(DIL_PATTERNS))]
    m = functools.reduce(jnp.maximum, ls)
    es = [jnp.exp2(l - m) for l in ls]
    den = functools.reduce(lambda a, b: a + b, es)
    o = functools.reduce(lambda a, b: a + b, [(e / den) * o_all[g] for g, e in enumerate(es)])
    y_ref[...] = (o * cz_ref[...].astype(F32)).astype(BF16)


def _dilated_mixture(qn, kn, vn, qs, ks, vs, cz):
    b, l, dc = qn.shape
    heads = dc // HEAD_DIM
    blk = DIL_BLK
    assert DIL_PATTERNS[0][1] == 1 and all(w == 2 * blk * d and DIL_SPLIT % d == 0 for w, d in DIL_PATTERNS)
    chunk = DIL_SPLIT * blk
    nchunks = l // chunk
    per = chunk // blk
    nat = pl.BlockSpec((None, chunk, dc), lambda i, n: (i, n, 0))
    nat_prev = pl.BlockSpec((None, blk, dc), lambda i, n: (i, jnp.maximum(n * per - 1, 0), 0))
    nat_next = pl.BlockSpec((None, blk, dc), lambda i, n: (i, jnp.minimum((n + 1) * per, nchunks * per - 1), 0))
    spl = pl.BlockSpec((None, blk, DIL_SPLIT * dc), lambda i, n: (i, n, 0))
    spl_prev = pl.BlockSpec((None, blk, DIL_SPLIT * dc), lambda i, n: (i, jnp.maximum(n - 1, 0), 0))
    spl_next = pl.BlockSpec((None, blk, DIL_SPLIT * dc), lambda i, n: (i, jnp.minimum(n + 1, nchunks - 1), 0))
    return pl.pallas_call(
        functools.partial(_dil_body, heads=heads, nchunks=nchunks),
        grid=(b, nchunks),
        in_specs=[nat, nat_prev, nat, nat_next, nat_prev, nat, nat_next,
                  spl, spl_prev, spl, spl_next, spl_prev, spl, spl_next, nat],
        out_specs=nat,
        out_shape=jax.ShapeDtypeStruct((b, l, dc), BF16),
        scratch_shapes=[pltpu.VMEM((chunk + 2 * blk, dc), BF16), pltpu.VMEM((chunk + 2 * blk, dc), BF16),
                        pltpu.VMEM((3 * blk, DIL_SPLIT * dc), BF16), pltpu.VMEM((3 * blk, DIL_SPLIT * dc), BF16),
                        pltpu.VMEM((DIL_SPLIT, blk, dc), F32), pltpu.VMEM((DIL_SPLIT, blk, dc), F32),
                        pltpu.VMEM((len(DIL_PATTERNS), chunk, dc), F32),
                        pltpu.VMEM((len(DIL_PATTERNS), chunk, dc), F32)],
        compiler_params=_params(("parallel", "parallel")),
        name="dilated",
    )(qn, kn, kn, kn, vn, vn, vn, qs, ks, ks, ks, vs, vs, vs, cz)


def _tail_body(x_ref, ya_ref, yb_ref, yc_ref, g_ref, gate_ref, wa_ref, wb_ref, wc_ref, wo_ref, lng_ref, lnb_ref,
               out_ref, *, alpha):
    d = x_ref.shape[-1]
    pa = jnp.dot(ya_ref[...], wa_ref[...], preferred_element_type=F32)
    pb = jnp.dot(yb_ref[...], wb_ref[...], preferred_element_type=F32)
    pc = jnp.dot(yc_ref[...], wc_ref[...], preferred_element_type=F32)
    g = g_ref[...].astype(F32)
    merged = g[:, :d] * pa + g[:, d:2 * d] * pb + g[:, 2 * d:] * pc
    sub = jnp.dot(merged.astype(BF16), wo_ref[...], preferred_element_type=F32) * gate_ref[...]
    res = alpha * x_ref[...] + sub
    out_ref[...] = _layernorm(res) * lng_ref[...] + lnb_ref[...]


def _tail(x, ya, yb, yc, g_all, gate, wa, wb, wc, wo, ln_g, ln_b, alpha):
    b, l, d = x.shape
    t = TOKEN_TILE
    tok = lambda w: pl.BlockSpec((None, t, w), lambda i, j: (i, j, 0))
    const = lambda a: pl.BlockSpec(a.shape, lambda i, j: (0,) * a.ndim)
    ln_g, ln_b = ln_g.reshape(1, d), ln_b.reshape(1, d)
    return pl.pallas_call(
        functools.partial(_tail_body, alpha=alpha),
        grid=(b, l // t),
        in_specs=[tok(d), tok(ya.shape[-1]), tok(yb.shape[-1]), tok(yc.shape[-1]), tok(3 * d),
                  pl.BlockSpec((None, 1, d), lambda i, j: (i, 0, 0)),
                  const(wa), const(wb), const(wc), const(wo), const(ln_g), const(ln_b)],
        out_specs=tok(d),
        out_shape=jax.ShapeDtypeStruct((b, l, d), F32),
        compiler_params=_params(("parallel", "parallel")),
        name="tail",
    )(x, ya, yb, yc, g_all, gate.reshape(b, 1, d), wa, wb, wc, wo, ln_g, ln_b)


def _rope_tables(l, heads):
    half = HEAD_DIM // 2
    inv = ROPE_THETA ** (-jnp.arange(half, dtype=F32) / half)
    ang = jnp.arange(l, dtype=F32)[:, None] * inv[None, :]
    cos, sin = jnp.cos(ang), jnp.sin(ang)
    return (jnp.tile(jnp.concatenate([cos, cos], -1), (1, heads)),
            jnp.tile(jnp.concatenate([-sin, sin], -1), (1, heads)))


def _layer(x, ada, lw, consts, alpha):
    d = x.shape[-1]
    shift, scale, gate = ada[:, :d], ada[:, d:2 * d], ada[:, 2 * d:]
    h = _ln_mod(x, scale, shift)
    w_in, b_in = lw["w_in"], lw["b_in"]
    proj_a = _proj(h, w_in[:, :2 * d], b_in[:2 * d])
    q_cols = jnp.arange(d) < d // 4
    proj_b = _proj(h, w_in[:, 2 * d:3 * d], b_in[2 * d:3 * d], col_scale=jnp.where(q_cols, QK_SCALE_LOG2, 1.0))
    cq, ck, cv, cz, cqs, cks, cvs = _proj_rope(h, w_in[:, 3 * d:4 * d], b_in[3 * d:4 * d], *consts["rope"])
    gates = _proj(h, w_in[:, 4 * d:], b_in[4 * d:], sigmoid=True)

    v, x1, x2g = _hy_pre(proj_a, lw["conv_w"], lw["conv_b"])
    ya = _hyena(v, x1, x2g, consts["kspec"], lw["skip"], consts["dft"])
    yb = _na(proj_b, lw["na_bias"])
    yc = _dilated_mixture(cq, ck, cv, cqs, cks, cvs, cz)
    return _tail(x, ya, yb, yc, gates, gate, lw["wa"], lw["wb"], lw["wc"], lw["wo"],
                 lw["ln_g"], lw["ln_b"], alpha)


def kernel(x_prompt, x_sample, c_prompt, c_sample, w_ada, b_ada, w_in, b_in, hy_conv_w, hy_conv_b, hy_w1, hy_b1, hy_freq, hy_w2, hy_b2, hy_w3, hy_b3, hy_decay, hy_skip, na_rpb, w_branch_a, w_branch_b, w_branch_c, w_out, ln_g, ln_b):
    depth, d, _ = w_in.shape
    heads_c = (d // 4) // HEAD_DIM
    alpha = (2 * depth) ** 0.25
    groups = [(x_prompt, c_prompt), (x_sample, c_sample)]

    nb_p = c_prompt.shape[0]
    c_all = jnp.concatenate([c_prompt, c_sample], 0)
    pad_rows = -c_all.shape[0] % 8
    ada_all = _ada(jnp.pad(c_all, ((0, pad_rows), (0, 0))), w_ada, b_ada)
    adas = [ada_all[:, :nb_p], ada_all[:, nb_p:nb_p + c_sample.shape[0]]]

    shared = {}
    for x, _ in groups:
        l = x.shape[1]
        if l not in shared:
            shared[l] = {"dft": _dft_tables(l), "rope": _rope_tables(l, heads_c)}

    ys = [x for x, _ in groups]
    for layer in range(depth):
        lw = {
            "w_in": w_in[layer].astype(BF16), "b_in": b_in[layer],
            "conv_w": hy_conv_w[layer], "conv_b": hy_conv_b[layer], "skip": hy_skip[layer],
            "na_bias": _na_bias_table(na_rpb[layer]),
            "wa": w_branch_a[layer].astype(BF16), "wb": w_branch_b[layer].astype(BF16),
            "wc": w_branch_c[layer].astype(BF16), "wo": w_out[layer].astype(BF16),
            "ln_g": ln_g[layer], "ln_b": ln_b[layer],
        }
        kspecs = {}
        for gi in range(len(groups)):
            l = ys[gi].shape[1]
            if l not in kspecs:
                kspecs[l] = _hyena_spectrum(l, shared[l]["dft"], hy_w1[layer], hy_b1[layer], hy_freq[layer],
                                            hy_w2[layer], hy_b2[layer], hy_w3[layer], hy_b3[layer],
                                            hy_decay[layer])
            consts = dict(shared[l], kspec=kspecs[l])
            ys[gi] = _layer(ys[gi], adas[gi][layer], lw, consts, alpha)
    return tuple(ys)
```

```python
import functools
import math

import jax
import jax.numpy as jnp
from jax import lax
from jax.experimental import pallas as pl
from jax.experimental.pallas import tpu as pltpu

F32 = jnp.float32
BF16 = jnp.bfloat16

GRID_W = 64
HEAD_DIM = 64
HYENA_BANDS = 16
HYENA_EMB = 2 * HYENA_BANDS + 1
NA_KR = 8
NA_KC = 16
DIL_PATTERNS = ((128, 1), (512, 4), (2048, 16))
DIL_BLK = 64
DIL_SPLIT = 16
NA_ROW_UNROLL = 32
ROPE_THETA = 10000.0
LN_EPS = 1e-5
NEG_INF = -1e30
LOG2_E = math.log2(math.e)
QK_SCALE_LOG2 = HEAD_DIM ** -0.5 * LOG2_E

V7X_LANES = 128
V7X_SUBLANES = 8
V7X_VMEM_LIMIT_BYTES = 56 * 1024 * 1024

DFT_N2 = V7X_LANES
PAIR_GROUP = V7X_SUBLANES
DFT_STEP_ROWS = 128
SLABS_PER_STEP = 8
FEAT_PAD = V7X_LANES
BF16_TILE_ROWS = 2 * V7X_SUBLANES

TOKEN_TILE = 1024
WIDE_TOKEN_TILE = 2048
PROJ_COL_TILE = 1024
ROPE_TOKEN_TILE = 1024
FILTER_ROW_TILE = 1024
NA_ROWS_PER_STEP = 32
SPEC_SLABS_PER_STEP = 2
HY_SHIFT_ROWS = 256


def _params(sem):
    return pltpu.CompilerParams(dimension_semantics=sem, vmem_limit_bytes=V7X_VMEM_LIMIT_BYTES)


def _sigmoid(x):
    return 1.0 / (1.0 + jnp.exp(-x))


def _silu(x):
    return x * _sigmoid(x)


def _ada_body(c_ref, w_ref, b_ref, o_ref):
    s = _silu(c_ref[...])
    o_ref[...] = jnp.dot(s, w_ref[...], preferred_element_type=F32,
                         precision=lax.Precision.HIGHEST) + b_ref[...]


def _ada(c_all, w_ada, b_ada):
    depth, d, n = w_ada.shape
    rows = c_all.shape[0]
    tn = PROJ_COL_TILE
    return pl.pallas_call(
        _ada_body,
        grid=(depth, n // tn),
        in_specs=[
            pl.BlockSpec((rows, d), lambda l, j: (0, 0)),
            pl.BlockSpec((None, d, tn), lambda l, j: (l, 0, j)),
            pl.BlockSpec((None, 1, tn), lambda l, j: (l, 0, j)),
        ],
        out_specs=pl.BlockSpec((None, rows, tn), lambda l, j: (l, 0, j)),
        out_shape=jax.ShapeDtypeStruct((depth, rows, n), F32),
        compiler_params=_params(("parallel", "parallel")),
        name="ada",
    )(c_all, w_ada, b_ada.reshape(depth, 1, n))


def _layernorm(x):
    mu = jnp.mean(x, -1, keepdims=True)
    xc = x - mu
    var = jnp.mean(xc * xc, -1, keepdims=True)
    return xc * lax.rsqrt(var + LN_EPS)


def _ln_mod_body(x_ref, sc_ref, sh_ref, o_ref):
    h = _layernorm(x_ref[...]) * (1.0 + sc_ref[...]) + sh_ref[...]
    o_ref[...] = h.astype(BF16)


def _ln_mod(x, scale, shift):
    b, l, d = x.shape
    t = WIDE_TOKEN_TILE
    return pl.pallas_call(
        _ln_mod_body,
        grid=(b, l // t),
        in_specs=[
            pl.BlockSpec((None, t, d), lambda i, j: (i, j, 0)),
            pl.BlockSpec((None, 1, d), lambda i, j: (i, 0, 0)),
            pl.BlockSpec((None, 1, d), lambda i, j: (i, 0, 0)),
        ],
        out_specs=pl.BlockSpec((None, t, d), lambda i, j: (i, j, 0)),
        out_shape=jax.ShapeDtypeStruct((b, l, d), BF16),
        compiler_params=_params(("parallel", "parallel")),
        name="ln_mod",
    )(x, scale.reshape(b, 1, d), shift.reshape(b, 1, d))


def _proj_body(h_ref, w_ref, b_ref, *rest, sigmoid, scaled):
    o_ref = rest[-1]
    acc = jnp.dot(h_ref[...], w_ref[...], preferred_element_type=F32) + b_ref[...]
    if scaled:
        acc = acc * rest[0][...]
    if sigmoid:
        acc = _sigmoid(acc)
    o_ref[...] = acc.astype(o_ref.dtype)


def _proj(h, w, bias, col_scale=None, sigmoid=False):
    b, l, d = h.shape
    n = w.shape[1]
    t, tn = TOKEN_TILE, PROJ_COL_TILE
    row = pl.BlockSpec((1, tn), lambda j, i, k: (0, j))
    scale_args = [] if col_scale is None else [col_scale.reshape(1, n)]
    return pl.pallas_call(
        functools.partial(_proj_body, sigmoid=sigmoid, scaled=col_scale is not None),
        grid=(n // tn, b, l // t),
        in_specs=[
            pl.BlockSpec((None, t, d), lambda j, i, k: (i, k, 0)),
            pl.BlockSpec((d, tn), lambda j, i, k: (0, j)),
            row,
        ] + [row] * len(scale_args),
        out_specs=pl.BlockSpec((None, t, tn), lambda j, i, k: (i, k, j)),
        out_shape=jax.ShapeDtypeStruct((b, l, n), BF16),
        compiler_params=_params(("parallel", "parallel", "parallel")),
        name="proj",
    )(h, w, bias.reshape(1, n), *scale_args)


def _rope_lanes(x, cos, sin_signed):
    outs = []
    lane = lax.broadcasted_iota(jnp.int32, (x.shape[0], V7X_LANES), 1)
    first_half = (lane % HEAD_DIM) < (HEAD_DIM // 2)
    for c0 in range(0, x.shape[1], V7X_LANES):
        xc = x[:, c0:c0 + V7X_LANES]
        partner = jnp.where(first_half,
                            pltpu.roll(xc, V7X_LANES - HEAD_DIM // 2, 1),
                            pltpu.roll(xc, HEAD_DIM // 2, 1))
        outs.append(xc * cos[:, c0:c0 + V7X_LANES] + partner * sin_signed[:, c0:c0 + V7X_LANES])
    return jnp.concatenate(outs, 1)


def _store_split(ref, x):
    t, dc = x.shape
    parts = jnp.swapaxes(x.reshape(t // DIL_SPLIT, DIL_SPLIT, dc), 0, 1)
    for r in range(DIL_SPLIT):
        ref[:, r * dc:(r + 1) * dc] = parts[r].astype(BF16)


def _proj_rope_body(h_ref, w_ref, b_ref, cos_ref, sin_ref, q_ref, k_ref, v_ref, z_ref,
                    qs_ref, ks_ref, vs_ref):
    dc = q_ref.shape[-1]
    acc = jnp.dot(h_ref[...], w_ref[...], preferred_element_type=F32) + b_ref[...]
    cos, sin = cos_ref[...], sin_ref[...]
    q = _rope_lanes(acc[:, :dc], cos, sin) * QK_SCALE_LOG2
    k = _rope_lanes(acc[:, dc:2 * dc], cos, sin)
    v = acc[:, 2 * dc:3 * dc]
    for ref, split_ref, val in ((q_ref, qs_ref, q), (k_ref, ks_ref, k), (v_ref, vs_ref, v)):
        ref[...] = val.astype(BF16)
        _store_split(split_ref, val)
    z_ref[...] = _silu(acc[:, 3 * dc:]).astype(BF16)


def _proj_rope(h, w, bias, cos_t, sin_t):
    b, l, d = h.shape
    n = w.shape[1]
    dc = n // 4
    t = ROPE_TOKEN_TILE
    tok = pl.BlockSpec((None, t, dc), lambda i, k: (i, k, 0))
    shp = jax.ShapeDtypeStruct((b, l, dc), BF16)
    spl = pl.BlockSpec((None, t // DIL_SPLIT, DIL_SPLIT * dc), lambda i, k: (i, k, 0))
    spl_shp = jax.ShapeDtypeStruct((b, l // DIL_SPLIT, DIL_SPLIT * dc), BF16)
    return pl.pallas_call(
        _proj_rope_body,
        grid=(b, l // t),
        in_specs=[
            pl.BlockSpec((None, t, d), lambda i, k: (i, k, 0)),
            pl.BlockSpec((d, n), lambda i, k: (0, 0)),
            pl.BlockSpec((1, n), lambda i, k: (0, 0)),
            pl.BlockSpec((t, dc), lambda i, k: (k, 0)),
            pl.BlockSpec((t, dc), lambda i, k: (k, 0)),
        ],
        out_specs=[tok, tok, tok, tok, spl, spl, spl],
        out_shape=[shp, shp, shp, shp, spl_shp, spl_shp, spl_shp],
        compiler_params=_params(("parallel", "parallel")),
        name="proj_rope",
    )(h, w, bias.reshape(1, n), cos_t, sin_t)


def _hy_pre_body(main_ref, prev_ref, next_ref, shift_ref, cw_ref, cb_ref, v_ref, x1_ref, x2_ref, *, da):
    i = pl.program_id(1)
    last = pl.num_programs(1) - 1
    t = main_ref.shape[0]
    halo = prev_ref.shape[0]
    blk = shift_ref.shape[1]
    nc = 3 * da
    outs = (v_ref, x1_ref, x2_ref)
    edge = lax.broadcasted_iota(jnp.int32, (V7X_SUBLANES, nc), 0)
    for r0 in range(0, t, blk):
        xb = main_ref[r0:r0 + blk, :nc]
        sh = jnp.dot(shift_ref[...], xb, preferred_element_type=F32)
        if r0 == 0:
            prev_row = jnp.where(i > 0, prev_ref[:, :nc].astype(F32)[halo - 1:halo], 0.0)
        else:
            prev_row = main_ref[r0 - halo:r0, :nc].astype(F32)[halo - 1:halo]
        if r0 + blk == t:
            next_row = jnp.where(i < last, next_ref[:, :nc].astype(F32)[0:1], 0.0)
        else:
            next_row = main_ref[r0 + blk:r0 + blk + halo, :nc].astype(F32)[0:1]
        up, dn = sh[:blk], sh[blk:]
        up = jnp.concatenate([up[:V7X_SUBLANES] + jnp.where(edge == 0, prev_row, 0.0), up[V7X_SUBLANES:]], 0)
        dn = jnp.concatenate([dn[:blk - V7X_SUBLANES],
                              dn[blk - V7X_SUBLANES:] + jnp.where(edge == V7X_SUBLANES - 1, next_row, 0.0)], 0)
        uc = up * cw_ref[0:1, :] + xb.astype(F32) * cw_ref[1:2, :] + dn * cw_ref[2:3, :] + cb_ref[...]
        for part in range(3):
            val = uc[:, part * da:(part + 1) * da]
            if part == 2:
                val = val * _silu(main_ref[r0:r0 + blk, nc:].astype(F32))
            outs[part][r0:r0 + blk, :] = val.astype(BF16)


def _hy_pre(proj_a, conv_w, conv_b):
    b, l, n = proj_a.shape
    da = n // 4
    t, halo = WIDE_TOKEN_TILE, BF16_TILE_ROWS
    nh = t // halo
    blk = HY_SHIFT_ROWS
    shift = jnp.concatenate([jnp.eye(blk, k=-1, dtype=BF16), jnp.eye(blk, k=1, dtype=BF16)], 0)
    tok = pl.BlockSpec((None, t, da), lambda i, j: (i, j, 0))
    shp = jax.ShapeDtypeStruct((b, l, da), BF16)
    return pl.pallas_call(
        functools.partial(_hy_pre_body, da=da),
        grid=(b, l // t),
        in_specs=[
            pl.BlockSpec((None, t, n), lambda i, j: (i, j, 0)),
            pl.BlockSpec((None, halo, n), lambda i, j: (i, jnp.maximum(j * nh - 1, 0), 0)),
            pl.BlockSpec((None, halo, n), lambda i, j: (i, jnp.minimum((j + 1) * nh, l // halo - 1), 0)),
            pl.BlockSpec((2 * blk, blk), lambda i, j: (0, 0)),
            pl.BlockSpec((3, 3 * da), lambda i, j: (0, 0)),
            pl.BlockSpec((1, 3 * da), lambda i, j: (0, 0)),
        ],
        out_specs=[tok, tok, tok],
        out_shape=[shp, shp, shp],
        compiler_params=_params(("parallel", "parallel")),
        name="hy_pre",
    )(proj_a, proj_a, proj_a, shift, conv_w, conv_b.reshape(1, 3 * da))


def _filter_body(feat_ref, w1_ref, b1_ref, f0_ref, w2_ref, b2_ref, f1_ref, w3_ref, b3_ref, dec_ref,
                 hf_ref, sum_ref):
    i = pl.program_id(0)
    hp = lax.Precision.HIGHEST
    feat = feat_ref[...]
    t = feat[:, 0:1]
    h = jnp.sin(f0_ref[...] * (jnp.dot(feat, w1_ref[...], preferred_element_type=F32, precision=hp)
                               + b1_ref[...]))
    h = jnp.sin(f1_ref[...] * (jnp.dot(h, w2_ref[...], preferred_element_type=F32, precision=hp)
                               + b2_ref[...]))
    h = jnp.dot(h.astype(BF16), w3_ref[...], preferred_element_type=F32) + b3_ref[...]
    h = h * jnp.exp(-t * jnp.abs(dec_ref[...]))
    hf_ref[...] = h.astype(BF16)
    abs_sum = jnp.sum(jnp.abs(h), 0, keepdims=True)

    @pl.when(i > 0)
    def _():
        sum_ref[...] += abs_sum

    @pl.when(i == 0)
    def _():
        head = h[0:BF16_TILE_ROWS]
        rows = lax.broadcasted_iota(jnp.int32, head.shape, 0)
        cols = lax.broadcasted_iota(jnp.int32, head.shape, 1)
        drop = (rows == 0) & (cols >= h.shape[1] // 2)
        hf_ref[0:BF16_TILE_ROWS, :] = jnp.where(drop, 0.0, head).astype(BF16)
        sum_ref[...] = abs_sum - jnp.sum(jnp.where(drop, jnp.abs(head), 0.0), 0, keepdims=True)


def _filter_taps(l, w1, b1, freq, w2, b2, w3, b3, decay):
    fo = w1.shape[1]
    n = w3.shape[1]
    da = decay.shape[0]
    t = jnp.arange(l, dtype=F32) / l
    bands = jnp.arange(1, HYENA_BANDS + 1, dtype=F32)
    ang = 2.0 * math.pi * t[:, None] * bands[None, :]
    feat = jnp.concatenate([t[:, None], jnp.cos(ang), jnp.sin(ang)], -1)
    feat = jnp.pad(feat, ((0, 0), (0, FEAT_PAD - HYENA_EMB)))
    w1p = jnp.pad(w1, ((0, FEAT_PAD - HYENA_EMB), (0, 0)))
    dec = jnp.tile(decay, n // da).reshape(1, n)
    tt = FILTER_ROW_TILE
    const = lambda shape: pl.BlockSpec(shape, lambda i: (0,) * len(shape))
    return pl.pallas_call(
        _filter_body,
        grid=(l // tt,),
        in_specs=[
            pl.BlockSpec((tt, FEAT_PAD), lambda i: (i, 0)),
            const((FEAT_PAD, fo)), const((1, fo)), const((1, fo)),
            const((fo, fo)), const((1, fo)), const((1, fo)),
            const((fo, n)), const((1, n)), const((1, n)),
        ],
        out_specs=[pl.BlockSpec((tt, n), lambda i: (i, 0)), const((1, n))],
        out_shape=[jax.ShapeDtypeStruct((l, n), BF16), jax.ShapeDtypeStruct((1, n), F32)],
        compiler_params=_params(("arbitrary",)),
        name="filter_taps",
    )(feat, w1p, b1.reshape(1, fo), freq[0].reshape(1, fo), w2, b2.reshape(1, fo),
      freq[1].reshape(1, fo), w3.astype(BF16), b3.reshape(1, n), dec)


def _dft_tables(l):
    n = 2 * l
    n1 = n // DFT_N2
    kk = jnp.arange(n1 // 2, dtype=jnp.int32)
    nn = jnp.arange(n1 // 2, dtype=jnp.int32)
    th = (2.0 * math.pi / (2 * n1)) * (((2 * kk[:, None] + 1) * nn[None, :]) % (2 * n1)).astype(F32)
    eye2 = jnp.eye(2, dtype=F32)
    f1 = jnp.kron(jnp.concatenate([jnp.cos(th), -jnp.sin(th)], 0), eye2).astype(BF16)
    g = jnp.kron(jnp.concatenate([jnp.cos(th).T, -jnp.sin(th).T], 1) * (2.0 / n), eye2).astype(BF16)
    k2 = jnp.arange(DFT_N2, dtype=jnp.int32)
    n2 = jnp.arange(DFT_N2, dtype=jnp.int32)
    ph = (n2[None, None, :] * (k2[None, :, None] * (2 * n1) + 2 * kk[:, None, None] + 1)) % (2 * n)
    ang = (2.0 * math.pi / (2 * n)) * ph.astype(F32)
    c, s = jnp.cos(ang), jnp.sin(ang)
    mf = jnp.concatenate([jnp.concatenate([c, s], 2), jnp.concatenate([-s, c], 2)], 1).astype(BF16)
    mi = jnp.swapaxes(mf, 1, 2)
    return f1, g, mf, mi


def _load_pair_group(ref, lead, g):
    start = pl.multiple_of(g * PAIR_GROUP, PAIR_GROUP)
    words = ref.bitcast(jnp.uint32)[(*lead, slice(None), pl.ds(start, PAIR_GROUP), slice(None))]
    words = jnp.swapaxes(words, 0, 1)
    return [pltpu.bitcast(words[i], BF16) for i in range(PAIR_GROUP)]


def _store_pair_group(ref, lead, g, vals):
    start = pl.multiple_of(g * PAIR_GROUP, PAIR_GROUP)
    words = jnp.stack([pltpu.bitcast(v, jnp.uint32) for v in vals], 0)
    ref.bitcast(jnp.uint32)[(*lead, slice(None), pl.ds(start, PAIR_GROUP), slice(None))] = (
        jnp.swapaxes(words, 0, 1))


def _dft_in_body(f_ref, z_ref, a_ref):
    n1 = a_ref.shape[2]
    cb = a_ref.shape[-1]

    def group(g, carry):
        zcat = jnp.concatenate(_load_pair_group(z_ref, (0,), g), 1)
        r = jnp.dot(f_ref[...], zcat, preferred_element_type=F32).astype(BF16)
        cols = [r[:, i * cb:(i + 1) * cb] for i in range(PAIR_GROUP)]
        _store_pair_group(a_ref, (0, 0), g, [c[:2 * n1] for c in cols])
        _store_pair_group(a_ref, (0, 1), g, [c[2 * n1:] for c in cols])
        return carry

    lax.fori_loop(0, z_ref.shape[2] // (2 * PAIR_GROUP), group, 0, unroll=4)


def _dft_in(f1p, z):
    b, half, n2, c = z.shape
    n1 = half
    cb, rs = V7X_LANES, DFT_STEP_ROWS
    return pl.pallas_call(
        _dft_in_body,
        grid=(b, c // cb, n2 // rs),
        in_specs=[
            pl.BlockSpec(f1p.shape, lambda i, j, s: (0, 0)),
            pl.BlockSpec((1, half, rs, cb), lambda i, j, s: (i, 0, s, j)),
        ],
        out_specs=pl.BlockSpec((1, 2, n1, rs, cb), lambda i, j, s: (i, 0, 0, s, j)),
        out_shape=jax.ShapeDtypeStruct((b, 2, n1, n2, c), BF16),
        compiler_params=_params(("parallel", "parallel", "parallel")),
        name="dft_in",
    )(f1p, z)


def _dft_out_body(g_ref, b_ref, x_ref, z_ref, skip_ref, o_ref):
    cb = o_ref.shape[-1]

    def group(g, carry):
        re, im = _load_pair_group(b_ref, (0, 0), g), _load_pair_group(b_ref, (0, 1), g)
        bcat = jnp.concatenate([jnp.concatenate([r, i], 0) for r, i in zip(re, im)], 1)
        y = jnp.dot(g_ref[...], bcat, preferred_element_type=F32)
        xs, zs = _load_pair_group(x_ref, (0,), g), _load_pair_group(z_ref, (0,), g)
        outs = [(xs[i].astype(F32) * (y[:, i * cb:(i + 1) * cb] + skip_ref[...] * zs[i].astype(F32))
                 ).astype(BF16) for i in range(PAIR_GROUP)]
        _store_pair_group(o_ref, (0,), g, outs)
        return carry

    lax.fori_loop(0, x_ref.shape[2] // (2 * PAIR_GROUP), group, 0)


def _dft_out(gp, bb, x, z, skip):
    b, half, n2, c = z.shape
    n1 = half
    cb, rs = V7X_LANES, DFT_STEP_ROWS
    slab = pl.BlockSpec((1, half, rs, cb), lambda i, j, s: (i, 0, s, j))
    return pl.pallas_call(
        _dft_out_body,
        grid=(b, c // cb, n2 // rs),
        in_specs=[
            pl.BlockSpec(gp.shape, lambda i, j, s: (0, 0)),
            pl.BlockSpec((1, 2, n1, rs, cb), lambda i, j, s: (i, 0, 0, s, j)),
            slab, slab,
            pl.BlockSpec((1, cb), lambda i, j, s: (0, j)),
        ],
        out_specs=slab,
        out_shape=jax.ShapeDtypeStruct(z.shape, BF16),
        compiler_params=_params(("parallel", "parallel", "parallel")),
        name="dft_out",
    )(gp, bb, x, z, skip.astype(F32).reshape(1, c))


def _slab_conv_body(mf_ref, mi_ref, k_ref, a_ref, o_ref):
    nb, _, ks, n2, c = a_ref.shape
    for kk in range(ks):
        kr, ki = k_ref[kk, 0].astype(F32), k_ref[kk, 1].astype(F32)
        for b in range(nb):
            a = a_ref[b, :, kk].reshape(2 * n2, c)
            x = jnp.dot(mf_ref[kk], a, preferred_element_type=F32)
            xr, xi = x[:n2], x[n2:]
            y = jnp.concatenate([xr * kr - xi * ki, xr * ki + xi * kr], 0).astype(BF16)
            out = jnp.dot(mi_ref[kk], y, preferred_element_type=F32)
            o_ref[b, :, kk] = out.astype(BF16).reshape(2, n2, c)


def _slab_conv(mf, mi, kspec, a, order):
    b, _, n1, n2, c = a.shape
    m = 2 * n2
    ks = max(1, SLABS_PER_STEP // b)
    return pl.pallas_call(
        _slab_conv_body,
        grid=(n1 // ks,),
        in_specs=[
            pl.BlockSpec((ks, m, m), lambda k: (k, 0, 0)),
            pl.BlockSpec((ks, m, m), lambda k: (k, 0, 0)),
            pl.BlockSpec((ks, 2, n2, c), lambda k: (k, 0, 0, order)),
            pl.BlockSpec((b, 2, ks, n2, c), lambda k: (0, 0, k, 0, 0)),
        ],
        out_specs=pl.BlockSpec((b, 2, ks, n2, c), lambda k: (0, 0, k, 0, 0)),
        out_shape=jax.ShapeDtypeStruct(a.shape, BF16),
        compiler_params=_params(("parallel",)),
        name="slab_conv",
    )(mf, mi, kspec, a)


def _slab_spec_body(mf_ref, sum_ref, a_ref, k_ref):
    _, ks, n2, c = a_ref.shape
    half = c // 2
    inv = 1.0 / (sum_ref[:, :half] + sum_ref[:, half:] + 1e-6)
    for kk in range(ks):
        x = jnp.dot(mf_ref[kk], a_ref[:, kk].reshape(2 * n2, c), preferred_element_type=F32)
        k_ref[kk, 0] = ((x[:n2, :half] + x[:n2, half:]) * inv).astype(BF16)
        k_ref[kk, 1] = ((x[n2:, :half] - x[n2:, half:]) * inv).astype(BF16)


def _slab_spec(mf, sums, a):
    _, _, n1, n2, c = a.shape
    m = 2 * n2
    ks = SPEC_SLABS_PER_STEP
    return pl.pallas_call(
        _slab_spec_body,
        grid=(n1 // ks,),
        in_specs=[
            pl.BlockSpec((ks, m, m), lambda k: (k, 0, 0)),
            pl.BlockSpec((1, c), lambda k: (0, 0)),
            pl.BlockSpec((None, 2, ks, n2, c), lambda k: (0, 0, k, 0, 0)),
        ],
        out_specs=pl.BlockSpec((ks, 2, n2, c // 2), lambda k: (k, 0, 0, 0)),
        out_shape=jax.ShapeDtypeStruct((n1, 2, n2, c // 2), BF16),
        compiler_params=_params(("parallel",)),
        name="slab_spec",
    )(mf, sums, a)


def _hyena_spectrum(l, tables, w1, b1, freq, w2, b2, w3, b3, decay):
    f1p, _, mf, _ = tables
    taps, sums = _filter_taps(l, w1, b1, freq, w2, b2, w3, b3, decay)
    a = _dft_in(f1p, taps.reshape(1, l // DFT_N2, DFT_N2, taps.shape[1]))
    return _slab_spec(mf, sums, a)


def _hyena(v, x1, x2g, kspec, skip, tables):
    f1p, gp, mf, mi = tables
    b, l, c = v.shape
    slabs = lambda u: u.reshape(b, l // DFT_N2, DFT_N2, c)
    z = slabs(v)
    for order, xg in enumerate((slabs(x1), slabs(x2g))):
        bb = _slab_conv(mf, mi, kspec, _dft_in(f1p, z), order)
        z = _dft_out(gp, bb, xg, z, skip[order])
    return z.reshape(b, l, c)


def _head_masks(rows, heads):
    lane = lax.broadcasted_iota(jnp.int32, (rows, heads * HEAD_DIM), 1)
    return [(lane >= h * HEAD_DIM) & (lane < (h + 1) * HEAD_DIM) for h in range(heads)]


def _stack_heads(q, masks):
    zero = jnp.zeros_like(q)
    return jnp.concatenate([jnp.where(m, q, zero) for m in masks], 0)


def _unstack_heads(res, masks, rows):
    out = jnp.where(masks[0], res[:rows], 0.0)
    for h in range(1, len(masks)):
        out = out + jnp.where(masks[h], res[h * rows:(h + 1) * rows], 0.0)
    return out


def _na_bias_table(rpb):
    heads = rpb.shape[0]
    c = jnp.arange(GRID_W)
    col_start = jnp.clip(c - NA_KC // 2, 0, GRID_W - NA_KC)
    col_ok = (c[None, :] >= col_start[:, None]) & (c[None, :] < col_start[:, None] + NA_KC)
    dc = jnp.clip(c[None, :] - c[:, None], -(NA_KC - 1), NA_KC - 1) + NA_KC - 1
    onehot = (dc[:, :, None] == jnp.arange(2 * NA_KC - 1)[None, None, :]).astype(F32)
    tcol = jnp.einsum("hrj,qkj->hrqk", rpb.astype(F32), onehot, precision=lax.Precision.HIGHEST)
    per_idx = [tcol[:, NA_KR - 1 - idx:2 * NA_KR - 1 - idx] for idx in range(NA_KR)]
    bias = jnp.transpose(jnp.stack(per_idx, 0), (0, 1, 3, 2, 4))
    bias = jnp.where(col_ok[None, None, :, None, :], bias * LOG2_E, NEG_INF)
    return bias.reshape(NA_KR, heads * GRID_W, NA_KR * GRID_W)


def _na_body(q_ref, k_ref, v_ref, z_ref, bias_ref, o_ref, *, rows, rb, heads):
    blk = pl.program_id(1)
    masks = _head_masks(GRID_W, heads)
    span = NA_KR * GRID_W

    def one_row(rr, carry):
        r = blk * rb + rr
        start = jnp.clip(r - NA_KR // 2, 0, rows - NA_KR)
        idx = r - start
        koff = pl.multiple_of(start * GRID_W, GRID_W)
        qoff = pl.multiple_of(rr * GRID_W, GRID_W)
        q = q_ref[pl.ds(qoff, GRID_W), :]
        kk = k_ref[pl.ds(koff, span), :]
        vv = v_ref[pl.ds(koff, span), :]
        s = lax.dot_general(_stack_heads(q, masks), kk, (((1,), (1,)), ((), ())),
                            preferred_element_type=F32)
        s = s + bias_ref[idx]
        m = jnp.max(s, -1, keepdims=True)
        e = jnp.exp2(s - m)
        p = (e / jnp.sum(e, -1, keepdims=True)).astype(BF16)
        o = _unstack_heads(jnp.dot(p, vv, preferred_element_type=F32), masks, GRID_W)
        z = z_ref[pl.ds(qoff, GRID_W), :].astype(F32)
        o_ref[pl.ds(qoff, GRID_W), :] = (o * _silu(z)).astype(BF16)
        return carry

    lax.fori_loop(0, rb, one_row, 0, unroll=NA_ROW_UNROLL)


def _na(proj_b, bias):
    b, l, n = proj_b.shape
    db = n // 4
    heads = db // HEAD_DIM
    rows = l // GRID_W
    rb = NA_ROWS_PER_STEP
    t = rb * GRID_W
    return pl.pallas_call(
        functools.partial(_na_body, rows=rows, rb=rb, heads=heads),
        grid=(b, rows // rb),
        in_specs=[
            pl.BlockSpec((None, t, db), lambda i, j: (i, j, 0)),
            pl.BlockSpec((None, l, db), lambda i, j: (i, 0, 1)),
            pl.BlockSpec((None, l, db), lambda i, j: (i, 0, 2)),
            pl.BlockSpec((None, t, db), lambda i, j: (i, j, 3)),
            pl.BlockSpec(bias.shape, lambda i, j: (0, 0, 0)),
        ],
        out_specs=pl.BlockSpec((None, t, db), lambda i, j: (i, j, 0)),
        out_shape=jax.ShapeDtypeStruct((b, l, db), BF16),
        compiler_params=_params(("parallel", "parallel")),
        name="na",
    )(proj_b, proj_b, proj_b, proj_b, bias)


def _attend(q, kk, vv, valid, masks):
    rows = q.shape[0]
    s = lax.dot_general(_stack_heads(q, masks), kk, (((1,), (1,)), ((), ())), preferred_element_type=F32)
    s = jnp.where(valid, s, NEG_INF)
    m = jnp.max(s, -1, keepdims=True)
    e = jnp.exp2(s - m)
    l = jnp.sum(e, -1, keepdims=True)
    p = (e / l).astype(BF16)
    o = _unstack_heads(jnp.dot(p, vv, preferred_element_type=F32), masks, rows)
    lse = _unstack_heads(jnp.broadcast_to(m + jnp.log2(l), (s.shape[0], q.shape[1])), masks, rows)
    return o, lse


def _split_pattern(dilation, n, nchunks, q_ref, kbuf, vbuf, o_split, l_split, masks, heads):
    blk = DIL_BLK
    dc = heads * HEAD_DIM
    phases = DIL_SPLIT // dilation
    mq = blk // phases
    row = lax.broadcasted_iota(jnp.int32, (heads * blk, 3 * blk), 0) % blk
    col = lax.broadcasted_iota(jnp.int32, (heads * blk, 3 * blk), 1)
    qa, qm = row // mq, row % mq
    ka = sum((col >= a * 3 * mq).astype(jnp.int32) for a in range(1, phases)) if phases > 1 else 0
    km = col - ka * (3 * mq)
    band = jnp.abs(phases * (km - mq - qm) + (ka - qa)) <= blk
    for m0 in range(0, blk, mq):
        gm = km + (n * blk + m0 - mq)
        valid = band & (gm >= 0) & (gm < nchunks * blk)
        for j in range(dilation):
            lanes = [slice((j + dilation * a) * dc, (j + dilation * a + 1) * dc) for a in range(phases)]
            q = jnp.concatenate([q_ref[m0:m0 + mq, ln] for ln in lanes], 0)
            krows = slice(blk + m0 - mq, blk + m0 + 2 * mq)
            kk = jnp.concatenate([kbuf[krows, ln] for ln in lanes], 0)
            vv = jnp.concatenate([vbuf[krows, ln] for ln in lanes], 0)
            o, lse = _attend(q, kk, vv, valid, masks)
            for a in range(phases):
                o_split[j + dilation * a, m0:m0 + mq, :] = o[a * mq:(a + 1) * mq]
                l_split[j + dilation * a, m0:m0 + mq, :] = lse[a * mq:(a + 1) * mq]


def _dil_body(qn_ref, knp_ref, kn_ref, knn_ref, vnp_ref, vn_ref, vnn_ref,
              qs_ref, ksp_ref, ks_ref, ksn_ref, vsp_ref, vs_ref, vsn_ref, cz_ref, y_ref,
              kbn, vbn, kbs, vbs, o_split, l_split, o_all, l_all, *, heads, nchunks):
    n = pl.program_id(1)
    blk = DIL_BLK
    chunk = qn_ref.shape[0]
    per = chunk // blk
    masks = _head_masks(blk, heads)
    for buf, prev, cur, nxt, halo in ((kbn, knp_ref, kn_ref, knn_ref, blk), (vbn, vnp_ref, vn_ref, vnn_ref, blk),
                                      (kbs, ksp_ref, ks_ref, ksn_ref, blk), (vbs, vsp_ref, vs_ref, vsn_ref, blk)):
        main = cur.shape[0]
        buf[0:halo] = prev[...]
        buf[halo:halo + main] = cur[...]
        buf[halo + main:] = nxt[...]

    qi = lax.broadcasted_iota(jnp.int32, (heads * blk, 3 * blk), 0) % blk
    ki = lax.broadcasted_iota(jnp.int32, (heads * blk, 3 * blk), 1)
    band = jnp.abs(ki - blk - qi) <= blk

    def token_block(i, carry):
        g = n * per + i
        off = pl.multiple_of(i * blk, blk)
        valid = band & ((ki >= blk) | (g > 0)) & ((ki < 2 * blk) | (g < nchunks * per - 1))
        o, lse = _attend(qn_ref[pl.ds(off, blk), :], kbn[pl.ds(off, 3 * blk), :], vbn[pl.ds(off, 3 * blk), :],
                         valid, masks)
        o_all[0, pl.ds(off, blk), :] = o
        l_all[0, pl.ds(off, blk), :] = lse
        return carry

    lax.fori_loop(0, per, token_block, 0, unroll=True)

    for g, (_, dilation) in enumerate(DIL_PATTERNS):
        if dilation == 1:
            continue
        _split_pattern(dilation, n, nchunks, qs_ref, kbs, vbs, o_split, l_split, masks, heads)
        o_all[g] = jnp.swapaxes(o_split[...], 0, 1).reshape(chunk, heads * HEAD_DIM)
        l_all[g] = jnp.swapaxes(l_split[...], 0, 1).reshape(chunk, heads * HEAD_DIM)

    ls = [l_all[g] for g in range(len(DIL_PATTERNS))]
    m = functools.reduce(jnp.maximum, ls)
    es = [jnp.exp2(l - m) for l in ls]
    den = functools.reduce(lambda a, b: a + b, es)
    o = functools.reduce(lambda a, b: a + b, [(e / den) * o_all[g] for g, e in enumerate(es)])
    y_ref[...] = (o * cz_ref[...].astype(F32)).astype(BF16)


def _dilated_mixture(qn, kn, vn, qs, ks, vs, cz):
    b, l, dc = qn.shape
    heads = dc // HEAD_DIM
    blk = DIL_BLK
    assert DIL_PATTERNS[0][1] == 1 and all(w == 2 * blk * d and DIL_SPLIT % d == 0 for w, d in DIL_PATTERNS)
    chunk = DIL_SPLIT * blk
    nchunks = l // chunk
    per = chunk // blk
    nat = pl.BlockSpec((None, chunk, dc), lambda i, n: (i, n, 0))
    nat_prev = pl.BlockSpec((None, blk, dc), lambda i, n: (i, jnp.maximum(n * per - 1, 0), 0))
    nat_next = pl.BlockSpec((None, blk, dc), lambda i, n: (i, jnp.minimum((n + 1) * per, nchunks * per - 1), 0))
    spl = pl.BlockSpec((None, blk, DIL_SPLIT * dc), lambda i, n: (i, n, 0))
    spl_prev = pl.BlockSpec((None, blk, DIL_SPLIT * dc), lambda i, n: (i, jnp.maximum(n - 1, 0), 0))
    spl_next = pl.BlockSpec((None, blk, DIL_SPLIT * dc), lambda i, n: (i, jnp.minimum(n + 1, nchunks - 1), 0))
    return pl.pallas_call(
        functools.partial(_dil_body, heads=heads, nchunks=nchunks),
        grid=(b, nchunks),
        in_specs=[nat, nat_prev, nat, nat_next, nat_prev, nat, nat_next,
                  spl, spl_prev, spl, spl_next, spl_prev, spl, spl_next, nat],
        out_specs=nat,
        out_shape=jax.ShapeDtypeStruct((b, l, dc), BF16),
        scratch_shapes=[pltpu.VMEM((chunk + 2 * blk, dc), BF16), pltpu.VMEM((chunk + 2 * blk, dc), BF16),
                        pltpu.VMEM((3 * blk, DIL_SPLIT * dc), BF16), pltpu.VMEM((3 * blk, DIL_SPLIT * dc), BF16),
                        pltpu.VMEM((DIL_SPLIT, blk, dc), F32), pltpu.VMEM((DIL_SPLIT, blk, dc), F32),
                        pltpu.VMEM((len(DIL_PATTERNS), chunk, dc), F32),
                        pltpu.VMEM((len(DIL_PATTERNS), chunk, dc), F32)],
        compiler_params=_params(("parallel", "parallel")),
        name="dilated",
    )(qn, kn, kn, kn, vn, vn, vn, qs, ks, ks, ks, vs, vs, vs, cz)


def _tail_body(x_ref, ya_ref, yb_ref, yc_ref, g_ref, gate_ref, wa_ref, wb_ref, wc_ref, wo_ref, lng_ref, lnb_ref,
               out_ref, *, alpha):
    d = x_ref.shape[-1]
    pa = jnp.dot(ya_ref[...], wa_ref[...], preferred_element_type=F32)
    pb = jnp.dot(yb_ref[...], wb_ref[...], preferred_element_type=F32)
    pc = jnp.dot(yc_ref[...], wc_ref[...], preferred_element_type=F32)
    g = g_ref[...].astype(F32)
    merged = g[:, :d] * pa + g[:, d:2 * d] * pb + g[:, 2 * d:] * pc
    sub = jnp.dot(merged.astype(BF16), wo_ref[...], preferred_element_type=F32) * gate_ref[...]
    res = alpha * x_ref[...] + sub
    out_ref[...] = _layernorm(res) * lng_ref[...] + lnb_ref[...]


def _tail(x, ya, yb, yc, g_all, gate, wa, wb, wc, wo, ln_g, ln_b, alpha):
    b, l, d = x.shape
    t = TOKEN_TILE
    tok = lambda w: pl.BlockSpec((None, t, w), lambda i, j: (i, j, 0))
    const = lambda a: pl.BlockSpec(a.shape, lambda i, j: (0,) * a.ndim)
    ln_g, ln_b = ln_g.reshape(1, d), ln_b.reshape(1, d)
    return pl.pallas_call(
        functools.partial(_tail_body, alpha=alpha),
        grid=(b, l // t),
        in_specs=[tok(d), tok(ya.shape[-1]), tok(yb.shape[-1]), tok(yc.shape[-1]), tok(3 * d),
                  pl.BlockSpec((None, 1, d), lambda i, j: (i, 0, 0)),
                  const(wa), const(wb), const(wc), const(wo), const(ln_g), const(ln_b)],
        out_specs=tok(d),
        out_shape=jax.ShapeDtypeStruct((b, l, d), F32),
        compiler_params=_params(("parallel", "parallel")),
        name="tail",
    )(x, ya, yb, yc, g_all, gate.reshape(b, 1, d), wa, wb, wc, wo, ln_g, ln_b)


def _rope_tables(l, heads):
    half = HEAD_DIM // 2
    inv = ROPE_THETA ** (-jnp.arange(half, dtype=F32) / half)
    ang = jnp.arange(l, dtype=F32)[:, None] * inv[None, :]
    cos, sin = jnp.cos(ang), jnp.sin(ang)
    return (jnp.tile(jnp.concatenate([cos, cos], -1), (1, heads)),
            jnp.tile(jnp.concatenate([-sin, sin], -1), (1, heads)))


def _layer(x, ada, lw, consts, alpha):
    d = x.shape[-1]
    shift, scale, gate = ada[:, :d], ada[:, d:2 * d], ada[:, 2 * d:]
    h = _ln_mod(x, scale, shift)
    w_in, b_in = lw["w_in"], lw["b_in"]
    proj_a = _proj(h, w_in[:, :2 * d], b_in[:2 * d])
    q_cols = jnp.arange(d) < d // 4
    proj_b = _proj(h, w_in[:, 2 * d:3 * d], b_in[2 * d:3 * d], col_scale=jnp.where(q_cols, QK_SCALE_LOG2, 1.0))
    cq, ck, cv, cz, cqs, cks, cvs = _proj_rope(h, w_in[:, 3 * d:4 * d], b_in[3 * d:4 * d], *consts["rope"])
    gates = _proj(h, w_in[:, 4 * d:], b_in[4 * d:], sigmoid=True)

    v, x1, x2g = _hy_pre(proj_a, lw["conv_w"], lw["conv_b"])
    ya = _hyena(v, x1, x2g, consts["kspec"], lw["skip"], consts["dft"])
    yb = _na(proj_b, lw["na_bias"])
    yc = _dilated_mixture(cq, ck, cv, cqs, cks, cvs, cz)
    return _tail(x, ya, yb, yc, gates, gate, lw["wa"], lw["wb"], lw["wc"], lw["wo"],
                 lw["ln_g"], lw["ln_b"], alpha)


def kernel(x_prompt, x_sample, c_prompt, c_sample, w_ada, b_ada, w_in, b_in, hy_conv_w, hy_conv_b, hy_w1, hy_b1, hy_freq, hy_w2, hy_b2, hy_w3, hy_b3, hy_decay, hy_skip, na_rpb, w_branch_a, w_branch_b, w_branch_c, w_out, ln_g, ln_b):
    depth, d, _ = w_in.shape
    heads_c = (d // 4) // HEAD_DIM
    alpha = (2 * depth) ** 0.25
    groups = [(x_prompt, c_prompt), (x_sample, c_sample)]

    nb_p = c_prompt.shape[0]
    c_all = jnp.concatenate([c_prompt, c_sample], 0)
    pad_rows = -c_all.shape[0] % 8
    ada_all = _ada(jnp.pad(c_all, ((0, pad_rows), (0, 0))), w_ada, b_ada)
    adas = [ada_all[:, :nb_p], ada_all[:, nb_p:nb_p + c_sample.shape[0]]]

    shared = {}
    for x, _ in groups:
        l = x.shape[1]
        if l not in shared:
            shared[l] = {"dft": _dft_tables(l), "rope": _rope_tables(l, heads_c)}

    ys = [x for x, _ in groups]
    for layer in range(depth):
        lw = {
            "w_in": w_in[layer].astype(BF16), "b_in": b_in[layer],
            "conv_w": hy_conv_w[layer], "conv_b": hy_conv_b[layer], "skip": hy_skip[layer],
            "na_bias": _na_bias_table(na_rpb[layer]),
            "wa": w_branch_a[layer].astype(BF16), "wb": w_branch_b[layer].astype(BF16),
            "wc": w_branch_c[layer].astype(BF16), "wo": w_out[layer].astype(BF16),
            "ln_g": ln_g[layer], "ln_b": ln_b[layer],
        }
        kspecs = {}
        for gi in range(len(groups)):
            l = ys[gi].shape[1]
            if l not in kspecs:
                kspecs[l] = _hyena_spectrum(l, shared[l]["dft"], hy_w1[layer], hy_b1[layer], hy_freq[layer],
                                            hy_w2[layer], hy_b2[layer], hy_w3[layer], hy_b3[layer],
                                            hy_decay[layer])
            consts = dict(shared[l], kspec=kspecs[l])
            ys[gi] = _layer(ys[gi], adas[gi][layer], lw, consts, alpha)
    return tuple(ys)
```

```python
import functools
import math

import jax
import jax.numpy as jnp
from jax import lax
from jax.experimental import pallas as pl
from jax.experimental.pallas import tpu as pltpu

F32 = jnp.float32
BF16 = jnp.bfloat16

GRID_W = 64
HEAD_DIM = 64
HYENA_BANDS = 16
HYENA_EMB = 2 * HYENA_BANDS + 1
NA_KR = 8
NA_KC = 16
DIL_PATTERNS = ((128, 1), (512, 4), (2048, 16))
DIL_BLK = 64
DIL_SPLIT = 16
NA_ROW_UNROLL = 32
ROPE_THETA = 10000.0
LN_EPS = 1e-5
NEG_INF = -1e30
LOG2_E = math.log2(math.e)
QK_SCALE_LOG2 = HEAD_DIM ** -0.5 * LOG2_E

V7X_LANES = 128
V7X_SUBLANES = 8
V7X_VMEM_LIMIT_BYTES = 56 * 1024 * 1024

DFT_N2 = V7X_LANES
PAIR_GROUP = V7X_SUBLANES
DFT_STEP_ROWS = 128
SLABS_PER_STEP = 16
FEAT_PAD = V7X_LANES
BF16_TILE_ROWS = 2 * V7X_SUBLANES

TOKEN_TILE = 1024
WIDE_TOKEN_TILE = 2048
PROJ_COL_TILE = 1024
ROPE_TOKEN_TILE = 1024
FILTER_ROW_TILE = 1024
NA_ROWS_PER_STEP = 32
SPEC_SLABS_PER_STEP = 2
HY_SHIFT_ROWS = 256


def _params(sem):
    return pltpu.CompilerParams(dimension_semantics=sem, vmem_limit_bytes=V7X_VMEM_LIMIT_BYTES)


def _sigmoid(x):
    return 1.0 / (1.0 + jnp.exp(-x))


def _silu(x):
    return x * _sigmoid(x)


def _ada_body(c_ref, w_ref, b_ref, o_ref):
    s = _silu(c_ref[...])
    o_ref[...] = jnp.dot(s, w_ref[...], preferred_element_type=F32,
                         precision=lax.Precision.HIGHEST) + b_ref[...]


def _ada(c_all, w_ada, b_ada):
    depth, d, n = w_ada.shape
    rows = c_all.shape[0]
    tn = PROJ_COL_TILE
    return pl.pallas_call(
        _ada_body,
        grid=(depth, n // tn),
        in_specs=[
            pl.BlockSpec((rows, d), lambda l, j: (0, 0)),
            pl.BlockSpec((None, d, tn), lambda l, j: (l, 0, j)),
            pl.BlockSpec((None, 1, tn), lambda l, j: (l, 0, j)),
        ],
        out_specs=pl.BlockSpec((None, rows, tn), lambda l, j: (l, 0, j)),
        out_shape=jax.ShapeDtypeStruct((depth, rows, n), F32),
        compiler_params=_params(("parallel", "parallel")),
        name="ada",
    )(c_all, w_ada, b_ada.reshape(depth, 1, n))


def _layernorm(x):
    mu = jnp.mean(x, -1, keepdims=True)
    xc = x - mu
    var = jnp.mean(xc * xc, -1, keepdims=True)
    return xc * lax.rsqrt(var + LN_EPS)


def _ln_mod_body(x_ref, sc_ref, sh_ref, o_ref):
    h = _layernorm(x_ref[...]) * (1.0 + sc_ref[...]) + sh_ref[...]
    o_ref[...] = h.astype(BF16)


def _ln_mod(x, scale, shift):
    b, l, d = x.shape
    t = WIDE_TOKEN_TILE
    return pl.pallas_call(
        _ln_mod_body,
        grid=(b, l // t),
        in_specs=[
            pl.BlockSpec((None, t, d), lambda i, j: (i, j, 0)),
            pl.BlockSpec((None, 1, d), lambda i, j: (i, 0, 0)),
            pl.BlockSpec((None, 1, d), lambda i, j: (i, 0, 0)),
        ],
        out_specs=pl.BlockSpec((None, t, d), lambda i, j: (i, j, 0)),
        out_shape=jax.ShapeDtypeStruct((b, l, d), BF16),
        compiler_params=_params(("parallel", "parallel")),
        name="ln_mod",
    )(x, scale.reshape(b, 1, d), shift.reshape(b, 1, d))


def _proj_body(h_ref, w_ref, b_ref, *rest, sigmoid, scaled):
    o_ref = rest[-1]
    acc = jnp.dot(h_ref[...], w_ref[...], preferred_element_type=F32) + b_ref[...]
    if scaled:
        acc = acc * rest[0][...]
    if sigmoid:
        acc = 0.5 * jnp.tanh(0.5 * acc) + 0.5
    o_ref[...] = acc.astype(o_ref.dtype)


def _proj(h, w, bias, col_scale=None, sigmoid=False):
    b, l, d = h.shape
    n = w.shape[1]
    t, tn = TOKEN_TILE, PROJ_COL_TILE
    row = pl.BlockSpec((1, tn), lambda j, i, k: (0, j))
    scale_args = [] if col_scale is None else [col_scale.reshape(1, n)]
    return pl.pallas_call(
        functools.partial(_proj_body, sigmoid=sigmoid, scaled=col_scale is not None),
        grid=(n // tn, b, l // t),
        in_specs=[
            pl.BlockSpec((None, t, d), lambda j, i, k: (i, k, 0)),
            pl.BlockSpec((d, tn), lambda j, i, k: (0, j)),
            row,
        ] + [row] * len(scale_args),
        out_specs=pl.BlockSpec((None, t, tn), lambda j, i, k: (i, k, j)),
        out_shape=jax.ShapeDtypeStruct((b, l, n), BF16),
        compiler_params=_params(("parallel", "parallel", "parallel")),
        name="proj",
    )(h, w, bias.reshape(1, n), *scale_args)


def _rope_lanes(x, cos, sin_signed):
    outs = []
    lane = lax.broadcasted_iota(jnp.int32, (x.shape[0], V7X_LANES), 1)
    first_half = (lane % HEAD_DIM) < (HEAD_DIM // 2)
    for c0 in range(0, x.shape[1], V7X_LANES):
        xc = x[:, c0:c0 + V7X_LANES]
        partner = jnp.where(first_half,
                            pltpu.roll(xc, V7X_LANES - HEAD_DIM // 2, 1),
                            pltpu.roll(xc, HEAD_DIM // 2, 1))
        outs.append(xc * cos[:, c0:c0 + V7X_LANES] + partner * sin_signed[:, c0:c0 + V7X_LANES])
    return jnp.concatenate(outs, 1)


def _store_split(ref, x):
    t, dc = x.shape
    parts = jnp.swapaxes(x.reshape(t // DIL_SPLIT, DIL_SPLIT, dc), 0, 1)
    for r in range(DIL_SPLIT):
        ref[:, r * dc:(r + 1) * dc] = parts[r].astype(BF16)


def _proj_rope_body(h_ref, w_ref, b_ref, cos_ref, sin_ref, q_ref, k_ref, v_ref, z_ref,
                    qs_ref, ks_ref, vs_ref):
    dc = q_ref.shape[-1]
    acc = jnp.dot(h_ref[...], w_ref[...], preferred_element_type=F32) + b_ref[...]
    cos, sin = cos_ref[...], sin_ref[...]
    q = _rope_lanes(acc[:, :dc], cos, sin) * QK_SCALE_LOG2
    k = _rope_lanes(acc[:, dc:2 * dc], cos, sin)
    v = acc[:, 2 * dc:3 * dc]
    for ref, split_ref, val in ((q_ref, qs_ref, q), (k_ref, ks_ref, k), (v_ref, vs_ref, v)):
        ref[...] = val.astype(BF16)
        _store_split(split_ref, val)
    z_ref[...] = _silu(acc[:, 3 * dc:]).astype(BF16)


def _proj_rope(h, w, bias, cos_t, sin_t):
    b, l, d = h.shape
    n = w.shape[1]
    dc = n // 4
    t = ROPE_TOKEN_TILE
    tok = pl.BlockSpec((None, t, dc), lambda i, k: (i, k, 0))
    shp = jax.ShapeDtypeStruct((b, l, dc), BF16)
    spl = pl.BlockSpec((None, t // DIL_SPLIT, DIL_SPLIT * dc), lambda i, k: (i, k, 0))
    spl_shp = jax.ShapeDtypeStruct((b, l // DIL_SPLIT, DIL_SPLIT * dc), BF16)
    return pl.pallas_call(
        _proj_rope_body,
        grid=(b, l // t),
        in_specs=[
            pl.BlockSpec((None, t, d), lambda i, k: (i, k, 0)),
            pl.BlockSpec((d, n), lambda i, k: (0, 0)),
            pl.BlockSpec((1, n), lambda i, k: (0, 0)),
            pl.BlockSpec((t, dc), lambda i, k: (k, 0)),
            pl.BlockSpec((t, dc), lambda i, k: (k, 0)),
        ],
        out_specs=[tok, tok, tok, tok, spl, spl, spl],
        out_shape=[shp, shp, shp, shp, spl_shp, spl_shp, spl_shp],
        compiler_params=_params(("parallel", "parallel")),
        name="proj_rope",
    )(h, w, bias.reshape(1, n), cos_t, sin_t)


def _hy_pre_body(main_ref, prev_ref, next_ref, shift_ref, cw_ref, cb_ref, v_ref, x1_ref, x2_ref, *, da):
    i = pl.program_id(1)
    last = pl.num_programs(1) - 1
    t = main_ref.shape[0]
    halo = prev_ref.shape[0]
    blk = shift_ref.shape[1]
    nc = 3 * da
    outs = (v_ref, x1_ref, x2_ref)
    edge = lax.broadcasted_iota(jnp.int32, (V7X_SUBLANES, nc), 0)
    for r0 in range(0, t, blk):
        xb = main_ref[r0:r0 + blk, :nc]
        sh = jnp.dot(shift_ref[...], xb, preferred_element_type=F32)
        if r0 == 0:
            prev_row = jnp.where(i > 0, prev_ref[:, :nc].astype(F32)[halo - 1:halo], 0.0)
        else:
            prev_row = main_ref[r0 - halo:r0, :nc].astype(F32)[halo - 1:halo]
        if r0 + blk == t:
            next_row = jnp.where(i < last, next_ref[:, :nc].astype(F32)[0:1], 0.0)
        else:
            next_row = main_ref[r0 + blk:r0 + blk + halo, :nc].astype(F32)[0:1]
        up, dn = sh[:blk], sh[blk:]
        up = jnp.concatenate([up[:V7X_SUBLANES] + jnp.where(edge == 0, prev_row, 0.0), up[V7X_SUBLANES:]], 0)
        dn = jnp.concatenate([dn[:blk - V7X_SUBLANES],
                              dn[blk - V7X_SUBLANES:] + jnp.where(edge == V7X_SUBLANES - 1, next_row, 0.0)], 0)
        uc = up * cw_ref[0:1, :] + xb.astype(F32) * cw_ref[1:2, :] + dn * cw_ref[2:3, :] + cb_ref[...]
        for part in range(3):
            val = uc[:, part * da:(part + 1) * da]
            if part == 2:
                val = val * _silu(main_ref[r0:r0 + blk, nc:].astype(F32))
            outs[part][r0:r0 + blk, :] = val.astype(BF16)


def _hy_pre(proj_a, conv_w, conv_b):
    b, l, n = proj_a.shape
    da = n // 4
    t, halo = WIDE_TOKEN_TILE, BF16_TILE_ROWS
    nh = t // halo
    blk = HY_SHIFT_ROWS
    shift = jnp.concatenate([jnp.eye(blk, k=-1, dtype=BF16), jnp.eye(blk, k=1, dtype=BF16)], 0)
    tok = pl.BlockSpec((None, t, da), lambda i, j: (i, j, 0))
    shp = jax.ShapeDtypeStruct((b, l, da), BF16)
    return pl.pallas_call(
        functools.partial(_hy_pre_body, da=da),
        grid=(b, l // t),
        in_specs=[
            pl.BlockSpec((None, t, n), lambda i, j: (i, j, 0)),
            pl.BlockSpec((None, halo, n), lambda i, j: (i, jnp.maximum(j * nh - 1, 0), 0)),
            pl.BlockSpec((None, halo, n), lambda i, j: (i, jnp.minimum((j + 1) * nh, l // halo - 1), 0)),
            pl.BlockSpec((2 * blk, blk), lambda i, j: (0, 0)),
            pl.BlockSpec((3, 3 * da), lambda i, j: (0, 0)),
            pl.BlockSpec((1, 3 * da), lambda i, j: (0, 0)),
        ],
        out_specs=[tok, tok, tok],
        out_shape=[shp, shp, shp],
        compiler_params=_params(("parallel", "parallel")),
        name="hy_pre",
    )(proj_a, proj_a, proj_a, shift, conv_w, conv_b.reshape(1, 3 * da))


def _filter_body(feat_ref, w1_ref, b1_ref, f0_ref, w2_ref, b2_ref, f1_ref, w3_ref, b3_ref, dec_ref,
                 hf_ref, sum_ref):
    i = pl.program_id(0)
    hp = lax.Precision.HIGHEST
    feat = feat_ref[...]
    t = feat[:, 0:1]
    h = jnp.sin(f0_ref[...] * (jnp.dot(feat, w1_ref[...], preferred_element_type=F32, precision=hp)
                               + b1_ref[...]))
    h = jnp.sin(f1_ref[...] * (jnp.dot(h, w2_ref[...], preferred_element_type=F32, precision=hp)
                               + b2_ref[...]))
    h = jnp.dot(h.astype(BF16), w3_ref[...], preferred_element_type=F32) + b3_ref[...]
    h = h * jnp.exp(-t * jnp.abs(dec_ref[...]))
    hf_ref[...] = h.astype(BF16)
    abs_sum = jnp.sum(jnp.abs(h), 0, keepdims=True)

    @pl.when(i > 0)
    def _():
        sum_ref[...] += abs_sum

    @pl.when(i == 0)
    def _():
        head = h[0:BF16_TILE_ROWS]
        rows = lax.broadcasted_iota(jnp.int32, head.shape, 0)
        cols = lax.broadcasted_iota(jnp.int32, head.shape, 1)
        drop = (rows == 0) & (cols >= h.shape[1] // 2)
        hf_ref[0:BF16_TILE_ROWS, :] = jnp.where(drop, 0.0, head).astype(BF16)
        sum_ref[...] = abs_sum - jnp.sum(jnp.where(drop, jnp.abs(head), 0.0), 0, keepdims=True)


def _filter_taps(l, w1, b1, freq, w2, b2, w3, b3, decay):
    fo = w1.shape[1]
    n = w3.shape[1]
    da = decay.shape[0]
    t = jnp.arange(l, dtype=F32) / l
    bands = jnp.arange(1, HYENA_BANDS + 1, dtype=F32)
    ang = 2.0 * math.pi * t[:, None] * bands[None, :]
    feat = jnp.concatenate([t[:, None], jnp.cos(ang), jnp.sin(ang)], -1)
    feat = jnp.pad(feat, ((0, 0), (0, FEAT_PAD - HYENA_EMB)))
    w1p = jnp.pad(w1, ((0, FEAT_PAD - HYENA_EMB), (0, 0)))
    dec = jnp.tile(decay, n // da).reshape(1, n)
    tt = FILTER_ROW_TILE
    const = lambda shape: pl.BlockSpec(shape, lambda i: (0,) * len(shape))
    return pl.pallas_call(
        _filter_body,
        grid=(l // tt,),
        in_specs=[
            pl.BlockSpec((tt, FEAT_PAD), lambda i: (i, 0)),
            const((FEAT_PAD, fo)), const((1, fo)), const((1, fo)),
            const((fo, fo)), const((1, fo)), const((1, fo)),
            const((fo, n)), const((1, n)), const((1, n)),
        ],
        out_specs=[pl.BlockSpec((tt, n), lambda i: (i, 0)), const((1, n))],
        out_shape=[jax.ShapeDtypeStruct((l, n), BF16), jax.ShapeDtypeStruct((1, n), F32)],
        compiler_params=_params(("arbitrary",)),
        name="filter_taps",
    )(feat, w1p, b1.reshape(1, fo), freq[0].reshape(1, fo), w2, b2.reshape(1, fo),
      freq[1].reshape(1, fo), w3.astype(BF16), b3.reshape(1, n), dec)


def _dft_tables(l):
    n = 2 * l
    n1 = n // DFT_N2
    kk = jnp.arange(n1 // 2, dtype=jnp.int32)
    nn = jnp.arange(n1 // 2, dtype=jnp.int32)
    th = (2.0 * math.pi / (2 * n1)) * (((2 * kk[:, None] + 1) * nn[None, :]) % (2 * n1)).astype(F32)
    eye2 = jnp.eye(2, dtype=F32)
    f1 = jnp.kron(jnp.concatenate([jnp.cos(th), -jnp.sin(th)], 0), eye2).astype(BF16)
    g = jnp.kron(jnp.concatenate([jnp.cos(th).T, -jnp.sin(th).T], 1) * (2.0 / n), eye2).astype(BF16)
    k2 = jnp.arange(DFT_N2, dtype=jnp.int32)
    n2 = jnp.arange(DFT_N2, dtype=jnp.int32)
    ph = (n2[None, None, :] * (k2[None, :, None] * (2 * n1) + 2 * kk[:, None, None] + 1)) % (2 * n)
    ang = (2.0 * math.pi / (2 * n)) * ph.astype(F32)
    c, s = jnp.cos(ang), jnp.sin(ang)
    mf = jnp.concatenate([jnp.concatenate([c, s], 2), jnp.concatenate([-s, c], 2)], 1).astype(BF16)
    mi = jnp.swapaxes(mf, 1, 2)
    return f1, g, mf, mi


def _load_pair_group(ref, lead, g):
    start = pl.multiple_of(g * PAIR_GROUP, PAIR_GROUP)
    words = ref.bitcast(jnp.uint32)[(*lead, slice(None), pl.ds(start, PAIR_GROUP), slice(None))]
    words = jnp.swapaxes(words, 0, 1)
    return [pltpu.bitcast(words[i], BF16) for i in range(PAIR_GROUP)]


def _store_pair_group(ref, lead, g, vals):
    start = pl.multiple_of(g * PAIR_GROUP, PAIR_GROUP)
    words = jnp.stack([pltpu.bitcast(v, jnp.uint32) for v in vals], 0)
    ref.bitcast(jnp.uint32)[(*lead, slice(None), pl.ds(start, PAIR_GROUP), slice(None))] = (
        jnp.swapaxes(words, 0, 1))


def _dft_in_body(f_ref, z_ref, a_ref):
    n1 = a_ref.shape[2]
    cb = a_ref.shape[-1]

    def group(g, carry):
        zcat = jnp.concatenate(_load_pair_group(z_ref, (0,), g), 1)
        r = jnp.dot(f_ref[...], zcat, preferred_element_type=F32).astype(BF16)
        cols = [r[:, i * cb:(i + 1) * cb] for i in range(PAIR_GROUP)]
        _store_pair_group(a_ref, (0, 0), g, [c[:2 * n1] for c in cols])
        _store_pair_group(a_ref, (0, 1), g, [c[2 * n1:] for c in cols])
        return carry

    lax.fori_loop(0, z_ref.shape[2] // (2 * PAIR_GROUP), group, 0, unroll=4)


def _dft_in(f1p, z):
    b, half, n2, c = z.shape
    n1 = half
    cb, rs = V7X_LANES, DFT_STEP_ROWS
    return pl.pallas_call(
        _dft_in_body,
        grid=(b, c // cb, n2 // rs),
        in_specs=[
            pl.BlockSpec(f1p.shape, lambda i, j, s: (0, 0)),
            pl.BlockSpec((1, half, rs, cb), lambda i, j, s: (i, 0, s, j)),
        ],
        out_specs=pl.BlockSpec((1, 2, n1, rs, cb), lambda i, j, s: (i, 0, 0, s, j)),
        out_shape=jax.ShapeDtypeStruct((b, 2, n1, n2, c), BF16),
        compiler_params=_params(("parallel", "parallel", "parallel")),
        name="dft_in",
    )(f1p, z)


def _dft_out_body(g_ref, b_ref, x_ref, z_ref, skip_ref, o_ref):
    cb = o_ref.shape[-1]

    def group(g, carry):
        re, im = _load_pair_group(b_ref, (0, 0), g), _load_pair_group(b_ref, (0, 1), g)
        bcat = jnp.concatenate([jnp.concatenate([r, i], 0) for r, i in zip(re, im)], 1)
        y = jnp.dot(g_ref[...], bcat, preferred_element_type=F32)
        xs, zs = _load_pair_group(x_ref, (0,), g), _load_pair_group(z_ref, (0,), g)
        outs = [(xs[i].astype(F32) * (y[:, i * cb:(i + 1) * cb] + skip_ref[...] * zs[i].astype(F32))
                 ).astype(BF16) for i in range(PAIR_GROUP)]
        _store_pair_group(o_ref, (0,), g, outs)
        return carry

    lax.fori_loop(0, x_ref.shape[2] // (2 * PAIR_GROUP), group, 0, unroll=2)


def _dft_out(gp, bb, x, z, skip):
    b, half, n2, c = z.shape
    n1 = half
    cb, rs = V7X_LANES, DFT_STEP_ROWS
    slab = pl.BlockSpec((1, half, rs, cb), lambda i, j, s: (i, 0, s, j))
    return pl.pallas_call(
        _dft_out_body,
        grid=(b, c // cb, n2 // rs),
        in_specs=[
            pl.BlockSpec(gp.shape, lambda i, j, s: (0, 0)),
            pl.BlockSpec((1, 2, n1, rs, cb), lambda i, j, s: (i, 0, 0, s, j)),
            slab, slab,
            pl.BlockSpec((1, cb), lambda i, j, s: (0, j)),
        ],
        out_specs=slab,
        out_shape=jax.ShapeDtypeStruct(z.shape, BF16),
        compiler_params=_params(("parallel", "parallel", "parallel")),
        name="dft_out",
    )(gp, bb, x, z, skip.astype(F32).reshape(1, c))


def _slab_conv_body(mf_ref, mi_ref, k_ref, a_ref, o_ref):
    nb, _, ks, n2, c = a_ref.shape
    for kk in range(ks):
        kr, ki = k_ref[kk, 0].astype(F32), k_ref[kk, 1].astype(F32)
        for b in range(nb):
            a = a_ref[b, :, kk].reshape(2 * n2, c)
            x = jnp.dot(mf_ref[kk], a, preferred_element_type=F32)
            xr, xi = x[:n2], x[n2:]
            y = jnp.concatenate([xr * kr - xi * ki, xr * ki + xi * kr], 0).astype(BF16)
            out = jnp.dot(mi_ref[kk], y, preferred_element_type=F32)
            o_ref[b, :, kk] = out.astype(BF16).reshape(2, n2, c)


def _slab_conv(mf, mi, kspec, a, order):
    b, _, n1, n2, c = a.shape
    m = 2 * n2
    ks = max(1, SLABS_PER_STEP // b)
    return pl.pallas_call(
        _slab_conv_body,
        grid=(n1 // ks,),
        in_specs=[
            pl.BlockSpec((ks, m, m), lambda k: (k, 0, 0)),
            pl.BlockSpec((ks, m, m), lambda k: (k, 0, 0)),
            pl.BlockSpec((ks, 2, n2, c), lambda k: (k, 0, 0, order)),
            pl.BlockSpec((b, 2, ks, n2, c), lambda k: (0, 0, k, 0, 0)),
        ],
        out_specs=pl.BlockSpec((b, 2, ks, n2, c), lambda k: (0, 0, k, 0, 0)),
        out_shape=jax.ShapeDtypeStruct(a.shape, BF16),
        compiler_params=_params(("parallel",)),
        name="slab_conv",
    )(mf, mi, kspec, a)


def _slab_spec_body(mf_ref, sum_ref, a_ref, k_ref):
    _, ks, n2, c = a_ref.shape
    half = c // 2
    inv = 1.0 / (sum_ref[:, :half] + sum_ref[:, half:] + 1e-6)
    for kk in range(ks):
        x = jnp.dot(mf_ref[kk], a_ref[:, kk].reshape(2 * n2, c), preferred_element_type=F32)
        k_ref[kk, 0] = ((x[:n2, :half] + x[:n2, half:]) * inv).astype(BF16)
        k_ref[kk, 1] = ((x[n2:, :half] - x[n2:, half:]) * inv).astype(BF16)


def _slab_spec(mf, sums, a):
    _, _, n1, n2, c = a.shape
    m = 2 * n2
    ks = SPEC_SLABS_PER_STEP
    return pl.pallas_call(
        _slab_spec_body,
        grid=(n1 // ks,),
        in_specs=[
            pl.BlockSpec((ks, m, m), lambda k: (k, 0, 0)),
            pl.BlockSpec((1, c), lambda k: (0, 0)),
            pl.BlockSpec((None, 2, ks, n2, c), lambda k: (0, 0, k, 0, 0)),
        ],
        out_specs=pl.BlockSpec((ks, 2, n2, c // 2), lambda k: (k, 0, 0, 0)),
        out_shape=jax.ShapeDtypeStruct((n1, 2, n2, c // 2), BF16),
        compiler_params=_params(("parallel",)),
        name="slab_spec",
    )(mf, sums, a)


def _hyena_spectrum(l, tables, w1, b1, freq, w2, b2, w3, b3, decay):
    f1p, _, mf, _ = tables
    taps, sums = _filter_taps(l, w1, b1, freq, w2, b2, w3, b3, decay)
    a = _dft_in(f1p, taps.reshape(1, l // DFT_N2, DFT_N2, taps.shape[1]))
    return _slab_spec(mf, sums, a)


def _hyena(v, x1, x2g, kspec, skip, tables):
    f1p, gp, mf, mi = tables
    b, l, c = v.shape
    slabs = lambda u: u.reshape(b, l // DFT_N2, DFT_N2, c)
    z = slabs(v)
    for order, xg in enumerate((slabs(x1), slabs(x2g))):
        bb = _slab_conv(mf, mi, kspec, _dft_in(f1p, z), order)
        z = _dft_out(gp, bb, xg, z, skip[order])
    return z.reshape(b, l, c)


def _head_masks(rows, heads):
    lane = lax.broadcasted_iota(jnp.int32, (rows, heads * HEAD_DIM), 1)
    return [(lane >= h * HEAD_DIM) & (lane < (h + 1) * HEAD_DIM) for h in range(heads)]


def _stack_heads(q, masks):
    zero = jnp.zeros_like(q)
    return jnp.concatenate([jnp.where(m, q, zero) for m in masks], 0)


def _unstack_heads(res, masks, rows):
    out = jnp.where(masks[0], res[:rows], 0.0)
    for h in range(1, len(masks)):
        out = out + jnp.where(masks[h], res[h * rows:(h + 1) * rows], 0.0)
    return out


def _na_bias_table(rpb):
    heads = rpb.shape[0]
    c = jnp.arange(GRID_W)
    col_start = jnp.clip(c - NA_KC // 2, 0, GRID_W - NA_KC)
    col_ok = (c[None, :] >= col_start[:, None]) & (c[None, :] < col_start[:, None] + NA_KC)
    dc = jnp.clip(c[None, :] - c[:, None], -(NA_KC - 1), NA_KC - 1) + NA_KC - 1
    onehot = (dc[:, :, None] == jnp.arange(2 * NA_KC - 1)[None, None, :]).astype(F32)
    tcol = jnp.einsum("hrj,qkj->hrqk", rpb.astype(F32), onehot, precision=lax.Precision.HIGHEST)
    per_idx = [tcol[:, NA_KR - 1 - idx:2 * NA_KR - 1 - idx] for idx in range(NA_KR)]
    bias = jnp.transpose(jnp.stack(per_idx, 0), (0, 1, 3, 2, 4))
    bias = jnp.where(col_ok[None, None, :, None, :], bias * LOG2_E, NEG_INF)
    return bias.reshape(NA_KR, heads * GRID_W, NA_KR * GRID_W)


def _na_body(q_ref, k_ref, v_ref, z_ref, bias_ref, o_ref, *, rows, rb, heads):
    blk = pl.program_id(1)
    masks = _head_masks(GRID_W, heads)
    span = NA_KR * GRID_W

    def one_row(rr, carry):
        r = blk * rb + rr
        start = jnp.clip(r - NA_KR // 2, 0, rows - NA_KR)
        idx = r - start
        koff = pl.multiple_of(start * GRID_W, GRID_W)
        qoff = pl.multiple_of(rr * GRID_W, GRID_W)
        q = q_ref[pl.ds(qoff, GRID_W), :]
        kk = k_ref[pl.ds(koff, span), :]
        vv = v_ref[pl.ds(koff, span), :]
        s = lax.dot_general(_stack_heads(q, masks), kk, (((1,), (1,)), ((), ())),
                            preferred_element_type=F32)
        s = s + bias_ref[idx]
        m = jnp.max(s, -1, keepdims=True)
        e = jnp.exp2(s - m)
        p = (e / jnp.sum(e, -1, keepdims=True)).astype(BF16)
        o = _unstack_heads(jnp.dot(p, vv, preferred_element_type=F32), masks, GRID_W)
        z = z_ref[pl.ds(qoff, GRID_W), :].astype(F32)
        o_ref[pl.ds(qoff, GRID_W), :] = (o * _silu(z)).astype(BF16)
        return carry

    lax.fori_loop(0, rb, one_row, 0, unroll=NA_ROW_UNROLL)


def _na(proj_b, bias):
    b, l, n = proj_b.shape
    db = n // 4
    heads = db // HEAD_DIM
    rows = l // GRID_W
    rb = NA_ROWS_PER_STEP
    t = rb * GRID_W
    return pl.pallas_call(
        functools.partial(_na_body, rows=rows, rb=rb, heads=heads),
        grid=(b, rows // rb),
        in_specs=[
            pl.BlockSpec((None, t, db), lambda i, j: (i, j, 0)),
            pl.BlockSpec((None, l, db), lambda i, j: (i, 0, 1)),
            pl.BlockSpec((None, l, db), lambda i, j: (i, 0, 2)),
            pl.BlockSpec((None, t, db), lambda i, j: (i, j, 3)),
            pl.BlockSpec(bias.shape, lambda i, j: (0, 0, 0)),
        ],
        out_specs=pl.BlockSpec((None, t, db), lambda i, j: (i, j, 0)),
        out_shape=jax.ShapeDtypeStruct((b, l, db), BF16),
        compiler_params=_params(("parallel", "parallel")),
        name="na",
    )(proj_b, proj_b, proj_b, proj_b, bias)


def _attend(q, kk, vv, valid, masks):
    rows = q.shape[0]
    s = lax.dot_general(_stack_heads(q, masks), kk, (((1,), (1,)), ((), ())), preferred_element_type=F32)
    s = jnp.where(valid, s, NEG_INF)
    m = jnp.max(s, -1, keepdims=True)
    e = jnp.exp2(s - m)
    l = jnp.sum(e, -1, keepdims=True)
    p = (e / l).astype(BF16)
    o = _unstack_heads(jnp.dot(p, vv, preferred_element_type=F32), masks, rows)
    lse = _unstack_heads(jnp.broadcast_to(m + jnp.log2(l), (s.shape[0], q.shape[1])), masks, rows)
    return o, lse


def _split_pattern(dilation, n, nchunks, q_ref, kbuf, vbuf, o_split, l_split, masks, heads):
    blk = DIL_BLK
    dc = heads * HEAD_DIM
    phases = DIL_SPLIT // dilation
    mq = blk // phases
    row = lax.broadcasted_iota(jnp.int32, (heads * blk, 3 * blk), 0) % blk
    col = lax.broadcasted_iota(jnp.int32, (heads * blk, 3 * blk), 1)
    qa, qm = row // mq, row % mq
    ka = sum((col >= a * 3 * mq).astype(jnp.int32) for a in range(1, phases)) if phases > 1 else 0
    km = col - ka * (3 * mq)
    band = jnp.abs(phases * (km - mq - qm) + (ka - qa)) <= blk
    for m0 in range(0, blk, mq):
        gm = km + (n * blk + m0 - mq)
        valid = band & (gm >= 0) & (gm < nchunks * blk)
        for j in range(dilation):
            lanes = [slice((j + dilation * a) * dc, (j + dilation * a + 1) * dc) for a in range(phases)]
            q = jnp.concatenate([q_ref[m0:m0 + mq, ln] for ln in lanes], 0)
            krows = slice(blk + m0 - mq, blk + m0 + 2 * mq)
            kk = jnp.concatenate([kbuf[krows, ln] for ln in lanes], 0)
            vv = jnp.concatenate([vbuf[krows, ln] for ln in lanes], 0)
            o, lse = _attend(q, kk, vv, valid, masks)
            for a in range(phases):
                o_split[j + dilation * a, m0:m0 + mq, :] = o[a * mq:(a + 1) * mq]
                l_split[j + dilation * a, m0:m0 + mq, :] = lse[a * mq:(a + 1) * mq]


def _dil_body(qn_ref, knp_ref, kn_ref, knn_ref, vnp_ref, vn_ref, vnn_ref,
              qs_ref, ksp_ref, ks_ref, ksn_ref, vsp_ref, vs_ref, vsn_ref, cz_ref, y_ref,
              kbn, vbn, kbs, vbs, o_split, l_split, o_all, l_all, *, heads, nchunks):
    n = pl.program_id(1)
    blk = DIL_BLK
    chunk = qn_ref.shape[0]
    per = chunk // blk
    masks = _head_masks(blk, heads)
    for buf, prev, cur, nxt, halo in ((kbn, knp_ref, kn_ref, knn_ref, blk), (vbn, vnp_ref, vn_ref, vnn_ref, blk),
                                      (kbs, ksp_ref, ks_ref, ksn_ref, blk), (vbs, vsp_ref, vs_ref, vsn_ref, blk)):
        main = cur.shape[0]
        buf[0:halo] = prev[...]
        buf[halo:halo + main] = cur[...]
        buf[halo + main:] = nxt[...]

    qi = lax.broadcasted_iota(jnp.int32, (heads * blk, 3 * blk), 0) % blk
    ki = lax.broadcasted_iota(jnp.int32, (heads * blk, 3 * blk), 1)
    band = jnp.abs(ki - blk - qi) <= blk

    def token_block(i, carry):
        g = n * per + i
        off = pl.multiple_of(i * blk, blk)
        valid = band & ((ki >= blk) | (g > 0)) & ((ki < 2 * blk) | (g < nchunks * per - 1))
        o, lse = _attend(qn_ref[pl.ds(off, blk), :], kbn[pl.ds(off, 3 * blk), :], vbn[pl.ds(off, 3 * blk), :],
                         valid, masks)
        o_all[0, pl.ds(off, blk), :] = o
        l_all[0, pl.ds(off, blk), :] = lse
        return carry

    lax.fori_loop(0, per, token_block, 0, unroll=True)

    for g, (_, dilation) in enumerate(DIL_PATTERNS):
        if dilation == 1:
            continue
        _split_pattern(dilation, n, nchunks, qs_ref, kbs, vbs, o_split, l_split, masks, heads)
        o_all[g] = jnp.swapaxes(o_split[...], 0, 1).reshape(chunk, heads * HEAD_DIM)
        l_all[g] = jnp.swapaxes(l_split[...], 0, 1).reshape(chunk, heads * HEAD_DIM)

    ls = [l_all[g] for g in range(len(DIL_PATTERNS))]
    m = functools.reduce(jnp.maximum, ls)
    es = [jnp.exp2(l - m) for l in ls]
    den = functools.reduce(lambda a, b: a + b, es)
    o = functools.reduce(lambda a, b: a + b, [(e / den) * o_all[g] for g, e in enumerate(es)])
    y_ref[...] = (o * cz_ref[...].astype(F32)).astype(BF16)


def _dilated_mixture(qn, kn, vn, qs, ks, vs, cz):
    b, l, dc = qn.shape
    heads = dc // HEAD_DIM
    blk = DIL_BLK
    assert DIL_PATTERNS[0][1] == 1 and all(w == 2 * blk * d and DIL_SPLIT % d == 0 for w, d in DIL_PATTERNS)
    chunk = DIL_SPLIT * blk
    nchunks = l // chunk
    per = chunk // blk
    nat = pl.BlockSpec((None, chunk, dc), lambda i, n: (i, n, 0))
    nat_prev = pl.BlockSpec((None, blk, dc), lambda i, n: (i, jnp.maximum(n * per - 1, 0), 0))
    nat_next = pl.BlockSpec((None, blk, dc), lambda i, n: (i, jnp.minimum((n + 1) * per, nchunks * per - 1), 0))
    spl = pl.BlockSpec((None, blk, DIL_SPLIT * dc), lambda i, n: (i, n, 0))
    spl_prev = pl.BlockSpec((None, blk, DIL_SPLIT * dc), lambda i, n: (i, jnp.maximum(n - 1, 0), 0))
    spl_next = pl.BlockSpec((None, blk, DIL_SPLIT * dc), lambda i, n: (i, jnp.minimum(n + 1, nchunks - 1), 0))
    return pl.pallas_call(
        functools.partial(_dil_body, heads=heads, nchunks=nchunks),
        grid=(b, nchunks),
        in_specs=[nat, nat_prev, nat, nat_next, nat_prev, nat, nat_next,
                  spl, spl_prev, spl, spl_next, spl_prev, spl, spl_next, nat],
        out_specs=nat,
        out_shape=jax.ShapeDtypeStruct((b, l, dc), BF16),
        scratch_shapes=[pltpu.VMEM((chunk + 2 * blk, dc), BF16), pltpu.VMEM((chunk + 2 * blk, dc), BF16),
                        pltpu.VMEM((3 * blk, DIL_SPLIT * dc), BF16), pltpu.VMEM((3 * blk, DIL_SPLIT * dc), BF16),
                        pltpu.VMEM((DIL_SPLIT, blk, dc), F32), pltpu.VMEM((DIL_SPLIT, blk, dc), F32),
                        pltpu.VMEM((len(DIL_PATTERNS), chunk, dc), F32),
                        pltpu.VMEM((len(DIL_PATTERNS), chunk, dc), F32)],
        compiler_params=_params(("parallel", "parallel")),
        name="dilated",
    )(qn, kn, kn, kn, vn, vn, vn, qs, ks, ks, ks, vs, vs, vs, cz)


def _tail_body(x_ref, ya_ref, yb_ref, yc_ref, g_ref, gate_ref, wa_ref, wb_ref, wc_ref, wo_ref, lng_ref, lnb_ref,
               out_ref, *, alpha):
    d = x_ref.shape[-1]
    pa = jnp.dot(ya_ref[...], wa_ref[...], preferred_element_type=F32)
    pb = jnp.dot(yb_ref[...], wb_ref[...], preferred_element_type=F32)
    pc = jnp.dot(yc_ref[...], wc_ref[...], preferred_element_type=F32)
    g = g_ref[...].astype(F32)
    merged = g[:, :d] * pa + g[:, d:2 * d] * pb + g[:, 2 * d:] * pc
    sub = jnp.dot(merged.astype(BF16), wo_ref[...], preferred_element_type=F32) * gate_ref[...]
    res = alpha * x_ref[...] + sub
    out_ref[...] = _layernorm(res) * lng_ref[...] + lnb_ref[...]


def _tail(x, ya, yb, yc, g_all, gate, wa, wb, wc, wo, ln_g, ln_b, alpha):
    b, l, d = x.shape
    t = TOKEN_TILE
    tok = lambda w: pl.BlockSpec((None, t, w), lambda i, j: (i, j, 0))
    const = lambda a: pl.BlockSpec(a.shape, lambda i, j: (0,) * a.ndim)
    ln_g, ln_b = ln_g.reshape(1, d), ln_b.reshape(1, d)
    return pl.pallas_call(
        functools.partial(_tail_body, alpha=alpha),
        grid=(b, l // t),
        in_specs=[tok(d), tok(ya.shape[-1]), tok(yb.shape[-1]), tok(yc.shape[-1]), tok(3 * d),
                  pl.BlockSpec((None, 1, d), lambda i, j: (i, 0, 0)),
                  const(wa), const(wb), const(wc), const(wo), const(ln_g), const(ln_b)],
        out_specs=tok(d),
        out_shape=jax.ShapeDtypeStruct((b, l, d), F32),
        compiler_params=_params(("parallel", "parallel")),
        name="tail",
    )(x, ya, yb, yc, g_all, gate.reshape(b, 1, d), wa, wb, wc, wo, ln_g, ln_b)


def _rope_tables(l, heads):
    half = HEAD_DIM // 2
    inv = ROPE_THETA ** (-jnp.arange(half, dtype=F32) / half)
    ang = jnp.arange(l, dtype=F32)[:, None] * inv[None, :]
    cos, sin = jnp.cos(ang), jnp.sin(ang)
    return (jnp.tile(jnp.concatenate([cos, cos], -1), (1, heads)),
            jnp.tile(jnp.concatenate([-sin, sin], -1), (1, heads)))


def _layer(x, ada, lw, consts, alpha):
    d = x.shape[-1]
    shift, scale, gate = ada[:, :d], ada[:, d:2 * d], ada[:, 2 * d:]
    h = _ln_mod(x, scale, shift)
    w_in, b_in = lw["w_in"], lw["b_in"]
    proj_a = _proj(h, w_in[:, :2 * d], b_in[:2 * d])
    q_cols = jnp.arange(d) < d // 4
    proj_b = _proj(h, w_in[:, 2 * d:3 * d], b_in[2 * d:3 * d], col_scale=jnp.where(q_cols, QK_SCALE_LOG2, 1.0))
    cq, ck, cv, cz, cqs, cks, cvs = _proj_rope(h, w_in[:, 3 * d:4 * d], b_in[3 * d:4 * d], *consts["rope"])
    gates = _proj(h, w_in[:, 4 * d:], b_in[4 * d:], sigmoid=True)

    v, x1, x2g = _hy_pre(proj_a, lw["conv_w"], lw["conv_b"])
    ya = _hyena(v, x1, x2g, consts["kspec"], lw["skip"], consts["dft"])
    yb = _na(proj_b, lw["na_bias"])
    yc = _dilated_mixture(cq, ck, cv, cqs, cks, cvs, cz)
    return _tail(x, ya, yb, yc, gates, gate, lw["wa"], lw["wb"], lw["wc"], lw["wo"],
                 lw["ln_g"], lw["ln_b"], alpha)


def kernel(x_prompt, x_sample, c_prompt, c_sample, w_ada, b_ada, w_in, b_in, hy_conv_w, hy_conv_b, hy_w1, hy_b1, hy_freq, hy_w2, hy_b2, hy_w3, hy_b3, hy_decay, hy_skip, na_rpb, w_branch_a, w_branch_b, w_branch_c, w_out, ln_g, ln_b):
    depth, d, _ = w_in.shape
    heads_c = (d // 4) // HEAD_DIM
    alpha = (2 * depth) ** 0.25
    groups = [(x_prompt, c_prompt), (x_sample, c_sample)]

    nb_p = c_prompt.shape[0]
    c_all = jnp.concatenate([c_prompt, c_sample], 0)
    pad_rows = -c_all.shape[0] % 8
    ada_all = _ada(jnp.pad(c_all, ((0, pad_rows), (0, 0))), w_ada, b_ada)
    adas = [ada_all[:, :nb_p], ada_all[:, nb_p:nb_p + c_sample.shape[0]]]

    shared = {}
    for x, _ in groups:
        l = x.shape[1]
        if l not in shared:
            shared[l] = {"dft": _dft_tables(l), "rope": _rope_tables(l, heads_c)}

    ys = [x for x, _ in groups]
    for layer in range(depth):
        lw = {
            "w_in": w_in[layer].astype(BF16), "b_in": b_in[layer],
            "conv_w": hy_conv_w[layer], "conv_b": hy_conv_b[layer], "skip": hy_skip[layer],
            "na_bias": _na_bias_table(na_rpb[layer]),
            "wa": w_branch_a[layer].astype(BF16), "wb": w_branch_b[layer].astype(BF16),
            "wc": w_branch_c[layer].astype(BF16), "wo": w_out[layer].astype(BF16),
            "ln_g": ln_g[layer], "ln_b": ln_b[layer],
        }
        kspecs = {}
        for gi in range(len(groups)):
            l = ys[gi].shape[1]
            if l not in kspecs:
                kspecs[l] = _hyena_spectrum(l, shared[l]["dft"], hy_w1[layer], hy_b1[layer], hy_freq[layer],
                                            hy_w2[layer], hy_b2[layer], hy_w3[layer], hy_b3[layer],
                                            hy_decay[layer])
            consts = dict(shared[l], kspec=kspecs[l])
            ys[gi] = _layer(ys[gi], adas[gi][layer], lw, consts, alpha)
    return tuple(ys)
```

```python
import functools
import math

import jax
import jax.numpy as jnp
from jax import lax
from jax.experimental import pallas as pl
from jax.experimental.pallas import tpu as pltpu

F32 = jnp.float32
BF16 = jnp.bfloat16

GRID_W = 64
HEAD_DIM = 64
HYENA_BANDS = 16
HYENA_EMB = 2 * HYENA_BANDS + 1
NA_KR = 8
NA_KC = 16
DIL_PATTERNS = ((128, 1), (512, 4), (2048, 16))
DIL_BLK = 64
DIL_SPLIT = 16
NA_ROW_UNROLL = 32
ROPE_THETA = 10000.0
LN_EPS = 1e-5
NEG_INF = -1e30
LOG2_E = math.log2(math.e)
QK_SCALE_LOG2 = HEAD_DIM ** -0.5 * LOG2_E

V7X_LANES = 128
V7X_SUBLANES = 8
V7X_VMEM_LIMIT_BYTES = 56 * 1024 * 1024

DFT_N2 = V7X_LANES
PAIR_GROUP = V7X_SUBLANES
DFT_STEP_ROWS = 128
SLABS_PER_STEP = 16
FEAT_PAD = V7X_LANES
BF16_TILE_ROWS = 2 * V7X_SUBLANES

TOKEN_TILE = 1024
WIDE_TOKEN_TILE = 2048
PROJ_COL_TILE = 1024
ROPE_TOKEN_TILE = 1024
FILTER_ROW_TILE = 1024
NA_ROWS_PER_STEP = 32
SPEC_SLABS_PER_STEP = 4
HY_SHIFT_ROWS = 256


def _params(sem):
    return pltpu.CompilerParams(dimension_semantics=sem, vmem_limit_bytes=V7X_VMEM_LIMIT_BYTES)


def _sigmoid(x):
    return 1.0 / (1.0 + jnp.exp(-x))


def _silu(x):
    return x * _sigmoid(x)


def _ada_body(c_ref, w_ref, b_ref, o_ref):
    s = _silu(c_ref[...])
    o_ref[...] = jnp.dot(s, w_ref[...], preferred_element_type=F32,
                         precision=lax.Precision.HIGHEST) + b_ref[...]


def _ada(c_all, w_ada, b_ada):
    depth, d, n = w_ada.shape
    rows = c_all.shape[0]
    tn = PROJ_COL_TILE
    return pl.pallas_call(
        _ada_body,
        grid=(depth, n // tn),
        in_specs=[
            pl.BlockSpec((rows, d), lambda l, j: (0, 0)),
            pl.BlockSpec((None, d, tn), lambda l, j: (l, 0, j)),
            pl.BlockSpec((None, 1, tn), lambda l, j: (l, 0, j)),
        ],
        out_specs=pl.BlockSpec((None, rows, tn), lambda l, j: (l, 0, j)),
        out_shape=jax.ShapeDtypeStruct((depth, rows, n), F32),
        compiler_params=_params(("parallel", "parallel")),
        name="ada",
    )(c_all, w_ada, b_ada.reshape(depth, 1, n))


def _layernorm(x):
    mu = jnp.mean(x, -1, keepdims=True)
    xc = x - mu
    var = jnp.mean(xc * xc, -1, keepdims=True)
    return xc * lax.rsqrt(var + LN_EPS)


def _ln_mod_body(x_ref, sc_ref, sh_ref, o_ref):
    h = _layernorm(x_ref[...]) * (1.0 + sc_ref[...]) + sh_ref[...]
    o_ref[...] = h.astype(BF16)


def _ln_mod(x, scale, shift):
    b, l, d = x.shape
    t = WIDE_TOKEN_TILE
    return pl.pallas_call(
        _ln_mod_body,
        grid=(b, l // t),
        in_specs=[
            pl.BlockSpec((None, t, d), lambda i, j: (i, j, 0)),
            pl.BlockSpec((None, 1, d), lambda i, j: (i, 0, 0)),
            pl.BlockSpec((None, 1, d), lambda i, j: (i, 0, 0)),
        ],
        out_specs=pl.BlockSpec((None, t, d), lambda i, j: (i, j, 0)),
        out_shape=jax.ShapeDtypeStruct((b, l, d), BF16),
        compiler_params=_params(("parallel", "parallel")),
        name="ln_mod",
    )(x, scale.reshape(b, 1, d), shift.reshape(b, 1, d))


def _proj_body(h_ref, w_ref, b_ref, *rest, sigmoid, scaled):
    o_ref = rest[-1]
    acc = jnp.dot(h_ref[...], w_ref[...], preferred_element_type=F32) + b_ref[...]
    if scaled:
        acc = acc * rest[0][...]
    if sigmoid:
        acc = 0.5 * jnp.tanh(0.5 * acc) + 0.5
    o_ref[...] = acc.astype(o_ref.dtype)


def _proj(h, w, bias, col_scale=None, sigmoid=False):
    b, l, d = h.shape
    n = w.shape[1]
    t, tn = TOKEN_TILE, PROJ_COL_TILE
    row = pl.BlockSpec((1, tn), lambda j, i, k: (0, j))
    scale_args = [] if col_scale is None else [col_scale.reshape(1, n)]
    return pl.pallas_call(
        functools.partial(_proj_body, sigmoid=sigmoid, scaled=col_scale is not None),
        grid=(n // tn, b, l // t),
        in_specs=[
            pl.BlockSpec((None, t, d), lambda j, i, k: (i, k, 0)),
            pl.BlockSpec((d, tn), lambda j, i, k: (0, j)),
            row,
        ] + [row] * len(scale_args),
        out_specs=pl.BlockSpec((None, t, tn), lambda j, i, k: (i, k, j)),
        out_shape=jax.ShapeDtypeStruct((b, l, n), BF16),
        compiler_params=_params(("parallel", "parallel", "parallel")),
        name="proj",
    )(h, w, bias.reshape(1, n), *scale_args)


def _rope_lanes(x, cos, sin_signed):
    outs = []
    lane = lax.broadcasted_iota(jnp.int32, (x.shape[0], V7X_LANES), 1)
    first_half = (lane % HEAD_DIM) < (HEAD_DIM // 2)
    for c0 in range(0, x.shape[1], V7X_LANES):
        xc = x[:, c0:c0 + V7X_LANES]
        partner = jnp.where(first_half,
                            pltpu.roll(xc, V7X_LANES - HEAD_DIM // 2, 1),
                            pltpu.roll(xc, HEAD_DIM // 2, 1))
        outs.append(xc * cos[:, c0:c0 + V7X_LANES] + partner * sin_signed[:, c0:c0 + V7X_LANES])
    return jnp.concatenate(outs, 1)


def _store_split(ref, x):
    t, dc = x.shape
    parts = jnp.swapaxes(x.reshape(t // DIL_SPLIT, DIL_SPLIT, dc), 0, 1)
    for r in range(DIL_SPLIT):
        ref[:, r * dc:(r + 1) * dc] = parts[r].astype(BF16)


def _proj_rope_body(h_ref, w_ref, b_ref, cos_ref, sin_ref, q_ref, k_ref, v_ref, z_ref,
                    qs_ref, ks_ref, vs_ref):
    dc = q_ref.shape[-1]
    acc = jnp.dot(h_ref[...], w_ref[...], preferred_element_type=F32) + b_ref[...]
    cos, sin = cos_ref[...], sin_ref[...]
    q = _rope_lanes(acc[:, :dc], cos, sin) * QK_SCALE_LOG2
    k = _rope_lanes(acc[:, dc:2 * dc], cos, sin)
    v = acc[:, 2 * dc:3 * dc]
    for ref, split_ref, val in ((q_ref, qs_ref, q), (k_ref, ks_ref, k), (v_ref, vs_ref, v)):
        ref[...] = val.astype(BF16)
        _store_split(split_ref, val)
    z_ref[...] = _silu(acc[:, 3 * dc:]).astype(BF16)


def _proj_rope(h, w, bias, cos_t, sin_t):
    b, l, d = h.shape
    n = w.shape[1]
    dc = n // 4
    t = ROPE_TOKEN_TILE
    tok = pl.BlockSpec((None, t, dc), lambda i, k: (i, k, 0))
    shp = jax.ShapeDtypeStruct((b, l, dc), BF16)
    spl = pl.BlockSpec((None, t // DIL_SPLIT, DIL_SPLIT * dc), lambda i, k: (i, k, 0))
    spl_shp = jax.ShapeDtypeStruct((b, l // DIL_SPLIT, DIL_SPLIT * dc), BF16)
    return pl.pallas_call(
        _proj_rope_body,
        grid=(b, l // t),
        in_specs=[
            pl.BlockSpec((None, t, d), lambda i, k: (i, k, 0)),
            pl.BlockSpec((d, n), lambda i, k: (0, 0)),
            pl.BlockSpec((1, n), lambda i, k: (0, 0)),
            pl.BlockSpec((t, dc), lambda i, k: (k, 0)),
            pl.BlockSpec((t, dc), lambda i, k: (k, 0)),
        ],
        out_specs=[tok, tok, tok, tok, spl, spl, spl],
        out_shape=[shp, shp, shp, shp, spl_shp, spl_shp, spl_shp],
        compiler_params=_params(("parallel", "parallel")),
        name="proj_rope",
    )(h, w, bias.reshape(1, n), cos_t, sin_t)


def _hy_pre_body(main_ref, prev_ref, next_ref, shift_ref, cw_ref, cb_ref, v_ref, x1_ref, x2_ref, *, da):
    i = pl.program_id(1)
    last = pl.num_programs(1) - 1
    t = main_ref.shape[0]
    halo = prev_ref.shape[0]
    blk = shift_ref.shape[1]
    nc = 3 * da
    outs = (v_ref, x1_ref, x2_ref)
    edge = lax.broadcasted_iota(jnp.int32, (V7X_SUBLANES, nc), 0)
    for r0 in range(0, t, blk):
        xb = main_ref[r0:r0 + blk, :nc]
        sh = jnp.dot(shift_ref[...], xb, preferred_element_type=F32)
        if r0 == 0:
            prev_row = jnp.where(i > 0, prev_ref[:, :nc].astype(F32)[halo - 1:halo], 0.0)
        else:
            prev_row = main_ref[r0 - halo:r0, :nc].astype(F32)[halo - 1:halo]
        if r0 + blk == t:
            next_row = jnp.where(i < last, next_ref[:, :nc].astype(F32)[0:1], 0.0)
        else:
            next_row = main_ref[r0 + blk:r0 + blk + halo, :nc].astype(F32)[0:1]
        up, dn = sh[:blk], sh[blk:]
        up = jnp.concatenate([up[:V7X_SUBLANES] + jnp.where(edge == 0, prev_row, 0.0), up[V7X_SUBLANES:]], 0)
        dn = jnp.concatenate([dn[:blk - V7X_SUBLANES],
                              dn[blk - V7X_SUBLANES:] + jnp.where(edge == V7X_SUBLANES - 1, next_row, 0.0)], 0)
        uc = up * cw_ref[0:1, :] + xb.astype(F32) * cw_ref[1:2, :] + dn * cw_ref[2:3, :] + cb_ref[...]
        for part in range(3):
            val = uc[:, part * da:(part + 1) * da]
            if part == 2:
                val = val * _silu(main_ref[r0:r0 + blk, nc:].astype(F32))
            outs[part][r0:r0 + blk, :] = val.astype(BF16)


def _hy_pre(proj_a, conv_w, conv_b):
    b, l, n = proj_a.shape
    da = n // 4
    t, halo = WIDE_TOKEN_TILE, BF16_TILE_ROWS
    nh = t // halo
    blk = HY_SHIFT_ROWS
    shift = jnp.concatenate([jnp.eye(blk, k=-1, dtype=BF16), jnp.eye(blk, k=1, dtype=BF16)], 0)
    tok = pl.BlockSpec((None, t, da), lambda i, j: (i, j, 0))
    shp = jax.ShapeDtypeStruct((b, l, da), BF16)
    return pl.pallas_call(
        functools.partial(_hy_pre_body, da=da),
        grid=(b, l // t),
        in_specs=[
            pl.BlockSpec((None, t, n), lambda i, j: (i, j, 0)),
            pl.BlockSpec((None, halo, n), lambda i, j: (i, jnp.maximum(j * nh - 1, 0), 0)),
            pl.BlockSpec((None, halo, n), lambda i, j: (i, jnp.minimum((j + 1) * nh, l // halo - 1), 0)),
            pl.BlockSpec((2 * blk, blk), lambda i, j: (0, 0)),
            pl.BlockSpec((3, 3 * da), lambda i, j: (0, 0)),
            pl.BlockSpec((1, 3 * da), lambda i, j: (0, 0)),
        ],
        out_specs=[tok, tok, tok],
        out_shape=[shp, shp, shp],
        compiler_params=_params(("parallel", "parallel")),
        name="hy_pre",
    )(proj_a, proj_a, proj_a, shift, conv_w, conv_b.reshape(1, 3 * da))


def _filter_body(feat_ref, w1_ref, b1_ref, f0_ref, w2_ref, b2_ref, f1_ref, w3_ref, b3_ref, dec_ref,
                 hf_ref, sum_ref):
    i = pl.program_id(0)
    hp = lax.Precision.HIGHEST
    half = feat_ref.shape[0] // 2
    n = hf_ref.shape[1]
    fa, fb = feat_ref[0:half, :], feat_ref[half:, :]
    feat = jnp.concatenate([fa, fb], 1)
    h = jnp.sin(f0_ref[...] * (jnp.dot(feat, w1_ref[...], preferred_element_type=F32, precision=hp)
                               + b1_ref[...]))
    h = jnp.sin(f1_ref[...] * (jnp.dot(h, w2_ref[...], preferred_element_type=F32, precision=hp)
                               + b2_ref[...]))
    h = jnp.dot(h.astype(BF16), w3_ref[...], preferred_element_type=F32) + b3_ref[...]
    t = jnp.concatenate([jnp.broadcast_to(fa[:, 0:1], (half, n)), jnp.broadcast_to(fb[:, 0:1], (half, n))], 1)
    h = h * jnp.exp(-t * jnp.abs(dec_ref[...]))
    hf_ref[0:half, :] = h[:, :n].astype(BF16)
    hf_ref[half:, :] = h[:, n:].astype(BF16)
    col_sum = jnp.sum(jnp.abs(h), 0, keepdims=True)
    abs_sum = col_sum[:, :n] + col_sum[:, n:]

    @pl.when(i > 0)
    def _():
        sum_ref[...] += abs_sum

    @pl.when(i == 0)
    def _():
        head = h[0:BF16_TILE_ROWS, :n]
        rows = lax.broadcasted_iota(jnp.int32, head.shape, 0)
        cols = lax.broadcasted_iota(jnp.int32, head.shape, 1)
        drop = (rows == 0) & (cols >= n // 2)
        hf_ref[0:BF16_TILE_ROWS, :] = jnp.where(drop, 0.0, head).astype(BF16)
        sum_ref[...] = abs_sum - jnp.sum(jnp.where(drop, jnp.abs(head), 0.0), 0, keepdims=True)


def _block_diag2(w):
    z = jnp.zeros_like(w)
    return jnp.concatenate([jnp.concatenate([w, z], 1), jnp.concatenate([z, w], 1)], 0)


def _filter_taps(l, w1, b1, freq, w2, b2, w3, b3, decay):
    fo = w1.shape[1]
    n = w3.shape[1]
    da = decay.shape[0]
    t = jnp.arange(l, dtype=F32) / l
    bands = jnp.arange(1, HYENA_BANDS + 1, dtype=F32)
    ang = 2.0 * math.pi * t[:, None] * bands[None, :]
    feat = jnp.concatenate([t[:, None], jnp.cos(ang), jnp.sin(ang)], -1)
    feat = jnp.pad(feat, ((0, 0), (0, FEAT_PAD - HYENA_EMB)))
    w1p = jnp.pad(w1, ((0, FEAT_PAD - HYENA_EMB), (0, 0)))
    twice = lambda v: jnp.tile(v.reshape(1, -1), (1, 2))
    dec = jnp.tile(decay, 2 * n // da).reshape(1, 2 * n)
    tt = FILTER_ROW_TILE
    const = lambda shape: pl.BlockSpec(shape, lambda i: (0,) * len(shape))
    return pl.pallas_call(
        _filter_body,
        grid=(l // tt,),
        in_specs=[
            pl.BlockSpec((tt, FEAT_PAD), lambda i: (i, 0)),
            const((2 * FEAT_PAD, 2 * fo)), const((1, 2 * fo)), const((1, 2 * fo)),
            const((2 * fo, 2 * fo)), const((1, 2 * fo)), const((1, 2 * fo)),
            const((2 * fo, 2 * n)), const((1, 2 * n)), const((1, 2 * n)),
        ],
        out_specs=[pl.BlockSpec((tt, n), lambda i: (i, 0)), const((1, n))],
        out_shape=[jax.ShapeDtypeStruct((l, n), BF16), jax.ShapeDtypeStruct((1, n), F32)],
        compiler_params=_params(("arbitrary",)),
        name="filter_taps",
    )(feat, _block_diag2(w1p), twice(b1), twice(freq[0]), _block_diag2(w2), twice(b2), twice(freq[1]),
      _block_diag2(w3.astype(BF16)), twice(b3), dec)


def _dft_tables(l):
    n = 2 * l
    n1 = n // DFT_N2
    kk = jnp.arange(n1 // 2, dtype=jnp.int32)
    nn = jnp.arange(n1 // 2, dtype=jnp.int32)
    th = (2.0 * math.pi / (2 * n1)) * (((2 * kk[:, None] + 1) * nn[None, :]) % (2 * n1)).astype(F32)
    eye2 = jnp.eye(2, dtype=F32)
    f1 = jnp.kron(jnp.concatenate([jnp.cos(th), -jnp.sin(th)], 0), eye2).astype(BF16)
    g = jnp.kron(jnp.concatenate([jnp.cos(th).T, -jnp.sin(th).T], 1) * (2.0 / n), eye2).astype(BF16)
    k2 = jnp.arange(DFT_N2, dtype=jnp.int32)
    n2 = jnp.arange(DFT_N2, dtype=jnp.int32)
    ph = (n2[None, None, :] * (k2[None, :, None] * (2 * n1) + 2 * kk[:, None, None] + 1)) % (2 * n)
    ang = (2.0 * math.pi / (2 * n)) * ph.astype(F32)
    c, s = jnp.cos(ang), jnp.sin(ang)
    mf = jnp.concatenate([jnp.concatenate([c, s], 2), jnp.concatenate([-s, c], 2)], 1).astype(BF16)
    mi = jnp.swapaxes(mf, 1, 2)
    return f1, g, mf, mi


def _load_pair_group(ref, lead, g):
    start = pl.multiple_of(g * PAIR_GROUP, PAIR_GROUP)
    words = ref.bitcast(jnp.uint32)[(*lead, slice(None), pl.ds(start, PAIR_GROUP), slice(None))]
    words = jnp.swapaxes(words, 0, 1)
    return [pltpu.bitcast(words[i], BF16) for i in range(PAIR_GROUP)]


def _store_pair_group(ref, lead, g, vals):
    start = pl.multiple_of(g * PAIR_GROUP, PAIR_GROUP)
    words = jnp.stack([pltpu.bitcast(v, jnp.uint32) for v in vals], 0)
    ref.bitcast(jnp.uint32)[(*lead, slice(None), pl.ds(start, PAIR_GROUP), slice(None))] = (
        jnp.swapaxes(words, 0, 1))


def _dft_in_body(f_ref, z_ref, a_ref):
    n1 = a_ref.shape[2]
    cb = a_ref.shape[-1]

    def group(g, carry):
        zcat = jnp.concatenate(_load_pair_group(z_ref, (0,), g), 1)
        r = jnp.dot(f_ref[...], zcat, preferred_element_type=F32).astype(BF16)
        cols = [r[:, i * cb:(i + 1) * cb] for i in range(PAIR_GROUP)]
        _store_pair_group(a_ref, (0, 0), g, [c[:2 * n1] for c in cols])
        _store_pair_group(a_ref, (0, 1), g, [c[2 * n1:] for c in cols])
        return carry

    lax.fori_loop(0, z_ref.shape[2] // (2 * PAIR_GROUP), group, 0, unroll=4)


def _dft_in(f1p, z):
    b, half, n2, c = z.shape
    n1 = half
    cb, rs = V7X_LANES, DFT_STEP_ROWS
    return pl.pallas_call(
        _dft_in_body,
        grid=(b, c // cb, n2 // rs),
        in_specs=[
            pl.BlockSpec(f1p.shape, lambda i, j, s: (0, 0)),
            pl.BlockSpec((1, half, rs, cb), lambda i, j, s: (i, 0, s, j)),
        ],
        out_specs=pl.BlockSpec((1, 2, n1, rs, cb), lambda i, j, s: (i, 0, 0, s, j)),
        out_shape=jax.ShapeDtypeStruct((b, 2, n1, n2, c), BF16),
        compiler_params=_params(("parallel", "parallel", "parallel")),
        name="dft_in",
    )(f1p, z)


def _dft_out_body(g_ref, b_ref, x_ref, z_ref, skip_ref, o_ref):
    cb = o_ref.shape[-1]

    def group(g, carry):
        re, im = _load_pair_group(b_ref, (0, 0), g), _load_pair_group(b_ref, (0, 1), g)
        bcat = jnp.concatenate([jnp.concatenate([r, i], 0) for r, i in zip(re, im)], 1)
        y = jnp.dot(g_ref[...], bcat, preferred_element_type=F32)
        xs, zs = _load_pair_group(x_ref, (0,), g), _load_pair_group(z_ref, (0,), g)
        outs = [(xs[i].astype(F32) * (y[:, i * cb:(i + 1) * cb] + skip_ref[...] * zs[i].astype(F32))
                 ).astype(BF16) for i in range(PAIR_GROUP)]
        _store_pair_group(o_ref, (0,), g, outs)
        return carry

    lax.fori_loop(0, x_ref.shape[2] // (2 * PAIR_GROUP), group, 0, unroll=2)


def _dft_out(gp, bb, x, z, skip):
    b, half, n2, c = z.shape
    n1 = half
    cb, rs = V7X_LANES, DFT_STEP_ROWS
    slab = pl.BlockSpec((1, half, rs, cb), lambda i, j, s: (i, 0, s, j))
    return pl.pallas_call(
        _dft_out_body,
        grid=(b, c // cb, n2 // rs),
        in_specs=[
            pl.BlockSpec(gp.shape, lambda i, j, s: (0, 0)),
            pl.BlockSpec((1, 2, n1, rs, cb), lambda i, j, s: (i, 0, 0, s, j)),
            slab, slab,
            pl.BlockSpec((1, cb), lambda i, j, s: (0, j)),
        ],
        out_specs=slab,
        out_shape=jax.ShapeDtypeStruct(z.shape, BF16),
        compiler_params=_params(("parallel", "parallel", "parallel")),
        name="dft_out",
    )(gp, bb, x, z, skip.astype(F32).reshape(1, c))


def _slab_conv_body(mf_ref, mi_ref, k_ref, a_ref, o_ref):
    nb, _, ks, n2, c = a_ref.shape
    for kk in range(ks):
        kr, ki = k_ref[kk, 0].astype(F32), k_ref[kk, 1].astype(F32)
        for b in range(nb):
            a = a_ref[b, :, kk].reshape(2 * n2, c)
            x = jnp.dot(mf_ref[kk], a, preferred_element_type=F32)
            xr, xi = x[:n2], x[n2:]
            y = jnp.concatenate([xr * kr - xi * ki, xr * ki + xi * kr], 0).astype(BF16)
            out = jnp.dot(mi_ref[kk], y, preferred_element_type=F32)
            o_ref[b, :, kk] = out.astype(BF16).reshape(2, n2, c)


def _slab_conv(mf, mi, kspec, a, order):
    b, _, n1, n2, c = a.shape
    m = 2 * n2
    ks = max(1, SLABS_PER_STEP // b)
    return pl.pallas_call(
        _slab_conv_body,
        grid=(n1 // ks,),
        in_specs=[
            pl.BlockSpec((ks, m, m), lambda k: (k, 0, 0)),
            pl.BlockSpec((ks, m, m), lambda k: (k, 0, 0)),
            pl.BlockSpec((ks, 2, n2, c), lambda k: (k, 0, 0, order)),
            pl.BlockSpec((b, 2, ks, n2, c), lambda k: (0, 0, k, 0, 0)),
        ],
        out_specs=pl.BlockSpec((b, 2, ks, n2, c), lambda k: (0, 0, k, 0, 0)),
        out_shape=jax.ShapeDtypeStruct(a.shape, BF16),
        compiler_params=_params(("parallel",)),
        name="slab_conv",
    )(mf, mi, kspec, a)


def _slab_spec_body(mf_ref, sum_ref, a_ref, k_ref):
    _, ks, n2, c = a_ref.shape
    half = c // 2
    inv = 1.0 / (sum_ref[:, :half] + sum_ref[:, half:] + 1e-6)
    for kk in range(ks):
        x = jnp.dot(mf_ref[kk], a_ref[:, kk].reshape(2 * n2, c), preferred_element_type=F32)
        k_ref[kk, 0] = ((x[:n2, :half] + x[:n2, half:]) * inv).astype(BF16)
        k_ref[kk, 1] = ((x[n2:, :half] - x[n2:, half:]) * inv).astype(BF16)


def _slab_spec(mf, sums, a):
    _, _, n1, n2, c = a.shape
    m = 2 * n2
    ks = SPEC_SLABS_PER_STEP
    return pl.pallas_call(
        _slab_spec_body,
        grid=(n1 // ks,),
        in_specs=[
            pl.BlockSpec((ks, m, m), lambda k: (k, 0, 0)),
            pl.BlockSpec((1, c), lambda k: (0, 0)),
            pl.BlockSpec((None, 2, ks, n2, c), lambda k: (0, 0, k, 0, 0)),
        ],
        out_specs=pl.BlockSpec((ks, 2, n2, c // 2), lambda k: (k, 0, 0, 0)),
        out_shape=jax.ShapeDtypeStruct((n1, 2, n2, c // 2), BF16),
        compiler_params=_params(("parallel",)),
        name="slab_spec",
    )(mf, sums, a)


def _hyena_spectrum(l, tables, w1, b1, freq, w2, b2, w3, b3, decay):
    f1p, _, mf, _ = tables
    taps, sums = _filter_taps(l, w1, b1, freq, w2, b2, w3, b3, decay)
    a = _dft_in(f1p, taps.reshape(1, l // DFT_N2, DFT_N2, taps.shape[1]))
    return _slab_spec(mf, sums, a)


def _hyena(v, x1, x2g, kspec, skip, tables):
    f1p, gp, mf, mi = tables
    b, l, c = v.shape
    slabs = lambda u: u.reshape(b, l // DFT_N2, DFT_N2, c)
    z = slabs(v)
    for order, xg in enumerate((slabs(x1), slabs(x2g))):
        bb = _slab_conv(mf, mi, kspec, _dft_in(f1p, z), order)
        z = _dft_out(gp, bb, xg, z, skip[order])
    return z.reshape(b, l, c)


def _head_masks(rows, heads):
    lane = lax.broadcasted_iota(jnp.int32, (rows, heads * HEAD_DIM), 1)
    return [(lane >= h * HEAD_DIM) & (lane < (h + 1) * HEAD_DIM) for h in range(heads)]


def _stack_heads(q, masks):
    zero = jnp.zeros_like(q)
    return jnp.concatenate([jnp.where(m, q, zero) for m in masks], 0)


def _unstack_heads(res, masks, rows):
    out = jnp.where(masks[0], res[:rows], 0.0)
    for h in range(1, len(masks)):
        out = out + jnp.where(masks[h], res[h * rows:(h + 1) * rows], 0.0)
    return out


def _na_bias_table(rpb):
    heads = rpb.shape[0]
    c = jnp.arange(GRID_W)
    col_start = jnp.clip(c - NA_KC // 2, 0, GRID_W - NA_KC)
    col_ok = (c[None, :] >= col_start[:, None]) & (c[None, :] < col_start[:, None] + NA_KC)
    dc = jnp.clip(c[None, :] - c[:, None], -(NA_KC - 1), NA_KC - 1) + NA_KC - 1
    onehot = (dc[:, :, None] == jnp.arange(2 * NA_KC - 1)[None, None, :]).astype(F32)
    tcol = jnp.einsum("hrj,qkj->hrqk", rpb.astype(F32), onehot, precision=lax.Precision.HIGHEST)
    per_idx = [tcol[:, NA_KR - 1 - idx:2 * NA_KR - 1 - idx] for idx in range(NA_KR)]
    bias = jnp.transpose(jnp.stack(per_idx, 0), (0, 1, 3, 2, 4))
    bias = jnp.where(col_ok[None, None, :, None, :], bias * LOG2_E, NEG_INF)
    return bias.reshape(NA_KR, heads * GRID_W, NA_KR * GRID_W)


def _na_body(q_ref, k_ref, v_ref, z_ref, bias_ref, o_ref, *, rows, rb, heads):
    blk = pl.program_id(1)
    masks = _head_masks(GRID_W, heads)
    span = NA_KR * GRID_W

    def one_row(rr, carry):
        r = blk * rb + rr
        start = jnp.clip(r - NA_KR // 2, 0, rows - NA_KR)
        idx = r - start
        koff = pl.multiple_of(start * GRID_W, GRID_W)
        qoff = pl.multiple_of(rr * GRID_W, GRID_W)
        q = q_ref[pl.ds(qoff, GRID_W), :]
        kk = k_ref[pl.ds(koff, span), :]
        vv = v_ref[pl.ds(koff, span), :]
        s = lax.dot_general(_stack_heads(q, masks), kk, (((1,), (1,)), ((), ())),
                            preferred_element_type=F32)
        s = s + bias_ref[idx]
        m = jnp.max(s, -1, keepdims=True)
        e = jnp.exp2(s - m)
        p = (e / jnp.sum(e, -1, keepdims=True)).astype(BF16)
        o = _unstack_heads(jnp.dot(p, vv, preferred_element_type=F32), masks, GRID_W)
        z = z_ref[pl.ds(qoff, GRID_W), :].astype(F32)
        o_ref[pl.ds(qoff, GRID_W), :] = (o * _silu(z)).astype(BF16)
        return carry

    lax.fori_loop(0, rb, one_row, 0, unroll=NA_ROW_UNROLL)


def _na(proj_b, bias):
    b, l, n = proj_b.shape
    db = n // 4
    heads = db // HEAD_DIM
    rows = l // GRID_W
    rb = NA_ROWS_PER_STEP
    t = rb * GRID_W
    return pl.pallas_call(
        functools.partial(_na_body, rows=rows, rb=rb, heads=heads),
        grid=(b, rows // rb),
        in_specs=[
            pl.BlockSpec((None, t, db), lambda i, j: (i, j, 0)),
            pl.BlockSpec((None, l, db), lambda i, j: (i, 0, 1)),
            pl.BlockSpec((None, l, db), lambda i, j: (i, 0, 2)),
            pl.BlockSpec((None, t, db), lambda i, j: (i, j, 3)),
            pl.BlockSpec(bias.shape, lambda i, j: (0, 0, 0)),
        ],
        out_specs=pl.BlockSpec((None, t, db), lambda i, j: (i, j, 0)),
        out_shape=jax.ShapeDtypeStruct((b, l, db), BF16),
        compiler_params=_params(("parallel", "parallel")),
        name="na",
    )(proj_b, proj_b, proj_b, proj_b, bias)


def _attend(q, kk, vv, valid, masks):
    rows = q.shape[0]
    s = lax.dot_general(_stack_heads(q, masks), kk, (((1,), (1,)), ((), ())), preferred_element_type=F32)
    s = jnp.where(valid, s, NEG_INF)
    m = jnp.max(s, -1, keepdims=True)
    e = jnp.exp2(s - m)
    l = jnp.sum(e, -1, keepdims=True)
    p = (e / l).astype(BF16)
    o = _unstack_heads(jnp.dot(p, vv, preferred_element_type=F32), masks, rows)
    lse = _unstack_heads(jnp.broadcast_to(m + jnp.log2(l), (s.shape[0], q.shape[1])), masks, rows)
    return o, lse


def _split_pattern(dilation, n, nchunks, q_ref, kbuf, vbuf, o_split, l_split, masks, heads):
    blk = DIL_BLK
    dc = heads * HEAD_DIM
    phases = DIL_SPLIT // dilation
    mq = blk // phases
    row = lax.broadcasted_iota(jnp.int32, (heads * blk, 3 * blk), 0) % blk
    col = lax.broadcasted_iota(jnp.int32, (heads * blk, 3 * blk), 1)
    qa, qm = row // mq, row % mq
    ka = sum((col >= a * 3 * mq).astype(jnp.int32) for a in range(1, phases)) if phases > 1 else 0
    km = col - ka * (3 * mq)
    band = jnp.abs(phases * (km - mq - qm) + (ka - qa)) <= blk
    for m0 in range(0, blk, mq):
        gm = km + (n * blk + m0 - mq)
        valid = band & (gm >= 0) & (gm < nchunks * blk)
        for j in range(dilation):
            lanes = [slice((j + dilation * a) * dc, (j + dilation * a + 1) * dc) for a in range(phases)]
            q = jnp.concatenate([q_ref[m0:m0 + mq, ln] for ln in lanes], 0)
            krows = slice(blk + m0 - mq, blk + m0 + 2 * mq)
            kk = jnp.concatenate([kbuf[krows, ln] for ln in lanes], 0)
            vv = jnp.concatenate([vbuf[krows, ln] for ln in lanes], 0)
            o, lse = _attend(q, kk, vv, valid, masks)
            for a in range(phases):
                o_split[j + dilation * a, m0:m0 + mq, :] = o[a * mq:(a + 1) * mq]
                l_split[j + dilation * a, m0:m0 + mq, :] = lse[a * mq:(a + 1) * mq]


def _dil_body(qn_ref, knp_ref, kn_ref, knn_ref, vnp_ref, vn_ref, vnn_ref,
              qs_ref, ksp_ref, ks_ref, ksn_ref, vsp_ref, vs_ref, vsn_ref, cz_ref, y_ref,
              kbn, vbn, kbs, vbs, o_split, l_split, o_all, l_all, *, heads, nchunks):
    n = pl.program_id(1)
    blk = DIL_BLK
    chunk = qn_ref.shape[0]
    per = chunk // blk
    masks = _head_masks(blk, heads)
    for buf, prev, cur, nxt, halo in ((kbn, knp_ref, kn_ref, knn_ref, blk), (vbn, vnp_ref, vn_ref, vnn_ref, blk),
                                      (kbs, ksp_ref, ks_ref, ksn_ref, blk), (vbs, vsp_ref, vs_ref, vsn_ref, blk)):
        main = cur.shape[0]
        buf[0:halo] = prev[...]
        buf[halo:halo + main] = cur[...]
        buf[halo + main:] = nxt[...]

    qi = lax.broadcasted_iota(jnp.int32, (heads * blk, 3 * blk), 0) % blk
    ki = lax.broadcasted_iota(jnp.int32, (heads * blk, 3 * blk), 1)
    band = jnp.abs(ki - blk - qi) <= blk

    def token_block(i, carry):
        g = n * per + i
        off = pl.multiple_of(i * blk, blk)
        valid = band & ((ki >= blk) | (g > 0)) & ((ki < 2 * blk) | (g < nchunks * per - 1))
        o, lse = _attend(qn_ref[pl.ds(off, blk), :], kbn[pl.ds(off, 3 * blk), :], vbn[pl.ds(off, 3 * blk), :],
                         valid, masks)
        o_all[0, pl.ds(off, blk), :] = o
        l_all[0, pl.ds(off, blk), :] = lse
        return carry

    lax.fori_loop(0, per, token_block, 0, unroll=True)

    for g, (_, dilation) in enumerate(DIL_PATTERNS):
        if dilation == 1:
            continue
        _split_pattern(dilation, n, nchunks, qs_ref, kbs, vbs, o_split, l_split, masks, heads)
        o_all[g] = jnp.swapaxes(o_split[...], 0, 1).reshape(chunk, heads * HEAD_DIM)
        l_all[g] = jnp.swapaxes(l_split[...], 0, 1).reshape(chunk, heads * HEAD_DIM)

    ls = [l_all[g] for g in range(len(DIL_PATTERNS))]
    m = functools.reduce(jnp.maximum, ls)
    es = [jnp.exp2(l - m) for l in ls]
    den = functools.reduce(lambda a, b: a + b, es)
    o = functools.reduce(lambda a, b: a + b, [(e / den) * o_all[g] for g, e in enumerate(es)])
    y_ref[...] = (o * cz_ref[...].astype(F32)).astype(BF16)


def _dilated_mixture(qn, kn, vn, qs, ks, vs, cz):
    b, l, dc = qn.shape
    heads = dc // HEAD_DIM
    blk = DIL_BLK
    assert DIL_PATTERNS[0][1] == 1 and all(w == 2 * blk * d and DIL_SPLIT % d == 0 for w, d in DIL_PATTERNS)
    chunk = DIL_SPLIT * blk
    nchunks = l // chunk
    per = chunk // blk
    nat = pl.BlockSpec((None, chunk, dc), lambda i, n: (i, n, 0))
    nat_prev = pl.BlockSpec((None, blk, dc), lambda i, n: (i, jnp.maximum(n * per - 1, 0), 0))
    nat_next = pl.BlockSpec((None, blk, dc), lambda i, n: (i, jnp.minimum((n + 1) * per, nchunks * per - 1), 0))
    spl = pl.BlockSpec((None, blk, DIL_SPLIT * dc), lambda i, n: (i, n, 0))
    spl_prev = pl.BlockSpec((None, blk, DIL_SPLIT * dc), lambda i, n: (i, jnp.maximum(n - 1, 0), 0))
    spl_next = pl.BlockSpec((None, blk, DIL_SPLIT * dc), lambda i, n: (i, jnp.minimum(n + 1, nchunks - 1), 0))
    return pl.pallas_call(
        functools.partial(_dil_body, heads=heads, nchunks=nchunks),
        grid=(b, nchunks),
        in_specs=[nat, nat_prev, nat, nat_next, nat_prev, nat, nat_next,
                  spl, spl_prev, spl, spl_next, spl_prev, spl, spl_next, nat],
        out_specs=nat,
        out_shape=jax.ShapeDtypeStruct((b, l, dc), BF16),
        scratch_shapes=[pltpu.VMEM((chunk + 2 * blk, dc), BF16), pltpu.VMEM((chunk + 2 * blk, dc), BF16),
                        pltpu.VMEM((3 * blk, DIL_SPLIT * dc), BF16), pltpu.VMEM((3 * blk, DIL_SPLIT * dc), BF16),
                        pltpu.VMEM((DIL_SPLIT, blk, dc), F32), pltpu.VMEM((DIL_SPLIT, blk, dc), F32),
                        pltpu.VMEM((len(DIL_PATTERNS), chunk, dc), F32),
                        pltpu.VMEM((len(DIL_PATTERNS), chunk, dc), F32)],
        compiler_params=_params(("parallel", "parallel")),
        name="dilated",
    )(qn, kn, kn, kn, vn, vn, vn, qs, ks, ks, ks, vs, vs, vs, cz)


def _tail_body(x_ref, ya_ref, yb_ref, yc_ref, g_ref, gate_ref, wa_ref, wb_ref, wc_ref, wo_ref, lng_ref, lnb_ref,
               out_ref, *, alpha):
    d = x_ref.shape[-1]
    pa = jnp.dot(ya_ref[...], wa_ref[...], preferred_element_type=F32)
    pb = jnp.dot(yb_ref[...], wb_ref[...], preferred_element_type=F32)
    pc = jnp.dot(yc_ref[...], wc_ref[...], preferred_element_type=F32)
    g = g_ref[...].astype(F32)
    merged = g[:, :d] * pa + g[:, d:2 * d] * pb + g[:, 2 * d:] * pc
    sub = jnp.dot(merged.astype(BF16), wo_ref[...], preferred_element_type=F32) * gate_ref[...]
    res = alpha * x_ref[...] + sub
    out_ref[...] = _layernorm(res) * lng_ref[...] + lnb_ref[...]


def _tail(x, ya, yb, yc, g_all, gate, wa, wb, wc, wo, ln_g, ln_b, alpha):
    b, l, d = x.shape
    t = TOKEN_TILE
    tok = lambda w: pl.BlockSpec((None, t, w), lambda i, j: (i, j, 0))
    const = lambda a: pl.BlockSpec(a.shape, lambda i, j: (0,) * a.ndim)
    ln_g, ln_b = ln_g.reshape(1, d), ln_b.reshape(1, d)
    return pl.pallas_call(
        functools.partial(_tail_body, alpha=alpha),
        grid=(b, l // t),
        in_specs=[tok(d), tok(ya.shape[-1]), tok(yb.shape[-1]), tok(yc.shape[-1]), tok(3 * d),
                  pl.BlockSpec((None, 1, d), lambda i, j: (i, 0, 0)),
                  const(wa), const(wb), const(wc), const(wo), const(ln_g), const(ln_b)],
        out_specs=tok(d),
        out_shape=jax.ShapeDtypeStruct((b, l, d), F32),
        compiler_params=_params(("parallel", "parallel")),
        name="tail",
    )(x, ya, yb, yc, g_all, gate.reshape(b, 1, d), wa, wb, wc, wo, ln_g, ln_b)


def _rope_tables(l, heads):
    half = HEAD_DIM // 2
    inv = ROPE_THETA ** (-jnp.arange(half, dtype=F32) / half)
    ang = jnp.arange(l, dtype=F32)[:, None] * inv[None, :]
    cos, sin = jnp.cos(ang), jnp.sin(ang)
    return (jnp.tile(jnp.concatenate([cos, cos], -1), (1, heads)),
            jnp.tile(jnp.concatenate([-sin, sin], -1), (1, heads)))


def _layer(x, ada, lw, consts, alpha):
    d = x.shape[-1]
    shift, scale, gate = ada[:, :d], ada[:, d:2 * d], ada[:, 2 * d:]
    h = _ln_mod(x, scale, shift)
    w_in, b_in = lw["w_in"], lw["b_in"]
    proj_a = _proj(h, w_in[:, :2 * d], b_in[:2 * d])
    q_cols = jnp.arange(d) < d // 4
    proj_b = _proj(h, w_in[:, 2 * d:3 * d], b_in[2 * d:3 * d], col_scale=jnp.where(q_cols, QK_SCALE_LOG2, 1.0))
    cq, ck, cv, cz, cqs, cks, cvs = _proj_rope(h, w_in[:, 3 * d:4 * d], b_in[3 * d:4 * d], *consts["rope"])
    gates = _proj(h, w_in[:, 4 * d:], b_in[4 * d:], sigmoid=True)

    v, x1, x2g = _hy_pre(proj_a, lw["conv_w"], lw["conv_b"])
    ya = _hyena(v, x1, x2g, consts["kspec"], lw["skip"], consts["dft"])
    yb = _na(proj_b, lw["na_bias"])
    yc = _dilated_mixture(cq, ck, cv, cqs, cks, cvs, cz)
    return _tail(x, ya, yb, yc, gates, gate, lw["wa"], lw["wb"], lw["wc"], lw["wo"],
                 lw["ln_g"], lw["ln_b"], alpha)


def kernel(x_prompt, x_sample, c_prompt, c_sample, w_ada, b_ada, w_in, b_in, hy_conv_w, hy_conv_b, hy_w1, hy_b1, hy_freq, hy_w2, hy_b2, hy_w3, hy_b3, hy_decay, hy_skip, na_rpb, w_branch_a, w_branch_b, w_branch_c, w_out, ln_g, ln_b):
    depth, d, _ = w_in.shape
    heads_c = (d // 4) // HEAD_DIM
    alpha = (2 * depth) ** 0.25
    groups = [(x_prompt, c_prompt), (x_sample, c_sample)]

    nb_p = c_prompt.shape[0]
    c_all = jnp.concatenate([c_prompt, c_sample], 0)
    pad_rows = -c_all.shape[0] % 8
    ada_all = _ada(jnp.pad(c_all, ((0, pad_rows), (0, 0))), w_ada, b_ada)
    adas = [ada_all[:, :nb_p], ada_all[:, nb_p:nb_p + c_sample.shape[0]]]

    shared = {}
    for x, _ in groups:
        l = x.shape[1]
        if l not in shared:
            shared[l] = {"dft": _dft_tables(l), "rope": _rope_tables(l, heads_c)}

    ys = [x for x, _ in groups]
    for layer in range(depth):
        lw = {
            "w_in": w_in[layer].astype(BF16), "b_in": b_in[layer],
            "conv_w": hy_conv_w[layer], "conv_b": hy_conv_b[layer], "skip": hy_skip[layer],
            "na_bias": _na_bias_table(na_rpb[layer]),
            "wa": w_branch_a[layer].astype(BF16), "wb": w_branch_b[layer].astype(BF16),
            "wc": w_branch_c[layer].astype(BF16), "wo": w_out[layer].astype(BF16),
            "ln_g": ln_g[layer], "ln_b": ln_b[layer],
        }
        kspecs = {}
        for gi in range(len(groups)):
            l = ys[gi].shape[1]
            if l not in kspecs:
                kspecs[l] = _hyena_spectrum(l, shared[l]["dft"], hy_w1[layer], hy_b1[layer], hy_freq[layer],
                                            hy_w2[layer], hy_b2[layer], hy_w3[layer], hy_b3[layer],
                                            hy_decay[layer])
            consts = dict(shared[l], kspec=kspecs[l])
            ys[gi] = _layer(ys[gi], adas[gi][layer], lw, consts, alpha)
    return tuple(ys)
```

```python
import functools
import math

import jax
import jax.numpy as jnp
from jax import lax
from jax.experimental import pallas as pl
from jax.experimental.pallas import tpu as pltpu

F32 = jnp.float32
BF16 = jnp.bfloat16

GRID_W = 64
HEAD_DIM = 64
HYENA_BANDS = 16
HYENA_EMB = 2 * HYENA_BANDS + 1
NA_KR = 8
NA_KC = 16
DIL_PATTERNS = ((128, 1), (512, 4), (2048, 16))
DIL_BLK = 64
DIL_SPLIT = 16
NA_ROW_UNROLL = 32
ROPE_THETA = 10000.0
LN_EPS = 1e-5
NEG_INF = -1e30
LOG2_E = math.log2(math.e)
QK_SCALE_LOG2 = HEAD_DIM ** -0.5 * LOG2_E

V7X_LANES = 128
V7X_SUBLANES = 8
V7X_VMEM_LIMIT_BYTES = 56 * 1024 * 1024

DFT_N2 = V7X_LANES
PAIR_GROUP = V7X_SUBLANES
DFT_STEP_ROWS = 128
SLABS_PER_STEP = 16
FEAT_PAD = V7X_LANES
BF16_TILE_ROWS = 2 * V7X_SUBLANES

TOKEN_TILE = 1024
WIDE_TOKEN_TILE = 2048
PROJ_COL_TILE = 1024
ROPE_TOKEN_TILE = 1024
FILTER_ROW_TILE = 1024
NA_ROWS_PER_STEP = 32
SPEC_SLABS_PER_STEP = 4
HY_SHIFT_ROWS = 256


def _params(sem):
    return pltpu.CompilerParams(dimension_semantics=sem, vmem_limit_bytes=V7X_VMEM_LIMIT_BYTES)


def _sigmoid(x):
    return 1.0 / (1.0 + jnp.exp(-x))


def _silu(x):
    return x * _sigmoid(x)


def _ada_body(c_ref, w_ref, b_ref, o_ref):
    s = _silu(c_ref[...])
    o_ref[...] = jnp.dot(s, w_ref[...], preferred_element_type=F32,
                         precision=lax.Precision.HIGHEST) + b_ref[...]


def _ada(c_all, w_ada, b_ada):
    depth, d, n = w_ada.shape
    rows = c_all.shape[0]
    tn = PROJ_COL_TILE
    return pl.pallas_call(
        _ada_body,
        grid=(depth, n // tn),
        in_specs=[
            pl.BlockSpec((rows, d), lambda l, j: (0, 0)),
            pl.BlockSpec((None, d, tn), lambda l, j: (l, 0, j)),
            pl.BlockSpec((None, 1, tn), lambda l, j: (l, 0, j)),
        ],
        out_specs=pl.BlockSpec((None, rows, tn), lambda l, j: (l, 0, j)),
        out_shape=jax.ShapeDtypeStruct((depth, rows, n), F32),
        compiler_params=_params(("parallel", "parallel")),
        name="ada",
    )(c_all, w_ada, b_ada.reshape(depth, 1, n))


def _layernorm(x):
    mu = jnp.mean(x, -1, keepdims=True)
    xc = x - mu
    var = jnp.mean(xc * xc, -1, keepdims=True)
    return xc * lax.rsqrt(var + LN_EPS)


def _ln_mod_body(x_ref, sc_ref, sh_ref, o_ref):
    h = _layernorm(x_ref[...]) * (1.0 + sc_ref[...]) + sh_ref[...]
    o_ref[...] = h.astype(BF16)


def _ln_mod(x, scale, shift):
    b, l, d = x.shape
    t = WIDE_TOKEN_TILE
    return pl.pallas_call(
        _ln_mod_body,
        grid=(b, l // t),
        in_specs=[
            pl.BlockSpec((None, t, d), lambda i, j: (i, j, 0)),
            pl.BlockSpec((None, 1, d), lambda i, j: (i, 0, 0)),
            pl.BlockSpec((None, 1, d), lambda i, j: (i, 0, 0)),
        ],
        out_specs=pl.BlockSpec((None, t, d), lambda i, j: (i, j, 0)),
        out_shape=jax.ShapeDtypeStruct((b, l, d), BF16),
        compiler_params=_params(("parallel", "parallel")),
        name="ln_mod",
    )(x, scale.reshape(b, 1, d), shift.reshape(b, 1, d))


def _proj_body(h_ref, w_ref, b_ref, *rest, sigmoid, scaled):
    o_ref = rest[-1]
    acc = jnp.dot(h_ref[...], w_ref[...], preferred_element_type=F32) + b_ref[...]
    if scaled:
        acc = acc * rest[0][...]
    if sigmoid:
        acc = 0.5 * jnp.tanh(0.5 * acc) + 0.5
    o_ref[...] = acc.astype(o_ref.dtype)


def _proj(h, w, bias, col_scale=None, sigmoid=False):
    b, l, d = h.shape
    n = w.shape[1]
    t, tn = TOKEN_TILE, PROJ_COL_TILE
    row = pl.BlockSpec((1, tn), lambda j, i, k: (0, j))
    scale_args = [] if col_scale is None else [col_scale.reshape(1, n)]
    return pl.pallas_call(
        functools.partial(_proj_body, sigmoid=sigmoid, scaled=col_scale is not None),
        grid=(n // tn, b, l // t),
        in_specs=[
            pl.BlockSpec((None, t, d), lambda j, i, k: (i, k, 0)),
            pl.BlockSpec((d, tn), lambda j, i, k: (0, j)),
            row,
        ] + [row] * len(scale_args),
        out_specs=pl.BlockSpec((None, t, tn), lambda j, i, k: (i, k, j)),
        out_shape=jax.ShapeDtypeStruct((b, l, n), BF16),
        compiler_params=_params(("parallel", "parallel", "parallel")),
        name="proj",
    )(h, w, bias.reshape(1, n), *scale_args)


def _rope_lanes(x, cos, sin_signed):
    outs = []
    lane = lax.broadcasted_iota(jnp.int32, (x.shape[0], V7X_LANES), 1)
    first_half = (lane % HEAD_DIM) < (HEAD_DIM // 2)
    for c0 in range(0, x.shape[1], V7X_LANES):
        xc = x[:, c0:c0 + V7X_LANES]
        partner = jnp.where(first_half,
                            pltpu.roll(xc, V7X_LANES - HEAD_DIM // 2, 1),
                            pltpu.roll(xc, HEAD_DIM // 2, 1))
        outs.append(xc * cos[:, c0:c0 + V7X_LANES] + partner * sin_signed[:, c0:c0 + V7X_LANES])
    return jnp.concatenate(outs, 1)


def _store_split(ref, x):
    t, dc = x.shape
    parts = jnp.swapaxes(x.reshape(t // DIL_SPLIT, DIL_SPLIT, dc), 0, 1)
    for r in range(DIL_SPLIT):
        ref[:, r * dc:(r + 1) * dc] = parts[r].astype(BF16)


def _proj_rope_body(h_ref, w_ref, b_ref, cos_ref, sin_ref, q_ref, k_ref, v_ref, z_ref,
                    qs_ref, ks_ref, vs_ref):
    dc = q_ref.shape[-1]
    acc = jnp.dot(h_ref[...], w_ref[...], preferred_element_type=F32) + b_ref[...]
    cos, sin = cos_ref[...], sin_ref[...]
    q = _rope_lanes(acc[:, :dc], cos, sin) * QK_SCALE_LOG2
    k = _rope_lanes(acc[:, dc:2 * dc], cos, sin)
    v = acc[:, 2 * dc:3 * dc]
    for ref, split_ref, val in ((q_ref, qs_ref, q), (k_ref, ks_ref, k), (v_ref, vs_ref, v)):
        ref[...] = val.astype(BF16)
        _store_split(split_ref, val)
    z_ref[...] = _silu(acc[:, 3 * dc:]).astype(BF16)


def _proj_rope(h, w, bias, cos_t, sin_t):
    b, l, d = h.shape
    n = w.shape[1]
    dc = n // 4
    t = ROPE_TOKEN_TILE
    tok = pl.BlockSpec((None, t, dc), lambda i, k: (i, k, 0))
    shp = jax.ShapeDtypeStruct((b, l, dc), BF16)
    spl = pl.BlockSpec((None, t // DIL_SPLIT, DIL_SPLIT * dc), lambda i, k: (i, k, 0))
    spl_shp = jax.ShapeDtypeStruct((b, l // DIL_SPLIT, DIL_SPLIT * dc), BF16)
    return pl.pallas_call(
        _proj_rope_body,
        grid=(b, l // t),
        in_specs=[
            pl.BlockSpec((None, t, d), lambda i, k: (i, k, 0)),
            pl.BlockSpec((d, n), lambda i, k: (0, 0)),
            pl.BlockSpec((1, n), lambda i, k: (0, 0)),
            pl.BlockSpec((t, dc), lambda i, k: (k, 0)),
            pl.BlockSpec((t, dc), lambda i, k: (k, 0)),
        ],
        out_specs=[tok, tok, tok, tok, spl, spl, spl],
        out_shape=[shp, shp, shp, shp, spl_shp, spl_shp, spl_shp],
        compiler_params=_params(("parallel", "parallel")),
        name="proj_rope",
    )(h, w, bias.reshape(1, n), cos_t, sin_t)


def _hy_pre_body(main_ref, prev_ref, next_ref, shift_ref, cw_ref, cb_ref, v_ref, x1_ref, x2_ref, *, da):
    i = pl.program_id(1)
    last = pl.num_programs(1) - 1
    t = main_ref.shape[0]
    halo = prev_ref.shape[0]
    blk = shift_ref.shape[1]
    nc = 3 * da
    outs = (v_ref, x1_ref, x2_ref)
    edge = lax.broadcasted_iota(jnp.int32, (V7X_SUBLANES, nc), 0)
    for r0 in range(0, t, blk):
        xb = main_ref[r0:r0 + blk, :nc]
        sh = jnp.dot(shift_ref[...], xb, preferred_element_type=F32)
        if r0 == 0:
            prev_row = jnp.where(i > 0, prev_ref[:, :nc].astype(F32)[halo - 1:halo], 0.0)
        else:
            prev_row = main_ref[r0 - halo:r0, :nc].astype(F32)[halo - 1:halo]
        if r0 + blk == t:
            next_row = jnp.where(i < last, next_ref[:, :nc].astype(F32)[0:1], 0.0)
        else:
            next_row = main_ref[r0 + blk:r0 + blk + halo, :nc].astype(F32)[0:1]
        up, dn = sh[:blk], sh[blk:]
        up = jnp.concatenate([up[:V7X_SUBLANES] + jnp.where(edge == 0, prev_row, 0.0), up[V7X_SUBLANES:]], 0)
        dn = jnp.concatenate([dn[:blk - V7X_SUBLANES],
                              dn[blk - V7X_SUBLANES:] + jnp.where(edge == V7X_SUBLANES - 1, next_row, 0.0)], 0)
        uc = up * cw_ref[0:1, :] + xb.astype(F32) * cw_ref[1:2, :] + dn * cw_ref[2:3, :] + cb_ref[...]
        for part in range(3):
            val = uc[:, part * da:(part + 1) * da]
            if part == 2:
                val = val * _silu(main_ref[r0:r0 + blk, nc:].astype(F32))
            outs[part][r0:r0 + blk, :] = val.astype(BF16)


def _hy_pre(proj_a, conv_w, conv_b):
    b, l, n = proj_a.shape
    da = n // 4
    t, halo = WIDE_TOKEN_TILE, BF16_TILE_ROWS
    nh = t // halo
    blk = HY_SHIFT_ROWS
    shift = jnp.concatenate([jnp.eye(blk, k=-1, dtype=BF16), jnp.eye(blk, k=1, dtype=BF16)], 0)
    tok = pl.BlockSpec((None, t, da), lambda i, j: (i, j, 0))
    shp = jax.ShapeDtypeStruct((b, l, da), BF16)
    return pl.pallas_call(
        functools.partial(_hy_pre_body, da=da),
        grid=(b, l // t),
        in_specs=[
            pl.BlockSpec((None, t, n), lambda i, j: (i, j, 0)),
            pl.BlockSpec((None, halo, n), lambda i, j: (i, jnp.maximum(j * nh - 1, 0), 0)),
            pl.BlockSpec((None, halo, n), lambda i, j: (i, jnp.minimum((j + 1) * nh, l // halo - 1), 0)),
            pl.BlockSpec((2 * blk, blk), lambda i, j: (0, 0)),
            pl.BlockSpec((3, 3 * da), lambda i, j: (0, 0)),
            pl.BlockSpec((1, 3 * da), lambda i, j: (0, 0)),
        ],
        out_specs=[tok, tok, tok],
        out_shape=[shp, shp, shp],
        compiler_params=_params(("parallel", "parallel")),
        name="hy_pre",
    )(proj_a, proj_a, proj_a, shift, conv_w, conv_b.reshape(1, 3 * da))


def _filter_body(feat_ref, w1_ref, b1_ref, f0_ref, w2_ref, b2_ref, f1_ref, w3_ref, b3_ref, dec_ref,
                 hf_ref, sum_ref):
    i = pl.program_id(0)
    hp = lax.Precision.HIGHEST
    half = feat_ref.shape[0] // 2
    n = hf_ref.shape[1]
    fa, fb = feat_ref[0:half, :], feat_ref[half:, :]
    feat = jnp.concatenate([fa, fb], 1)
    h = jnp.sin(f0_ref[...] * (jnp.dot(feat, w1_ref[...], preferred_element_type=F32, precision=hp)
                               + b1_ref[...]))
    h = jnp.sin(f1_ref[...] * (jnp.dot(h, w2_ref[...], preferred_element_type=F32, precision=hp)
                               + b2_ref[...]))
    h = jnp.dot(h.astype(BF16), w3_ref[...], preferred_element_type=F32) + b3_ref[...]
    t = jnp.concatenate([jnp.broadcast_to(fa[:, 0:1], (half, n)), jnp.broadcast_to(fb[:, 0:1], (half, n))], 1)
    h = h * jnp.exp(-t * jnp.abs(dec_ref[...]))
    hf_ref[0:half, :] = h[:, :n].astype(BF16)
    hf_ref[half:, :] = h[:, n:].astype(BF16)
    col_sum = jnp.sum(jnp.abs(h), 0, keepdims=True)
    abs_sum = col_sum[:, :n] + col_sum[:, n:]

    @pl.when(i > 0)
    def _():
        sum_ref[...] += abs_sum

    @pl.when(i == 0)
    def _():
        head = h[0:BF16_TILE_ROWS, :n]
        rows = lax.broadcasted_iota(jnp.int32, head.shape, 0)
        cols = lax.broadcasted_iota(jnp.int32, head.shape, 1)
        drop = (rows == 0) & (cols >= n // 2)
        hf_ref[0:BF16_TILE_ROWS, :] = jnp.where(drop, 0.0, head).astype(BF16)
        sum_ref[...] = abs_sum - jnp.sum(jnp.where(drop, jnp.abs(head), 0.0), 0, keepdims=True)


def _block_diag2(w):
    z = jnp.zeros_like(w)
    return jnp.concatenate([jnp.concatenate([w, z], 1), jnp.concatenate([z, w], 1)], 0)


def _filter_taps(l, w1, b1, freq, w2, b2, w3, b3, decay):
    fo = w1.shape[1]
    n = w3.shape[1]
    da = decay.shape[0]
    t = jnp.arange(l, dtype=F32) / l
    bands = jnp.arange(1, HYENA_BANDS + 1, dtype=F32)
    ang = 2.0 * math.pi * t[:, None] * bands[None, :]
    feat = jnp.concatenate([t[:, None], jnp.cos(ang), jnp.sin(ang)], -1)
    feat = jnp.pad(feat, ((0, 0), (0, FEAT_PAD - HYENA_EMB)))
    w1p = jnp.pad(w1, ((0, FEAT_PAD - HYENA_EMB), (0, 0)))
    twice = lambda v: jnp.tile(v.reshape(1, -1), (1, 2))
    dec = jnp.tile(decay, 2 * n // da).reshape(1, 2 * n)
    tt = FILTER_ROW_TILE
    const = lambda shape: pl.BlockSpec(shape, lambda i: (0,) * len(shape))
    return pl.pallas_call(
        _filter_body,
        grid=(l // tt,),
        in_specs=[
            pl.BlockSpec((tt, FEAT_PAD), lambda i: (i, 0)),
            const((2 * FEAT_PAD, 2 * fo)), const((1, 2 * fo)), const((1, 2 * fo)),
            const((2 * fo, 2 * fo)), const((1, 2 * fo)), const((1, 2 * fo)),
            const((2 * fo, 2 * n)), const((1, 2 * n)), const((1, 2 * n)),
        ],
        out_specs=[pl.BlockSpec((tt, n), lambda i: (i, 0)), const((1, n))],
        out_shape=[jax.ShapeDtypeStruct((l, n), BF16), jax.ShapeDtypeStruct((1, n), F32)],
        compiler_params=_params(("arbitrary",)),
        name="filter_taps",
    )(feat, _block_diag2(w1p), twice(b1), twice(freq[0]), _block_diag2(w2), twice(b2), twice(freq[1]),
      _block_diag2(w3.astype(BF16)), twice(b3), dec)


def _dft_tables(l):
    n = 2 * l
    n1 = n // DFT_N2
    kk = jnp.arange(n1 // 2, dtype=jnp.int32)
    nn = jnp.arange(n1 // 2, dtype=jnp.int32)
    th = (2.0 * math.pi / (2 * n1)) * (((2 * kk[:, None] + 1) * nn[None, :]) % (2 * n1)).astype(F32)
    eye2 = jnp.eye(2, dtype=F32)
    f1 = jnp.kron(jnp.concatenate([jnp.cos(th), -jnp.sin(th)], 0), eye2).astype(BF16)
    g = jnp.kron(jnp.concatenate([jnp.cos(th).T, -jnp.sin(th).T], 1) * (2.0 / n), eye2).astype(BF16)
    k2 = jnp.arange(DFT_N2, dtype=jnp.int32)
    n2 = jnp.arange(DFT_N2, dtype=jnp.int32)
    ph = (n2[None, None, :] * (k2[None, :, None] * (2 * n1) + 2 * kk[:, None, None] + 1)) % (2 * n)
    ang = (2.0 * math.pi / (2 * n)) * ph.astype(F32)
    c, s = jnp.cos(ang), jnp.sin(ang)
    mf = jnp.concatenate([jnp.concatenate([c, s], 2), jnp.concatenate([-s, c], 2)], 1).astype(BF16)
    return f1, g, mf


def _load_pair_group(ref, lead, g):
    start = pl.multiple_of(g * PAIR_GROUP, PAIR_GROUP)
    words = ref.bitcast(jnp.uint32)[(*lead, slice(None), pl.ds(start, PAIR_GROUP), slice(None))]
    words = jnp.swapaxes(words, 0, 1)
    return [pltpu.bitcast(words[i], BF16) for i in range(PAIR_GROUP)]


def _store_pair_group(ref, lead, g, vals):
    start = pl.multiple_of(g * PAIR_GROUP, PAIR_GROUP)
    words = jnp.stack([pltpu.bitcast(v, jnp.uint32) for v in vals], 0)
    ref.bitcast(jnp.uint32)[(*lead, slice(None), pl.ds(start, PAIR_GROUP), slice(None))] = (
        jnp.swapaxes(words, 0, 1))


def _dft_in_body(f_ref, z_ref, a_ref):
    n1 = a_ref.shape[2]
    cb = a_ref.shape[-1]

    def group(g, carry):
        zcat = jnp.concatenate(_load_pair_group(z_ref, (0,), g), 1)
        r = jnp.dot(f_ref[...], zcat, preferred_element_type=F32).astype(BF16)
        cols = [r[:, i * cb:(i + 1) * cb] for i in range(PAIR_GROUP)]
        _store_pair_group(a_ref, (0, 0), g, [c[:2 * n1] for c in cols])
        _store_pair_group(a_ref, (0, 1), g, [c[2 * n1:] for c in cols])
        return carry

    lax.fori_loop(0, z_ref.shape[2] // (2 * PAIR_GROUP), group, 0, unroll=4)


def _dft_in(f1p, z):
    b, half, n2, c = z.shape
    n1 = half
    cb, rs = V7X_LANES, DFT_STEP_ROWS
    return pl.pallas_call(
        _dft_in_body,
        grid=(b, c // cb, n2 // rs),
        in_specs=[
            pl.BlockSpec(f1p.shape, lambda i, j, s: (0, 0)),
            pl.BlockSpec((1, half, rs, cb), lambda i, j, s: (i, 0, s, j)),
        ],
        out_specs=pl.BlockSpec((1, 2, n1, rs, cb), lambda i, j, s: (i, 0, 0, s, j)),
        out_shape=jax.ShapeDtypeStruct((b, 2, n1, n2, c), BF16),
        compiler_params=_params(("parallel", "parallel", "parallel")),
        name="dft_in",
    )(f1p, z)


def _dft_out_body(g_ref, b_ref, x_ref, z_ref, skip_ref, o_ref):
    cb = o_ref.shape[-1]

    def group(g, carry):
        re, im = _load_pair_group(b_ref, (0, 0), g), _load_pair_group(b_ref, (0, 1), g)
        bcat = jnp.concatenate([jnp.concatenate([r, i], 0) for r, i in zip(re, im)], 1)
        y = jnp.dot(g_ref[...], bcat, preferred_element_type=F32)
        xs, zs = _load_pair_group(x_ref, (0,), g), _load_pair_group(z_ref, (0,), g)
        outs = [(xs[i].astype(F32) * (y[:, i * cb:(i + 1) * cb] + skip_ref[...] * zs[i].astype(F32))
                 ).astype(BF16) for i in range(PAIR_GROUP)]
        _store_pair_group(o_ref, (0,), g, outs)
        return carry

    lax.fori_loop(0, x_ref.shape[2] // (2 * PAIR_GROUP), group, 0, unroll=2)


def _dft_out(gp, bb, x, z, skip):
    b, half, n2, c = z.shape
    n1 = half
    cb, rs = V7X_LANES, DFT_STEP_ROWS
    slab = pl.BlockSpec((1, half, rs, cb), lambda i, j, s: (i, 0, s, j))
    return pl.pallas_call(
        _dft_out_body,
        grid=(b, c // cb, n2 // rs),
        in_specs=[
            pl.BlockSpec(gp.shape, lambda i, j, s: (0, 0)),
            pl.BlockSpec((1, 2, n1, rs, cb), lambda i, j, s: (i, 0, 0, s, j)),
            slab, slab,
            pl.BlockSpec((1, cb), lambda i, j, s: (0, j)),
        ],
        out_specs=slab,
        out_shape=jax.ShapeDtypeStruct(z.shape, BF16),
        compiler_params=_params(("parallel", "parallel", "parallel")),
        name="dft_out",
    )(gp, bb, x, z, skip.astype(F32).reshape(1, c))


def _slab_conv_body(mf_ref, k_ref, a_ref, o_ref):
    nb, _, ks, n2, c = a_ref.shape
    for kk in range(ks):
        kr, ki = k_ref[kk, 0].astype(F32), k_ref[kk, 1].astype(F32)
        for b in range(nb):
            a = a_ref[b, :, kk].reshape(2 * n2, c)
            x = jnp.dot(mf_ref[kk], a, preferred_element_type=F32)
            xr, xi = x[:n2], x[n2:]
            y = jnp.concatenate([xr * kr - xi * ki, xr * ki + xi * kr], 0).astype(BF16)
            out = lax.dot_general(mf_ref[kk], y, (((0,), (0,)), ((), ())), preferred_element_type=F32)
            o_ref[b, :, kk] = out.astype(BF16).reshape(2, n2, c)


def _slab_conv(mf, kspec, a, order):
    b, _, n1, n2, c = a.shape
    m = 2 * n2
    ks = max(1, SLABS_PER_STEP // b)
    return pl.pallas_call(
        _slab_conv_body,
        grid=(n1 // ks,),
        in_specs=[
            pl.BlockSpec((ks, m, m), lambda k: (k, 0, 0)),
            pl.BlockSpec((ks, 2, n2, c), lambda k: (k, 0, 0, order)),
            pl.BlockSpec((b, 2, ks, n2, c), lambda k: (0, 0, k, 0, 0)),
        ],
        out_specs=pl.BlockSpec((b, 2, ks, n2, c), lambda k: (0, 0, k, 0, 0)),
        out_shape=jax.ShapeDtypeStruct(a.shape, BF16),
        compiler_params=_params(("parallel",)),
        name="slab_conv",
    )(mf, kspec, a)


def _slab_spec_body(mf_ref, sum_ref, a_ref, k_ref):
    _, ks, n2, c = a_ref.shape
    half = c // 2
    inv = 1.0 / (sum_ref[:, :half] + sum_ref[:, half:] + 1e-6)
    for kk in range(ks):
        x = jnp.dot(mf_ref[kk], a_ref[:, kk].reshape(2 * n2, c), preferred_element_type=F32)
        k_ref[kk, 0] = ((x[:n2, :half] + x[:n2, half:]) * inv).astype(BF16)
        k_ref[kk, 1] = ((x[n2:, :half] - x[n2:, half:]) * inv).astype(BF16)


def _slab_spec(mf, sums, a):
    _, _, n1, n2, c = a.shape
    m = 2 * n2
    ks = SPEC_SLABS_PER_STEP
    return pl.pallas_call(
        _slab_spec_body,
        grid=(n1 // ks,),
        in_specs=[
            pl.BlockSpec((ks, m, m), lambda k: (k, 0, 0)),
            pl.BlockSpec((1, c), lambda k: (0, 0)),
            pl.BlockSpec((None, 2, ks, n2, c), lambda k: (0, 0, k, 0, 0)),
        ],
        out_specs=pl.BlockSpec((ks, 2, n2, c // 2), lambda k: (k, 0, 0, 0)),
        out_shape=jax.ShapeDtypeStruct((n1, 2, n2, c // 2), BF16),
        compiler_params=_params(("parallel",)),
        name="slab_spec",
    )(mf, sums, a)


def _hyena_spectrum(l, tables, w1, b1, freq, w2, b2, w3, b3, decay):
    f1p, _, mf = tables
    taps, sums = _filter_taps(l, w1, b1, freq, w2, b2, w3, b3, decay)
    a = _dft_in(f1p, taps.reshape(1, l // DFT_N2, DFT_N2, taps.shape[1]))
    return _slab_spec(mf, sums, a)


def _hyena(v, x1, x2g, kspec, skip, tables):
    f1p, gp, mf = tables
    b, l, c = v.shape
    slabs = lambda u: u.reshape(b, l // DFT_N2, DFT_N2, c)
    z = slabs(v)
    for order, xg in enumerate((slabs(x1), slabs(x2g))):
        bb = _slab_conv(mf, kspec, _dft_in(f1p, z), order)
        z = _dft_out(gp, bb, xg, z, skip[order])
    return z.reshape(b, l, c)


def _head_masks(rows, heads):
    lane = lax.broadcasted_iota(jnp.int32, (rows, heads * HEAD_DIM), 1)
    return [(lane >= h * HEAD_DIM) & (lane < (h + 1) * HEAD_DIM) for h in range(heads)]


def _stack_heads(q, masks):
    zero = jnp.zeros_like(q)
    return jnp.concatenate([jnp.where(m, q, zero) for m in masks], 0)


def _unstack_heads(res, masks, rows):
    out = jnp.where(masks[0], res[:rows], 0.0)
    for h in range(1, len(masks)):
        out = out + jnp.where(masks[h], res[h * rows:(h + 1) * rows], 0.0)
    return out


def _na_bias_table(rpb):
    heads = rpb.shape[0]
    c = jnp.arange(GRID_W)
    col_start = jnp.clip(c - NA_KC // 2, 0, GRID_W - NA_KC)
    col_ok = (c[None, :] >= col_start[:, None]) & (c[None, :] < col_start[:, None] + NA_KC)
    dc = jnp.clip(c[None, :] - c[:, None], -(NA_KC - 1), NA_KC - 1) + NA_KC - 1
    onehot = (dc[:, :, None] == jnp.arange(2 * NA_KC - 1)[None, None, :]).astype(F32)
    tcol = jnp.einsum("hrj,qkj->hrqk", rpb.astype(F32), onehot, precision=lax.Precision.HIGHEST)
    per_idx = [tcol[:, NA_KR - 1 - idx:2 * NA_KR - 1 - idx] for idx in range(NA_KR)]
    bias = jnp.transpose(jnp.stack(per_idx, 0), (0, 1, 3, 2, 4))
    bias = jnp.where(col_ok[None, None, :, None, :], bias * LOG2_E, NEG_INF)
    return bias.reshape(NA_KR, heads * GRID_W, NA_KR * GRID_W)


def _na_body(q_ref, k_ref, v_ref, z_ref, bias_ref, o_ref, *, rows, rb, heads):
    blk = pl.program_id(1)
    masks = _head_masks(GRID_W, heads)
    span = NA_KR * GRID_W

    def one_row(rr, carry):
        r = blk * rb + rr
        start = jnp.clip(r - NA_KR // 2, 0, rows - NA_KR)
        idx = r - start
        koff = pl.multiple_of(start * GRID_W, GRID_W)
        qoff = pl.multiple_of(rr * GRID_W, GRID_W)
        q = q_ref[pl.ds(qoff, GRID_W), :]
        kk = k_ref[pl.ds(koff, span), :]
        vv = v_ref[pl.ds(koff, span), :]
        s = lax.dot_general(_stack_heads(q, masks), kk, (((1,), (1,)), ((), ())),
                            preferred_element_type=F32)
        s = s + bias_ref[idx]
        m = jnp.max(s, -1, keepdims=True)
        e = jnp.exp2(s - m)
        p = (e / jnp.sum(e, -1, keepdims=True)).astype(BF16)
        o = _unstack_heads(jnp.dot(p, vv, preferred_element_type=F32), masks, GRID_W)
        z = z_ref[pl.ds(qoff, GRID_W), :].astype(F32)
        o_ref[pl.ds(qoff, GRID_W), :] = (o * _silu(z)).astype(BF16)
        return carry

    lax.fori_loop(0, rb, one_row, 0, unroll=NA_ROW_UNROLL)


def _na(proj_b, bias):
    b, l, n = proj_b.shape
    db = n // 4
    heads = db // HEAD_DIM
    rows = l // GRID_W
    rb = NA_ROWS_PER_STEP
    t = rb * GRID_W
    return pl.pallas_call(
        functools.partial(_na_body, rows=rows, rb=rb, heads=heads),
        grid=(b, rows // rb),
        in_specs=[
            pl.BlockSpec((None, t, db), lambda i, j: (i, j, 0)),
            pl.BlockSpec((None, l, db), lambda i, j: (i, 0, 1)),
            pl.BlockSpec((None, l, db), lambda i, j: (i, 0, 2)),
            pl.BlockSpec((None, t, db), lambda i, j: (i, j, 3)),
            pl.BlockSpec(bias.shape, lambda i, j: (0, 0, 0)),
        ],
        out_specs=pl.BlockSpec((None, t, db), lambda i, j: (i, j, 0)),
        out_shape=jax.ShapeDtypeStruct((b, l, db), BF16),
        compiler_params=_params(("parallel", "parallel")),
        name="na",
    )(proj_b, proj_b, proj_b, proj_b, bias)


def _attend(q, kk, vv, valid, masks):
    rows = q.shape[0]
    s = lax.dot_general(_stack_heads(q, masks), kk, (((1,), (1,)), ((), ())), preferred_element_type=F32)
    s = jnp.where(valid, s, NEG_INF)
    m = jnp.max(s, -1, keepdims=True)
    e = jnp.exp2(s - m)
    l = jnp.sum(e, -1, keepdims=True)
    p = (e / l).astype(BF16)
    o = _unstack_heads(jnp.dot(p, vv, preferred_element_type=F32), masks, rows)
    lse = _unstack_heads(jnp.broadcast_to(m + jnp.log2(l), (s.shape[0], q.shape[1])), masks, rows)
    return o, lse


def _split_pattern(dilation, n, nchunks, q_ref, kbuf, vbuf, o_split, l_split, masks, heads):
    blk = DIL_BLK
    dc = heads * HEAD_DIM
    phases = DIL_SPLIT // dilation
    mq = blk // phases
    row = lax.broadcasted_iota(jnp.int32, (heads * blk, 3 * blk), 0) % blk
    col = lax.broadcasted_iota(jnp.int32, (heads * blk, 3 * blk), 1)
    qa, qm = row // mq, row % mq
    ka = sum((col >= a * 3 * mq).astype(jnp.int32) for a in range(1, phases)) if phases > 1 else 0
    km = col - ka * (3 * mq)
    band = jnp.abs(phases * (km - mq - qm) + (ka - qa)) <= blk
    for m0 in range(0, blk, mq):
        gm = km + (n * blk + m0 - mq)
        valid = band & (gm >= 0) & (gm < nchunks * blk)
        for j in range(dilation):
            lanes = [slice((j + dilation * a) * dc, (j + dilation * a + 1) * dc) for a in range(phases)]
            q = jnp.concatenate([q_ref[m0:m0 + mq, ln] for ln in lanes], 0)
            krows = slice(blk + m0 - mq, blk + m0 + 2 * mq)
            kk = jnp.concatenate([kbuf[krows, ln] for ln in lanes], 0)
            vv = jnp.concatenate([vbuf[krows, ln] for ln in lanes], 0)
            o, lse = _attend(q, kk, vv, valid, masks)
            for a in range(phases):
                o_split[j + dilation * a, m0:m0 + mq, :] = o[a * mq:(a + 1) * mq]
                l_split[j + dilation * a, m0:m0 + mq, :] = lse[a * mq:(a + 1) * mq]


def _dil_body(qn_ref, knp_ref, kn_ref, knn_ref, vnp_ref, vn_ref, vnn_ref,
              qs_ref, ksp_ref, ks_ref, ksn_ref, vsp_ref, vs_ref, vsn_ref, cz_ref, y_ref,
              kbn, vbn, kbs, vbs, o_split, l_split, o_all, l_all, *, heads, nchunks):
    n = pl.program_id(1)
    blk = DIL_BLK
    chunk = qn_ref.shape[0]
    per = chunk // blk
    masks = _head_masks(blk, heads)
    for buf, prev, cur, nxt, halo in ((kbn, knp_ref, kn_ref, knn_ref, blk), (vbn, vnp_ref, vn_ref, vnn_ref, blk),
                                      (kbs, ksp_ref, ks_ref, ksn_ref, blk), (vbs, vsp_ref, vs_ref, vsn_ref, blk)):
        main = cur.shape[0]
        buf[0:halo] = prev[...]
        buf[halo:halo + main] = cur[...]
        buf[halo + main:] = nxt[...]

    qi = lax.broadcasted_iota(jnp.int32, (heads * blk, 3 * blk), 0) % blk
    ki = lax.broadcasted_iota(jnp.int32, (heads * blk, 3 * blk), 1)
    band = jnp.abs(ki - blk - qi) <= blk

    def token_block(i, carry):
        g = n * per + i
        off = pl.multiple_of(i * blk, blk)
        valid = band & ((ki >= blk) | (g > 0)) & ((ki < 2 * blk) | (g < nchunks * per - 1))
        o, lse = _attend(qn_ref[pl.ds(off, blk), :], kbn[pl.ds(off, 3 * blk), :], vbn[pl.ds(off, 3 * blk), :],
                         valid, masks)
        o_all[0, pl.ds(off, blk), :] = o
        l_all[0, pl.ds(off, blk), :] = lse
        return carry

    lax.fori_loop(0, per, token_block, 0, unroll=True)

    for g, (_, dilation) in enumerate(DIL_PATTERNS):
        if dilation == 1:
            continue
        _split_pattern(dilation, n, nchunks, qs_ref, kbs, vbs, o_split, l_split, masks, heads)
        o_all[g] = jnp.swapaxes(o_split[...], 0, 1).reshape(chunk, heads * HEAD_DIM)
        l_all[g] = jnp.swapaxes(l_split[...], 0, 1).reshape(chunk, heads * HEAD_DIM)

    ls = [l_all[g] for g in range(len(DIL_PATTERNS))]
    m = functools.reduce(jnp.maximum, ls)
    es = [jnp.exp2(l - m) for l in ls]
    den = functools.reduce(lambda a, b: a + b, es)
    o = functools.reduce(lambda a, b: a + b, [(e / den) * o_all[g] for g, e in enumerate(es)])
    y_ref[...] = (o * cz_ref[...].astype(F32)).astype(BF16)


def _dilated_mixture(qn, kn, vn, qs, ks, vs, cz):
    b, l, dc = qn.shape
    heads = dc // HEAD_DIM
    blk = DIL_BLK
    assert DIL_PATTERNS[0][1] == 1 and all(w == 2 * blk * d and DIL_SPLIT % d == 0 for w, d in DIL_PATTERNS)
    chunk = DIL_SPLIT * blk
    nchunks = l // chunk
    per = chunk // blk
    nat = pl.BlockSpec((None, chunk, dc), lambda i, n: (i, n, 0))
    nat_prev = pl.BlockSpec((None, blk, dc), lambda i, n: (i, jnp.maximum(n * per - 1, 0), 0))
    nat_next = pl.BlockSpec((None, blk, dc), lambda i, n: (i, jnp.minimum((n + 1) * per, nchunks * per - 1), 0))
    spl = pl.BlockSpec((None, blk, DIL_SPLIT * dc), lambda i, n: (i, n, 0))
    spl_prev = pl.BlockSpec((None, blk, DIL_SPLIT * dc), lambda i, n: (i, jnp.maximum(n - 1, 0), 0))
    spl_next = pl.BlockSpec((None, blk, DIL_SPLIT * dc), lambda i, n: (i, jnp.minimum(n + 1, nchunks - 1), 0))
    return pl.pallas_call(
        functools.partial(_dil_body, heads=heads, nchunks=nchunks),
        grid=(b, nchunks),
        in_specs=[nat, nat_prev, nat, nat_next, nat_prev, nat, nat_next,
                  spl, spl_prev, spl, spl_next, spl_prev, spl, spl_next, nat],
        out_specs=nat,
        out_shape=jax.ShapeDtypeStruct((b, l, dc), BF16),
        scratch_shapes=[pltpu.VMEM((chunk + 2 * blk, dc), BF16), pltpu.VMEM((chunk + 2 * blk, dc), BF16),
                        pltpu.VMEM((3 * blk, DIL_SPLIT * dc), BF16), pltpu.VMEM((3 * blk, DIL_SPLIT * dc), BF16),
                        pltpu.VMEM((DIL_SPLIT, blk, dc), F32), pltpu.VMEM((DIL_SPLIT, blk, dc), F32),
                        pltpu.VMEM((len(DIL_PATTERNS), chunk, dc), F32),
                        pltpu.VMEM((len(DIL_PATTERNS), chunk, dc), F32)],
        compiler_params=_params(("parallel", "parallel")),
        name="dilated",
    )(qn, kn, kn, kn, vn, vn, vn, qs, ks, ks, ks, vs, vs, vs, cz)


def _tail_body(x_ref, ya_ref, yb_ref, yc_ref, g_ref, gate_ref, wa_ref, wb_ref, wc_ref, wo_ref, lng_ref, lnb_ref,
               out_ref, *, alpha):
    d = x_ref.shape[-1]
    pa = jnp.dot(ya_ref[...], wa_ref[...], preferred_element_type=F32)
    pb = jnp.dot(yb_ref[...], wb_ref[...], preferred_element_type=F32)
    pc = jnp.dot(yc_ref[...], wc_ref[...], preferred_element_type=F32)
    g = g_ref[...].astype(F32)
    merged = g[:, :d] * pa + g[:, d:2 * d] * pb + g[:, 2 * d:] * pc
    sub = jnp.dot(merged.astype(BF16), wo_ref[...], preferred_element_type=F32) * gate_ref[...]
    res = alpha * x_ref[...] + sub
    out_ref[...] = _layernorm(res) * lng_ref[...] + lnb_ref[...]


def _tail(x, ya, yb, yc, g_all, gate, wa, wb, wc, wo, ln_g, ln_b, alpha):
    b, l, d = x.shape
    t = TOKEN_TILE
    tok = lambda w: pl.BlockSpec((None, t, w), lambda i, j: (i, j, 0))
    const = lambda a: pl.BlockSpec(a.shape, lambda i, j: (0,) * a.ndim)
    ln_g, ln_b = ln_g.reshape(1, d), ln_b.reshape(1, d)
    return pl.pallas_call(
        functools.partial(_tail_body, alpha=alpha),
        grid=(b, l // t),
        in_specs=[tok(d), tok(ya.shape[-1]), tok(yb.shape[-1]), tok(yc.shape[-1]), tok(3 * d),
                  pl.BlockSpec((None, 1, d), lambda i, j: (i, 0, 0)),
                  const(wa), const(wb), const(wc), const(wo), const(ln_g), const(ln_b)],
        out_specs=tok(d),
        out_shape=jax.ShapeDtypeStruct((b, l, d), F32),
        compiler_params=_params(("parallel", "parallel")),
        name="tail",
    )(x, ya, yb, yc, g_all, gate.reshape(b, 1, d), wa, wb, wc, wo, ln_g, ln_b)


def _rope_tables(l, heads):
    half = HEAD_DIM // 2
    inv = ROPE_THETA ** (-jnp.arange(half, dtype=F32) / half)
    ang = jnp.arange(l, dtype=F32)[:, None] * inv[None, :]
    cos, sin = jnp.cos(ang), jnp.sin(ang)
    return (jnp.tile(jnp.concatenate([cos, cos], -1), (1, heads)),
            jnp.tile(jnp.concatenate([-sin, sin], -1), (1, heads)))


def _layer(x, ada, lw, consts, alpha):
    d = x.shape[-1]
    shift, scale, gate = ada[:, :d], ada[:, d:2 * d], ada[:, 2 * d:]
    h = _ln_mod(x, scale, shift)
    w_in, b_in = lw["w_in"], lw["b_in"]
    proj_a = _proj(h, w_in[:, :2 * d], b_in[:2 * d])
    q_cols = jnp.arange(d) < d // 4
    proj_b = _proj(h, w_in[:, 2 * d:3 * d], b_in[2 * d:3 * d], col_scale=jnp.where(q_cols, QK_SCALE_LOG2, 1.0))
    cq, ck, cv, cz, cqs, cks, cvs = _proj_rope(h, w_in[:, 3 * d:4 * d], b_in[3 * d:4 * d], *consts["rope"])
    gates = _proj(h, w_in[:, 4 * d:], b_in[4 * d:], sigmoid=True)

    v, x1, x2g = _hy_pre(proj_a, lw["conv_w"], lw["conv_b"])
    ya = _hyena(v, x1, x2g, consts["kspec"], lw["skip"], consts["dft"])
    yb = _na(proj_b, lw["na_bias"])
    yc = _dilated_mixture(cq, ck, cv, cqs, cks, cvs, cz)
    return _tail(x, ya, yb, yc, gates, gate, lw["wa"], lw["wb"], lw["wc"], lw["wo"],
                 lw["ln_g"], lw["ln_b"], alpha)


def kernel(x_prompt, x_sample, c_prompt, c_sample, w_ada, b_ada, w_in, b_in, hy_conv_w, hy_conv_b, hy_w1, hy_b1, hy_freq, hy_w2, hy_b2, hy_w3, hy_b3, hy_decay, hy_skip, na_rpb, w_branch_a, w_branch_b, w_branch_c, w_out, ln_g, ln_b):
    depth, d, _ = w_in.shape
    heads_c = (d // 4) // HEAD_DIM
    alpha = (2 * depth) ** 0.25
    groups = [(x_prompt, c_prompt), (x_sample, c_sample)]

    nb_p = c_prompt.shape[0]
    c_all = jnp.concatenate([c_prompt, c_sample], 0)
    pad_rows = -c_all.shape[0] % 8
    ada_all = _ada(jnp.pad(c_all, ((0, pad_rows), (0, 0))), w_ada, b_ada)
    adas = [ada_all[:, :nb_p], ada_all[:, nb_p:nb_p + c_sample.shape[0]]]

    shared = {}
    for x, _ in groups:
        l = x.shape[1]
        if l not in shared:
            shared[l] = {"dft": _dft_tables(l), "rope": _rope_tables(l, heads_c)}

    ys = [x for x, _ in groups]
    for layer in range(depth):
        lw = {
            "w_in": w_in[layer].astype(BF16), "b_in": b_in[layer],
            "conv_w": hy_conv_w[layer], "conv_b": hy_conv_b[layer], "skip": hy_skip[layer],
            "na_bias": _na_bias_table(na_rpb[layer]),
            "wa": w_branch_a[layer].astype(BF16), "wb": w_branch_b[layer].astype(BF16),
            "wc": w_branch_c[layer].astype(BF16), "wo": w_out[layer].astype(BF16),
            "ln_g": ln_g[layer], "ln_b": ln_b[layer],
        }
        kspecs = {}
        for gi in range(len(groups)):
            l = ys[gi].shape[1]
            if l not in kspecs:
                kspecs[l] = _hyena_spectrum(l, shared[l]["dft"], hy_w1[layer], hy_b1[layer], hy_freq[layer],
                                            hy_w2[layer], hy_b2[layer], hy_w3[layer], hy_b3[layer],
                                            hy_decay[layer])
            consts = dict(shared[l], kspec=kspecs[l])
            ys[gi] = _layer(ys[gi], adas[gi][layer], lw, consts, alpha)
    return tuple(ys)
```

```python
import functools
import math

import jax
import jax.numpy as jnp
from jax import lax
from jax.experimental import pallas as pl
from jax.experimental.pallas import tpu as pltpu

F32 = jnp.float32
BF16 = jnp.bfloat16

GRID_W = 64
HEAD_DIM = 64
HYENA_BANDS = 16
HYENA_EMB = 2 * HYENA_BANDS + 1
NA_KR = 8
NA_KC = 16
DIL_PATTERNS = ((128, 1), (512, 4), (2048, 16))
DIL_BLK = 64
DIL_SPLIT = 16
NA_ROW_UNROLL = 32
ROPE_THETA = 10000.0
LN_EPS = 1e-5
NEG_INF = -1e30
LOG2_E = math.log2(math.e)
QK_SCALE_LOG2 = HEAD_DIM ** -0.5 * LOG2_E

V7X_LANES = 128
V7X_SUBLANES = 8
V7X_VMEM_LIMIT_BYTES = 56 * 1024 * 1024

DFT_N2 = V7X_LANES
PAIR_GROUP = V7X_SUBLANES
DFT_STEP_ROWS = 128
SLABS_PER_STEP = 16
FEAT_PAD = V7X_LANES
BF16_TILE_ROWS = 2 * V7X_SUBLANES

TOKEN_TILE = 1024
WIDE_TOKEN_TILE = 2048
PROJ_COL_TILE = 1024
ROPE_TOKEN_TILE = 1024
FILTER_ROW_TILE = 1024
NA_ROWS_PER_STEP = 32
SPEC_SLABS_PER_STEP = 8
HY_SHIFT_ROWS = 256


def _params(sem):
    return pltpu.CompilerParams(dimension_semantics=sem, vmem_limit_bytes=V7X_VMEM_LIMIT_BYTES)


def _sigmoid(x):
    return 1.0 / (1.0 + jnp.exp(-x))


def _silu(x):
    return x * _sigmoid(x)


def _ada_body(c_ref, w_ref, b_ref, o_ref):
    s = _silu(c_ref[...])
    o_ref[...] = jnp.dot(s, w_ref[...], preferred_element_type=F32,
                         precision=lax.Precision.HIGHEST) + b_ref[...]


def _ada(c_all, w_ada, b_ada):
    depth, d, n = w_ada.shape
    rows = c_all.shape[0]
    tn = PROJ_COL_TILE
    return pl.pallas_call(
        _ada_body,
        grid=(depth, n // tn),
        in_specs=[
            pl.BlockSpec((rows, d), lambda l, j: (0, 0)),
            pl.BlockSpec((None, d, tn), lambda l, j: (l, 0, j)),
            pl.BlockSpec((None, 1, tn), lambda l, j: (l, 0, j)),
        ],
        out_specs=pl.BlockSpec((None, rows, tn), lambda l, j: (l, 0, j)),
        out_shape=jax.ShapeDtypeStruct((depth, rows, n), F32),
        compiler_params=_params(("parallel", "parallel")),
        name="ada",
    )(c_all, w_ada, b_ada.reshape(depth, 1, n))


def _layernorm(x):
    mu = jnp.mean(x, -1, keepdims=True)
    xc = x - mu
    var = jnp.mean(xc * xc, -1, keepdims=True)
    return xc * lax.rsqrt(var + LN_EPS)


def _ln_mod_body(x_ref, sc_ref, sh_ref, o_ref):
    h = _layernorm(x_ref[...]) * (1.0 + sc_ref[...]) + sh_ref[...]
    o_ref[...] = h.astype(BF16)


def _ln_mod(x, scale, shift):
    b, l, d = x.shape
    t = WIDE_TOKEN_TILE
    return pl.pallas_call(
        _ln_mod_body,
        grid=(b, l // t),
        in_specs=[
            pl.BlockSpec((None, t, d), lambda i, j: (i, j, 0)),
            pl.BlockSpec((None, 1, d), lambda i, j: (i, 0, 0)),
            pl.BlockSpec((None, 1, d), lambda i, j: (i, 0, 0)),
        ],
        out_specs=pl.BlockSpec((None, t, d), lambda i, j: (i, j, 0)),
        out_shape=jax.ShapeDtypeStruct((b, l, d), BF16),
        compiler_params=_params(("parallel", "parallel")),
        name="ln_mod",
    )(x, scale.reshape(b, 1, d), shift.reshape(b, 1, d))


def _proj_body(h_ref, w_ref, b_ref, *rest, sigmoid, scaled):
    o_ref = rest[-1]
    acc = jnp.dot(h_ref[...], w_ref[...], preferred_element_type=F32) + b_ref[...]
    if scaled:
        acc = acc * rest[0][...]
    if sigmoid:
        acc = 0.5 * jnp.tanh(0.5 * acc) + 0.5
    o_ref[...] = acc.astype(o_ref.dtype)


def _proj(h, w, bias, col_scale=None, sigmoid=False):
    b, l, d = h.shape
    n = w.shape[1]
    t, tn = TOKEN_TILE, PROJ_COL_TILE
    row = pl.BlockSpec((1, tn), lambda j, i, k: (0, j))
    scale_args = [] if col_scale is None else [col_scale.reshape(1, n)]
    return pl.pallas_call(
        functools.partial(_proj_body, sigmoid=sigmoid, scaled=col_scale is not None),
        grid=(n // tn, b, l // t),
        in_specs=[
            pl.BlockSpec((None, t, d), lambda j, i, k: (i, k, 0)),
            pl.BlockSpec((d, tn), lambda j, i, k: (0, j)),
            row,
        ] + [row] * len(scale_args),
        out_specs=pl.BlockSpec((None, t, tn), lambda j, i, k: (i, k, j)),
        out_shape=jax.ShapeDtypeStruct((b, l, n), BF16),
        compiler_params=_params(("parallel", "parallel", "parallel")),
        name="proj",
    )(h, w, bias.reshape(1, n), *scale_args)


def _rope_lanes(x, cos, sin_signed):
    outs = []
    lane = lax.broadcasted_iota(jnp.int32, (x.shape[0], V7X_LANES), 1)
    first_half = (lane % HEAD_DIM) < (HEAD_DIM // 2)
    for c0 in range(0, x.shape[1], V7X_LANES):
        xc = x[:, c0:c0 + V7X_LANES]
        partner = jnp.where(first_half,
                            pltpu.roll(xc, V7X_LANES - HEAD_DIM // 2, 1),
                            pltpu.roll(xc, HEAD_DIM // 2, 1))
        outs.append(xc * cos[:, c0:c0 + V7X_LANES] + partner * sin_signed[:, c0:c0 + V7X_LANES])
    return jnp.concatenate(outs, 1)


def _store_split(ref, x):
    t, dc = x.shape
    parts = jnp.swapaxes(x.reshape(t // DIL_SPLIT, DIL_SPLIT, dc), 0, 1)
    for r in range(DIL_SPLIT):
        ref[:, r * dc:(r + 1) * dc] = parts[r].astype(BF16)


def _proj_rope_body(h_ref, w_ref, b_ref, cos_ref, sin_ref, q_ref, k_ref, v_ref, z_ref,
                    qs_ref, ks_ref, vs_ref):
    dc = q_ref.shape[-1]
    acc = jnp.dot(h_ref[...], w_ref[...], preferred_element_type=F32) + b_ref[...]
    cos, sin = cos_ref[...], sin_ref[...]
    q = _rope_lanes(acc[:, :dc], cos, sin) * QK_SCALE_LOG2
    k = _rope_lanes(acc[:, dc:2 * dc], cos, sin)
    v = acc[:, 2 * dc:3 * dc]
    for ref, split_ref, val in ((q_ref, qs_ref, q), (k_ref, ks_ref, k), (v_ref, vs_ref, v)):
        ref[...] = val.astype(BF16)
        _store_split(split_ref, val)
    z_ref[...] = _silu(acc[:, 3 * dc:]).astype(BF16)


def _proj_rope(h, w, bias, cos_t, sin_t):
    b, l, d = h.shape
    n = w.shape[1]
    dc = n // 4
    t = ROPE_TOKEN_TILE
    tok = pl.BlockSpec((None, t, dc), lambda i, k: (i, k, 0))
    shp = jax.ShapeDtypeStruct((b, l, dc), BF16)
    spl = pl.BlockSpec((None, t // DIL_SPLIT, DIL_SPLIT * dc), lambda i, k: (i, k, 0))
    spl_shp = jax.ShapeDtypeStruct((b, l // DIL_SPLIT, DIL_SPLIT * dc), BF16)
    return pl.pallas_call(
        _proj_rope_body,
        grid=(b, l // t),
        in_specs=[
            pl.BlockSpec((None, t, d), lambda i, k: (i, k, 0)),
            pl.BlockSpec((d, n), lambda i, k: (0, 0)),
            pl.BlockSpec((1, n), lambda i, k: (0, 0)),
            pl.BlockSpec((t, dc), lambda i, k: (k, 0)),
            pl.BlockSpec((t, dc), lambda i, k: (k, 0)),
        ],
        out_specs=[tok, tok, tok, tok, spl, spl, spl],
        out_shape=[shp, shp, shp, shp, spl_shp, spl_shp, spl_shp],
        compiler_params=_params(("parallel", "parallel")),
        name="proj_rope",
    )(h, w, bias.reshape(1, n), cos_t, sin_t)


def _hy_pre_body(main_ref, prev_ref, next_ref, shift_ref, cw_ref, cb_ref, v_ref, x1_ref, x2_ref, *, da):
    i = pl.program_id(1)
    last = pl.num_programs(1) - 1
    t = main_ref.shape[0]
    halo = prev_ref.shape[0]
    blk = shift_ref.shape[1]
    nc = 3 * da
    outs = (v_ref, x1_ref, x2_ref)
    edge = lax.broadcasted_iota(jnp.int32, (V7X_SUBLANES, nc), 0)
    for r0 in range(0, t, blk):
        xb = main_ref[r0:r0 + blk, :nc]
        sh = jnp.dot(shift_ref[...], xb, preferred_element_type=F32)
        if r0 == 0:
            prev_row = jnp.where(i > 0, prev_ref[:, :nc].astype(F32)[halo - 1:halo], 0.0)
        else:
            prev_row = main_ref[r0 - halo:r0, :nc].astype(F32)[halo - 1:halo]
        if r0 + blk == t:
            next_row = jnp.where(i < last, next_ref[:, :nc].astype(F32)[0:1], 0.0)
        else:
            next_row = main_ref[r0 + blk:r0 + blk + halo, :nc].astype(F32)[0:1]
        up, dn = sh[:blk], sh[blk:]
        up = jnp.concatenate([up[:V7X_SUBLANES] + jnp.where(edge == 0, prev_row, 0.0), up[V7X_SUBLANES:]], 0)
        dn = jnp.concatenate([dn[:blk - V7X_SUBLANES],
                              dn[blk - V7X_SUBLANES:] + jnp.where(edge == V7X_SUBLANES - 1, next_row, 0.0)], 0)
        uc = up * cw_ref[0:1, :] + xb.astype(F32) * cw_ref[1:2, :] + dn * cw_ref[2:3, :] + cb_ref[...]
        for part in range(3):
            val = uc[:, part * da:(part + 1) * da]
            if part == 2:
                val = val * _silu(main_ref[r0:r0 + blk, nc:].astype(F32))
            outs[part][r0:r0 + blk, :] = val.astype(BF16)


def _hy_pre(proj_a, conv_w, conv_b):
    b, l, n = proj_a.shape
    da = n // 4
    t, halo = WIDE_TOKEN_TILE, BF16_TILE_ROWS
    nh = t // halo
    blk = HY_SHIFT_ROWS
    shift = jnp.concatenate([jnp.eye(blk, k=-1, dtype=BF16), jnp.eye(blk, k=1, dtype=BF16)], 0)
    tok = pl.BlockSpec((None, t, da), lambda i, j: (i, j, 0))
    shp = jax.ShapeDtypeStruct((b, l, da), BF16)
    return pl.pallas_call(
        functools.partial(_hy_pre_body, da=da),
        grid=(b, l // t),
        in_specs=[
            pl.BlockSpec((None, t, n), lambda i, j: (i, j, 0)),
            pl.BlockSpec((None, halo, n), lambda i, j: (i, jnp.maximum(j * nh - 1, 0), 0)),
            pl.BlockSpec((None, halo, n), lambda i, j: (i, jnp.minimum((j + 1) * nh, l // halo - 1), 0)),
            pl.BlockSpec((2 * blk, blk), lambda i, j: (0, 0)),
            pl.BlockSpec((3, 3 * da), lambda i, j: (0, 0)),
            pl.BlockSpec((1, 3 * da), lambda i, j: (0, 0)),
        ],
        out_specs=[tok, tok, tok],
        out_shape=[shp, shp, shp],
        compiler_params=_params(("parallel", "parallel")),
        name="hy_pre",
    )(proj_a, proj_a, proj_a, shift, conv_w, conv_b.reshape(1, 3 * da))


def _filter_body(feat_ref, w1_ref, b1_ref, f0_ref, w2_ref, b2_ref, f1_ref, w3_ref, b3_ref, dec_ref,
                 hf_ref, sum_ref):
    i = pl.program_id(0)
    hp = lax.Precision.HIGHEST
    half = feat_ref.shape[0] // 2
    n = hf_ref.shape[1]
    fa, fb = feat_ref[0:half, :], feat_ref[half:, :]
    feat = jnp.concatenate([fa, fb], 1)
    h = jnp.sin(f0_ref[...] * (jnp.dot(feat, w1_ref[...], preferred_element_type=F32, precision=hp)
                               + b1_ref[...]))
    h = jnp.sin(f1_ref[...] * (jnp.dot(h, w2_ref[...], preferred_element_type=F32, precision=hp)
                               + b2_ref[...]))
    h = jnp.dot(h.astype(BF16), w3_ref[...], preferred_element_type=F32) + b3_ref[...]
    t = jnp.concatenate([jnp.broadcast_to(fa[:, 0:1], (half, n)), jnp.broadcast_to(fb[:, 0:1], (half, n))], 1)
    h = h * jnp.exp(-t * jnp.abs(dec_ref[...]))
    hf_ref[0:half, :] = h[:, :n].astype(BF16)
    hf_ref[half:, :] = h[:, n:].astype(BF16)
    col_sum = jnp.sum(jnp.abs(h), 0, keepdims=True)
    abs_sum = col_sum[:, :n] + col_sum[:, n:]

    @pl.when(i > 0)
    def _():
        sum_ref[...] += abs_sum

    @pl.when(i == 0)
    def _():
        head = h[0:BF16_TILE_ROWS, :n]
        rows = lax.broadcasted_iota(jnp.int32, head.shape, 0)
        cols = lax.broadcasted_iota(jnp.int32, head.shape, 1)
        drop = (rows == 0) & (cols >= n // 2)
        hf_ref[0:BF16_TILE_ROWS, :] = jnp.where(drop, 0.0, head).astype(BF16)
        sum_ref[...] = abs_sum - jnp.sum(jnp.where(drop, jnp.abs(head), 0.0), 0, keepdims=True)


def _block_diag2(w):
    z = jnp.zeros_like(w)
    return jnp.concatenate([jnp.concatenate([w, z], 1), jnp.concatenate([z, w], 1)], 0)


def _filter_taps(l, w1, b1, freq, w2, b2, w3, b3, decay):
    fo = w1.shape[1]
    n = w3.shape[1]
    da = decay.shape[0]
    t = jnp.arange(l, dtype=F32) / l
    bands = jnp.arange(1, HYENA_BANDS + 1, dtype=F32)
    ang = 2.0 * math.pi * t[:, None] * bands[None, :]
    feat = jnp.concatenate([t[:, None], jnp.cos(ang), jnp.sin(ang)], -1)
    feat = jnp.pad(feat, ((0, 0), (0, FEAT_PAD - HYENA_EMB)))
    w1p = jnp.pad(w1, ((0, FEAT_PAD - HYENA_EMB), (0, 0)))
    twice = lambda v: jnp.tile(v.reshape(1, -1), (1, 2))
    dec = jnp.tile(decay, 2 * n // da).reshape(1, 2 * n)
    tt = FILTER_ROW_TILE
    const = lambda shape: pl.BlockSpec(shape, lambda i: (0,) * len(shape))
    return pl.pallas_call(
        _filter_body,
        grid=(l // tt,),
        in_specs=[
            pl.BlockSpec((tt, FEAT_PAD), lambda i: (i, 0)),
            const((2 * FEAT_PAD, 2 * fo)), const((1, 2 * fo)), const((1, 2 * fo)),
            const((2 * fo, 2 * fo)), const((1, 2 * fo)), const((1, 2 * fo)),
            const((2 * fo, 2 * n)), const((1, 2 * n)), const((1, 2 * n)),
        ],
        out_specs=[pl.BlockSpec((tt, n), lambda i: (i, 0)), const((1, n))],
        out_shape=[jax.ShapeDtypeStruct((l, n), BF16), jax.ShapeDtypeStruct((1, n), F32)],
        compiler_params=_params(("arbitrary",)),
        name="filter_taps",
    )(feat, _block_diag2(w1p), twice(b1), twice(freq[0]), _block_diag2(w2), twice(b2), twice(freq[1]),
      _block_diag2(w3.astype(BF16)), twice(b3), dec)


def _dft_tables(l):
    n = 2 * l
    n1 = n // DFT_N2
    kk = jnp.arange(n1 // 2, dtype=jnp.int32)
    nn = jnp.arange(n1 // 2, dtype=jnp.int32)
    th = (2.0 * math.pi / (2 * n1)) * (((2 * kk[:, None] + 1) * nn[None, :]) % (2 * n1)).astype(F32)
    eye2 = jnp.eye(2, dtype=F32)
    f1 = jnp.kron(jnp.concatenate([jnp.cos(th), -jnp.sin(th)], 0), eye2).astype(BF16)
    g = jnp.kron(jnp.concatenate([jnp.cos(th).T, -jnp.sin(th).T], 1) * (2.0 / n), eye2).astype(BF16)
    k2 = jnp.arange(DFT_N2, dtype=jnp.int32)
    n2 = jnp.arange(DFT_N2, dtype=jnp.int32)
    ph = (n2[None, None, :] * (k2[None, :, None] * (2 * n1) + 2 * kk[:, None, None] + 1)) % (2 * n)
    ang = (2.0 * math.pi / (2 * n)) * ph.astype(F32)
    c, s = jnp.cos(ang), jnp.sin(ang)
    mf = jnp.concatenate([jnp.concatenate([c, s], 2), jnp.concatenate([-s, c], 2)], 1).astype(BF16)
    return f1, g, mf


def _load_pair_group(ref, lead, g):
    start = pl.multiple_of(g * PAIR_GROUP, PAIR_GROUP)
    words = ref.bitcast(jnp.uint32)[(*lead, slice(None), pl.ds(start, PAIR_GROUP), slice(None))]
    words = jnp.swapaxes(words, 0, 1)
    return [pltpu.bitcast(words[i], BF16) for i in range(PAIR_GROUP)]


def _store_pair_group(ref, lead, g, vals):
    start = pl.multiple_of(g * PAIR_GROUP, PAIR_GROUP)
    words = jnp.stack([pltpu.bitcast(v, jnp.uint32) for v in vals], 0)
    ref.bitcast(jnp.uint32)[(*lead, slice(None), pl.ds(start, PAIR_GROUP), slice(None))] = (
        jnp.swapaxes(words, 0, 1))


def _dft_in_body(f_ref, z_ref, a_ref):
    n1 = a_ref.shape[2]
    cb = a_ref.shape[-1]

    def group(g, carry):
        zcat = jnp.concatenate(_load_pair_group(z_ref, (0,), g), 1)
        r = jnp.dot(f_ref[...], zcat, preferred_element_type=F32).astype(BF16)
        cols = [r[:, i * cb:(i + 1) * cb] for i in range(PAIR_GROUP)]
        _store_pair_group(a_ref, (0, 0), g, [c[:2 * n1] for c in cols])
        _store_pair_group(a_ref, (0, 1), g, [c[2 * n1:] for c in cols])
        return carry

    lax.fori_loop(0, z_ref.shape[2] // (2 * PAIR_GROUP), group, 0, unroll=8)


def _dft_in(f1p, z):
    b, half, n2, c = z.shape
    n1 = half
    cb, rs = V7X_LANES, DFT_STEP_ROWS
    return pl.pallas_call(
        _dft_in_body,
        grid=(b, c // cb, n2 // rs),
        in_specs=[
            pl.BlockSpec(f1p.shape, lambda i, j, s: (0, 0)),
            pl.BlockSpec((1, half, rs, cb), lambda i, j, s: (i, 0, s, j)),
        ],
        out_specs=pl.BlockSpec((1, 2, n1, rs, cb), lambda i, j, s: (i, 0, 0, s, j)),
        out_shape=jax.ShapeDtypeStruct((b, 2, n1, n2, c), BF16),
        compiler_params=_params(("parallel", "parallel", "parallel")),
        name="dft_in",
    )(f1p, z)


def _dft_out_body(g_ref, b_ref, x_ref, z_ref, skip_ref, o_ref):
    cb = o_ref.shape[-1]

    def group(g, carry):
        re, im = _load_pair_group(b_ref, (0, 0), g), _load_pair_group(b_ref, (0, 1), g)
        bcat = jnp.concatenate([jnp.concatenate([r, i], 0) for r, i in zip(re, im)], 1)
        y = jnp.dot(g_ref[...], bcat, preferred_element_type=F32)
        xs, zs = _load_pair_group(x_ref, (0,), g), _load_pair_group(z_ref, (0,), g)
        outs = [(xs[i].astype(F32) * (y[:, i * cb:(i + 1) * cb] + skip_ref[...] * zs[i].astype(F32))
                 ).astype(BF16) for i in range(PAIR_GROUP)]
        _store_pair_group(o_ref, (0,), g, outs)
        return carry

    lax.fori_loop(0, x_ref.shape[2] // (2 * PAIR_GROUP), group, 0, unroll=2)


def _dft_out(gp, bb, x, z, skip):
    b, half, n2, c = z.shape
    n1 = half
    cb, rs = V7X_LANES, DFT_STEP_ROWS
    slab = pl.BlockSpec((1, half, rs, cb), lambda i, j, s: (i, 0, s, j))
    return pl.pallas_call(
        _dft_out_body,
        grid=(b, c // cb, n2 // rs),
        in_specs=[
            pl.BlockSpec(gp.shape, lambda i, j, s: (0, 0)),
            pl.BlockSpec((1, 2, n1, rs, cb), lambda i, j, s: (i, 0, 0, s, j)),
            slab, slab,
            pl.BlockSpec((1, cb), lambda i, j, s: (0, j)),
        ],
        out_specs=slab,
        out_shape=jax.ShapeDtypeStruct(z.shape, BF16),
        compiler_params=_params(("parallel", "parallel", "parallel")),
        name="dft_out",
    )(gp, bb, x, z, skip.astype(F32).reshape(1, c))


def _slab_conv_body(mf_ref, k_ref, a_ref, o_ref):
    nb, _, ks, n2, c = a_ref.shape
    for kk in range(ks):
        kr, ki = k_ref[kk, 0].astype(F32), k_ref[kk, 1].astype(F32)
        for b in range(nb):
            a = a_ref[b, :, kk].reshape(2 * n2, c)
            x = jnp.dot(mf_ref[kk], a, preferred_element_type=F32)
            xr, xi = x[:n2], x[n2:]
            y = jnp.concatenate([xr * kr - xi * ki, xr * ki + xi * kr], 0).astype(BF16)
            out = lax.dot_general(mf_ref[kk], y, (((0,), (0,)), ((), ())), preferred_element_type=F32)
            o_ref[b, :, kk] = out.astype(BF16).reshape(2, n2, c)


def _slab_conv(mf, kspec, a, order):
    b, _, n1, n2, c = a.shape
    m = 2 * n2
    ks = max(1, SLABS_PER_STEP // b)
    return pl.pallas_call(
        _slab_conv_body,
        grid=(n1 // ks,),
        in_specs=[
            pl.BlockSpec((ks, m, m), lambda k: (k, 0, 0)),
            pl.BlockSpec((ks, 2, n2, c), lambda k: (k, 0, 0, order)),
            pl.BlockSpec((b, 2, ks, n2, c), lambda k: (0, 0, k, 0, 0)),
        ],
        out_specs=pl.BlockSpec((b, 2, ks, n2, c), lambda k: (0, 0, k, 0, 0)),
        out_shape=jax.ShapeDtypeStruct(a.shape, BF16),
        compiler_params=_params(("parallel",)),
        name="slab_conv",
    )(mf, kspec, a)


def _slab_spec_body(mf_ref, sum_ref, a_ref, k_ref):
    _, ks, n2, c = a_ref.shape
    half = c // 2
    inv = 1.0 / (sum_ref[:, :half] + sum_ref[:, half:] + 1e-6)
    for kk in range(ks):
        x = jnp.dot(mf_ref[kk], a_ref[:, kk].reshape(2 * n2, c), preferred_element_type=F32)
        k_ref[kk, 0] = ((x[:n2, :half] + x[:n2, half:]) * inv).astype(BF16)
        k_ref[kk, 1] = ((x[n2:, :half] - x[n2:, half:]) * inv).astype(BF16)


def _slab_spec(mf, sums, a):
    _, _, n1, n2, c = a.shape
    m = 2 * n2
    ks = SPEC_SLABS_PER_STEP
    return pl.pallas_call(
        _slab_spec_body,
        grid=(n1 // ks,),
        in_specs=[
            pl.BlockSpec((ks, m, m), lambda k: (k, 0, 0)),
            pl.BlockSpec((1, c), lambda k: (0, 0)),
            pl.BlockSpec((None, 2, ks, n2, c), lambda k: (0, 0, k, 0, 0)),
        ],
        out_specs=pl.BlockSpec((ks, 2, n2, c // 2), lambda k: (k, 0, 0, 0)),
        out_shape=jax.ShapeDtypeStruct((n1, 2, n2, c // 2), BF16),
        compiler_params=_params(("parallel",)),
        name="slab_spec",
    )(mf, sums, a)


def _hyena_spectrum(l, tables, w1, b1, freq, w2, b2, w3, b3, decay):
    f1p, _, mf = tables
    taps, sums = _filter_taps(l, w1, b1, freq, w2, b2, w3, b3, decay)
    a = _dft_in(f1p, taps.reshape(1, l // DFT_N2, DFT_N2, taps.shape[1]))
    return _slab_spec(mf, sums, a)


def _hyena(v, x1, x2g, kspec, skip, tables):
    f1p, gp, mf = tables
    b, l, c = v.shape
    slabs = lambda u: u.reshape(b, l // DFT_N2, DFT_N2, c)
    z = slabs(v)
    for order, xg in enumerate((slabs(x1), slabs(x2g))):
        bb = _slab_conv(mf, kspec, _dft_in(f1p, z), order)
        z = _dft_out(gp, bb, xg, z, skip[order])
    return z.reshape(b, l, c)


def _head_masks(rows, heads):
    lane = lax.broadcasted_iota(jnp.int32, (rows, heads * HEAD_DIM), 1)
    return [(lane >= h * HEAD_DIM) & (lane < (h + 1) * HEAD_DIM) for h in range(heads)]


def _stack_heads(q, masks):
    zero = jnp.zeros_like(q)
    return jnp.concatenate([jnp.where(m, q, zero) for m in masks], 0)


def _unstack_heads(res, masks, rows):
    out = jnp.where(masks[0], res[:rows], 0.0)
    for h in range(1, len(masks)):
        out = out + jnp.where(masks[h], res[h * rows:(h + 1) * rows], 0.0)
    return out


def _na_bias_table(rpb):
    heads = rpb.shape[0]
    c = jnp.arange(GRID_W)
    col_start = jnp.clip(c - NA_KC // 2, 0, GRID_W - NA_KC)
    col_ok = (c[None, :] >= col_start[:, None]) & (c[None, :] < col_start[:, None] + NA_KC)
    dc = jnp.clip(c[None, :] - c[:, None], -(NA_KC - 1), NA_KC - 1) + NA_KC - 1
    onehot = (dc[:, :, None] == jnp.arange(2 * NA_KC - 1)[None, None, :]).astype(F32)
    tcol = jnp.einsum("hrj,qkj->hrqk", rpb.astype(F32), onehot, precision=lax.Precision.HIGHEST)
    per_idx = [tcol[:, NA_KR - 1 - idx:2 * NA_KR - 1 - idx] for idx in range(NA_KR)]
    bias = jnp.transpose(jnp.stack(per_idx, 0), (0, 1, 3, 2, 4))
    bias = jnp.where(col_ok[None, None, :, None, :], bias * LOG2_E, NEG_INF)
    return bias.reshape(NA_KR, heads * GRID_W, NA_KR * GRID_W)


def _na_body(q_ref, k_ref, v_ref, z_ref, bias_ref, o_ref, *, rows, rb, heads):
    blk = pl.program_id(1)
    masks = _head_masks(GRID_W, heads)
    span = NA_KR * GRID_W

    def one_row(rr, carry):
        r = blk * rb + rr
        start = jnp.clip(r - NA_KR // 2, 0, rows - NA_KR)
        idx = r - start
        koff = pl.multiple_of(start * GRID_W, GRID_W)
        qoff = pl.multiple_of(rr * GRID_W, GRID_W)
        q = q_ref[pl.ds(qoff, GRID_W), :]
        kk = k_ref[pl.ds(koff, span), :]
        vv = v_ref[pl.ds(koff, span), :]
        s = lax.dot_general(_stack_heads(q, masks), kk, (((1,), (1,)), ((), ())),
                            preferred_element_type=F32)
        s = s + bias_ref[idx]
        m = jnp.max(s, -1, keepdims=True)
        e = jnp.exp2(s - m)
        p = (e / jnp.sum(e, -1, keepdims=True)).astype(BF16)
        o = _unstack_heads(jnp.dot(p, vv, preferred_element_type=F32), masks, GRID_W)
        z = z_ref[pl.ds(qoff, GRID_W), :].astype(F32)
        o_ref[pl.ds(qoff, GRID_W), :] = (o * _silu(z)).astype(BF16)
        return carry

    lax.fori_loop(0, rb, one_row, 0, unroll=NA_ROW_UNROLL)


def _na(proj_b, bias):
    b, l, n = proj_b.shape
    db = n // 4
    heads = db // HEAD_DIM
    rows = l // GRID_W
    rb = NA_ROWS_PER_STEP
    t = rb * GRID_W
    return pl.pallas_call(
        functools.partial(_na_body, rows=rows, rb=rb, heads=heads),
        grid=(b, rows // rb),
        in_specs=[
            pl.BlockSpec((None, t, db), lambda i, j: (i, j, 0)),
            pl.BlockSpec((None, l, db), lambda i, j: (i, 0, 1)),
            pl.BlockSpec((None, l, db), lambda i, j: (i, 0, 2)),
            pl.BlockSpec((None, t, db), lambda i, j: (i, j, 3)),
            pl.BlockSpec(bias.shape, lambda i, j: (0, 0, 0)),
        ],
        out_specs=pl.BlockSpec((None, t, db), lambda i, j: (i, j, 0)),
        out_shape=jax.ShapeDtypeStruct((b, l, db), BF16),
        compiler_params=_params(("parallel", "parallel")),
        name="na",
    )(proj_b, proj_b, proj_b, proj_b, bias)


def _attend(q, kk, vv, valid, masks):
    rows = q.shape[0]
    s = lax.dot_general(_stack_heads(q, masks), kk, (((1,), (1,)), ((), ())), preferred_element_type=F32)
    s = jnp.where(valid, s, NEG_INF)
    m = jnp.max(s, -1, keepdims=True)
    e = jnp.exp2(s - m)
    l = jnp.sum(e, -1, keepdims=True)
    p = (e / l).astype(BF16)
    o = _unstack_heads(jnp.dot(p, vv, preferred_element_type=F32), masks, rows)
    lse = _unstack_heads(jnp.broadcast_to(m + jnp.log2(l), (s.shape[0], q.shape[1])), masks, rows)
    return o, lse


def _split_pattern(dilation, n, nchunks, q_ref, kbuf, vbuf, o_split, l_split, masks, heads):
    blk = DIL_BLK
    dc = heads * HEAD_DIM
    phases = DIL_SPLIT // dilation
    mq = blk // phases
    row = lax.broadcasted_iota(jnp.int32, (heads * blk, 3 * blk), 0) % blk
    col = lax.broadcasted_iota(jnp.int32, (heads * blk, 3 * blk), 1)
    qa, qm = row // mq, row % mq
    ka = sum((col >= a * 3 * mq).astype(jnp.int32) for a in range(1, phases)) if phases > 1 else 0
    km = col - ka * (3 * mq)
    band = jnp.abs(phases * (km - mq - qm) + (ka - qa)) <= blk
    for m0 in range(0, blk, mq):
        gm = km + (n * blk + m0 - mq)
        valid = band & (gm >= 0) & (gm < nchunks * blk)
        for j in range(dilation):
            lanes = [slice((j + dilation * a) * dc, (j + dilation * a + 1) * dc) for a in range(phases)]
            q = jnp.concatenate([q_ref[m0:m0 + mq, ln] for ln in lanes], 0)
            krows = slice(blk + m0 - mq, blk + m0 + 2 * mq)
            kk = jnp.concatenate([kbuf[krows, ln] for ln in lanes], 0)
            vv = jnp.concatenate([vbuf[krows, ln] for ln in lanes], 0)
            o, lse = _attend(q, kk, vv, valid, masks)
            for a in range(phases):
                o_split[j + dilation * a, m0:m0 + mq, :] = o[a * mq:(a + 1) * mq]
                l_split[j + dilation * a, m0:m0 + mq, :] = lse[a * mq:(a + 1) * mq]


def _dil_body(qn_ref, knp_ref, kn_ref, knn_ref, vnp_ref, vn_ref, vnn_ref,
              qs_ref, ksp_ref, ks_ref, ksn_ref, vsp_ref, vs_ref, vsn_ref, cz_ref, y_ref,
              kbn, vbn, kbs, vbs, o_split, l_split, o_all, l_all, *, heads, nchunks):
    n = pl.program_id(1)
    blk = DIL_BLK
    chunk = qn_ref.shape[0]
    per = chunk // blk
    masks = _head_masks(blk, heads)
    for buf, prev, cur, nxt, halo in ((kbn, knp_ref, kn_ref, knn_ref, blk), (vbn, vnp_ref, vn_ref, vnn_ref, blk),
                                      (kbs, ksp_ref, ks_ref, ksn_ref, blk), (vbs, vsp_ref, vs_ref, vsn_ref, blk)):
        main = cur.shape[0]
        buf[0:halo] = prev[...]
        buf[halo:halo + main] = cur[...]
        buf[halo + main:] = nxt[...]

    qi = lax.broadcasted_iota(jnp.int32, (heads * blk, 3 * blk), 0) % blk
    ki = lax.broadcasted_iota(jnp.int32, (heads * blk, 3 * blk), 1)
    band = jnp.abs(ki - blk - qi) <= blk

    def token_block(i, carry):
        g = n * per + i
        off = pl.multiple_of(i * blk, blk)
        valid = band & ((ki >= blk) | (g > 0)) & ((ki < 2 * blk) | (g < nchunks * per - 1))
        o, lse = _attend(qn_ref[pl.ds(off, blk), :], kbn[pl.ds(off, 3 * blk), :], vbn[pl.ds(off, 3 * blk), :],
                         valid, masks)
        o_all[0, pl.ds(off, blk), :] = o
        l_all[0, pl.ds(off, blk), :] = lse
        return carry

    lax.fori_loop(0, per, token_block, 0, unroll=True)

    for g, (_, dilation) in enumerate(DIL_PATTERNS):
        if dilation == 1:
            continue
        _split_pattern(dilation, n, nchunks, qs_ref, kbs, vbs, o_split, l_split, masks, heads)
        o_all[g] = jnp.swapaxes(o_split[...], 0, 1).reshape(chunk, heads * HEAD_DIM)
        l_all[g] = jnp.swapaxes(l_split[...], 0, 1).reshape(chunk, heads * HEAD_DIM)

    ls = [l_all[g] for g in range(len(DIL_PATTERNS))]
    m = functools.reduce(jnp.maximum, ls)
    es = [jnp.exp2(l - m) for l in ls]
    den = functools.reduce(lambda a, b: a + b, es)
    o = functools.reduce(lambda a, b: a + b, [(e / den) * o_all[g] for g, e in enumerate(es)])
    y_ref[...] = (o * cz_ref[...].astype(F32)).astype(BF16)


def _dilated_mixture(qn, kn, vn, qs, ks, vs, cz):
    b, l, dc = qn.shape
    heads = dc // HEAD_DIM
    blk = DIL_BLK
    assert DIL_PATTERNS[0][1] == 1 and all(w == 2 * blk * d and DIL_SPLIT % d == 0 for w, d in DIL_PATTERNS)
    chunk = DIL_SPLIT * blk
    nchunks = l // chunk
    per = chunk // blk
    nat = pl.BlockSpec((None, chunk, dc), lambda i, n: (i, n, 0))
    nat_prev = pl.BlockSpec((None, blk, dc), lambda i, n: (i, jnp.maximum(n * per - 1, 0), 0))
    nat_next = pl.BlockSpec((None, blk, dc), lambda i, n: (i, jnp.minimum((n + 1) * per, nchunks * per - 1), 0))
    spl = pl.BlockSpec((None, blk, DIL_SPLIT * dc), lambda i, n: (i, n, 0))
    spl_prev = pl.BlockSpec((None, blk, DIL_SPLIT * dc), lambda i, n: (i, jnp.maximum(n - 1, 0), 0))
    spl_next = pl.BlockSpec((None, blk, DIL_SPLIT * dc), lambda i, n: (i, jnp.minimum(n + 1, nchunks - 1), 0))
    return pl.pallas_call(
        functools.partial(_dil_body, heads=heads, nchunks=nchunks),
        grid=(b, nchunks),
        in_specs=[nat, nat_prev, nat, nat_next, nat_prev, nat, nat_next,
                  spl, spl_prev, spl, spl_next, spl_prev, spl, spl_next, nat],
        out_specs=nat,
        out_shape=jax.ShapeDtypeStruct((b, l, dc), BF16),
        scratch_shapes=[pltpu.VMEM((chunk + 2 * blk, dc), BF16), pltpu.VMEM((chunk + 2 * blk, dc), BF16),
                        pltpu.VMEM((3 * blk, DIL_SPLIT * dc), BF16), pltpu.VMEM((3 * blk, DIL_SPLIT * dc), BF16),
                        pltpu.VMEM((DIL_SPLIT, blk, dc), F32), pltpu.VMEM((DIL_SPLIT, blk, dc), F32),
                        pltpu.VMEM((len(DIL_PATTERNS), chunk, dc), F32),
                        pltpu.VMEM((len(DIL_PATTERNS), chunk, dc), F32)],
        compiler_params=_params(("parallel", "parallel")),
        name="dilated",
    )(qn, kn, kn, kn, vn, vn, vn, qs, ks, ks, ks, vs, vs, vs, cz)


def _tail_body(x_ref, ya_ref, yb_ref, yc_ref, g_ref, gate_ref, wa_ref, wb_ref, wc_ref, wo_ref, lng_ref, lnb_ref,
               out_ref, *, alpha):
    d = x_ref.shape[-1]
    pa = jnp.dot(ya_ref[...], wa_ref[...], preferred_element_type=F32)
    pb = jnp.dot(yb_ref[...], wb_ref[...], preferred_element_type=F32)
    pc = jnp.dot(yc_ref[...], wc_ref[...], preferred_element_type=F32)
    g = g_ref[...].astype(F32)
    merged = g[:, :d] * pa + g[:, d:2 * d] * pb + g[:, 2 * d:] * pc
    sub = jnp.dot(merged.astype(BF16), wo_ref[...], preferred_element_type=F32) * gate_ref[...]
    res = alpha * x_ref[...] + sub
    out_ref[...] = _layernorm(res) * lng_ref[...] + lnb_ref[...]


def _tail(x, ya, yb, yc, g_all, gate, wa, wb, wc, wo, ln_g, ln_b, alpha):
    b, l, d = x.shape
    t = TOKEN_TILE
    tok = lambda w: pl.BlockSpec((None, t, w), lambda i, j: (i, j, 0))
    const = lambda a: pl.BlockSpec(a.shape, lambda i, j: (0,) * a.ndim)
    ln_g, ln_b = ln_g.reshape(1, d), ln_b.reshape(1, d)
    return pl.pallas_call(
        functools.partial(_tail_body, alpha=alpha),
        grid=(b, l // t),
        in_specs=[tok(d), tok(ya.shape[-1]), tok(yb.shape[-1]), tok(yc.shape[-1]), tok(3 * d),
                  pl.BlockSpec((None, 1, d), lambda i, j: (i, 0, 0)),
                  const(wa), const(wb), const(wc), const(wo), const(ln_g), const(ln_b)],
        out_specs=tok(d),
        out_shape=jax.ShapeDtypeStruct((b, l, d), F32),
        compiler_params=_params(("parallel", "parallel")),
        name="tail",
    )(x, ya, yb, yc, g_all, gate.reshape(b, 1, d), wa, wb, wc, wo, ln_g, ln_b)


def _rope_tables(l, heads):
    half = HEAD_DIM // 2
    inv = ROPE_THETA ** (-jnp.arange(half, dtype=F32) / half)
    ang = jnp.arange(l, dtype=F32)[:, None] * inv[None, :]
    cos, sin = jnp.cos(ang), jnp.sin(ang)
    return (jnp.tile(jnp.concatenate([cos, cos], -1), (1, heads)),
            jnp.tile(jnp.concatenate([-sin, sin], -1), (1, heads)))


def _layer(x, ada, lw, consts, alpha):
    d = x.shape[-1]
    shift, scale, gate = ada[:, :d], ada[:, d:2 * d], ada[:, 2 * d:]
    h = _ln_mod(x, scale, shift)
    w_in, b_in = lw["w_in"], lw["b_in"]
    proj_a = _proj(h, w_in[:, :2 * d], b_in[:2 * d])
    q_cols = jnp.arange(d) < d // 4
    proj_b = _proj(h, w_in[:, 2 * d:3 * d], b_in[2 * d:3 * d], col_scale=jnp.where(q_cols, QK_SCALE_LOG2, 1.0))
    cq, ck, cv, cz, cqs, cks, cvs = _proj_rope(h, w_in[:, 3 * d:4 * d], b_in[3 * d:4 * d], *consts["rope"])
    gates = _proj(h, w_in[:, 4 * d:], b_in[4 * d:], sigmoid=True)

    v, x1, x2g = _hy_pre(proj_a, lw["conv_w"], lw["conv_b"])
    ya = _hyena(v, x1, x2g, consts["kspec"], lw["skip"], consts["dft"])
    yb = _na(proj_b, lw["na_bias"])
    yc = _dilated_mixture(cq, ck, cv, cqs, cks, cvs, cz)
    return _tail(x, ya, yb, yc, gates, gate, lw["wa"], lw["wb"], lw["wc"], lw["wo"],
                 lw["ln_g"], lw["ln_b"], alpha)


def kernel(x_prompt, x_sample, c_prompt, c_sample, w_ada, b_ada, w_in, b_in, hy_conv_w, hy_conv_b, hy_w1, hy_b1, hy_freq, hy_w2, hy_b2, hy_w3, hy_b3, hy_decay, hy_skip, na_rpb, w_branch_a, w_branch_b, w_branch_c, w_out, ln_g, ln_b):
    depth, d, _ = w_in.shape
    heads_c = (d // 4) // HEAD_DIM
    alpha = (2 * depth) ** 0.25
    groups = [(x_prompt, c_prompt), (x_sample, c_sample)]

    nb_p = c_prompt.shape[0]
    c_all = jnp.concatenate([c_prompt, c_sample], 0)
    pad_rows = -c_all.shape[0] % 8
    ada_all = _ada(jnp.pad(c_all, ((0, pad_rows), (0, 0))), w_ada, b_ada)
    adas = [ada_all[:, :nb_p], ada_all[:, nb_p:nb_p + c_sample.shape[0]]]

    shared = {}
    for x, _ in groups:
        l = x.shape[1]
        if l not in shared:
            shared[l] = {"dft": _dft_tables(l), "rope": _rope_tables(l, heads_c)}

    ys = [x for x, _ in groups]
    for layer in range(depth):
        lw = {
            "w_in": w_in[layer].astype(BF16), "b_in": b_in[layer],
            "conv_w": hy_conv_w[layer], "conv_b": hy_conv_b[layer], "skip": hy_skip[layer],
            "na_bias": _na_bias_table(na_rpb[layer]),
            "wa": w_branch_a[layer].astype(BF16), "wb": w_branch_b[layer].astype(BF16),
            "wc": w_branch_c[layer].astype(BF16), "wo": w_out[layer].astype(BF16),
            "ln_g": ln_g[layer], "ln_b": ln_b[layer],
        }
        kspecs = {}
        for gi in range(len(groups)):
            l = ys[gi].shape[1]
            if l not in kspecs:
                kspecs[l] = _hyena_spectrum(l, shared[l]["dft"], hy_w1[layer], hy_b1[layer], hy_freq[layer],
                                            hy_w2[layer], hy_b2[layer], hy_w3[layer], hy_b3[layer],
                                            hy_decay[layer])
            consts = dict(shared[l], kspec=kspecs[l])
            ys[gi] = _layer(ys[gi], adas[gi][layer], lw, consts, alpha)
    return tuple(ys)
```

```python
import functools
import math

import jax
import jax.numpy as jnp
from jax import lax
from jax.experimental import pallas as pl
from jax.experimental.pallas import tpu as pltpu

F32 = jnp.float32
BF16 = jnp.bfloat16

GRID_W = 64
HEAD_DIM = 64
HYENA_BANDS = 16
HYENA_EMB = 2 * HYENA_BANDS + 1
NA_KR = 8
NA_KC = 16
DIL_PATTERNS = ((128, 1), (512, 4), (2048, 16))
DIL_BLK = 64
DIL_SPLIT = 16
NA_ROW_UNROLL = 32
ROPE_THETA = 10000.0
LN_EPS = 1e-5
NEG_INF = -1e30
LOG2_E = math.log2(math.e)
QK_SCALE_LOG2 = HEAD_DIM ** -0.5 * LOG2_E

V7X_LANES = 128
V7X_SUBLANES = 8
V7X_VMEM_LIMIT_BYTES = 56 * 1024 * 1024

DFT_N2 = V7X_LANES
PAIR_GROUP = V7X_SUBLANES
DFT_STEP_ROWS = 128
SLABS_PER_STEP = 16
FEAT_PAD = V7X_LANES
BF16_TILE_ROWS = 2 * V7X_SUBLANES

TOKEN_TILE = 1024
WIDE_TOKEN_TILE = 2048
PROJ_COL_TILE = 1024
ROPE_TOKEN_TILE = 1024
FILTER_ROW_TILE = 1024
NA_ROWS_PER_STEP = 32
SPEC_SLABS_PER_STEP = 4
HY_SHIFT_ROWS = 256


def _params(sem, fuse_inputs=None):
    return pltpu.CompilerParams(dimension_semantics=sem, vmem_limit_bytes=V7X_VMEM_LIMIT_BYTES,
                                allow_input_fusion=fuse_inputs)


def _sigmoid(x):
    return 1.0 / (1.0 + jnp.exp(-x))


def _silu(x):
    return x * _sigmoid(x)


def _ada_body(c_ref, w_ref, b_ref, o_ref):
    s = _silu(c_ref[...])
    o_ref[...] = jnp.dot(s, w_ref[...], preferred_element_type=F32,
                         precision=lax.Precision.HIGHEST) + b_ref[...]


def _ada(c_all, w_ada, b_ada):
    depth, d, n = w_ada.shape
    rows = c_all.shape[0]
    tn = PROJ_COL_TILE
    return pl.pallas_call(
        _ada_body,
        grid=(depth, n // tn),
        in_specs=[
            pl.BlockSpec((rows, d), lambda l, j: (0, 0)),
            pl.BlockSpec((None, d, tn), lambda l, j: (l, 0, j)),
            pl.BlockSpec((None, 1, tn), lambda l, j: (l, 0, j)),
        ],
        out_specs=pl.BlockSpec((None, rows, tn), lambda l, j: (l, 0, j)),
        out_shape=jax.ShapeDtypeStruct((depth, rows, n), F32),
        compiler_params=_params(("parallel", "parallel")),
        name="ada",
    )(c_all, w_ada, b_ada.reshape(depth, 1, n))


def _layernorm(x):
    mu = jnp.mean(x, -1, keepdims=True)
    xc = x - mu
    var = jnp.mean(xc * xc, -1, keepdims=True)
    return xc * lax.rsqrt(var + LN_EPS)


def _ln_mod_body(x_ref, sc_ref, sh_ref, o_ref):
    h = _layernorm(x_ref[...]) * (1.0 + sc_ref[...]) + sh_ref[...]
    o_ref[...] = h.astype(BF16)


def _ln_mod(x, scale, shift):
    b, l, d = x.shape
    t = WIDE_TOKEN_TILE
    return pl.pallas_call(
        _ln_mod_body,
        grid=(b, l // t),
        in_specs=[
            pl.BlockSpec((None, t, d), lambda i, j: (i, j, 0)),
            pl.BlockSpec((None, 1, d), lambda i, j: (i, 0, 0)),
            pl.BlockSpec((None, 1, d), lambda i, j: (i, 0, 0)),
        ],
        out_specs=pl.BlockSpec((None, t, d), lambda i, j: (i, j, 0)),
        out_shape=jax.ShapeDtypeStruct((b, l, d), BF16),
        compiler_params=_params(("parallel", "parallel")),
        name="ln_mod",
    )(x, scale.reshape(b, 1, d), shift.reshape(b, 1, d))


def _proj_body(h_ref, w_ref, b_ref, *rest, sigmoid, scaled):
    o_ref = rest[-1]
    acc = jnp.dot(h_ref[...], w_ref[...], preferred_element_type=F32) + b_ref[...]
    if scaled:
        acc = acc * rest[0][...]
    if sigmoid:
        acc = 0.5 * jnp.tanh(0.5 * acc) + 0.5
    o_ref[...] = acc.astype(o_ref.dtype)


def _proj(h, w, bias, col_scale=None, sigmoid=False):
    b, l, d = h.shape
    n = w.shape[1]
    t, tn = TOKEN_TILE, PROJ_COL_TILE
    row = pl.BlockSpec((1, tn), lambda j, i, k: (0, j))
    scale_args = [] if col_scale is None else [col_scale.reshape(1, n)]
    return pl.pallas_call(
        functools.partial(_proj_body, sigmoid=sigmoid, scaled=col_scale is not None),
        grid=(n // tn, b, l // t),
        in_specs=[
            pl.BlockSpec((None, t, d), lambda j, i, k: (i, k, 0)),
            pl.BlockSpec((d, tn), lambda j, i, k: (0, j)),
            row,
        ] + [row] * len(scale_args),
        out_specs=pl.BlockSpec((None, t, tn), lambda j, i, k: (i, k, j)),
        out_shape=jax.ShapeDtypeStruct((b, l, n), BF16),
        compiler_params=_params(("parallel", "parallel", "parallel"),
                                fuse_inputs=[False, True, True] + [True] * len(scale_args)),
        name="proj",
    )(h, w, bias.reshape(1, n), *scale_args)


def _rope_lanes(x, cos, sin_signed):
    outs = []
    lane = lax.broadcasted_iota(jnp.int32, (x.shape[0], V7X_LANES), 1)
    first_half = (lane % HEAD_DIM) < (HEAD_DIM // 2)
    for c0 in range(0, x.shape[1], V7X_LANES):
        xc = x[:, c0:c0 + V7X_LANES]
        partner = jnp.where(first_half,
                            pltpu.roll(xc, V7X_LANES - HEAD_DIM // 2, 1),
                            pltpu.roll(xc, HEAD_DIM // 2, 1))
        outs.append(xc * cos[:, c0:c0 + V7X_LANES] + partner * sin_signed[:, c0:c0 + V7X_LANES])
    return jnp.concatenate(outs, 1)


def _store_split(ref, x):
    t, dc = x.shape
    parts = jnp.swapaxes(x.reshape(t // DIL_SPLIT, DIL_SPLIT, dc), 0, 1)
    for r in range(DIL_SPLIT):
        ref[:, r * dc:(r + 1) * dc] = parts[r].astype(BF16)


def _proj_rope_body(h_ref, w_ref, b_ref, cos_ref, sin_ref, q_ref, k_ref, v_ref, z_ref,
                    qs_ref, ks_ref, vs_ref):
    dc = q_ref.shape[-1]
    acc = jnp.dot(h_ref[...], w_ref[...], preferred_element_type=F32) + b_ref[...]
    cos, sin = cos_ref[...], sin_ref[...]
    q = _rope_lanes(acc[:, :dc], cos, sin) * QK_SCALE_LOG2
    k = _rope_lanes(acc[:, dc:2 * dc], cos, sin)
    v = acc[:, 2 * dc:3 * dc]
    for ref, split_ref, val in ((q_ref, qs_ref, q), (k_ref, ks_ref, k), (v_ref, vs_ref, v)):
        ref[...] = val.astype(BF16)
        _store_split(split_ref, val)
    z_ref[...] = _silu(acc[:, 3 * dc:]).astype(BF16)


def _proj_rope(h, w, bias, cos_t, sin_t):
    b, l, d = h.shape
    n = w.shape[1]
    dc = n // 4
    t = ROPE_TOKEN_TILE
    tok = pl.BlockSpec((None, t, dc), lambda i, k: (i, k, 0))
    shp = jax.ShapeDtypeStruct((b, l, dc), BF16)
    spl = pl.BlockSpec((None, t // DIL_SPLIT, DIL_SPLIT * dc), lambda i, k: (i, k, 0))
    spl_shp = jax.ShapeDtypeStruct((b, l // DIL_SPLIT, DIL_SPLIT * dc), BF16)
    return pl.pallas_call(
        _proj_rope_body,
        grid=(b, l // t),
        in_specs=[
            pl.BlockSpec((None, t, d), lambda i, k: (i, k, 0)),
            pl.BlockSpec((d, n), lambda i, k: (0, 0)),
            pl.BlockSpec((1, n), lambda i, k: (0, 0)),
            pl.BlockSpec((t, dc), lambda i, k: (k, 0)),
            pl.BlockSpec((t, dc), lambda i, k: (k, 0)),
        ],
        out_specs=[tok, tok, tok, tok, spl, spl, spl],
        out_shape=[shp, shp, shp, shp, spl_shp, spl_shp, spl_shp],
        compiler_params=_params(("parallel", "parallel")),
        name="proj_rope",
    )(h, w, bias.reshape(1, n), cos_t, sin_t)


def _hy_pre_body(main_ref, prev_ref, next_ref, shift_ref, cw_ref, cb_ref, v_ref, x1_ref, x2_ref, *, da):
    i = pl.program_id(1)
    last = pl.num_programs(1) - 1
    t = main_ref.shape[0]
    halo = prev_ref.shape[0]
    blk = shift_ref.shape[1]
    nc = 3 * da
    outs = (v_ref, x1_ref, x2_ref)
    edge = lax.broadcasted_iota(jnp.int32, (V7X_SUBLANES, nc), 0)
    for r0 in range(0, t, blk):
        xb = main_ref[r0:r0 + blk, :nc]
        sh = jnp.dot(shift_ref[...], xb, preferred_element_type=F32)
        if r0 == 0:
            prev_row = jnp.where(i > 0, prev_ref[:, :nc].astype(F32)[halo - 1:halo], 0.0)
        else:
            prev_row = main_ref[r0 - halo:r0, :nc].astype(F32)[halo - 1:halo]
        if r0 + blk == t:
            next_row = jnp.where(i < last, next_ref[:, :nc].astype(F32)[0:1], 0.0)
        else:
            next_row = main_ref[r0 + blk:r0 + blk + halo, :nc].astype(F32)[0:1]
        up, dn = sh[:blk], sh[blk:]
        up = jnp.concatenate([up[:V7X_SUBLANES] + jnp.where(edge == 0, prev_row, 0.0), up[V7X_SUBLANES:]], 0)
        dn = jnp.concatenate([dn[:blk - V7X_SUBLANES],
                              dn[blk - V7X_SUBLANES:] + jnp.where(edge == V7X_SUBLANES - 1, next_row, 0.0)], 0)
        uc = up * cw_ref[0:1, :] + xb.astype(F32) * cw_ref[1:2, :] + dn * cw_ref[2:3, :] + cb_ref[...]
        for part in range(3):
            val = uc[:, part * da:(part + 1) * da]
            if part == 2:
                val = val * _silu(main_ref[r0:r0 + blk, nc:].astype(F32))
            outs[part][r0:r0 + blk, :] = val.astype(BF16)


def _hy_pre(proj_a, conv_w, conv_b):
    b, l, n = proj_a.shape
    da = n // 4
    t, halo = WIDE_TOKEN_TILE, BF16_TILE_ROWS
    nh = t // halo
    blk = HY_SHIFT_ROWS
    shift = jnp.concatenate([jnp.eye(blk, k=-1, dtype=BF16), jnp.eye(blk, k=1, dtype=BF16)], 0)
    tok = pl.BlockSpec((None, t, da), lambda i, j: (i, j, 0))
    shp = jax.ShapeDtypeStruct((b, l, da), BF16)
    return pl.pallas_call(
        functools.partial(_hy_pre_body, da=da),
        grid=(b, l // t),
        in_specs=[
            pl.BlockSpec((None, t, n), lambda i, j: (i, j, 0)),
            pl.BlockSpec((None, halo, n), lambda i, j: (i, jnp.maximum(j * nh - 1, 0), 0)),
            pl.BlockSpec((None, halo, n), lambda i, j: (i, jnp.minimum((j + 1) * nh, l // halo - 1), 0)),
            pl.BlockSpec((2 * blk, blk), lambda i, j: (0, 0)),
            pl.BlockSpec((3, 3 * da), lambda i, j: (0, 0)),
            pl.BlockSpec((1, 3 * da), lambda i, j: (0, 0)),
        ],
        out_specs=[tok, tok, tok],
        out_shape=[shp, shp, shp],
        compiler_params=_params(("parallel", "parallel")),
        name="hy_pre",
    )(proj_a, proj_a, proj_a, shift, conv_w, conv_b.reshape(1, 3 * da))


def _filter_body(feat_ref, w1_ref, b1_ref, f0_ref, w2_ref, b2_ref, f1_ref, w3_ref, b3_ref, dec_ref,
                 hf_ref, sum_ref):
    i = pl.program_id(0)
    hp = lax.Precision.HIGHEST
    half = feat_ref.shape[0] // 2
    n = hf_ref.shape[1]
    fa, fb = feat_ref[0:half, :], feat_ref[half:, :]
    feat = jnp.concatenate([fa, fb], 1)
    h = jnp.sin(f0_ref[...] * (jnp.dot(feat, w1_ref[...], preferred_element_type=F32, precision=hp)
                               + b1_ref[...]))
    h = jnp.sin(f1_ref[...] * (jnp.dot(h, w2_ref[...], preferred_element_type=F32, precision=hp)
                               + b2_ref[...]))
    h = jnp.dot(h.astype(BF16), w3_ref[...], preferred_element_type=F32) + b3_ref[...]
    t = jnp.concatenate([jnp.broadcast_to(fa[:, 0:1], (half, n)), jnp.broadcast_to(fb[:, 0:1], (half, n))], 1)
    h = h * jnp.exp(-t * jnp.abs(dec_ref[...]))
    hf_ref[0:half, :] = h[:, :n].astype(BF16)
    hf_ref[half:, :] = h[:, n:].astype(BF16)
    col_sum = jnp.sum(jnp.abs(h), 0, keepdims=True)
    abs_sum = col_sum[:, :n] + col_sum[:, n:]

    @pl.when(i > 0)
    def _():
        sum_ref[...] += abs_sum

    @pl.when(i == 0)
    def _():
        head = h[0:BF16_TILE_ROWS, :n]
        rows = lax.broadcasted_iota(jnp.int32, head.shape, 0)
        cols = lax.broadcasted_iota(jnp.int32, head.shape, 1)
        drop = (rows == 0) & (cols >= n // 2)
        hf_ref[0:BF16_TILE_ROWS, :] = jnp.where(drop, 0.0, head).astype(BF16)
        sum_ref[...] = abs_sum - jnp.sum(jnp.where(drop, jnp.abs(head), 0.0), 0, keepdims=True)


def _block_diag2(w):
    z = jnp.zeros_like(w)
    return jnp.concatenate([jnp.concatenate([w, z], 1), jnp.concatenate([z, w], 1)], 0)


def _filter_taps(l, w1, b1, freq, w2, b2, w3, b3, decay):
    fo = w1.shape[1]
    n = w3.shape[1]
    da = decay.shape[0]
    t = jnp.arange(l, dtype=F32) / l
    bands = jnp.arange(1, HYENA_BANDS + 1, dtype=F32)
    ang = 2.0 * math.pi * t[:, None] * bands[None, :]
    feat = jnp.concatenate([t[:, None], jnp.cos(ang), jnp.sin(ang)], -1)
    feat = jnp.pad(feat, ((0, 0), (0, FEAT_PAD - HYENA_EMB)))
    w1p = jnp.pad(w1, ((0, FEAT_PAD - HYENA_EMB), (0, 0)))
    twice = lambda v: jnp.tile(v.reshape(1, -1), (1, 2))
    dec = jnp.tile(decay, 2 * n // da).reshape(1, 2 * n)
    tt = FILTER_ROW_TILE
    const = lambda shape: pl.BlockSpec(shape, lambda i: (0,) * len(shape))
    return pl.pallas_call(
        _filter_body,
        grid=(l // tt,),
        in_specs=[
            pl.BlockSpec((tt, FEAT_PAD), lambda i: (i, 0)),
            const((2 * FEAT_PAD, 2 * fo)), const((1, 2 * fo)), const((1, 2 * fo)),
            const((2 * fo, 2 * fo)), const((1, 2 * fo)), const((1, 2 * fo)),
            const((2 * fo, 2 * n)), const((1, 2 * n)), const((1, 2 * n)),
        ],
        out_specs=[pl.BlockSpec((tt, n), lambda i: (i, 0)), const((1, n))],
        out_shape=[jax.ShapeDtypeStruct((l, n), BF16), jax.ShapeDtypeStruct((1, n), F32)],
        compiler_params=_params(("arbitrary",)),
        name="filter_taps",
    )(feat, _block_diag2(w1p), twice(b1), twice(freq[0]), _block_diag2(w2), twice(b2), twice(freq[1]),
      _block_diag2(w3.astype(BF16)), twice(b3), dec)


def _dft_tables(l):
    n = 2 * l
    n1 = n // DFT_N2
    kk = jnp.arange(n1 // 2, dtype=jnp.int32)
    nn = jnp.arange(n1 // 2, dtype=jnp.int32)
    th = (2.0 * math.pi / (2 * n1)) * (((2 * kk[:, None] + 1) * nn[None, :]) % (2 * n1)).astype(F32)
    eye2 = jnp.eye(2, dtype=F32)
    f1 = jnp.kron(jnp.concatenate([jnp.cos(th), -jnp.sin(th)], 0), eye2).astype(BF16)
    g = jnp.kron(jnp.concatenate([jnp.cos(th).T, -jnp.sin(th).T], 1) * (2.0 / n), eye2).astype(BF16)
    k2 = jnp.arange(DFT_N2, dtype=jnp.int32)
    n2 = jnp.arange(DFT_N2, dtype=jnp.int32)
    ph = (n2[None, None, :] * (k2[None, :, None] * (2 * n1) + 2 * kk[:, None, None] + 1)) % (2 * n)
    ang = (2.0 * math.pi / (2 * n)) * ph.astype(F32)
    c, s = jnp.cos(ang), jnp.sin(ang)
    mf = jnp.concatenate([jnp.concatenate([c, s], 2), jnp.concatenate([-s, c], 2)], 1).astype(BF16)
    return f1, g, mf


def _load_pair_group(ref, lead, g):
    start = pl.multiple_of(g * PAIR_GROUP, PAIR_GROUP)
    words = ref.bitcast(jnp.uint32)[(*lead, slice(None), pl.ds(start, PAIR_GROUP), slice(None))]
    words = jnp.swapaxes(words, 0, 1)
    return [pltpu.bitcast(words[i], BF16) for i in range(PAIR_GROUP)]


def _store_pair_group(ref, lead, g, vals):
    start = pl.multiple_of(g * PAIR_GROUP, PAIR_GROUP)
    words = jnp.stack([pltpu.bitcast(v, jnp.uint32) for v in vals], 0)
    ref.bitcast(jnp.uint32)[(*lead, slice(None), pl.ds(start, PAIR_GROUP), slice(None))] = (
        jnp.swapaxes(words, 0, 1))


def _dft_in_body(f_ref, z_ref, a_ref):
    n1 = a_ref.shape[2]
    cb = a_ref.shape[-1]

    def group(g, carry):
        zcat = jnp.concatenate(_load_pair_group(z_ref, (0,), g), 1)
        r = jnp.dot(f_ref[...], zcat, preferred_element_type=F32).astype(BF16)
        cols = [r[:, i * cb:(i + 1) * cb] for i in range(PAIR_GROUP)]
        _store_pair_group(a_ref, (0, 0), g, [c[:2 * n1] for c in cols])
        _store_pair_group(a_ref, (0, 1), g, [c[2 * n1:] for c in cols])
        return carry

    lax.fori_loop(0, z_ref.shape[2] // (2 * PAIR_GROUP), group, 0, unroll=4)


def _dft_in(f1p, z):
    b, half, n2, c = z.shape
    n1 = half
    cb, rs = V7X_LANES, DFT_STEP_ROWS
    return pl.pallas_call(
        _dft_in_body,
        grid=(b, c // cb, n2 // rs),
        in_specs=[
            pl.BlockSpec(f1p.shape, lambda i, j, s: (0, 0)),
            pl.BlockSpec((1, half, rs, cb), lambda i, j, s: (i, 0, s, j)),
        ],
        out_specs=pl.BlockSpec((1, 2, n1, rs, cb), lambda i, j, s: (i, 0, 0, s, j)),
        out_shape=jax.ShapeDtypeStruct((b, 2, n1, n2, c), BF16),
        compiler_params=_params(("parallel", "parallel", "parallel")),
        name="dft_in",
    )(f1p, z)


def _dft_out_body(g_ref, b_ref, x_ref, z_ref, skip_ref, o_ref):
    cb = o_ref.shape[-1]

    def group(g, carry):
        re, im = _load_pair_group(b_ref, (0, 0), g), _load_pair_group(b_ref, (0, 1), g)
        bcat = jnp.concatenate([jnp.concatenate([r, i], 0) for r, i in zip(re, im)], 1)
        y = jnp.dot(g_ref[...], bcat, preferred_element_type=F32)
        xs, zs = _load_pair_group(x_ref, (0,), g), _load_pair_group(z_ref, (0,), g)
        outs = [(xs[i].astype(F32) * (y[:, i * cb:(i + 1) * cb] + skip_ref[...] * zs[i].astype(F32))
                 ).astype(BF16) for i in range(PAIR_GROUP)]
        _store_pair_group(o_ref, (0,), g, outs)
        return carry

    lax.fori_loop(0, x_ref.shape[2] // (2 * PAIR_GROUP), group, 0, unroll=2)


def _dft_out(gp, bb, x, z, skip):
    b, half, n2, c = z.shape
    n1 = half
    cb, rs = V7X_LANES, DFT_STEP_ROWS
    slab = pl.BlockSpec((1, half, rs, cb), lambda i, j, s: (i, 0, s, j))
    return pl.pallas_call(
        _dft_out_body,
        grid=(b, c // cb, n2 // rs),
        in_specs=[
            pl.BlockSpec(gp.shape, lambda i, j, s: (0, 0)),
            pl.BlockSpec((1, 2, n1, rs, cb), lambda i, j, s: (i, 0, 0, s, j)),
            slab, slab,
            pl.BlockSpec((1, cb), lambda i, j, s: (0, j)),
        ],
        out_specs=slab,
        out_shape=jax.ShapeDtypeStruct(z.shape, BF16),
        compiler_params=_params(("parallel", "parallel", "parallel")),
        name="dft_out",
    )(gp, bb, x, z, skip.astype(F32).reshape(1, c))


def _slab_conv_body(mf_ref, k_ref, a_ref, o_ref):
    nb, _, ks, n2, c = a_ref.shape
    for kk in range(ks):
        kr, ki = k_ref[kk, 0].astype(F32), k_ref[kk, 1].astype(F32)
        for b in range(nb):
            a = a_ref[b, :, kk].reshape(2 * n2, c)
            x = jnp.dot(mf_ref[kk], a, preferred_element_type=F32)
            xr, xi = x[:n2], x[n2:]
            y = jnp.concatenate([xr * kr - xi * ki, xr * ki + xi * kr], 0).astype(BF16)
            out = lax.dot_general(mf_ref[kk], y, (((0,), (0,)), ((), ())), preferred_element_type=F32)
            o_ref[b, :, kk] = out.astype(BF16).reshape(2, n2, c)


def _slab_conv(mf, kspec, a, order):
    b, _, n1, n2, c = a.shape
    m = 2 * n2
    ks = max(1, SLABS_PER_STEP // b)
    return pl.pallas_call(
        _slab_conv_body,
        grid=(n1 // ks,),
        in_specs=[
            pl.BlockSpec((ks, m, m), lambda k: (k, 0, 0)),
            pl.BlockSpec((ks, 2, n2, c), lambda k: (k, 0, 0, order)),
            pl.BlockSpec((b, 2, ks, n2, c), lambda k: (0, 0, k, 0, 0)),
        ],
        out_specs=pl.BlockSpec((b, 2, ks, n2, c), lambda k: (0, 0, k, 0, 0)),
        out_shape=jax.ShapeDtypeStruct(a.shape, BF16),
        compiler_params=_params(("parallel",)),
        name="slab_conv",
    )(mf, kspec, a)


def _slab_spec_body(mf_ref, sum_ref, a_ref, k_ref):
    _, ks, n2, c = a_ref.shape
    half = c // 2
    inv = 1.0 / (sum_ref[:, :half] + sum_ref[:, half:] + 1e-6)
    for kk in range(ks):
        x = jnp.dot(mf_ref[kk], a_ref[:, kk].reshape(2 * n2, c), preferred_element_type=F32)
        k_ref[kk, 0] = ((x[:n2, :half] + x[:n2, half:]) * inv).astype(BF16)
        k_ref[kk, 1] = ((x[n2:, :half] - x[n2:, half:]) * inv).astype(BF16)


def _slab_spec(mf, sums, a):
    _, _, n1, n2, c = a.shape
    m = 2 * n2
    ks = SPEC_SLABS_PER_STEP
    return pl.pallas_call(
        _slab_spec_body,
        grid=(n1 // ks,),
        in_specs=[
            pl.BlockSpec((ks, m, m), lambda k: (k, 0, 0)),
            pl.BlockSpec((1, c), lambda k: (0, 0)),
            pl.BlockSpec((None, 2, ks, n2, c), lambda k: (0, 0, k, 0, 0)),
        ],
        out_specs=pl.BlockSpec((ks, 2, n2, c // 2), lambda k: (k, 0, 0, 0)),
        out_shape=jax.ShapeDtypeStruct((n1, 2, n2, c // 2), BF16),
        compiler_params=_params(("parallel",)),
        name="slab_spec",
    )(mf, sums, a)


def _hyena_spectrum(l, tables, w1, b1, freq, w2, b2, w3, b3, decay):
    f1p, _, mf = tables
    taps, sums = _filter_taps(l, w1, b1, freq, w2, b2, w3, b3, decay)
    a = _dft_in(f1p, taps.reshape(1, l // DFT_N2, DFT_N2, taps.shape[1]))
    return _slab_spec(mf, sums, a)


def _hyena(v, x1, x2g, kspec, skip, tables):
    f1p, gp, mf = tables
    b, l, c = v.shape
    slabs = lambda u: u.reshape(b, l // DFT_N2, DFT_N2, c)
    z = slabs(v)
    for order, xg in enumerate((slabs(x1), slabs(x2g))):
        bb = _slab_conv(mf, kspec, _dft_in(f1p, z), order)
        z = _dft_out(gp, bb, xg, z, skip[order])
    return z.reshape(b, l, c)


def _head_masks(rows, heads):
    lane = lax.broadcasted_iota(jnp.int32, (rows, heads * HEAD_DIM), 1)
    return [(lane >= h * HEAD_DIM) & (lane < (h + 1) * HEAD_DIM) for h in range(heads)]


def _stack_heads(q, masks):
    zero = jnp.zeros_like(q)
    return jnp.concatenate([jnp.where(m, q, zero) for m in masks], 0)


def _unstack_heads(res, masks, rows):
    out = jnp.where(masks[0], res[:rows], 0.0)
    for h in range(1, len(masks)):
        out = out + jnp.where(masks[h], res[h * rows:(h + 1) * rows], 0.0)
    return out


def _na_bias_table(rpb):
    heads = rpb.shape[0]
    c = jnp.arange(GRID_W)
    col_start = jnp.clip(c - NA_KC // 2, 0, GRID_W - NA_KC)
    col_ok = (c[None, :] >= col_start[:, None]) & (c[None, :] < col_start[:, None] + NA_KC)
    dc = jnp.clip(c[None, :] - c[:, None], -(NA_KC - 1), NA_KC - 1) + NA_KC - 1
    onehot = (dc[:, :, None] == jnp.arange(2 * NA_KC - 1)[None, None, :]).astype(F32)
    tcol = jnp.einsum("hrj,qkj->hrqk", rpb.astype(F32), onehot, precision=lax.Precision.HIGHEST)
    per_idx = [tcol[:, NA_KR - 1 - idx:2 * NA_KR - 1 - idx] for idx in range(NA_KR)]
    bias = jnp.transpose(jnp.stack(per_idx, 0), (0, 1, 3, 2, 4))
    bias = jnp.where(col_ok[None, None, :, None, :], bias * LOG2_E, NEG_INF)
    return bias.reshape(NA_KR, heads * GRID_W, NA_KR * GRID_W)


def _na_body(q_ref, k_ref, v_ref, z_ref, bias_ref, o_ref, *, rows, rb, heads):
    blk = pl.program_id(1)
    masks = _head_masks(GRID_W, heads)
    span = NA_KR * GRID_W

    def one_row(rr, carry):
        r = blk * rb + rr
        start = jnp.clip(r - NA_KR // 2, 0, rows - NA_KR)
        idx = r - start
        koff = pl.multiple_of(start * GRID_W, GRID_W)
        qoff = pl.multiple_of(rr * GRID_W, GRID_W)
        q = q_ref[pl.ds(qoff, GRID_W), :]
        kk = k_ref[pl.ds(koff, span), :]
        vv = v_ref[pl.ds(koff, span), :]
        s = lax.dot_general(_stack_heads(q, masks), kk, (((1,), (1,)), ((), ())),
                            preferred_element_type=F32)
        s = s + bias_ref[idx]
        m = jnp.max(s, -1, keepdims=True)
        e = jnp.exp2(s - m)
        p = (e / jnp.sum(e, -1, keepdims=True)).astype(BF16)
        o = _unstack_heads(jnp.dot(p, vv, preferred_element_type=F32), masks, GRID_W)
        z = z_ref[pl.ds(qoff, GRID_W), :].astype(F32)
        o_ref[pl.ds(qoff, GRID_W), :] = (o * _silu(z)).astype(BF16)
        return carry

    lax.fori_loop(0, rb, one_row, 0, unroll=NA_ROW_UNROLL)


def _na(proj_b, bias):
    b, l, n = proj_b.shape
    db = n // 4
    heads = db // HEAD_DIM
    rows = l // GRID_W
    rb = NA_ROWS_PER_STEP
    t = rb * GRID_W
    return pl.pallas_call(
        functools.partial(_na_body, rows=rows, rb=rb, heads=heads),
        grid=(b, rows // rb),
        in_specs=[
            pl.BlockSpec((None, t, db), lambda i, j: (i, j, 0)),
            pl.BlockSpec((None, l, db), lambda i, j: (i, 0, 1)),
            pl.BlockSpec((None, l, db), lambda i, j: (i, 0, 2)),
            pl.BlockSpec((None, t, db), lambda i, j: (i, j, 3)),
            pl.BlockSpec(bias.shape, lambda i, j: (0, 0, 0)),
        ],
        out_specs=pl.BlockSpec((None, t, db), lambda i, j: (i, j, 0)),
        out_shape=jax.ShapeDtypeStruct((b, l, db), BF16),
        compiler_params=_params(("parallel", "parallel")),
        name="na",
    )(proj_b, proj_b, proj_b, proj_b, bias)


def _attend(q, kk, vv, valid, masks):
    rows = q.shape[0]
    s = lax.dot_general(_stack_heads(q, masks), kk, (((1,), (1,)), ((), ())), preferred_element_type=F32)
    s = jnp.where(valid, s, NEG_INF)
    m = jnp.max(s, -1, keepdims=True)
    e = jnp.exp2(s - m)
    l = jnp.sum(e, -1, keepdims=True)
    p = (e / l).astype(BF16)
    o = _unstack_heads(jnp.dot(p, vv, preferred_element_type=F32), masks, rows)
    lse = _unstack_heads(jnp.broadcast_to(m + jnp.log2(l), (s.shape[0], q.shape[1])), masks, rows)
    return o, lse


def _split_pattern(dilation, n, nchunks, q_ref, kbuf, vbuf, o_split, l_split, masks, heads):
    blk = DIL_BLK
    dc = heads * HEAD_DIM
    phases = DIL_SPLIT // dilation
    mq = blk // phases
    row = lax.broadcasted_iota(jnp.int32, (heads * blk, 3 * blk), 0) % blk
    col = lax.broadcasted_iota(jnp.int32, (heads * blk, 3 * blk), 1)
    qa, qm = row // mq, row % mq
    ka = sum((col >= a * 3 * mq).astype(jnp.int32) for a in range(1, phases)) if phases > 1 else 0
    km = col - ka * (3 * mq)
    band = jnp.abs(phases * (km - mq - qm) + (ka - qa)) <= blk
    for m0 in range(0, blk, mq):
        gm = km + (n * blk + m0 - mq)
        valid = band & (gm >= 0) & (gm < nchunks * blk)
        for j in range(dilation):
            lanes = [slice((j + dilation * a) * dc, (j + dilation * a + 1) * dc) for a in range(phases)]
            q = jnp.concatenate([q_ref[m0:m0 + mq, ln] for ln in lanes], 0)
            krows = slice(blk + m0 - mq, blk + m0 + 2 * mq)
            kk = jnp.concatenate([kbuf[krows, ln] for ln in lanes], 0)
            vv = jnp.concatenate([vbuf[krows, ln] for ln in lanes], 0)
            o, lse = _attend(q, kk, vv, valid, masks)
            for a in range(phases):
                o_split[j + dilation * a, m0:m0 + mq, :] = o[a * mq:(a + 1) * mq]
                l_split[j + dilation * a, m0:m0 + mq, :] = lse[a * mq:(a + 1) * mq]


def _dil_body(qn_ref, knp_ref, kn_ref, knn_ref, vnp_ref, vn_ref, vnn_ref,
              qs_ref, ksp_ref, ks_ref, ksn_ref, vsp_ref, vs_ref, vsn_ref, cz_ref, y_ref,
              kbn, vbn, kbs, vbs, o_split, l_split, o_all, l_all, *, heads, nchunks):
    n = pl.program_id(1)
    blk = DIL_BLK
    chunk = qn_ref.shape[0]
    per = chunk // blk
    masks = _head_masks(blk, heads)
    for buf, prev, cur, nxt, halo in ((kbn, knp_ref, kn_ref, knn_ref, blk), (vbn, vnp_ref, vn_ref, vnn_ref, blk),
                                      (kbs, ksp_ref, ks_ref, ksn_ref, blk), (vbs, vsp_ref, vs_ref, vsn_ref, blk)):
        main = cur.shape[0]
        buf[0:halo] = prev[...]
        buf[halo:halo + main] = cur[...]
        buf[halo + main:] = nxt[...]

    qi = lax.broadcasted_iota(jnp.int32, (heads * blk, 3 * blk), 0) % blk
    ki = lax.broadcasted_iota(jnp.int32, (heads * blk, 3 * blk), 1)
    band = jnp.abs(ki - blk - qi) <= blk

    def token_block(i, carry):
        g = n * per + i
        off = pl.multiple_of(i * blk, blk)
        valid = band & ((ki >= blk) | (g > 0)) & ((ki < 2 * blk) | (g < nchunks * per - 1))
        o, lse = _attend(qn_ref[pl.ds(off, blk), :], kbn[pl.ds(off, 3 * blk), :], vbn[pl.ds(off, 3 * blk), :],
                         valid, masks)
        o_all[0, pl.ds(off, blk), :] = o
        l_all[0, pl.ds(off, blk), :] = lse
        return carry

    lax.fori_loop(0, per, token_block, 0, unroll=True)

    for g, (_, dilation) in enumerate(DIL_PATTERNS):
        if dilation == 1:
            continue
        _split_pattern(dilation, n, nchunks, qs_ref, kbs, vbs, o_split, l_split, masks, heads)
        o_all[g] = jnp.swapaxes(o_split[...], 0, 1).reshape(chunk, heads * HEAD_DIM)
        l_all[g] = jnp.swapaxes(l_split[...], 0, 1).reshape(chunk, heads * HEAD_DIM)

    ls = [l_all[g] for g in range(len(DIL_PATTERNS))]
    m = functools.reduce(jnp.maximum, ls)
    es = [jnp.exp2(l - m) for l in ls]
    den = functools.reduce(lambda a, b: a + b, es)
    o = functools.reduce(lambda a, b: a + b, [(e / den) * o_all[g] for g, e in enumerate(es)])
    y_ref[...] = (o * cz_ref[...].astype(F32)).astype(BF16)


def _dilated_mixture(qn, kn, vn, qs, ks, vs, cz):
    b, l, dc = qn.shape
    heads = dc // HEAD_DIM
    blk = DIL_BLK
    assert DIL_PATTERNS[0][1] == 1 and all(w == 2 * blk * d and DIL_SPLIT % d == 0 for w, d in DIL_PATTERNS)
    chunk = DIL_SPLIT * blk
    nchunks = l // chunk
    per = chunk // blk
    nat = pl.BlockSpec((None, chunk, dc), lambda i, n: (i, n, 0))
    nat_prev = pl.BlockSpec((None, blk, dc), lambda i, n: (i, jnp.maximum(n * per - 1, 0), 0))
    nat_next = pl.BlockSpec((None, blk, dc), lambda i, n: (i, jnp.minimum((n + 1) * per, nchunks * per - 1), 0))
    spl = pl.BlockSpec((None, blk, DIL_SPLIT * dc), lambda i, n: (i, n, 0))
    spl_prev = pl.BlockSpec((None, blk, DIL_SPLIT * dc), lambda i, n: (i, jnp.maximum(n - 1, 0), 0))
    spl_next = pl.BlockSpec((None, blk, DIL_SPLIT * dc), lambda i, n: (i, jnp.minimum(n + 1, nchunks - 1), 0))
    return pl.pallas_call(
        functools.partial(_dil_body, heads=heads, nchunks=nchunks),
        grid=(b, nchunks),
        in_specs=[nat, nat_prev, nat, nat_next, nat_prev, nat, nat_next,
                  spl, spl_prev, spl, spl_next, spl_prev, spl, spl_next, nat],
        out_specs=nat,
        out_shape=jax.ShapeDtypeStruct((b, l, dc), BF16),
        scratch_shapes=[pltpu.VMEM((chunk + 2 * blk, dc), BF16), pltpu.VMEM((chunk + 2 * blk, dc), BF16),
                        pltpu.VMEM((3 * blk, DIL_SPLIT * dc), BF16), pltpu.VMEM((3 * blk, DIL_SPLIT * dc), BF16),
                        pltpu.VMEM((DIL_SPLIT, blk, dc), F32), pltpu.VMEM((DIL_SPLIT, blk, dc), F32),
                        pltpu.VMEM((len(DIL_PATTERNS), chunk, dc), F32),
                        pltpu.VMEM((len(DIL_PATTERNS), chunk, dc), F32)],
        compiler_params=_params(("parallel", "parallel")),
        name="dilated",
    )(qn, kn, kn, kn, vn, vn, vn, qs, ks, ks, ks, vs, vs, vs, cz)


def _tail_body(x_ref, ya_ref, yb_ref, yc_ref, g_ref, gate_ref, wa_ref, wb_ref, wc_ref, wo_ref, lng_ref, lnb_ref,
               out_ref, *, alpha):
    d = x_ref.shape[-1]
    pa = jnp.dot(ya_ref[...], wa_ref[...], preferred_element_type=F32)
    pb = jnp.dot(yb_ref[...], wb_ref[...], preferred_element_type=F32)
    pc = jnp.dot(yc_ref[...], wc_ref[...], preferred_element_type=F32)
    g = g_ref[...].astype(F32)
    merged = g[:, :d] * pa + g[:, d:2 * d] * pb + g[:, 2 * d:] * pc
    sub = jnp.dot(merged.astype(BF16), wo_ref[...], preferred_element_type=F32) * gate_ref[...]
    res = alpha * x_ref[...] + sub
    out_ref[...] = _layernorm(res) * lng_ref[...] + lnb_ref[...]


def _tail(x, ya, yb, yc, g_all, gate, wa, wb, wc, wo, ln_g, ln_b, alpha):
    b, l, d = x.shape
    t = TOKEN_TILE
    tok = lambda w: pl.BlockSpec((None, t, w), lambda i, j: (i, j, 0))
    const = lambda a: pl.BlockSpec(a.shape, lambda i, j: (0,) * a.ndim)
    ln_g, ln_b = ln_g.reshape(1, d), ln_b.reshape(1, d)
    return pl.pallas_call(
        functools.partial(_tail_body, alpha=alpha),
        grid=(b, l // t),
        in_specs=[tok(d), tok(ya.shape[-1]), tok(yb.shape[-1]), tok(yc.shape[-1]), tok(3 * d),
                  pl.BlockSpec((None, 1, d), lambda i, j: (i, 0, 0)),
                  const(wa), const(wb), const(wc), const(wo), const(ln_g), const(ln_b)],
        out_specs=tok(d),
        out_shape=jax.ShapeDtypeStruct((b, l, d), F32),
        compiler_params=_params(("parallel", "parallel")),
        name="tail",
    )(x, ya, yb, yc, g_all, gate.reshape(b, 1, d), wa, wb, wc, wo, ln_g, ln_b)


def _rope_tables(l, heads):
    half = HEAD_DIM // 2
    inv = ROPE_THETA ** (-jnp.arange(half, dtype=F32) / half)
    ang = jnp.arange(l, dtype=F32)[:, None] * inv[None, :]
    cos, sin = jnp.cos(ang), jnp.sin(ang)
    return (jnp.tile(jnp.concatenate([cos, cos], -1), (1, heads)),
            jnp.tile(jnp.concatenate([-sin, sin], -1), (1, heads)))


def _layer(x, ada, lw, consts, alpha):
    d = x.shape[-1]
    shift, scale, gate = ada[:, :d], ada[:, d:2 * d], ada[:, 2 * d:]
    h = _ln_mod(x, scale, shift)
    w_in, b_in = lw["w_in"], lw["b_in"]
    proj_a = _proj(h, w_in[:, :2 * d], b_in[:2 * d])
    q_cols = jnp.arange(d) < d // 4
    proj_b = _proj(h, w_in[:, 2 * d:3 * d], b_in[2 * d:3 * d], col_scale=jnp.where(q_cols, QK_SCALE_LOG2, 1.0))
    cq, ck, cv, cz, cqs, cks, cvs = _proj_rope(h, w_in[:, 3 * d:4 * d], b_in[3 * d:4 * d], *consts["rope"])
    gates = _proj(h, w_in[:, 4 * d:], b_in[4 * d:], sigmoid=True)

    v, x1, x2g = _hy_pre(proj_a, lw["conv_w"], lw["conv_b"])
    ya = _hyena(v, x1, x2g, consts["kspec"], lw["skip"], consts["dft"])
    yb = _na(proj_b, lw["na_bias"])
    yc = _dilated_mixture(cq, ck, cv, cqs, cks, cvs, cz)
    return _tail(x, ya, yb, yc, gates, gate, lw["wa"], lw["wb"], lw["wc"], lw["wo"],
                 lw["ln_g"], lw["ln_b"], alpha)


def kernel(x_prompt, x_sample, c_prompt, c_sample, w_ada, b_ada, w_in, b_in, hy_conv_w, hy_conv_b, hy_w1, hy_b1, hy_freq, hy_w2, hy_b2, hy_w3, hy_b3, hy_decay, hy_skip, na_rpb, w_branch_a, w_branch_b, w_branch_c, w_out, ln_g, ln_b):
    depth, d, _ = w_in.shape
    heads_c = (d // 4) // HEAD_DIM
    alpha = (2 * depth) ** 0.25
    groups = [(x_prompt, c_prompt), (x_sample, c_sample)]

    nb_p = c_prompt.shape[0]
    c_all = jnp.concatenate([c_prompt, c_sample], 0)
    pad_rows = -c_all.shape[0] % 8
    ada_all = _ada(jnp.pad(c_all, ((0, pad_rows), (0, 0))), w_ada, b_ada)
    adas = [ada_all[:, :nb_p], ada_all[:, nb_p:nb_p + c_sample.shape[0]]]

    shared = {}
    for x, _ in groups:
        l = x.shape[1]
        if l not in shared:
            shared[l] = {"dft": _dft_tables(l), "rope": _rope_tables(l, heads_c)}

    ys = [x for x, _ in groups]
    for layer in range(depth):
        lw = {
            "w_in": w_in[layer].astype(BF16), "b_in": b_in[layer],
            "conv_w": hy_conv_w[layer], "conv_b": hy_conv_b[layer], "skip": hy_skip[layer],
            "na_bias": _na_bias_table(na_rpb[layer]),
            "wa": w_branch_a[layer].astype(BF16), "wb": w_branch_b[layer].astype(BF16),
            "wc": w_branch_c[layer].astype(BF16), "wo": w_out[layer].astype(BF16),
            "ln_g": ln_g[layer], "ln_b": ln_b[layer],
        }
        kspecs = {}
        for gi in range(len(groups)):
            l = ys[gi].shape[1]
            if l not in kspecs:
                kspecs[l] = _hyena_spectrum(l, shared[l]["dft"], hy_w1[layer], hy_b1[layer], hy_freq[layer],
                                            hy_w2[layer], hy_b2[layer], hy_w3[layer], hy_b3[layer],
                                            hy_decay[layer])
            consts = dict(shared[l], kspec=kspecs[l])
            ys[gi] = _layer(ys[gi], adas[gi][layer], lw, consts, alpha)
    return tuple(ys)
```
